```python
import math
import jax, jax.numpy as jnp
from jax import lax
import numpy as np

D_MODEL = 1024
BATCH = 16
SEQ = 4096
DEPTH = 4

EXPAND = 2
D_INNER = EXPAND * D_MODEL
CONV_WIDTH = 31
CHUNK = 128
SGU_GROUPS = 8
PLE_DIM = 256
N_MIXERS = 2
N_CONV_LAYERS = (DEPTH + 1) // 2
N_SGU_LAYERS = DEPTH // 2
EPS = 1e-6

kernel_name = "hybrid_conformer_conv_gmlp_trunk"


def _rmsnorm(x, g):
    xf = x.astype(jnp.float32)
    y = xf * lax.rsqrt(jnp.mean(xf * xf, axis=-1, keepdims=True) + EPS)
    return (y * g.astype(jnp.float32)).astype(x.dtype)


def _layernorm(x, g, b):
    xf = x.astype(jnp.float32)
    mu = jnp.mean(xf, axis=-1, keepdims=True)
    xc = xf - mu
    var = jnp.mean(xc * xc, axis=-1, keepdims=True)
    y = xc * lax.rsqrt(var + EPS)
    return (y * g.astype(jnp.float32) + b.astype(jnp.float32)).astype(x.dtype)


def _causal_depthwise_conv(x, w, b):
    k = w.shape[0]
    y = lax.conv_general_dilated(
        x, w[:, None, :].astype(x.dtype),
        window_strides=(1,), padding=[(k - 1, 0)],
        dimension_numbers=("NWC", "WIO", "NWC"),
        feature_group_count=x.shape[-1])
    return y + b.astype(x.dtype)


def _conformer_conv_mixer(a, b_gate, conv_w, conv_b, ln_g, ln_b):
    y = a * jax.nn.sigmoid(b_gate)
    y = _causal_depthwise_conv(y, conv_w, conv_b)
    y = _layernorm(y, ln_g, ln_b)
    return jax.nn.silu(y)


def _chunked_sgu_mixer(a, b_half, ln_g, ln_b, w_s, b_s):
    bsz, seq, e = a.shape
    n_chunks = seq // CHUNK
    u = jax.nn.gelu(a, approximate=False)
    v = _layernorm(jax.nn.gelu(b_half, approximate=False), ln_g, ln_b)
    vg = v.reshape(bsz, n_chunks, CHUNK, SGU_GROUPS, e // SGU_GROUPS)
    mask = jnp.tril(jnp.ones((CHUNK, CHUNK), dtype=bool))
    w = jnp.where(mask[None], w_s, jnp.zeros((), w_s.dtype)).astype(v.dtype)
    mixed = jnp.einsum("gts,bnsgc->bntgc", w, vg)
    mixed = mixed + b_s.T.astype(v.dtype)[None, None, :, :, None]
    return u * mixed.reshape(bsz, seq, e)


def _fwd_setup_inputs(seed: int = 0) -> dict:
    key = jax.random.key(seed)
    ks = jax.random.split(key, 20)
    f32 = jnp.float32
    E = D_INNER
    x = jax.random.normal(ks[0], (BATCH, SEQ, D_MODEL), f32)
    p = jax.random.normal(ks[1], (DEPTH, BATCH, SEQ, PLE_DIM), f32)
    norm_g = 1.0 + 0.02 * jax.random.normal(ks[2], (DEPTH, D_MODEL), f32)
    w_in = jax.random.normal(ks[3], (DEPTH, D_MODEL, 3 * E), f32) * D_MODEL ** -0.5
    w_out = jax.random.normal(ks[4], (DEPTH, E, D_MODEL), f32) * E ** -0.5
    conv_w = jax.random.normal(ks[5], (N_CONV_LAYERS, CONV_WIDTH, E), f32) * CONV_WIDTH ** -0.5
    conv_b = 0.02 * jax.random.normal(ks[6], (N_CONV_LAYERS, E), f32)
    conv_ln_g = 1.0 + 0.02 * jax.random.normal(ks[7], (N_CONV_LAYERS, E), f32)
    conv_ln_b = 0.02 * jax.random.normal(ks[8], (N_CONV_LAYERS, E), f32)
    sgu_ln_g = 1.0 + 0.02 * jax.random.normal(ks[9], (N_SGU_LAYERS, E), f32)
    sgu_ln_b = 0.02 * jax.random.normal(ks[10], (N_SGU_LAYERS, E), f32)
    sgu_w = jax.random.normal(ks[11], (N_SGU_LAYERS, SGU_GROUPS, CHUNK, CHUNK), f32) * CHUNK ** -0.5
    sgu_b = 1.0 + 0.1 * jax.random.normal(ks[12], (N_SGU_LAYERS, SGU_GROUPS, CHUNK), f32)
    pl_norm_g = 1.0 + 0.02 * jax.random.normal(ks[13], (DEPTH, D_MODEL), f32)
    pl_gate_w = jax.random.normal(ks[14], (DEPTH, D_MODEL, D_MODEL), f32) * D_MODEL ** -0.5
    pl_proj_w = jax.random.normal(ks[15], (DEPTH, PLE_DIM, D_MODEL), f32) * PLE_DIM ** -0.5
    final_g = 1.0 + 0.02 * jax.random.normal(ks[16], (D_MODEL,), f32)
    return {"x": x, "p": p, "norm_g": norm_g, "w_in": w_in, "w_out": w_out,
            "conv_w": conv_w, "conv_b": conv_b, "conv_ln_g": conv_ln_g, "conv_ln_b": conv_ln_b,
            "sgu_ln_g": sgu_ln_g, "sgu_ln_b": sgu_ln_b, "sgu_w": sgu_w, "sgu_b": sgu_b,
            "pl_norm_g": pl_norm_g, "pl_gate_w": pl_gate_w, "pl_proj_w": pl_proj_w,
            "final_g": final_g}


def _fwd_reference(x, p, norm_g, w_in, w_out, conv_w, conv_b, conv_ln_g, conv_ln_b,
              sgu_ln_g, sgu_ln_b, sgu_w, sgu_b, pl_norm_g, pl_gate_w, pl_proj_w, final_g):
    for i in range(DEPTH):
        h = _rmsnorm(x, norm_g[i])
        proj = jnp.einsum("bsd,de->bse", h, w_in[i])
        a, b_half, z = jnp.split(proj, 3, axis=-1)
        j = i // N_MIXERS
        if i % N_MIXERS == 0:
            y = _conformer_conv_mixer(a, b_half, conv_w[j], conv_b[j], conv_ln_g[j], conv_ln_b[j])
        else:
            y = _chunked_sgu_mixer(a, b_half, sgu_ln_g[j], sgu_ln_b[j], sgu_w[j], sgu_b[j])
        x = x + jnp.einsum("bse,ed->bsd", y * jax.nn.silu(z), w_out[i])
        gate = jax.nn.sigmoid(jnp.einsum("bsd,de->bse", _rmsnorm(x, pl_norm_g[i]), pl_gate_w[i]))
        x = x + gate * jnp.einsum("bsk,kd->bsd", p[i], pl_proj_w[i])
    return _rmsnorm(x, final_g)


import jax as _jax
import jax.numpy as _jnp

TWIN_FORMAT = 'train_step'
FWD_PARAMS = ['x', 'p', 'norm_g', 'w_in', 'w_out', 'conv_w', 'conv_b', 'conv_ln_g', 'conv_ln_b', 'sgu_ln_g', 'sgu_ln_b', 'sgu_w', 'sgu_b', 'pl_norm_g', 'pl_gate_w', 'pl_proj_w', 'final_g']
TWIN_WEIGHTS = ['norm_g', 'w_in', 'w_out', 'conv_w', 'conv_b', 'conv_ln_g', 'conv_ln_b', 'sgu_ln_g', 'sgu_ln_b', 'sgu_w', 'sgu_b', 'pl_norm_g', 'pl_gate_w', 'pl_proj_w', 'final_g']
TWIN_DIFF_INPUT = 'x'
TWIN_INPUTS = ['x', 'p', 'norm_g', 'w_in', 'w_out', 'conv_w', 'conv_b', 'conv_ln_g', 'conv_ln_b', 'sgu_ln_g', 'sgu_ln_b', 'sgu_w', 'sgu_b', 'pl_norm_g', 'pl_gate_w', 'pl_proj_w', 'final_g', 'loss_target', 'm_norm_g', 'm_w_in', 'm_w_out', 'm_conv_w', 'm_conv_b', 'm_conv_ln_g', 'm_conv_ln_b', 'm_sgu_ln_g', 'm_sgu_ln_b', 'm_sgu_w', 'm_sgu_b', 'm_pl_norm_g', 'm_pl_gate_w', 'm_pl_proj_w', 'm_final_g', 'v_norm_g', 'v_w_in', 'v_w_out', 'v_conv_w', 'v_conv_b', 'v_conv_ln_g', 'v_conv_ln_b', 'v_sgu_ln_g', 'v_sgu_ln_b', 'v_sgu_w', 'v_sgu_b', 'v_pl_norm_g', 'v_pl_gate_w', 'v_pl_proj_w', 'v_final_g']
TWIN_OUTPUTS = ['loss', 'grad_x', 'grad_norm_g', 'grad_w_in', 'grad_w_out', 'grad_conv_w', 'grad_conv_b', 'grad_conv_ln_g', 'grad_conv_ln_b', 'grad_sgu_ln_g', 'grad_sgu_ln_b', 'grad_sgu_w', 'grad_sgu_b', 'grad_pl_norm_g', 'grad_pl_gate_w', 'grad_pl_proj_w', 'grad_final_g', 'delta_norm_g', 'delta_w_in', 'delta_w_out', 'delta_conv_w', 'delta_conv_b', 'delta_conv_ln_g', 'delta_conv_ln_b', 'delta_sgu_ln_g', 'delta_sgu_ln_b', 'delta_sgu_w', 'delta_sgu_b', 'delta_pl_norm_g', 'delta_pl_gate_w', 'delta_pl_proj_w', 'delta_final_g', 'new_m_norm_g', 'new_m_w_in', 'new_m_w_out', 'new_m_conv_w', 'new_m_conv_b', 'new_m_conv_ln_g', 'new_m_conv_ln_b', 'new_m_sgu_ln_g', 'new_m_sgu_ln_b', 'new_m_sgu_w', 'new_m_sgu_b', 'new_m_pl_norm_g', 'new_m_pl_gate_w', 'new_m_pl_proj_w', 'new_m_final_g', 'new_v_norm_g', 'new_v_w_in', 'new_v_w_out', 'new_v_conv_w', 'new_v_conv_b', 'new_v_conv_ln_g', 'new_v_conv_ln_b', 'new_v_sgu_ln_g', 'new_v_sgu_ln_b', 'new_v_sgu_w', 'new_v_sgu_b', 'new_v_pl_norm_g', 'new_v_pl_gate_w', 'new_v_pl_proj_w', 'new_v_final_g']
TWIN_LEAF_KINDS = {'loss': 'loss', 'grad_x': 'grad_x', 'grad_norm_g': 'grad_w', 'grad_w_in': 'grad_w', 'grad_w_out': 'grad_w', 'grad_conv_w': 'grad_w', 'grad_conv_b': 'grad_w', 'grad_conv_ln_g': 'grad_w', 'grad_conv_ln_b': 'grad_w', 'grad_sgu_ln_g': 'grad_w', 'grad_sgu_ln_b': 'grad_w', 'grad_sgu_w': 'grad_w', 'grad_sgu_b': 'grad_w', 'grad_pl_norm_g': 'grad_w', 'grad_pl_gate_w': 'grad_w', 'grad_pl_proj_w': 'grad_w', 'grad_final_g': 'grad_w', 'delta_norm_g': 'delta_w', 'delta_w_in': 'delta_w', 'delta_w_out': 'delta_w', 'delta_conv_w': 'delta_w', 'delta_conv_b': 'delta_w', 'delta_conv_ln_g': 'delta_w', 'delta_conv_ln_b': 'delta_w', 'delta_sgu_ln_g': 'delta_w', 'delta_sgu_ln_b': 'delta_w', 'delta_sgu_w': 'delta_w', 'delta_sgu_b': 'delta_w', 'delta_pl_norm_g': 'delta_w', 'delta_pl_gate_w': 'delta_w', 'delta_pl_proj_w': 'delta_w', 'delta_final_g': 'delta_w', 'new_m_norm_g': 'new_m', 'new_m_w_in': 'new_m', 'new_m_w_out': 'new_m', 'new_m_conv_w': 'new_m', 'new_m_conv_b': 'new_m', 'new_m_conv_ln_g': 'new_m', 'new_m_conv_ln_b': 'new_m', 'new_m_sgu_ln_g': 'new_m', 'new_m_sgu_ln_b': 'new_m', 'new_m_sgu_w': 'new_m', 'new_m_sgu_b': 'new_m', 'new_m_pl_norm_g': 'new_m', 'new_m_pl_gate_w': 'new_m', 'new_m_pl_proj_w': 'new_m', 'new_m_final_g': 'new_m', 'new_v_norm_g': 'new_v', 'new_v_w_in': 'new_v', 'new_v_w_out': 'new_v', 'new_v_conv_w': 'new_v', 'new_v_conv_b': 'new_v', 'new_v_conv_ln_g': 'new_v', 'new_v_conv_ln_b': 'new_v', 'new_v_sgu_ln_g': 'new_v', 'new_v_sgu_ln_b': 'new_v', 'new_v_sgu_w': 'new_v', 'new_v_sgu_b': 'new_v', 'new_v_pl_norm_g': 'new_v', 'new_v_pl_gate_w': 'new_v', 'new_v_pl_proj_w': 'new_v', 'new_v_final_g': 'new_v'}


def _forward(args):
    return _fwd_reference(*[args[k] for k in FWD_PARAMS])


def _output_shape():
    out = _jax.eval_shape(lambda: _forward(_fwd_setup_inputs(0)))
    return out.shape, out.dtype

N_MICROBATCH = 1
ADAM_LR = 0.001
ADAM_B1 = 0.9
ADAM_B2 = 0.999
ADAM_EPS = 1e-08
ADAM_WD = 0.01
ADAM_STEP = 10
PER_EXAMPLE_BATCH_AXIS = {'x': 0, 'p': 1, 'loss_target': 0}
SHARED_INPUTS = []
_WEIGHT_DTYPES = {'norm_g': _jnp.float32, 'w_in': _jnp.float32, 'w_out': _jnp.float32, 'conv_w': _jnp.float32, 'conv_b': _jnp.float32, 'conv_ln_g': _jnp.float32, 'conv_ln_b': _jnp.float32, 'sgu_ln_g': _jnp.float32, 'sgu_ln_b': _jnp.float32, 'sgu_w': _jnp.float32, 'sgu_b': _jnp.float32, 'pl_norm_g': _jnp.float32, 'pl_gate_w': _jnp.float32, 'pl_proj_w': _jnp.float32, 'final_g': _jnp.float32}
MOMENT_SCALE = {'norm_g': 1.166391e-01, 'w_in': 4.826861e-02, 'w_out': 7.540610e-02, 'conv_w': 5.181913e-02, 'conv_b': 1.073591e-01, 'conv_ln_g': 6.337454e-02, 'conv_ln_b': 5.644643e-02, 'sgu_ln_g': 3.245886e-02, 'sgu_ln_b': 3.193748e-02, 'sgu_w': 4.679244e-02, 'sgu_b': 6.479206e-02, 'pl_norm_g': 3.970237e-02, 'pl_gate_w': 3.823355e-02, 'pl_proj_w': 9.759240e-02, 'final_g': 6.400192e+01}


def _to_microbatches(a, axis):
    t = _jnp.moveaxis(a, axis, 0)
    t = t.reshape((N_MICROBATCH, t.shape[0] // N_MICROBATCH) + t.shape[1:])
    return _jnp.moveaxis(t, 1, axis + 1)


def setup_inputs(seed: int = 0) -> dict:
    inp = _fwd_setup_inputs(seed)
    key = _jax.random.fold_in(_jax.random.key(seed), 7919)
    shape, _ = _output_shape()
    out = dict(inp)
    out["loss_target"] = _jax.random.normal(_jax.random.fold_in(key, 0), shape, _jnp.float32)
    for i, name in enumerate(TWIN_WEIGHTS):
        w = inp[name].astype(_jnp.float32)
        if MOMENT_SCALE is None:
            s = _jnp.sqrt(_jnp.mean(_jnp.square(w)) + 1e-30)
        else:
            s = MOMENT_SCALE[name]
        km, kv = _jax.random.split(_jax.random.fold_in(key, i + 1))
        out[name] = w
        out["m_" + name] = s * _jax.random.normal(km, w.shape, _jnp.float32)
        out["v_" + name] = (s * s) * _jax.random.uniform(kv, w.shape, _jnp.float32, 0.5, 1.5)
    if N_MICROBATCH > 1:
        for name, axis in PER_EXAMPLE_BATCH_AXIS.items():
            out[name] = _to_microbatches(out[name], axis)
    return {'x': out['x'], 'p': out['p'], 'norm_g': out['norm_g'], 'w_in': out['w_in'], 'w_out': out['w_out'], 'conv_w': out['conv_w'], 'conv_b': out['conv_b'], 'conv_ln_g': out['conv_ln_g'], 'conv_ln_b': out['conv_ln_b'], 'sgu_ln_g': out['sgu_ln_g'], 'sgu_ln_b': out['sgu_ln_b'], 'sgu_w': out['sgu_w'], 'sgu_b': out['sgu_b'], 'pl_norm_g': out['pl_norm_g'], 'pl_gate_w': out['pl_gate_w'], 'pl_proj_w': out['pl_proj_w'], 'final_g': out['final_g'], 'loss_target': out['loss_target'], 'm_norm_g': out['m_norm_g'], 'm_w_in': out['m_w_in'], 'm_w_out': out['m_w_out'], 'm_conv_w': out['m_conv_w'], 'm_conv_b': out['m_conv_b'], 'm_conv_ln_g': out['m_conv_ln_g'], 'm_conv_ln_b': out['m_conv_ln_b'], 'm_sgu_ln_g': out['m_sgu_ln_g'], 'm_sgu_ln_b': out['m_sgu_ln_b'], 'm_sgu_w': out['m_sgu_w'], 'm_sgu_b': out['m_sgu_b'], 'm_pl_norm_g': out['m_pl_norm_g'], 'm_pl_gate_w': out['m_pl_gate_w'], 'm_pl_proj_w': out['m_pl_proj_w'], 'm_final_g': out['m_final_g'], 'v_norm_g': out['v_norm_g'], 'v_w_in': out['v_w_in'], 'v_w_out': out['v_w_out'], 'v_conv_w': out['v_conv_w'], 'v_conv_b': out['v_conv_b'], 'v_conv_ln_g': out['v_conv_ln_g'], 'v_conv_ln_b': out['v_conv_ln_b'], 'v_sgu_ln_g': out['v_sgu_ln_g'], 'v_sgu_ln_b': out['v_sgu_ln_b'], 'v_sgu_w': out['v_sgu_w'], 'v_sgu_b': out['v_sgu_b'], 'v_pl_norm_g': out['v_pl_norm_g'], 'v_pl_gate_w': out['v_pl_gate_w'], 'v_pl_proj_w': out['v_pl_proj_w'], 'v_final_g': out['v_final_g']}


def _loss(weights, diff, rest, loss_target):
    with _jax.named_scope("forward"):
        args = {**rest, TWIN_DIFF_INPUT: diff, **{k: w.astype(_WEIGHT_DTYPES[k]) for k, w in weights.items()}}
        y = _forward(args)
    with _jax.named_scope("loss_head"):
        err = _jnp.square(y.astype(_jnp.float32) - loss_target)
        return 0.5 * _jnp.sum(_jnp.mean(err, axis=-1)) if err.ndim else 0.5 * err


def _adamw(w, g, m, v):
    m = ADAM_B1 * m + (1.0 - ADAM_B1) * g
    v = ADAM_B2 * v + (1.0 - ADAM_B2) * _jnp.square(g)
    m_hat = m / (1.0 - ADAM_B1 ** ADAM_STEP)
    v_hat = v / (1.0 - ADAM_B2 ** ADAM_STEP)
    delta = -ADAM_LR * (m_hat / (_jnp.sqrt(v_hat) + ADAM_EPS) + ADAM_WD * w)
    return delta, m, v


def reference(x, p, norm_g, w_in, w_out, conv_w, conv_b, conv_ln_g, conv_ln_b, sgu_ln_g, sgu_ln_b, sgu_w, sgu_b, pl_norm_g, pl_gate_w, pl_proj_w, final_g, loss_target, m_norm_g, m_w_in, m_w_out, m_conv_w, m_conv_b, m_conv_ln_g, m_conv_ln_b, m_sgu_ln_g, m_sgu_ln_b, m_sgu_w, m_sgu_b, m_pl_norm_g, m_pl_gate_w, m_pl_proj_w, m_final_g, v_norm_g, v_w_in, v_w_out, v_conv_w, v_conv_b, v_conv_ln_g, v_conv_ln_b, v_sgu_ln_g, v_sgu_ln_b, v_sgu_w, v_sgu_b, v_pl_norm_g, v_pl_gate_w, v_pl_proj_w, v_final_g):
    given = dict(x=x, p=p, norm_g=norm_g, w_in=w_in, w_out=w_out, conv_w=conv_w, conv_b=conv_b, conv_ln_g=conv_ln_g, conv_ln_b=conv_ln_b, sgu_ln_g=sgu_ln_g, sgu_ln_b=sgu_ln_b, sgu_w=sgu_w, sgu_b=sgu_b, pl_norm_g=pl_norm_g, pl_gate_w=pl_gate_w, pl_proj_w=pl_proj_w, final_g=final_g, loss_target=loss_target, m_norm_g=m_norm_g, m_w_in=m_w_in, m_w_out=m_w_out, m_conv_w=m_conv_w, m_conv_b=m_conv_b, m_conv_ln_g=m_conv_ln_g, m_conv_ln_b=m_conv_ln_b, m_sgu_ln_g=m_sgu_ln_g, m_sgu_ln_b=m_sgu_ln_b, m_sgu_w=m_sgu_w, m_sgu_b=m_sgu_b, m_pl_norm_g=m_pl_norm_g, m_pl_gate_w=m_pl_gate_w, m_pl_proj_w=m_pl_proj_w, m_final_g=m_final_g, v_norm_g=v_norm_g, v_w_in=v_w_in, v_w_out=v_w_out, v_conv_w=v_conv_w, v_conv_b=v_conv_b, v_conv_ln_g=v_conv_ln_g, v_conv_ln_b=v_conv_ln_b, v_sgu_ln_g=v_sgu_ln_g, v_sgu_ln_b=v_sgu_ln_b, v_sgu_w=v_sgu_w, v_sgu_b=v_sgu_b, v_pl_norm_g=v_pl_norm_g, v_pl_gate_w=v_pl_gate_w, v_pl_proj_w=v_pl_proj_w, v_final_g=v_final_g)
    weights = {n: given[n] for n in TWIN_WEIGHTS}
    shared = {n: given[n] for n in SHARED_INPUTS}
    per_example = {n: given[n] for n in ['x', 'p']}
    grad_fn = _jax.value_and_grad(_loss, argnums=(0, 1))

    def one_microbatch(ex, loss_target):
        ex = dict(ex)
        diff = ex.pop(TWIN_DIFF_INPUT)
        return grad_fn(weights, diff, {**shared, **ex}, loss_target)

    if N_MICROBATCH == 1:
        loss, (grad_w, grad_x) = one_microbatch(per_example, given["loss_target"])
    else:
        def body(carry, xs):
            loss_sum, grad_sum = carry
            l_k, (gw_k, gx_k) = one_microbatch(xs[0], xs[1])
            with _jax.named_scope("update"):
                return (loss_sum + l_k, _jax.tree.map(_jnp.add, grad_sum, gw_k)), gx_k

        init = (_jnp.zeros((), _jnp.float32), _jax.tree.map(_jnp.zeros_like, weights))
        (loss, grad_w), grad_x = _jax.lax.scan(body, init, (per_example, given["loss_target"]))
    with _jax.named_scope("update"):
        delta_w, new_m, new_v = {}, {}, {}
        for n in TWIN_WEIGHTS:
            delta_w[n], new_m[n], new_v[n] = _adamw(weights[n], grad_w[n], given["m_" + n], given["v_" + n])
    return (loss, grad_x, *[grad_w[n] for n in TWIN_WEIGHTS], *[delta_w[n] for n in TWIN_WEIGHTS],
            *[new_m[n] for n in TWIN_WEIGHTS], *[new_v[n] for n in TWIN_WEIGHTS])
```

```python
import functools

import jax
import jax.numpy as jnp
from jax import lax
from jax.experimental import pallas as pl
from jax.experimental.pallas import tpu as pltpu

F32 = jnp.float32
BF16 = jnp.bfloat16
SDS = jax.ShapeDtypeStruct

EPS = 1e-6
CHUNK = 128
GROUPS = 8
HALO = 32
N_CHIPS = 4
LANES = 128
SUBLANES = 8
V7X_VMEM_LIMIT = 56 << 20

ADAM_LR = 0.001
ADAM_B1 = 0.9
ADAM_B2 = 0.999
ADAM_EPS = 1e-08
ADAM_WD = 0.01
ADAM_STEP = 10

MESH_IDS = pl.DeviceIdType.MESH
ANY = pl.BlockSpec(memory_space=pl.ANY)


def _params(n_axes):
    return pltpu.CompilerParams(dimension_semantics=("arbitrary",) * n_axes, vmem_limit_bytes=V7X_VMEM_LIMIT)


def _const(shape):
    zeros = (0,) * len(shape)
    return pl.BlockSpec(shape, lambda *_: zeros)


def _layer(shape, layer):
    zeros = (0,) * len(shape)
    return pl.BlockSpec((None,) + tuple(shape), lambda *_: (layer,) + zeros)


def _sigmoid(v):
    return jax.nn.sigmoid(v)


def _dsilu(v, s):
    return s * (1.0 + v * (1.0 - s))


def _gelu_parts(v):
    cdf = 0.5 * (1.0 + lax.erf(v * 0.7071067811865476))
    pdf = jnp.exp(-0.5 * v * v) * 0.3989422804014327
    return v * cdf, cdf + v * pdf


def _gelu(v):
    return 0.5 * v * (1.0 + lax.erf(v * 0.7071067811865476))


def _rms_stats(x):
    r = lax.rsqrt(jnp.mean(x * x, axis=-1, keepdims=True) + EPS)
    return r, x * r


def _rms_bwd(dy, g, r, xh):
    gdy = dy * g
    return r * (gdy - xh * jnp.mean(xh * gdy, axis=-1, keepdims=True))


def _ln_stats(x):
    mu = jnp.mean(x, axis=-1, keepdims=True)
    xc = x - mu
    rs = lax.rsqrt(jnp.mean(xc * xc, axis=-1, keepdims=True) + EPS)
    return rs, xc * rs


def _ln_bwd(dy, g, rs, xh):
    dxh = dy * g
    return rs * (dxh - jnp.mean(dxh, axis=-1, keepdims=True) - xh * jnp.mean(dxh * xh, axis=-1, keepdims=True))


def _dot(a, b):
    return jnp.dot(a, b, preferred_element_type=F32)


def _dot_nt(a, b):
    return lax.dot_general(a, b, (((1,), (1,)), ((), ())), preferred_element_type=F32)


def _dot_tn(a, b):
    return lax.dot_general(a, b, (((0,), (0,)), ((), ())), preferred_element_type=F32)


def _fwd_in(x, norm_g, w_in_full, layer, tm):
    t, d = x.shape
    _, nk, _, n4 = w_in_full.shape

    def body(x_ref, g_ref, w_ref, h_ref, proj_ref):
        r, xh = _rms_stats(x_ref[...])
        h = (xh * g_ref[...]).astype(BF16)
        h_ref[...] = h
        for k in range(nk):
            proj_ref[:, k * n4:(k + 1) * n4] = _dot(h, w_ref[k]).astype(BF16)

    return pl.pallas_call(
        body, name="fwd_in", grid=(t // tm,),
        in_specs=[pl.BlockSpec((tm, d), lambda i: (i, 0)), _layer((1, d), layer), _layer((nk, d, n4), layer)],
        out_specs=[pl.BlockSpec((tm, d), lambda i: (i, 0)), pl.BlockSpec((tm, nk * n4), lambda i: (i, 0))],
        out_shape=[SDS((t, d), BF16), SDS((t, nk * n4), BF16)],
        compiler_params=_params(1),
    )(x, norm_g, w_in_full)


def _halo_maps(nt, hb, n_halo_blocks):
    def prev(b, i):
        return (jnp.maximum((b * nt + i) * hb - 1, 0), 0)

    def nxt(b, i):
        return (jnp.minimum((b * nt + i + 1) * hb, n_halo_blocks - 1), 0)

    return prev, nxt


def _col_blocks(e):
    cb = min(2 * LANES, e)
    return cb, e // cb


def _conv_taps(src_ref, w_ref, dst_ref, cb_idx, n_rows, rb, first, reverse):
    k_taps = w_ref.shape[1] - 1
    for r0 in range(0, n_rows, rb):
        acc = None
        for k in range(k_taps):
            wk = w_ref[cb_idx, pl.ds((k_taps - 1 - k) if reverse else k, 1), :]
            term = wk * src_ref[cb_idx, pl.ds(r0 + first + k, rb), :]
            acc = term if acc is None else acc + term
        dst_ref[cb_idx, pl.ds(r0, rb), :] = acc


def _fwd_conv(proj, conv_w_blk, conv_b, ln_g, ln_b, layer, bsz, seq, tm):
    t, e3 = proj.shape
    e = e3 // 3
    nt = seq // tm
    hb = tm // HALO
    cb, ncb = _col_blocks(e)
    rb = min(64, tm)
    kp = conv_w_blk.shape[2]
    prev, _ = _halo_maps(nt, hb, t // HALO)

    def body(proj_ref, halo_ref, w_ref, b_ref, g_ref, lb_ref, u_ref, y1_ref, y0s, y1s):
        i = pl.program_id(1)
        a = proj_ref[:, 0:e].astype(F32)
        b = proj_ref[:, e:2 * e].astype(F32)
        y0 = a * _sigmoid(b)
        ah = halo_ref[:, 0:e].astype(F32)
        bh = halo_ref[:, e:2 * e].astype(F32)
        y0h = jnp.where(i > 0, ah * _sigmoid(bh), 0.0)
        for c in range(ncb):
            y0s[c, 0:HALO, :] = y0h[:, c * cb:(c + 1) * cb]
            y0s[c, HALO:HALO + tm, :] = y0[:, c * cb:(c + 1) * cb]

        def per_block(c, carry):
            _conv_taps(y0s, w_ref, y1s, c, tm, rb, HALO - (kp - 2), False)
            return carry

        lax.fori_loop(0, ncb, per_block, 0)
        y1 = jnp.concatenate([y1s[c] for c in range(ncb)], axis=1) + b_ref[...]
        y1_ref[...] = y1
        rs, xh = _ln_stats(y1)
        y2 = xh * g_ref[...] + lb_ref[...]
        y = y2 * _sigmoid(y2)
        z = proj_ref[:, 2 * e:3 * e].astype(F32)
        u_ref[...] = (y * (z * _sigmoid(z))).astype(BF16)

    return pl.pallas_call(
        body, name="fwd_conv", grid=(bsz, nt),
        in_specs=[pl.BlockSpec((tm, e3), lambda b, i: (b * nt + i, 0)),
                  pl.BlockSpec((HALO, 2 * e), prev),
                  _layer((ncb, kp, cb), layer), _layer((1, e), layer), _layer((1, e), layer), _layer((1, e), layer)],
        out_specs=[pl.BlockSpec((tm, e), lambda b, i: (b * nt + i, 0)), pl.BlockSpec((tm, e), lambda b, i: (b * nt + i, 0))],
        out_shape=[SDS((t, e), BF16), SDS((t, e), F32)],
        scratch_shapes=[pltpu.VMEM((ncb, HALO + tm, cb), F32), pltpu.VMEM((ncb, tm, cb), F32)],
        compiler_params=_params(2),
    )(proj, proj, conv_w_blk, conv_b, ln_g, ln_b)


def _tril_mask():
    rows = lax.broadcasted_iota(jnp.int32, (CHUNK, CHUNK), 0)
    cols = lax.broadcasted_iota(jnp.int32, (CHUNK, CHUNK), 1)
    return rows >= cols


def _fwd_sgu(proj, ln_g, ln_b, sgu_w, sgu_bt, layer, tm):
    t, e3 = proj.shape
    e = e3 // 3
    gw = e // GROUPS
    nch = tm // CHUNK

    def body(proj_ref, g_ref, lb_ref, w_ref, bt_ref, u_ref, mixed):
        a = proj_ref[:, 0:e].astype(F32)
        b = proj_ref[:, e:2 * e].astype(F32)
        z = proj_ref[:, 2 * e:3 * e].astype(F32)
        rs, xh = _ln_stats(_gelu(b))
        v = (xh * g_ref[...] + lb_ref[...]).astype(BF16)
        mask = _tril_mask()
        for g in range(GROUPS):
            wm = jnp.where(mask, w_ref[g], 0.0).astype(BF16)
            bias = bt_ref[:, g:g + 1]
            for n in range(nch):
                blk = v[n * CHUNK:(n + 1) * CHUNK, g * gw:(g + 1) * gw]
                mixed[n * CHUNK:(n + 1) * CHUNK, g * gw:(g + 1) * gw] = _dot(wm, blk) + bias
        y = _gelu(a) * mixed[...]
        u_ref[...] = (y * (z * _sigmoid(z))).astype(BF16)

    return pl.pallas_call(
        body, name="fwd_sgu", grid=(t // tm,),
        in_specs=[pl.BlockSpec((tm, e3), lambda i: (i, 0)), _layer((1, e), layer), _layer((1, e), layer),
                  _layer((GROUPS, CHUNK, CHUNK), layer), _layer((CHUNK, GROUPS), layer)],
        out_specs=pl.BlockSpec((tm, e), lambda i: (i, 0)),
        out_shape=SDS((t, e), BF16),
        scratch_shapes=[pltpu.VMEM((tm, e), F32)],
        compiler_params=_params(1),
    )(proj, ln_g, ln_b, sgu_w, sgu_bt)


def _ple_forward(x1, p_ref, plg_ref, gw_ref, pw_ref):
    nk, _, dq = pw_ref.shape
    r, xh = _rms_stats(x1)
    rn = (xh * plg_ref[...]).astype(BF16)
    gate = _sigmoid(_dot(rn, gw_ref[...]))
    pb = p_ref[...].astype(BF16)
    q = jnp.concatenate([_dot(pb, pw_ref[k]) for k in range(nk)], axis=1)
    return r, xh, rn, gate, q


def _fwd_out(x, u, w_out_full, pl_norm_g, gate_w_full, p, proj_w_full, layer, tm):
    t, d = x.shape
    e = u.shape[1]
    ple = p.shape[-1]
    nk, dq = proj_w_full.shape[1], proj_w_full.shape[3]

    def body(x_ref, u_ref, wo_ref, plg_ref, gw_ref, p_ref, pw_ref, x1_ref, x2_ref):
        x1 = x_ref[...] + _dot(u_ref[...], wo_ref[...])
        x1_ref[...] = x1
        _, _, _, gate, q = _ple_forward(x1, p_ref, plg_ref, gw_ref, pw_ref)
        x2_ref[...] = x1 + gate * q

    return pl.pallas_call(
        body, name="fwd_out", grid=(t // tm,),
        in_specs=[pl.BlockSpec((tm, d), lambda i: (i, 0)), pl.BlockSpec((tm, e), lambda i: (i, 0)),
                  _layer((e, d), layer), _layer((1, d), layer), _layer((d, d), layer),
                  pl.BlockSpec((None, tm, ple), lambda i: (layer, i, 0)), _layer((nk, ple, dq), layer)],
        out_specs=[pl.BlockSpec((tm, d), lambda i: (i, 0)), pl.BlockSpec((tm, d), lambda i: (i, 0))],
        out_shape=[SDS((t, d), F32), SDS((t, d), F32)],
        compiler_params=_params(1),
    )(x, u, w_out_full, pl_norm_g, gate_w_full, p, proj_w_full)


def _loss_head(x, final_g, target, tm):
    t, d = x.shape
    n_steps = t // tm

    def body(x_ref, g_ref, tgt_ref, loss_ref, dx_ref, dg_ref, sq_acc):
        i = pl.program_id(0)

        @pl.when(i == 0)
        def _():
            sq_acc[...] = jnp.zeros_like(sq_acc)
            dg_ref[...] = jnp.zeros_like(dg_ref)

        g = g_ref[...]
        r, xh = _rms_stats(x_ref[...])
        diff = xh * g - tgt_ref[...]
        sq_acc[...] += jnp.sum(diff * diff, axis=0, keepdims=True)
        dout = diff * (1.0 / d)
        dg_ref[...] += jnp.sum(dout * xh, axis=0, keepdims=True)
        dx_ref[...] = _rms_bwd(dout, g, r, xh)

        @pl.when(i == n_steps - 1)
        def _():
            loss_ref[...] = jnp.sum(sq_acc[...], axis=1, keepdims=True) * (0.5 / d)

    return pl.pallas_call(
        body, name="loss_head", grid=(n_steps,),
        in_specs=[pl.BlockSpec((tm, d), lambda i: (i, 0)), _const((1, d)), pl.BlockSpec((tm, d), lambda i: (i, 0))],
        out_specs=[_const((1, 1)), pl.BlockSpec((tm, d), lambda i: (i, 0)), _const((1, d))],
        out_shape=[SDS((1, 1), F32), SDS((t, d), F32), SDS((1, d), F32)],
        scratch_shapes=[pltpu.VMEM((1, d), F32)],
        compiler_params=_params(1),
    )(x, final_g, target)


def _bwd_out(dx2, x1, p, pl_norm_g, gate_w_full, proj_w_full, w_out_full, layer, tm):
    t, d = dx2.shape
    e = w_out_full.shape[1]
    ple = p.shape[-1]
    nk, dq_w = proj_w_full.shape[1], proj_w_full.shape[3]

    def body(dx2_ref, x1_ref, p_ref, plg_ref, gw_ref, pw_ref, wo_ref, dx1_ref, du_ref, rn_ref, ds_ref, dq_ref, dplg_ref):
        @pl.when(pl.program_id(0) == 0)
        def _():
            dplg_ref[...] = jnp.zeros_like(dplg_ref)

        dx2v = dx2_ref[...]
        r, xh, rn, gate, q = _ple_forward(x1_ref[...], p_ref, plg_ref, gw_ref, pw_ref)
        rn_ref[...] = rn
        dq_ref[...] = (dx2v * gate).astype(BF16)
        ds = (dx2v * q * gate * (1.0 - gate)).astype(BF16)
        ds_ref[...] = ds
        dr = _dot_nt(ds, gw_ref[...])
        dplg_ref[...] += jnp.sum(dr * xh, axis=0, keepdims=True)
        dx1 = dx2v + _rms_bwd(dr, plg_ref[...], r, xh)
        dx1_ref[...] = dx1
        du_ref[...] = _dot_nt(dx1.astype(BF16), wo_ref[...]).astype(BF16)

    row = lambda w: pl.BlockSpec((tm, w), lambda i: (i, 0))
    return pl.pallas_call(
        body, name="bwd_out", grid=(t // tm,),
        in_specs=[row(d), row(d), pl.BlockSpec((None, tm, ple), lambda i: (layer, i, 0)),
                  _layer((1, d), layer), _layer((d, d), layer), _layer((nk, ple, dq_w), layer), _layer((e, d), layer)],
        out_specs=[row(d), row(e), row(d), row(d), row(d), _const((1, d))],
        out_shape=[SDS((t, d), F32), SDS((t, e), BF16), SDS((t, d), BF16), SDS((t, d), BF16), SDS((t, d), BF16),
                   SDS((1, d), F32)],
        compiler_params=_params(1),
    )(dx2, x1, p, pl_norm_g, gate_w_full, proj_w_full, w_out_full)


def _bwd_conv(du, proj, y1, conv_w_blk, ln_g, ln_b, layer, bsz, seq, tm):
    t, e3 = proj.shape
    e = e3 // 3
    nt = seq // tm
    hb = tm // HALO
    cb, ncb = _col_blocks(e)
    rb = min(64, tm)
    kp = conv_w_blk.shape[2]
    k_taps = kp - 1
    prev, nxt = _halo_maps(nt, hb, t // HALO)
    z_halo = lambda b, i: (nxt(b, i)[0], 2)

    def ln_silu_bwd(du_v, z_v, y1_v, g, lb):
        rs, xh = _ln_stats(y1_v)
        y2 = xh * g + lb
        sg = _sigmoid(y2)
        sz = _sigmoid(z_v)
        dy = du_v * (z_v * sz)
        dy2 = dy * _dsilu(y2, sg)
        return _ln_bwd(dy2, g, rs, xh), dy2, xh, du_v * (y2 * sg) * _dsilu(z_v, sz)

    def body(du_ref, proj_ref, y1_ref, duh_ref, zh_ref, y1h_ref, abh_ref, w_ref, g_ref, lb_ref,
             dproj_ref, dw_ref, dcb_ref, dg_ref, dlb_ref, y0s, dy1s, dy0s):
        b_id, i = pl.program_id(0), pl.program_id(1)

        @pl.when((b_id == 0) & (i == 0))
        def _():
            dw_ref[...] = jnp.zeros_like(dw_ref)
            dcb_ref[...] = jnp.zeros_like(dcb_ref)
            dg_ref[...] = jnp.zeros_like(dg_ref)
            dlb_ref[...] = jnp.zeros_like(dlb_ref)

        g, lb = g_ref[...], lb_ref[...]
        a = proj_ref[:, 0:e].astype(F32)
        b = proj_ref[:, e:2 * e].astype(F32)
        z = proj_ref[:, 2 * e:3 * e].astype(F32)
        sb = _sigmoid(b)
        y0 = a * sb
        dy1, dy2, xh, dz = ln_silu_bwd(du_ref[...].astype(F32), z, y1_ref[...], g, lb)
        dproj_ref[:, 2 * e:3 * e] = dz.astype(BF16)
        dg_ref[...] += jnp.sum(dy2 * xh, axis=0, keepdims=True)
        dlb_ref[...] += jnp.sum(dy2, axis=0, keepdims=True)
        dcb_ref[...] += jnp.sum(dy1, axis=0, keepdims=True)
        dy1h, _, _, _ = ln_silu_bwd(duh_ref[...].astype(F32), zh_ref[...].astype(F32), y1h_ref[...], g, lb)
        dy1h = jnp.where(i < nt - 1, dy1h, 0.0)
        ah = abh_ref[:, 0:e].astype(F32)
        bh = abh_ref[:, e:2 * e].astype(F32)
        y0h = jnp.where(i > 0, ah * _sigmoid(bh), 0.0)
        for c in range(ncb):
            cols = slice(c * cb, (c + 1) * cb)
            y0s[c, 0:HALO, :] = y0h[:, cols]
            y0s[c, HALO:HALO + tm, :] = y0[:, cols]
            dy1s[c, 0:tm, :] = dy1[:, cols]
            dy1s[c, tm:tm + HALO, :] = dy1h[:, cols]

        def per_block(c, carry):
            _conv_taps(dy1s, w_ref, dy0s, c, tm, rb, 0, True)
            for k in range(k_taps):
                acc = None
                for r0 in range(0, tm, rb):
                    term = dy1s[c, pl.ds(r0, rb), :] * y0s[c, pl.ds(r0 + HALO - (k_taps - 1) + k, rb), :]
                    acc = term if acc is None else acc + term
                dw_ref[c, pl.ds(k, 1), :] += jnp.sum(acc, axis=0, keepdims=True)
            return carry

        lax.fori_loop(0, ncb, per_block, 0)
        dy0 = jnp.concatenate([dy0s[c] for c in range(ncb)], axis=1)
        dproj_ref[:, 0:e] = (dy0 * sb).astype(BF16)
        dproj_ref[:, e:2 * e] = (dy0 * a * sb * (1.0 - sb)).astype(BF16)

    tile = lambda w: pl.BlockSpec((tm, w), lambda b, i: (b * nt + i, 0))
    return pl.pallas_call(
        body, name="bwd_conv", grid=(bsz, nt),
        in_specs=[tile(e), tile(e3), tile(e),
                  pl.BlockSpec((HALO, e), nxt), pl.BlockSpec((HALO, e), z_halo), pl.BlockSpec((HALO, e), nxt),
                  pl.BlockSpec((HALO, 2 * e), prev),
                  _layer((ncb, kp, cb), layer), _layer((1, e), layer), _layer((1, e), layer)],
        out_specs=[tile(e3), _const((ncb, kp, cb)), _const((1, e)), _const((1, e)), _const((1, e))],
        out_shape=[SDS((t, e3), BF16), SDS((ncb, kp, cb), F32), SDS((1, e), F32), SDS((1, e), F32), SDS((1, e), F32)],
        scratch_shapes=[pltpu.VMEM((ncb, HALO + tm, cb), F32), pltpu.VMEM((ncb, tm + HALO, cb), F32),
                        pltpu.VMEM((ncb, tm, cb), F32)],
        compiler_params=_params(2),
    )(du, proj, y1, du, proj, y1, proj, conv_w_blk, ln_g, ln_b)


def _bwd_sgu(du, proj, ln_g, ln_b, sgu_w, sgu_bt, layer, tm):
    t, e3 = proj.shape
    e = e3 // 3
    gw = e // GROUPS
    nch = tm // CHUNK

    def body(du_ref, proj_ref, g_ref, lb_ref, w_ref, bt_ref, dproj_ref, dw_ref, dbt_ref, dg_ref, dlb_ref, mixed, dmix, dv):
        @pl.when(pl.program_id(0) == 0)
        def _():
            dw_ref[...] = jnp.zeros_like(dw_ref)
            dbt_ref[...] = jnp.zeros_like(dbt_ref)
            dg_ref[...] = jnp.zeros_like(dg_ref)
            dlb_ref[...] = jnp.zeros_like(dlb_ref)

        g, lb = g_ref[...], lb_ref[...]
        a = proj_ref[:, 0:e].astype(F32)
        b = proj_ref[:, e:2 * e].astype(F32)
        z = proj_ref[:, 2 * e:3 * e].astype(F32)
        ug, dug = _gelu_parts(a)
        vb, dvb_db = _gelu_parts(b)
        rs, xh = _ln_stats(vb)
        v = (xh * g + lb).astype(BF16)
        mask = _tril_mask()
        for gi in range(GROUPS):
            wm = jnp.where(mask, w_ref[gi], 0.0).astype(BF16)
            bias = bt_ref[:, gi:gi + 1]
            for n in range(nch):
                blk = v[n * CHUNK:(n + 1) * CHUNK, gi * gw:(gi + 1) * gw]
                mixed[n * CHUNK:(n + 1) * CHUNK, gi * gw:(gi + 1) * gw] = _dot(wm, blk) + bias
        mx = mixed[...]
        sz = _sigmoid(z)
        duv = du_ref[...].astype(F32)
        dy = duv * (z * sz)
        dproj_ref[:, 2 * e:3 * e] = (duv * (ug * mx) * _dsilu(z, sz)).astype(BF16)
        dproj_ref[:, 0:e] = (dy * mx * dug).astype(BF16)
        dmix[...] = dy * ug
        for gi in range(GROUPS):
            wm = jnp.where(mask, w_ref[gi], 0.0).astype(BF16)
            dw_acc = None
            db_acc = None
            for n in range(nch):
                rows, cols = slice(n * CHUNK, (n + 1) * CHUNK), slice(gi * gw, (gi + 1) * gw)
                dm = dmix[rows, cols]
                dmb = dm.astype(BF16)
                dw_n = _dot_nt(dmb, v[rows, cols])
                db_n = jnp.sum(dm, axis=1, keepdims=True)
                dw_acc = dw_n if dw_acc is None else dw_acc + dw_n
                db_acc = db_n if db_acc is None else db_acc + db_n
                dv[rows, cols] = _dot_tn(wm, dmb)
            dw_ref[gi] += jnp.where(mask, dw_acc, 0.0)
            dbt_ref[:, gi:gi + 1] += db_acc
        dvv = dv[...]
        dg_ref[...] += jnp.sum(dvv * xh, axis=0, keepdims=True)
        dlb_ref[...] += jnp.sum(dvv, axis=0, keepdims=True)
        dproj_ref[:, e:2 * e] = (_ln_bwd(dvv, g, rs, xh) * dvb_db).astype(BF16)

    return pl.pallas_call(
        body, name="bwd_sgu", grid=(t // tm,),
        in_specs=[pl.BlockSpec((tm, e), lambda i: (i, 0)), pl.BlockSpec((tm, e3), lambda i: (i, 0)),
                  _layer((1, e), layer), _layer((1, e), layer),
                  _layer((GROUPS, CHUNK, CHUNK), layer), _layer((CHUNK, GROUPS), layer)],
        out_specs=[pl.BlockSpec((tm, e3), lambda i: (i, 0)), _const((GROUPS, CHUNK, CHUNK)), _const((CHUNK, GROUPS)),
                   _const((1, e)), _const((1, e))],
        out_shape=[SDS((t, e3), BF16), SDS((GROUPS, CHUNK, CHUNK), F32), SDS((CHUNK, GROUPS), F32),
                   SDS((1, e), F32), SDS((1, e), F32)],
        scratch_shapes=[pltpu.VMEM((tm, e), F32), pltpu.VMEM((tm, e), F32), pltpu.VMEM((tm, e), F32)],
        compiler_params=_params(1),
    )(du, proj, ln_g, ln_b, sgu_w, sgu_bt)


def _bwd_in(dproj, dx1, x, norm_g, w_in_full, layer, tm):
    t, d = x.shape
    _, nk, _, n4 = w_in_full.shape

    def body(dproj_ref, dx1_ref, x_ref, g_ref, w_ref, dx_ref, dg_ref):
        @pl.when(pl.program_id(0) == 0)
        def _():
            dg_ref[...] = jnp.zeros_like(dg_ref)

        dh = None
        for k in range(nk):
            part = _dot_nt(dproj_ref[:, k * n4:(k + 1) * n4], w_ref[k])
            dh = part if dh is None else dh + part
        r, xh = _rms_stats(x_ref[...])
        dg_ref[...] += jnp.sum(dh * xh, axis=0, keepdims=True)
        dx_ref[...] = dx1_ref[...] + _rms_bwd(dh, g_ref[...], r, xh)

    row = lambda w: pl.BlockSpec((tm, w), lambda i: (i, 0))
    return pl.pallas_call(
        body, name="bwd_in", grid=(t // tm,),
        in_specs=[row(nk * n4), row(d), row(d), _layer((1, d), layer), _layer((nk, d, n4), layer)],
        out_specs=[row(d), _const((1, d))],
        out_shape=[SDS((t, d), F32), SDS((1, d), F32)],
        compiler_params=_params(1),
    )(dproj, dx1, x, norm_g, w_in_full)


def _wgrad(a, b, kblk, nblk, tm, name, a_layer=None):
    t, n = b.shape
    k = a.shape[-1]
    kw, nw = k // kblk, n // nblk
    n_steps = t // tm

    def body(a_ref, b_ref, o_ref):
        @pl.when(pl.program_id(2) == 0)
        def _():
            o_ref[...] = jnp.zeros_like(o_ref)

        o_ref[...] += _dot_tn(a_ref[...].astype(BF16), b_ref[...].astype(BF16))

    if a_layer is None:
        a_spec = pl.BlockSpec((tm, kw), lambda kb, nb, i: (i, kb))
    else:
        a_spec = pl.BlockSpec((None, tm, kw), lambda kb, nb, i: (a_layer, i, kb))
    return pl.pallas_call(
        body, name=name, grid=(kblk, nblk, n_steps),
        in_specs=[a_spec, pl.BlockSpec((tm, nw), lambda kb, nb, i: (i, nb))],
        out_specs=pl.BlockSpec((None, None, kw, nw), lambda kb, nb, i: (kb, nb, 0, 0)),
        out_shape=SDS((kblk, nblk, kw, nw), F32),
        compiler_params=_params(3),
    )(a, b)


def _row_tile(rows, cols, budget_bytes=1 << 20):
    best = None
    for cand in range(SUBLANES, rows + 1, SUBLANES):
        if rows % cand == 0 and cand * cols * 4 <= budget_bytes:
            best = cand
    return best if best is not None else rows


def _pair_sum(grads, recv, my_c):
    n, _, h, c = grads.shape
    th = _row_tile(h, c)

    def body(c_ref, g_ref, r_ref, o_ref):
        o_ref[...] = g_ref[...] + r_ref[...]

    grid_spec = pltpu.PrefetchScalarGridSpec(
        num_scalar_prefetch=1, grid=(n, h // th),
        in_specs=[pl.BlockSpec((None, None, th, c), lambda j, i, c_ref: (j, c_ref[0], i, 0)),
                  pl.BlockSpec((None, None, th, c), lambda j, i, c_ref: (j, 0, i, 0))],
        out_specs=pl.BlockSpec((None, th, c), lambda j, i, c_ref: (j, i, 0)))
    return pl.pallas_call(body, name="pair_sum", grid_spec=grid_spec, out_shape=SDS((n, h, c), F32),
                          compiler_params=_params(2))(my_c, grads, recv)


def _chip_sum(parts):
    nl, nk, h, c = parts.shape
    th = _row_tile(h, c)

    def body(p_ref, o_ref):
        acc = p_ref[0]
        for k in range(1, nk):
            acc = acc + p_ref[k]
        o_ref[...] = acc

    return pl.pallas_call(
        body, name="chip_sum", grid=(nl, h // th),
        in_specs=[pl.BlockSpec((None, nk, th, c), lambda j, i: (j, 0, i, 0))],
        out_specs=pl.BlockSpec((None, th, c), lambda j, i: (j, i, 0)),
        out_shape=SDS((nl, h, c), F32), compiler_params=_params(2))(parts)


def _adamw(w, g, m, v):
    rows, cols = w.shape
    tr = _row_tile(rows, cols, 512 << 10)
    c1 = 1.0 - ADAM_B1 ** ADAM_STEP
    c2 = 1.0 - ADAM_B2 ** ADAM_STEP

    def body(w_ref, g_ref, m_ref, v_ref, go_ref, d_ref, mo_ref, vo_ref):
        gv = g_ref[...]
        mn = ADAM_B1 * m_ref[...] + (1.0 - ADAM_B1) * gv
        vn = ADAM_B2 * v_ref[...] + (1.0 - ADAM_B2) * (gv * gv)
        m_hat = mn / c1
        v_hat = vn / c2
        go_ref[...] = gv
        d_ref[...] = -ADAM_LR * (m_hat / (jnp.sqrt(v_hat) + ADAM_EPS) + ADAM_WD * w_ref[...])
        mo_ref[...] = mn
        vo_ref[...] = vn

    spec = pl.BlockSpec((tr, cols), lambda i: (i, 0))
    return pl.pallas_call(
        body, name="adamw", grid=(rows // tr,), in_specs=[spec] * 4, out_specs=[spec] * 4,
        out_shape=[SDS((rows, cols), F32)] * 4, compiler_params=_params(1))(w, g, m, v)


def _place():
    x, y, c = lax.axis_index("x"), lax.axis_index("y"), lax.axis_index("c")
    chips = [(1 - x, y), (x, 1 - y), (1 - x, 1 - y)]
    return x, y, c, 2 * x + y, chips


def _remote(src, dst, send_sem, recv_sem, device):
    return pltpu.make_async_remote_copy(src_ref=src, dst_ref=dst, send_sem=send_sem, recv_sem=recv_sem,
                                        device_id=device, device_id_type=MESH_IDS)


def _gather_weights(shards, small):
    n = len(shards)

    def body(*refs):
        ins, small_in = refs[:n], refs[n]
        outs, small_out = refs[n + 1:2 * n + 1], refs[2 * n + 1]
        ici_send, ici_recv, d2d_send, d2d_recv, local_sem = refs[2 * n + 2:]
        x, y, c, k, chips = _place()
        local, sends = [], []
        for j in range(n):
            h = ins[j].shape[2] // 2
            mine = pl.ds(c * h, h)
            own = pltpu.make_async_copy(ins[j], outs[j].at[:, pl.ds(k, 1)], local_sem.at[j])
            own.start()
            local.append(own)
            for ti, (cx, cy) in enumerate(chips):
                cp = _remote(ins[j].at[:, :, mine], outs[j].at[:, pl.ds(k, 1), mine],
                             ici_send.at[3 * j + ti], ici_recv.at[3 * j + ti], (cx, cy, c))
                cp.start()
                sends.append(cp)
        own = pltpu.make_async_copy(small_in, small_out.at[pl.ds(k, 1)], local_sem.at[n])
        own.start()
        local.append(own)
        for ti, (cx, cy) in enumerate(chips):
            cp = _remote(small_in, small_out.at[pl.ds(k, 1)], ici_send.at[3 * n + ti], ici_recv.at[3 * n + ti], (cx, cy, c))
            cp.start()
            sends.append(cp)
        for j in range(n):
            h = ins[j].shape[2] // 2
            mine = pl.ds(c * h, h)
            for ti, (cx, cy) in enumerate(chips):
                landed = outs[j].at[:, pl.ds(2 * cx + cy, 1), mine]
                _remote(ins[j].at[:, :, mine], landed, ici_send.at[3 * j + ti], ici_recv.at[3 * j + ti], (cx, cy, c)).wait_recv()
                fwd = _remote(landed, landed, d2d_send.at[3 * j + ti], d2d_recv.at[3 * j + ti], (x, y, 1 - c))
                fwd.start()
                sends.append(fwd)
        for ti, (cx, cy) in enumerate(chips):
            _remote(small_in, small_out.at[pl.ds(2 * cx + cy, 1)], ici_send.at[3 * n + ti], ici_recv.at[3 * n + ti],
                    (cx, cy, c)).wait_recv()
        for j in range(n):
            h = ins[j].shape[2] // 2
            theirs = pl.ds((1 - c) * h, h)
            for ti, (cx, cy) in enumerate(chips):
                landed = outs[j].at[:, pl.ds(2 * cx + cy, 1), theirs]
                _remote(landed, landed, d2d_send.at[3 * j + ti], d2d_recv.at[3 * j + ti], (x, y, 1 - c)).wait_recv()
        for cp in sends:
            cp.wait_send()
        for cp in local:
            cp.wait()

    out_shape = [SDS((s.shape[0], N_CHIPS) + s.shape[2:], s.dtype) for s in shards]
    out_shape.append(SDS((N_CHIPS,) + small.shape[1:], small.dtype))
    return pl.pallas_call(
        body, name="gather_weights", in_specs=[ANY] * (n + 1), out_specs=[ANY] * (n + 1), out_shape=out_shape,
        scratch_shapes=[pltpu.SemaphoreType.DMA((3 * n + 3,)), pltpu.SemaphoreType.DMA((3 * n + 3,)),
                        pltpu.SemaphoreType.DMA((3 * n,)), pltpu.SemaphoreType.DMA((3 * n,)),
                        pltpu.SemaphoreType.DMA((n + 1,))],
    )(*shards, small)


def _swap_halves(grads):
    n = len(grads)

    def body(*refs):
        ins, outs = refs[:n], refs[n:2 * n]
        send_sem, recv_sem = refs[2 * n:]
        x, y, c, _, _ = _place()
        copies = [_remote(ins[j].at[:, pl.ds(1 - c, 1)], outs[j], send_sem.at[j], recv_sem.at[j], (x, y, 1 - c))
                  for j in range(n)]
        for cp in copies:
            cp.start()
        for cp in copies:
            cp.wait()

    return pl.pallas_call(
        body, name="swap_halves", in_specs=[ANY] * n, out_specs=[ANY] * n,
        out_shape=[SDS((g.shape[0], 1) + g.shape[2:], g.dtype) for g in grads],
        scratch_shapes=[pltpu.SemaphoreType.DMA((n,)), pltpu.SemaphoreType.DMA((n,))],
    )(*grads)


def _scatter_to_chips(sums):
    n = len(sums)

    def body(*refs):
        ins, outs = refs[:n], refs[n:2 * n]
        send_sem, recv_sem, local_sem = refs[2 * n:]
        x, y, c, k, chips = _place()
        copies, local = [], []
        for j in range(n):
            own = pltpu.make_async_copy(ins[j].at[:, pl.ds(k, 1)], outs[j].at[:, pl.ds(k, 1)], local_sem.at[j])
            own.start()
            local.append(own)
            for ti, (cx, cy) in enumerate(chips):
                cp = _remote(ins[j].at[:, pl.ds(2 * cx + cy, 1)], outs[j].at[:, pl.ds(k, 1)],
                             send_sem.at[3 * j + ti], recv_sem.at[3 * j + ti], (cx, cy, c))
                cp.start()
                copies.append(cp)
        for j in range(n):
            for ti, (cx, cy) in enumerate(chips):
                slot = outs[j].at[:, pl.ds(2 * cx + cy, 1)]
                _remote(slot, slot, send_sem.at[3 * j + ti], recv_sem.at[3 * j + ti], (cx, cy, c)).wait_recv()
        for cp in copies:
            cp.wait_send()
        for cp in local:
            cp.wait()

    return pl.pallas_call(
        body, name="scatter_to_chips", in_specs=[ANY] * n, out_specs=[ANY] * n,
        out_shape=[SDS(s.shape, s.dtype) for s in sums],
        scratch_shapes=[pltpu.SemaphoreType.DMA((3 * n,)), pltpu.SemaphoreType.DMA((3 * n,)), pltpu.SemaphoreType.DMA((n,))],
    )(*sums)


def _join_halves(pieces):
    n = len(pieces)

    def body(*refs):
        ins, outs = refs[:n], refs[n:2 * n]
        send_sem, recv_sem, local_sem = refs[2 * n:]
        x, y, c, _, _ = _place()
        copies, local = [], []
        for j in range(n):
            own = pltpu.make_async_copy(ins[j], outs[j].at[:, pl.ds(c, 1)], local_sem.at[j])
            own.start()
            local.append(own)
            cp = _remote(ins[j], outs[j].at[:, pl.ds(c, 1)], send_sem.at[j], recv_sem.at[j], (x, y, 1 - c))
            cp.start()
            copies.append(cp)
        for j in range(n):
            slot = outs[j].at[:, pl.ds(1 - c, 1)]
            _remote(ins[j], slot, send_sem.at[j], recv_sem.at[j], (x, y, 1 - c)).wait_recv()
        for cp in copies:
            cp.wait_send()
        for cp in local:
            cp.wait()

    return pl.pallas_call(
        body, name="join_halves", in_specs=[ANY] * n, out_specs=[ANY] * n,
        out_shape=[SDS((p.shape[0], 2) + p.shape[2:], p.dtype) for p in pieces],
        scratch_shapes=[pltpu.SemaphoreType.DMA((n,)), pltpu.SemaphoreType.DMA((n,)), pltpu.SemaphoreType.DMA((n,))],
    )(*pieces)


def _gather_chips(block):
    def body(in_ref, out_ref, send_sem, recv_sem, local_sem):
        x, y, c, k, chips = _place()
        own = pltpu.make_async_copy(in_ref, out_ref.at[pl.ds(k, 1)], local_sem)
        own.start()
        copies = []
        for ti, (cx, cy) in enumerate(chips):
            cp = _remote(in_ref, out_ref.at[pl.ds(k, 1)], send_sem.at[ti], recv_sem.at[ti], (cx, cy, c))
            cp.start()
            copies.append(cp)
        for ti, (cx, cy) in enumerate(chips):
            _remote(in_ref, out_ref.at[pl.ds(2 * cx + cy, 1)], send_sem.at[ti], recv_sem.at[ti], (cx, cy, c)).wait_recv()
        for cp in copies:
            cp.wait_send()
        own.wait()

    return pl.pallas_call(
        body, name="gather_chips", in_specs=[ANY], out_specs=ANY,
        out_shape=SDS((N_CHIPS,) + block.shape[1:], block.dtype),
        scratch_shapes=[pltpu.SemaphoreType.DMA((3,)), pltpu.SemaphoreType.DMA((3,)), pltpu.SemaphoreType.DMA],
    )(block)


def _pack_rows(parts, width, total_rows=None):
    rows = []
    for a in parts:
        a2 = a.reshape(-1, width)
        pad = (-a2.shape[0]) % SUBLANES
        rows.append(jnp.pad(a2, ((0, pad), (0, 0))) if pad else a2)
    out = jnp.concatenate(rows, axis=0)
    if total_rows is not None and out.shape[0] < total_rows:
        out = jnp.pad(out, ((0, total_rows - out.shape[0]), (0, 0)))
    return out


def _unpack_rows(packed, shapes, width):
    out, r = [], 0
    for shp in shapes:
        size = 1
        for s in shp:
            size *= s
        nr = size // width
        out.append(packed[r:r + nr].reshape(shp))
        r += nr + ((-nr) % SUBLANES)
    return out


def kernel(x, p, norm_g, w_in, w_out, conv_w, conv_b, conv_ln_g, conv_ln_b, sgu_ln_g, sgu_ln_b, sgu_w, sgu_b, pl_norm_g, pl_gate_w, pl_proj_w, final_g, loss_target, m_norm_g, m_w_in, m_w_out, m_conv_w, m_conv_b, m_conv_ln_g, m_conv_ln_b, m_sgu_ln_g, m_sgu_ln_b, m_sgu_w, m_sgu_b, m_pl_norm_g, m_pl_gate_w, m_pl_proj_w, m_final_g, v_norm_g, v_w_in, v_w_out, v_conv_w, v_conv_b, v_conv_ln_g, v_conv_ln_b, v_sgu_ln_g, v_sgu_ln_b, v_sgu_w, v_sgu_b, v_pl_norm_g, v_pl_gate_w, v_pl_proj_w, v_final_g):
    bsz, seq, d = x.shape
    depth = w_in.shape[0]
    e = w_out.shape[1] * N_CHIPS
    e3 = 3 * e
    n4 = w_in.shape[2]
    ple = p.shape[-1]
    dq = pl_proj_w.shape[2]
    k_taps = conv_w.shape[1]
    kp = k_taps + 1
    n_conv, n_sgu = conv_w.shape[0], sgu_ln_g.shape[0]
    t = bsz * seq
    tm_mm = min(512, seq)
    tm_mix = min(256, seq)
    cb, ncb = _col_blocks(e)
    my_c = lax.axis_index("c")
    my_k = 2 * lax.axis_index("x") + lax.axis_index("y")

    ec = e // N_CHIPS
    small_w = _pack_rows([conv_w.reshape(n_conv * k_taps, ec), sgu_ln_g, sgu_ln_b], ec)[None]
    big = [w_in.astype(BF16)[:, None], w_out.astype(BF16)[:, None], pl_gate_w.astype(BF16)[:, None],
           pl_proj_w.astype(BF16)[:, None]]
    w_in_f, w_out_f, gate_f, proj_f, small_f = _gather_weights(big, small_w)
    w_out_f = w_out_f.reshape(depth, e, d)
    gate_f = gate_f.reshape(depth, d, d)
    conv_w_rows, sgu_g_rows, sgu_b_rows = _unpack_rows(
        jnp.transpose(small_f, (1, 0, 2)).reshape(small_f.shape[1], e),
        [(n_conv * k_taps, e), (n_sgu, e), (n_sgu, e)], e)
    conv_w_full = conv_w_rows.reshape(n_conv, k_taps, e)
    conv_w_blk = jnp.transpose(jnp.pad(conv_w_full, ((0, 0), (0, 1), (0, 0))).reshape(n_conv, kp, ncb, cb), (0, 2, 1, 3))
    sgu_ln_g_full = sgu_g_rows.reshape(n_sgu, 1, e)
    sgu_ln_b_full = sgu_b_rows.reshape(n_sgu, 1, e)
    sgu_bt = jnp.transpose(sgu_b, (0, 2, 1))

    norm_g3 = norm_g[:, None]
    pl_norm_g3 = pl_norm_g[:, None]
    conv_b3, conv_ln_g3, conv_ln_b3 = conv_b[:, None], conv_ln_g[:, None], conv_ln_b[:, None]
    p3 = p.reshape(depth, t, ple)

    xs, hs, projs, us, x1s, y1s = [], [], [], [], [], {}
    xc = x.reshape(t, d)
    for l in range(depth):
        j = l // 2
        xs.append(xc)
        h, proj = _fwd_in(xc, norm_g3, w_in_f, l, tm_mm)
        if l % 2 == 0:
            u, y1 = _fwd_conv(proj, conv_w_blk, conv_b3, conv_ln_g3, conv_ln_b3, j, bsz, seq, tm_mix)
            y1s[l] = y1
        else:
            u = _fwd_sgu(proj, sgu_ln_g_full, sgu_ln_b_full, sgu_w, sgu_bt, j, tm_mix)
        x1, xc = _fwd_out(xc, u, w_out_f, pl_norm_g3, gate_f, p3, proj_f, l, tm_mm)
        hs.append(h)
        projs.append(proj)
        us.append(u)
        x1s.append(x1)

    loss_local, dx, d_final_g = _loss_head(xc, final_g[None], loss_target.reshape(t, d), tm_mm)
    loss = lax.psum(loss_local[0, 0], ("x", "y", "c"))

    g_in, g_out, g_gate, g_proj = [None] * depth, [None] * depth, [None] * depth, [None] * depth
    d_norm_g, d_pl_norm_g = [None] * depth, [None] * depth
    d_conv = [None] * n_conv
    d_sgu = [None] * n_sgu
    for l in reversed(range(depth)):
        j = l // 2
        dx1, du, rn, ds, dqv, d_pl_norm_g[l] = _bwd_out(dx, x1s[l], p3, pl_norm_g3, gate_f, proj_f, w_out_f, l, tm_mm)
        g_proj[l] = _wgrad(p3, dqv, 1, N_CHIPS, tm_mm, "wgrad_proj", a_layer=l)
        g_gate[l] = _wgrad(rn, ds, 1, 1, tm_mm, "wgrad_gate")
        g_out[l] = _wgrad(us[l], dx1, N_CHIPS, 1, tm_mm, "wgrad_out")
        if l % 2 == 0:
            dproj, dcw, dcb, dlg, dlb = _bwd_conv(du, projs[l], y1s[l], conv_w_blk, conv_ln_g3, conv_ln_b3, j, bsz, seq, tm_mix)
            d_conv[j] = (dcw, dcb, dlg, dlb)
        else:
            dproj, dsw, dsbt, dlg, dlb = _bwd_sgu(du, projs[l], sgu_ln_g_full, sgu_ln_b_full, sgu_w, sgu_bt, j, tm_mix)
            d_sgu[j] = (dsw, dsbt, dlg, dlb)
        g_in[l] = _wgrad(hs[l], dproj, 1, N_CHIPS, tm_mm, "wgrad_in")
        dx, d_norm_g[l] = _bwd_in(dproj, dx1, xs[l], norm_g3, w_in_f, l, tm_mm)
    grad_x = dx.reshape(bsz, seq, d)

    gl_in = jnp.stack(g_in).reshape(depth * N_CHIPS, 2, d // 2, n4)
    gl_out = jnp.stack(g_out).reshape(depth * N_CHIPS, 2, e // (2 * N_CHIPS), d)
    gl_gate = jnp.stack(g_gate).reshape(depth * N_CHIPS, 2, d // (2 * N_CHIPS), d)
    gl_proj = jnp.stack(g_proj).reshape(depth * N_CHIPS, 2, ple // 2, dq)

    d_conv_w = jnp.stack([jnp.transpose(dc[0], (1, 0, 2)).reshape(kp, e)[:k_taps] for dc in d_conv])
    d_conv_b = jnp.stack([dc[1][0] for dc in d_conv])
    d_conv_ln_g = jnp.stack([dc[2][0] for dc in d_conv])
    d_conv_ln_b = jnp.stack([dc[3][0] for dc in d_conv])
    d_sgu_w = jnp.stack([dsg[0] for dsg in d_sgu])
    d_sgu_b = jnp.stack([jnp.transpose(dsg[1]) for dsg in d_sgu])
    d_sgu_ln_g = jnp.stack([dsg[2][0] for dsg in d_sgu])
    d_sgu_ln_b = jnp.stack([dsg[3][0] for dsg in d_sgu])
    small_grads = [jnp.concatenate(d_norm_g), d_conv_w, d_conv_b, d_conv_ln_g, d_conv_ln_b, d_sgu_ln_g, d_sgu_ln_b,
                   d_sgu_w, d_sgu_b, jnp.concatenate(d_pl_norm_g), d_final_g]
    small_shapes = [a.shape for a in small_grads]
    packed = _pack_rows(small_grads, d)
    pack_rows = packed.shape[0] + ((-packed.shape[0]) % (8 * SUBLANES))
    packed = _pack_rows(small_grads, d, pack_rows)
    gl_small = packed.reshape(N_CHIPS, 2, pack_rows // 8, d)

    locals_ = [gl_in, gl_out, gl_gate, gl_proj, gl_small]
    recv = _swap_halves(locals_)
    c_arr = my_c.astype(jnp.int32).reshape(1)
    pair = [_pair_sum(gl, rc, c_arr) for gl, rc in zip(locals_, recv)]
    pair = [pr.reshape((pr.shape[0] // N_CHIPS, N_CHIPS) + pr.shape[1:]) for pr in pair]
    from_chips = _scatter_to_chips(pair)
    reduced = [_chip_sum(fc)[:, None] for fc in from_chips]
    joined = _join_halves(reduced)
    r_in, r_out, r_gate, r_proj, r_small = joined
    small_all = _gather_chips(r_small.reshape(1, pack_rows // N_CHIPS, d)).reshape(pack_rows, d)
    small_red = _unpack_rows(small_all, small_shapes, d)
    (gr_norm_g, gr_conv_w, gr_conv_b, gr_conv_ln_g, gr_conv_ln_b, gr_sgu_ln_g, gr_sgu_ln_b, gr_sgu_w, gr_sgu_b,
     gr_pl_norm_g, gr_final_g) = small_red
    gr_final_g = gr_final_g.reshape(d)
    gr_conv_w = lax.dynamic_slice_in_dim(gr_conv_w, my_k * ec, ec, axis=2)
    gr_sgu_ln_g = lax.dynamic_slice_in_dim(gr_sgu_ln_g, my_k * ec, ec, axis=1)
    gr_sgu_ln_b = lax.dynamic_slice_in_dim(gr_sgu_ln_b, my_k * ec, ec, axis=1)

    def big_update(w, g, m, v):
        rows = w.size // w.shape[-1]
        flat = lambda a: a.reshape(rows, w.shape[-1])
        return [o.reshape(w.shape) for o in _adamw(flat(w), flat(g), flat(m), flat(v))]

    up_in = big_update(w_in, r_in, m_w_in, v_w_in)
    up_out = big_update(w_out, r_out, m_w_out, v_w_out)
    up_gate = big_update(pl_gate_w, r_gate, m_pl_gate_w, v_pl_gate_w)
    up_proj = big_update(pl_proj_w, r_proj, m_pl_proj_w, v_pl_proj_w)

    small_names = ["norm_g", "conv_w", "conv_b", "conv_ln_g", "conv_ln_b", "sgu_ln_g", "sgu_ln_b", "sgu_w", "sgu_b",
                   "pl_norm_g", "final_g"]
    small_w_list = [norm_g, conv_w, conv_b, conv_ln_g, conv_ln_b, sgu_ln_g, sgu_ln_b, sgu_w, sgu_b, pl_norm_g, final_g]
    small_m_list = [m_norm_g, m_conv_w, m_conv_b, m_conv_ln_g, m_conv_ln_b, m_sgu_ln_g, m_sgu_ln_b, m_sgu_w, m_sgu_b,
                    m_pl_norm_g, m_final_g]
    small_v_list = [v_norm_g, v_conv_w, v_conv_b, v_conv_ln_g, v_conv_ln_b, v_sgu_ln_g, v_sgu_ln_b, v_sgu_w, v_sgu_b,
                    v_pl_norm_g, v_final_g]
    small_g_list = [gr_norm_g, gr_conv_w, gr_conv_b, gr_conv_ln_g, gr_conv_ln_b, gr_sgu_ln_g, gr_sgu_ln_b, gr_sgu_w,
                    gr_sgu_b, gr_pl_norm_g, gr_final_g]
    width = ec
    shapes_local = [a.shape for a in small_w_list]
    outs_small = _adamw(_pack_rows(small_w_list, width), _pack_rows(small_g_list, width),
                        _pack_rows(small_m_list, width), _pack_rows(small_v_list, width))
    unpacked = [_unpack_rows(o, shapes_local, width) for o in outs_small]
    ups = {name: [unpacked[kind][i] for kind in range(4)] for i, name in enumerate(small_names)}
    ups["w_in"], ups["w_out"], ups["pl_gate_w"], ups["pl_proj_w"] = up_in, up_out, up_gate, up_proj

    order = ["norm_g", "w_in", "w_out", "conv_w", "conv_b", "conv_ln_g", "conv_ln_b", "sgu_ln_g", "sgu_ln_b", "sgu_w",
             "sgu_b", "pl_norm_g", "pl_gate_w", "pl_proj_w", "final_g"]
    result = [loss, grad_x]
    for kind in range(4):
        result.extend(ups[name][kind] for name in order)
    return tuple(result)
```

```python
import functools

import jax
import jax.numpy as jnp
from jax import lax
from jax.experimental import pallas as pl
from jax.experimental.pallas import tpu as pltpu

F32 = jnp.float32
BF16 = jnp.bfloat16
SDS = jax.ShapeDtypeStruct

EPS = 1e-6
CHUNK = 128
GROUPS = 8
HALO = 32
N_CHIPS = 4
LANES = 128
SUBLANES = 8
V7X_VMEM_LIMIT = 56 << 20

ADAM_LR = 0.001
ADAM_B1 = 0.9
ADAM_B2 = 0.999
ADAM_EPS = 1e-08
ADAM_WD = 0.01
ADAM_STEP = 10

MESH_IDS = pl.DeviceIdType.MESH
ANY = pl.BlockSpec(memory_space=pl.ANY)


def _params(n_axes):
    return pltpu.CompilerParams(dimension_semantics=("arbitrary",) * n_axes, vmem_limit_bytes=V7X_VMEM_LIMIT)


def _const(shape):
    zeros = (0,) * len(shape)
    return pl.BlockSpec(shape, lambda *_: zeros)


def _layer(shape, layer):
    zeros = (0,) * len(shape)
    return pl.BlockSpec((None,) + tuple(shape), lambda *_: (layer,) + zeros)


def _sigmoid(v):
    return jax.nn.sigmoid(v)


def _dsilu(v, s):
    return s * (1.0 + v * (1.0 - s))


def _gelu_parts(v):
    cdf = 0.5 * (1.0 + lax.erf(v * 0.7071067811865476))
    pdf = jnp.exp(-0.5 * v * v) * 0.3989422804014327
    return v * cdf, cdf + v * pdf


def _gelu(v):
    return 0.5 * v * (1.0 + lax.erf(v * 0.7071067811865476))


def _rms_stats(x):
    r = lax.rsqrt(jnp.mean(x * x, axis=-1, keepdims=True) + EPS)
    return r, x * r


def _rms_bwd(dy, g, r, xh):
    gdy = dy * g
    return r * (gdy - xh * jnp.mean(xh * gdy, axis=-1, keepdims=True))


def _ln_stats(x):
    mu = jnp.mean(x, axis=-1, keepdims=True)
    xc = x - mu
    rs = lax.rsqrt(jnp.mean(xc * xc, axis=-1, keepdims=True) + EPS)
    return rs, xc * rs


def _ln_bwd(dy, g, rs, xh):
    dxh = dy * g
    return rs * (dxh - jnp.mean(dxh, axis=-1, keepdims=True) - xh * jnp.mean(dxh * xh, axis=-1, keepdims=True))


def _dot(a, b):
    return jnp.dot(a, b, preferred_element_type=F32)


def _dot_nt(a, b):
    return lax.dot_general(a, b, (((1,), (1,)), ((), ())), preferred_element_type=F32)


def _dot_tn(a, b):
    return lax.dot_general(a, b, (((0,), (0,)), ((), ())), preferred_element_type=F32)


def _fwd_in(x, norm_g, w_in_full, layer, tm):
    t, d = x.shape
    _, nk, _, n4 = w_in_full.shape

    def body(x_ref, g_ref, w_ref, h_ref, proj_ref):
        r, xh = _rms_stats(x_ref[...])
        h = (xh * g_ref[...]).astype(BF16)
        h_ref[...] = h
        for k in range(nk):
            proj_ref[:, k * n4:(k + 1) * n4] = _dot(h, w_ref[k]).astype(BF16)

    return pl.pallas_call(
        body, name="fwd_in", grid=(t // tm,),
        in_specs=[pl.BlockSpec((tm, d), lambda i: (i, 0)), _layer((1, d), layer), _layer((nk, d, n4), layer)],
        out_specs=[pl.BlockSpec((tm, d), lambda i: (i, 0)), pl.BlockSpec((tm, nk * n4), lambda i: (i, 0))],
        out_shape=[SDS((t, d), BF16), SDS((t, nk * n4), BF16)],
        compiler_params=_params(1),
    )(x, norm_g, w_in_full)


def _halo_maps(nt, hb, n_halo_blocks):
    def prev(b, i):
        return (jnp.maximum((b * nt + i) * hb - 1, 0), 0)

    def nxt(b, i):
        return (jnp.minimum((b * nt + i + 1) * hb, n_halo_blocks - 1), 0)

    return prev, nxt


def _col_blocks(e):
    cb = min(2 * LANES, e)
    return cb, e // cb


def _conv_taps(src_ref, w_ref, dst_ref, cb_idx, n_rows, rb, first, reverse):
    k_taps = w_ref.shape[1] - 1
    for r0 in range(0, n_rows, rb):
        acc = None
        for res in range(SUBLANES):
            rows = rb + (SUBLANES if res else 0)
            group = None
            for k in range(k_taps):
                off = first + k
                if off % SUBLANES != res:
                    continue
                wk = w_ref[cb_idx, pl.ds((k_taps - 1 - k) if reverse else k, 1), :]
                term = wk * src_ref[cb_idx, pl.ds(r0 + off - res, rows), :]
                group = term if group is None else group + term
            if group is None:
                continue
            part = group[res:res + rb] if res else group
            acc = part if acc is None else acc + part
        dst_ref[cb_idx, pl.ds(r0, rb), :] = acc


def _fwd_conv(proj, conv_w_blk, conv_b, ln_g, ln_b, layer, bsz, seq, tm):
    t, e3 = proj.shape
    e = e3 // 3
    nt = seq // tm
    hb = tm // HALO
    cb, ncb = _col_blocks(e)
    rb = min(64, tm)
    kp = conv_w_blk.shape[2]
    prev, _ = _halo_maps(nt, hb, t // HALO)

    def body(proj_ref, halo_ref, w_ref, b_ref, g_ref, lb_ref, u_ref, y1_ref, y0s, y1s):
        i = pl.program_id(1)
        a = proj_ref[:, 0:e].astype(F32)
        b = proj_ref[:, e:2 * e].astype(F32)
        y0 = a * _sigmoid(b)
        ah = halo_ref[:, 0:e].astype(F32)
        bh = halo_ref[:, e:2 * e].astype(F32)
        y0h = jnp.where(i > 0, ah * _sigmoid(bh), 0.0)
        for c in range(ncb):
            y0s[c, 0:HALO, :] = y0h[:, c * cb:(c + 1) * cb]
            y0s[c, HALO:HALO + tm, :] = y0[:, c * cb:(c + 1) * cb]

        def per_block(c, carry):
            _conv_taps(y0s, w_ref, y1s, c, tm, rb, HALO - (kp - 2), False)
            return carry

        lax.fori_loop(0, ncb, per_block, 0)
        y1 = jnp.concatenate([y1s[c] for c in range(ncb)], axis=1) + b_ref[...]
        y1_ref[...] = y1
        rs, xh = _ln_stats(y1)
        y2 = xh * g_ref[...] + lb_ref[...]
        y = y2 * _sigmoid(y2)
        z = proj_ref[:, 2 * e:3 * e].astype(F32)
        u_ref[...] = (y * (z * _sigmoid(z))).astype(BF16)

    return pl.pallas_call(
        body, name="fwd_conv", grid=(bsz, nt),
        in_specs=[pl.BlockSpec((tm, e3), lambda b, i: (b * nt + i, 0)),
                  pl.BlockSpec((HALO, 2 * e), prev),
                  _layer((ncb, kp, cb), layer), _layer((1, e), layer), _layer((1, e), layer), _layer((1, e), layer)],
        out_specs=[pl.BlockSpec((tm, e), lambda b, i: (b * nt + i, 0)), pl.BlockSpec((tm, e), lambda b, i: (b * nt + i, 0))],
        out_shape=[SDS((t, e), BF16), SDS((t, e), F32)],
        scratch_shapes=[pltpu.VMEM((ncb, HALO + tm, cb), F32), pltpu.VMEM((ncb, tm, cb), F32)],
        compiler_params=_params(2),
    )(proj, proj, conv_w_blk, conv_b, ln_g, ln_b)


def _tril_mask():
    rows = lax.broadcasted_iota(jnp.int32, (CHUNK, CHUNK), 0)
    cols = lax.broadcasted_iota(jnp.int32, (CHUNK, CHUNK), 1)
    return rows >= cols


def _fwd_sgu(proj, ln_g, ln_b, sgu_w, sgu_bt, layer, tm):
    t, e3 = proj.shape
    e = e3 // 3
    gw = e // GROUPS
    nch = tm // CHUNK

    def body(proj_ref, g_ref, lb_ref, w_ref, bt_ref, u_ref, mixed):
        a = proj_ref[:, 0:e].astype(F32)
        b = proj_ref[:, e:2 * e].astype(F32)
        z = proj_ref[:, 2 * e:3 * e].astype(F32)
        rs, xh = _ln_stats(_gelu(b))
        v = (xh * g_ref[...] + lb_ref[...]).astype(BF16)
        mask = _tril_mask()
        for g in range(GROUPS):
            wm = jnp.where(mask, w_ref[g], 0.0).astype(BF16)
            bias = bt_ref[:, g:g + 1]
            for n in range(nch):
                blk = v[n * CHUNK:(n + 1) * CHUNK, g * gw:(g + 1) * gw]
                mixed[n * CHUNK:(n + 1) * CHUNK, g * gw:(g + 1) * gw] = _dot(wm, blk) + bias
        y = _gelu(a) * mixed[...]
        u_ref[...] = (y * (z * _sigmoid(z))).astype(BF16)

    return pl.pallas_call(
        body, name="fwd_sgu", grid=(t // tm,),
        in_specs=[pl.BlockSpec((tm, e3), lambda i: (i, 0)), _layer((1, e), layer), _layer((1, e), layer),
                  _layer((GROUPS, CHUNK, CHUNK), layer), _layer((CHUNK, GROUPS), layer)],
        out_specs=pl.BlockSpec((tm, e), lambda i: (i, 0)),
        out_shape=SDS((t, e), BF16),
        scratch_shapes=[pltpu.VMEM((tm, e), F32)],
        compiler_params=_params(1),
    )(proj, ln_g, ln_b, sgu_w, sgu_bt)


def _ple_forward(x1, p_ref, plg_ref, gw_ref, pw_ref):
    nk, _, dq = pw_ref.shape
    r, xh = _rms_stats(x1)
    rn = (xh * plg_ref[...]).astype(BF16)
    gate = _sigmoid(_dot(rn, gw_ref[...]))
    pb = p_ref[...].astype(BF16)
    q = jnp.concatenate([_dot(pb, pw_ref[k]) for k in range(nk)], axis=1)
    return r, xh, rn, gate, q


def _fwd_out(x, u, w_out_full, pl_norm_g, gate_w_full, p, proj_w_full, layer, tm):
    t, d = x.shape
    e = u.shape[1]
    ple = p.shape[-1]
    nk, dq = proj_w_full.shape[1], proj_w_full.shape[3]

    def body(x_ref, u_ref, wo_ref, plg_ref, gw_ref, p_ref, pw_ref, x1_ref, x2_ref):
        x1 = x_ref[...] + _dot(u_ref[...], wo_ref[...])
        x1_ref[...] = x1
        _, _, _, gate, q = _ple_forward(x1, p_ref, plg_ref, gw_ref, pw_ref)
        x2_ref[...] = x1 + gate * q

    return pl.pallas_call(
        body, name="fwd_out", grid=(t // tm,),
        in_specs=[pl.BlockSpec((tm, d), lambda i: (i, 0)), pl.BlockSpec((tm, e), lambda i: (i, 0)),
                  _layer((e, d), layer), _layer((1, d), layer), _layer((d, d), layer),
                  pl.BlockSpec((None, tm, ple), lambda i: (layer, i, 0)), _layer((nk, ple, dq), layer)],
        out_specs=[pl.BlockSpec((tm, d), lambda i: (i, 0)), pl.BlockSpec((tm, d), lambda i: (i, 0))],
        out_shape=[SDS((t, d), F32), SDS((t, d), F32)],
        compiler_params=_params(1),
    )(x, u, w_out_full, pl_norm_g, gate_w_full, p, proj_w_full)


def _loss_head(x, final_g, target, tm):
    t, d = x.shape
    n_steps = t // tm

    def body(x_ref, g_ref, tgt_ref, loss_ref, dx_ref, dg_ref, sq_acc):
        i = pl.program_id(0)

        @pl.when(i == 0)
        def _():
            sq_acc[...] = jnp.zeros_like(sq_acc)
            dg_ref[...] = jnp.zeros_like(dg_ref)

        g = g_ref[...]
        r, xh = _rms_stats(x_ref[...])
        diff = xh * g - tgt_ref[...]
        sq_acc[...] += jnp.sum(diff * diff, axis=0, keepdims=True)
        dout = diff * (1.0 / d)
        dg_ref[...] += jnp.sum(dout * xh, axis=0, keepdims=True)
        dx_ref[...] = _rms_bwd(dout, g, r, xh)

        @pl.when(i == n_steps - 1)
        def _():
            loss_ref[...] = jnp.sum(sq_acc[...], axis=1, keepdims=True) * (0.5 / d)

    return pl.pallas_call(
        body, name="loss_head", grid=(n_steps,),
        in_specs=[pl.BlockSpec((tm, d), lambda i: (i, 0)), _const((1, d)), pl.BlockSpec((tm, d), lambda i: (i, 0))],
        out_specs=[_const((1, 1)), pl.BlockSpec((tm, d), lambda i: (i, 0)), _const((1, d))],
        out_shape=[SDS((1, 1), F32), SDS((t, d), F32), SDS((1, d), F32)],
        scratch_shapes=[pltpu.VMEM((1, d), F32)],
        compiler_params=_params(1),
    )(x, final_g, target)


def _bwd_out(dx2, x1, p, pl_norm_g, gate_w_full, proj_w_full, w_out_full, layer, tm):
    t, d = dx2.shape
    e = w_out_full.shape[1]
    ple = p.shape[-1]
    nk, dq_w = proj_w_full.shape[1], proj_w_full.shape[3]

    def body(dx2_ref, x1_ref, p_ref, plg_ref, gw_ref, pw_ref, wo_ref, dx1_ref, du_ref, rn_ref, ds_ref, dq_ref, dplg_ref):
        @pl.when(pl.program_id(0) == 0)
        def _():
            dplg_ref[...] = jnp.zeros_like(dplg_ref)

        dx2v = dx2_ref[...]
        r, xh, rn, gate, q = _ple_forward(x1_ref[...], p_ref, plg_ref, gw_ref, pw_ref)
        rn_ref[...] = rn
        dq_ref[...] = (dx2v * gate).astype(BF16)
        ds = (dx2v * q * gate * (1.0 - gate)).astype(BF16)
        ds_ref[...] = ds
        dr = _dot_nt(ds, gw_ref[...])
        dplg_ref[...] += jnp.sum(dr * xh, axis=0, keepdims=True)
        dx1 = dx2v + _rms_bwd(dr, plg_ref[...], r, xh)
        dx1_ref[...] = dx1
        du_ref[...] = _dot_nt(dx1.astype(BF16), wo_ref[...]).astype(BF16)

    row = lambda w: pl.BlockSpec((tm, w), lambda i: (i, 0))
    return pl.pallas_call(
        body, name="bwd_out", grid=(t // tm,),
        in_specs=[row(d), row(d), pl.BlockSpec((None, tm, ple), lambda i: (layer, i, 0)),
                  _layer((1, d), layer), _layer((d, d), layer), _layer((nk, ple, dq_w), layer), _layer((e, d), layer)],
        out_specs=[row(d), row(e), row(d), row(d), row(d), _const((1, d))],
        out_shape=[SDS((t, d), F32), SDS((t, e), BF16), SDS((t, d), BF16), SDS((t, d), BF16), SDS((t, d), BF16),
                   SDS((1, d), F32)],
        compiler_params=_params(1),
    )(dx2, x1, p, pl_norm_g, gate_w_full, proj_w_full, w_out_full)


def _bwd_conv(du, proj, y1, conv_w_blk, ln_g, ln_b, layer, bsz, seq, tm):
    t, e3 = proj.shape
    e = e3 // 3
    nt = seq // tm
    hb = tm // HALO
    cb, ncb = _col_blocks(e)
    rb = min(64, tm)
    kp = conv_w_blk.shape[2]
    k_taps = kp - 1
    prev, nxt = _halo_maps(nt, hb, t // HALO)
    z_halo = lambda b, i: (nxt(b, i)[0], 2)

    def ln_silu_bwd(du_v, z_v, y1_v, g, lb):
        rs, xh = _ln_stats(y1_v)
        y2 = xh * g + lb
        sg = _sigmoid(y2)
        sz = _sigmoid(z_v)
        dy = du_v * (z_v * sz)
        dy2 = dy * _dsilu(y2, sg)
        return _ln_bwd(dy2, g, rs, xh), dy2, xh, du_v * (y2 * sg) * _dsilu(z_v, sz)

    def body(du_ref, proj_ref, y1_ref, duh_ref, zh_ref, y1h_ref, abh_ref, w_ref, g_ref, lb_ref,
             dproj_ref, dw_ref, dcb_ref, dg_ref, dlb_ref, y0s, dy1s, dy0s, ysh):
        b_id, i = pl.program_id(0), pl.program_id(1)

        @pl.when((b_id == 0) & (i == 0))
        def _():
            dw_ref[...] = jnp.zeros_like(dw_ref)
            dcb_ref[...] = jnp.zeros_like(dcb_ref)
            dg_ref[...] = jnp.zeros_like(dg_ref)
            dlb_ref[...] = jnp.zeros_like(dlb_ref)

        g, lb = g_ref[...], lb_ref[...]
        a = proj_ref[:, 0:e].astype(F32)
        b = proj_ref[:, e:2 * e].astype(F32)
        z = proj_ref[:, 2 * e:3 * e].astype(F32)
        sb = _sigmoid(b)
        y0 = a * sb
        dy1, dy2, xh, dz = ln_silu_bwd(du_ref[...].astype(F32), z, y1_ref[...], g, lb)
        dproj_ref[:, 2 * e:3 * e] = dz.astype(BF16)
        dg_ref[...] += jnp.sum(dy2 * xh, axis=0, keepdims=True)
        dlb_ref[...] += jnp.sum(dy2, axis=0, keepdims=True)
        dcb_ref[...] += jnp.sum(dy1, axis=0, keepdims=True)
        dy1h, _, _, _ = ln_silu_bwd(duh_ref[...].astype(F32), zh_ref[...].astype(F32), y1h_ref[...], g, lb)
        dy1h = jnp.where(i < nt - 1, dy1h, 0.0)
        ah = abh_ref[:, 0:e].astype(F32)
        bh = abh_ref[:, e:2 * e].astype(F32)
        y0h = jnp.where(i > 0, ah * _sigmoid(bh), 0.0)
        for c in range(ncb):
            cols = slice(c * cb, (c + 1) * cb)
            y0s[c, 0:HALO, :] = y0h[:, cols]
            y0s[c, HALO:HALO + tm, :] = y0[:, cols]
            dy1s[c, 0:tm, :] = dy1[:, cols]
            dy1s[c, tm:tm + HALO, :] = dy1h[:, cols]

        def per_block(c, carry):
            _conv_taps(dy1s, w_ref, dy0s, c, tm, rb, 0, True)
            for res in range(1, SUBLANES):
                ysh[res - 1] = y0s[c, pl.ds(res, tm + HALO - SUBLANES), :]
            for k in range(k_taps):
                off = HALO - (k_taps - 1) + k
                res = off % SUBLANES
                acc = None
                for r0 in range(0, tm, rb):
                    rows = pl.ds(r0 + off - res, rb)
                    shifted = ysh[res - 1, rows, :] if res else y0s[c, rows, :]
                    term = dy1s[c, pl.ds(r0, rb), :] * shifted
                    acc = term if acc is None else acc + term
                dw_ref[c, pl.ds(k, 1), :] += jnp.sum(acc, axis=0, keepdims=True)
            return carry

        lax.fori_loop(0, ncb, per_block, 0)
        dy0 = jnp.concatenate([dy0s[c] for c in range(ncb)], axis=1)
        dproj_ref[:, 0:e] = (dy0 * sb).astype(BF16)
        dproj_ref[:, e:2 * e] = (dy0 * a * sb * (1.0 - sb)).astype(BF16)

    tile = lambda w: pl.BlockSpec((tm, w), lambda b, i: (b * nt + i, 0))
    return pl.pallas_call(
        body, name="bwd_conv", grid=(bsz, nt),
        in_specs=[tile(e), tile(e3), tile(e),
                  pl.BlockSpec((HALO, e), nxt), pl.BlockSpec((HALO, e), z_halo), pl.BlockSpec((HALO, e), nxt),
                  pl.BlockSpec((HALO, 2 * e), prev),
                  _layer((ncb, kp, cb), layer), _layer((1, e), layer), _layer((1, e), layer)],
        out_specs=[tile(e3), _const((ncb, kp, cb)), _const((1, e)), _const((1, e)), _const((1, e))],
        out_shape=[SDS((t, e3), BF16), SDS((ncb, kp, cb), F32), SDS((1, e), F32), SDS((1, e), F32), SDS((1, e), F32)],
        scratch_shapes=[pltpu.VMEM((ncb, HALO + tm, cb), F32), pltpu.VMEM((ncb, tm + HALO, cb), F32),
                        pltpu.VMEM((ncb, tm, cb), F32), pltpu.VMEM((SUBLANES - 1, tm + HALO - SUBLANES, cb), F32)],
        compiler_params=_params(2),
    )(du, proj, y1, du, proj, y1, proj, conv_w_blk, ln_g, ln_b)


def _bwd_sgu(du, proj, ln_g, ln_b, sgu_w, sgu_bt, layer, tm):
    t, e3 = proj.shape
    e = e3 // 3
    gw = e // GROUPS
    nch = tm // CHUNK

    def body(du_ref, proj_ref, g_ref, lb_ref, w_ref, bt_ref, dproj_ref, dw_ref, dbt_ref, dg_ref, dlb_ref, mixed, dmix, dv):
        @pl.when(pl.program_id(0) == 0)
        def _():
            dw_ref[...] = jnp.zeros_like(dw_ref)
            dbt_ref[...] = jnp.zeros_like(dbt_ref)
            dg_ref[...] = jnp.zeros_like(dg_ref)
            dlb_ref[...] = jnp.zeros_like(dlb_ref)

        g, lb = g_ref[...], lb_ref[...]
        a = proj_ref[:, 0:e].astype(F32)
        b = proj_ref[:, e:2 * e].astype(F32)
        z = proj_ref[:, 2 * e:3 * e].astype(F32)
        ug, dug = _gelu_parts(a)
        vb, dvb_db = _gelu_parts(b)
        rs, xh = _ln_stats(vb)
        v = (xh * g + lb).astype(BF16)
        mask = _tril_mask()
        for gi in range(GROUPS):
            wm = jnp.where(mask, w_ref[gi], 0.0).astype(BF16)
            bias = bt_ref[:, gi:gi + 1]
            for n in range(nch):
                blk = v[n * CHUNK:(n + 1) * CHUNK, gi * gw:(gi + 1) * gw]
                mixed[n * CHUNK:(n + 1) * CHUNK, gi * gw:(gi + 1) * gw] = _dot(wm, blk) + bias
        mx = mixed[...]
        sz = _sigmoid(z)
        duv = du_ref[...].astype(F32)
        dy = duv * (z * sz)
        dproj_ref[:, 2 * e:3 * e] = (duv * (ug * mx) * _dsilu(z, sz)).astype(BF16)
        dproj_ref[:, 0:e] = (dy * mx * dug).astype(BF16)
        dmix[...] = dy * ug
        for gi in range(GROUPS):
            wm = jnp.where(mask, w_ref[gi], 0.0).astype(BF16)
            dw_acc = None
            db_acc = None
            for n in range(nch):
                rows, cols = slice(n * CHUNK, (n + 1) * CHUNK), slice(gi * gw, (gi + 1) * gw)
                dm = dmix[rows, cols]
                dmb = dm.astype(BF16)
                dw_n = _dot_nt(dmb, v[rows, cols])
                db_n = jnp.sum(dm, axis=1, keepdims=True)
                dw_acc = dw_n if dw_acc is None else dw_acc + dw_n
                db_acc = db_n if db_acc is None else db_acc + db_n
                dv[rows, cols] = _dot_tn(wm, dmb)
            dw_ref[gi] += jnp.where(mask, dw_acc, 0.0)
            dbt_ref[:, gi:gi + 1] += db_acc
        dvv = dv[...]
        dg_ref[...] += jnp.sum(dvv * xh, axis=0, keepdims=True)
        dlb_ref[...] += jnp.sum(dvv, axis=0, keepdims=True)
        dproj_ref[:, e:2 * e] = (_ln_bwd(dvv, g, rs, xh) * dvb_db).astype(BF16)

    return pl.pallas_call(
        body, name="bwd_sgu", grid=(t // tm,),
        in_specs=[pl.BlockSpec((tm, e), lambda i: (i, 0)), pl.BlockSpec((tm, e3), lambda i: (i, 0)),
                  _layer((1, e), layer), _layer((1, e), layer),
                  _layer((GROUPS, CHUNK, CHUNK), layer), _layer((CHUNK, GROUPS), layer)],
        out_specs=[pl.BlockSpec((tm, e3), lambda i: (i, 0)), _const((GROUPS, CHUNK, CHUNK)), _const((CHUNK, GROUPS)),
                   _const((1, e)), _const((1, e))],
        out_shape=[SDS((t, e3), BF16), SDS((GROUPS, CHUNK, CHUNK), F32), SDS((CHUNK, GROUPS), F32),
                   SDS((1, e), F32), SDS((1, e), F32)],
        scratch_shapes=[pltpu.VMEM((tm, e), F32), pltpu.VMEM((tm, e), F32), pltpu.VMEM((tm, e), F32)],
        compiler_params=_params(1),
    )(du, proj, ln_g, ln_b, sgu_w, sgu_bt)


def _bwd_in(dproj, dx1, x, norm_g, w_in_full, layer, tm):
    t, d = x.shape
    _, nk, _, n4 = w_in_full.shape

    def body(dproj_ref, dx1_ref, x_ref, g_ref, w_ref, dx_ref, dg_ref):
        @pl.when(pl.program_id(0) == 0)
        def _():
            dg_ref[...] = jnp.zeros_like(dg_ref)

        dh = None
        for k in range(nk):
            part = _dot_nt(dproj_ref[:, k * n4:(k + 1) * n4], w_ref[k])
            dh = part if dh is None else dh + part
        r, xh = _rms_stats(x_ref[...])
        dg_ref[...] += jnp.sum(dh * xh, axis=0, keepdims=True)
        dx_ref[...] = dx1_ref[...] + _rms_bwd(dh, g_ref[...], r, xh)

    row = lambda w: pl.BlockSpec((tm, w), lambda i: (i, 0))
    return pl.pallas_call(
        body, name="bwd_in", grid=(t // tm,),
        in_specs=[row(nk * n4), row(d), row(d), _layer((1, d), layer), _layer((nk, d, n4), layer)],
        out_specs=[row(d), _const((1, d))],
        out_shape=[SDS((t, d), F32), SDS((1, d), F32)],
        compiler_params=_params(1),
    )(dproj, dx1, x, norm_g, w_in_full)


def _wgrad(a, b, kblk, nblk, tm, name, a_layer=None):
    t, n = b.shape
    k = a.shape[-1]
    kw, nw = k // kblk, n // nblk
    n_steps = t // tm

    def body(a_ref, b_ref, o_ref):
        @pl.when(pl.program_id(2) == 0)
        def _():
            o_ref[...] = jnp.zeros_like(o_ref)

        o_ref[...] += _dot_tn(a_ref[...].astype(BF16), b_ref[...].astype(BF16))

    if a_layer is None:
        a_spec = pl.BlockSpec((tm, kw), lambda kb, nb, i: (i, kb))
    else:
        a_spec = pl.BlockSpec((None, tm, kw), lambda kb, nb, i: (a_layer, i, kb))
    return pl.pallas_call(
        body, name=name, grid=(kblk, nblk, n_steps),
        in_specs=[a_spec, pl.BlockSpec((tm, nw), lambda kb, nb, i: (i, nb))],
        out_specs=pl.BlockSpec((None, None, kw, nw), lambda kb, nb, i: (kb, nb, 0, 0)),
        out_shape=SDS((kblk, nblk, kw, nw), F32),
        compiler_params=_params(3),
    )(a, b)


def _row_tile(rows, cols, budget_bytes=1 << 20):
    best = None
    for cand in range(SUBLANES, rows + 1, SUBLANES):
        if rows % cand == 0 and cand * cols * 4 <= budget_bytes:
            best = cand
    return best if best is not None else rows


def _pair_sum(grads, recv, my_c, wire_dtype):
    n, _, h, c = grads.shape
    th = _row_tile(h, c)
    two = wire_dtype != F32

    def body(c_ref, g_ref, r_ref, o_ref, *wire_ref):
        total = g_ref[...] + r_ref[...]
        o_ref[...] = total
        if two:
            wire_ref[0][...] = total.astype(wire_dtype)

    out_spec = pl.BlockSpec((None, th, c), lambda j, i, c_ref: (j, i, 0))
    grid_spec = pltpu.PrefetchScalarGridSpec(
        num_scalar_prefetch=1, grid=(n, h // th),
        in_specs=[pl.BlockSpec((None, None, th, c), lambda j, i, c_ref: (j, c_ref[0], i, 0)),
                  pl.BlockSpec((None, None, th, c), lambda j, i, c_ref: (j, 0, i, 0))],
        out_specs=[out_spec, out_spec] if two else [out_spec])
    out_shape = [SDS((n, h, c), F32)] + ([SDS((n, h, c), wire_dtype)] if two else [])
    outs = pl.pallas_call(body, name="pair_sum", grid_spec=grid_spec, out_shape=out_shape,
                          compiler_params=_params(2))(my_c, grads, recv)
    return outs[0], outs[-1]


def _chip_sum(pair, recv, my_k):
    nl, _, h, c = pair.shape
    th = _row_tile(h, c)

    def body(k_ref, own_ref, r1_ref, r2_ref, r3_ref, o_ref):
        acc = own_ref[...]
        for ref in (r1_ref, r2_ref, r3_ref):
            acc = acc + ref[...].astype(F32)
        o_ref[...] = acc

    def slot(flip):
        return pl.BlockSpec((None, None, th, c), lambda j, i, k_ref: (j, jnp.bitwise_xor(k_ref[0], flip), i, 0))

    grid_spec = pltpu.PrefetchScalarGridSpec(
        num_scalar_prefetch=1, grid=(nl, h // th),
        in_specs=[slot(0), slot(1), slot(2), slot(3)],
        out_specs=pl.BlockSpec((None, th, c), lambda j, i, k_ref: (j, i, 0)))
    return pl.pallas_call(body, name="chip_sum", grid_spec=grid_spec, out_shape=SDS((nl, h, c), F32),
                          compiler_params=_params(2))(my_k, pair, recv, recv, recv)


def _adam_math(w, gv, m, v):
    c1 = 1.0 - ADAM_B1 ** ADAM_STEP
    c2 = 1.0 - ADAM_B2 ** ADAM_STEP
    mn = ADAM_B1 * m + (1.0 - ADAM_B1) * gv
    vn = ADAM_B2 * v + (1.0 - ADAM_B2) * (gv * gv)
    m_hat = mn / c1
    v_hat = vn / c2
    return -ADAM_LR * (m_hat / (jnp.sqrt(v_hat) + ADAM_EPS) + ADAM_WD * w), mn, vn


def _adamw_halves(w, g_mine, g_theirs, m, v, my_c):
    nl, rows, cols = w.shape
    h = rows // 2
    th = _row_tile(h, cols, 512 << 10)
    as4 = lambda a: a.reshape(nl, 2, h, cols)

    def body(c_ref, w_ref, gm_ref, gt_ref, m_ref, v_ref, go_ref, d_ref, mo_ref, vo_ref):
        gv = jnp.where(pl.program_id(1) == c_ref[0], gm_ref[...], gt_ref[...])
        go_ref[...] = gv
        d_ref[...], mo_ref[...], vo_ref[...] = _adam_math(w_ref[...], gv, m_ref[...], v_ref[...])

    full = pl.BlockSpec((None, None, th, cols), lambda j, s, i, c_ref: (j, s, i, 0))
    half = pl.BlockSpec((None, th, cols), lambda j, s, i, c_ref: (j, i, 0))
    grid_spec = pltpu.PrefetchScalarGridSpec(
        num_scalar_prefetch=1, grid=(nl, 2, h // th),
        in_specs=[full, half, half, full, full], out_specs=[full] * 4)
    outs = pl.pallas_call(body, name="adamw_halves", grid_spec=grid_spec, out_shape=[SDS((nl, 2, h, cols), F32)] * 4,
                          compiler_params=_params(3))(my_c, as4(w), g_mine, g_theirs, as4(m), as4(v))
    return [o.reshape(nl, rows, cols) for o in outs]


def _adamw(w, g, m, v):
    rows, cols = w.shape
    tr = _row_tile(rows, cols, 512 << 10)

    def body(w_ref, g_ref, m_ref, v_ref, go_ref, d_ref, mo_ref, vo_ref):
        gv = g_ref[...]
        go_ref[...] = gv
        d_ref[...], mo_ref[...], vo_ref[...] = _adam_math(w_ref[...], gv, m_ref[...], v_ref[...])

    spec = pl.BlockSpec((tr, cols), lambda i: (i, 0))
    return pl.pallas_call(
        body, name="adamw", grid=(rows // tr,), in_specs=[spec] * 4, out_specs=[spec] * 4,
        out_shape=[SDS((rows, cols), F32)] * 4, compiler_params=_params(1))(w, g, m, v)


def _place():
    x, y, c = lax.axis_index("x"), lax.axis_index("y"), lax.axis_index("c")
    chips = [(1 - x, y), (x, 1 - y), (1 - x, 1 - y)]
    return x, y, c, 2 * x + y, chips


def _remote(src, dst, send_sem, recv_sem, device):
    return pltpu.make_async_remote_copy(src_ref=src, dst_ref=dst, send_sem=send_sem, recv_sem=recv_sem,
                                        device_id=device, device_id_type=MESH_IDS)


def _gather_weights(shards, small):
    n = len(shards)

    def body(*refs):
        ins, small_in = refs[:n], refs[n]
        outs, small_out = refs[n + 1:2 * n + 1], refs[2 * n + 1]
        ici_send, ici_recv, d2d_send, d2d_recv, own_send, own_recv = refs[2 * n + 2:]
        x, y, c, k, chips = _place()
        local, sends = [], []
        for j in range(n):
            h = ins[j].shape[2] // 2
            mine = pl.ds(c * h, h)
            own = _remote(ins[j], outs[j].at[:, pl.ds(k, 1)], own_send.at[j], own_recv.at[j], (x, y, 1 - c))
            own.start()
            local.append(own)
            for ti, (cx, cy) in enumerate(chips):
                cp = _remote(ins[j].at[:, :, mine], outs[j].at[:, pl.ds(k, 1), mine],
                             ici_send.at[3 * j + ti], ici_recv.at[3 * j + ti], (cx, cy, c))
                cp.start()
                sends.append(cp)
        own = _remote(small_in, small_out.at[pl.ds(k, 1)], own_send.at[n], own_recv.at[n], (x, y, 1 - c))
        own.start()
        local.append(own)
        for ti, (cx, cy) in enumerate(chips):
            cp = _remote(small_in, small_out.at[pl.ds(k, 1)], ici_send.at[3 * n + ti], ici_recv.at[3 * n + ti], (cx, cy, c))
            cp.start()
            sends.append(cp)
        for j in range(n):
            h = ins[j].shape[2] // 2
            mine = pl.ds(c * h, h)
            for ti, (cx, cy) in enumerate(chips):
                landed = outs[j].at[:, pl.ds(2 * cx + cy, 1), mine]
                _remote(ins[j].at[:, :, mine], landed, ici_send.at[3 * j + ti], ici_recv.at[3 * j + ti], (cx, cy, c)).wait_recv()
                fwd = _remote(landed, landed, d2d_send.at[3 * j + ti], d2d_recv.at[3 * j + ti], (x, y, 1 - c))
                fwd.start()
                sends.append(fwd)
        for ti, (cx, cy) in enumerate(chips):
            _remote(small_in, small_out.at[pl.ds(2 * cx + cy, 1)], ici_send.at[3 * n + ti], ici_recv.at[3 * n + ti],
                    (cx, cy, c)).wait_recv()
        for j in range(n):
            h = ins[j].shape[2] // 2
            theirs = pl.ds((1 - c) * h, h)
            for ti, (cx, cy) in enumerate(chips):
                landed = outs[j].at[:, pl.ds(2 * cx + cy, 1), theirs]
                _remote(landed, landed, d2d_send.at[3 * j + ti], d2d_recv.at[3 * j + ti], (x, y, 1 - c)).wait_recv()
        for cp in sends:
            cp.wait_send()
        for cp in local:
            cp.wait()

    out_shape = [SDS((s.shape[0], N_CHIPS) + s.shape[2:], s.dtype) for s in shards]
    out_shape.append(SDS((N_CHIPS,) + small.shape[1:], small.dtype))
    return pl.pallas_call(
        body, name="gather_weights", in_specs=[ANY] * (n + 1), out_specs=[ANY] * (n + 1), out_shape=out_shape,
        scratch_shapes=[pltpu.SemaphoreType.DMA((3 * n + 3,)), pltpu.SemaphoreType.DMA((3 * n + 3,)),
                        pltpu.SemaphoreType.DMA((3 * n,)), pltpu.SemaphoreType.DMA((3 * n,)),
                        pltpu.SemaphoreType.DMA((n + 1,)), pltpu.SemaphoreType.DMA((n + 1,))],
    )(*shards, small)


def _swap_halves(grads):
    n = len(grads)

    def body(*refs):
        ins, outs = refs[:n], refs[n:2 * n]
        send_sem, recv_sem = refs[2 * n:]
        x, y, c, _, _ = _place()
        copies = [_remote(ins[j].at[:, pl.ds(1 - c, 1)], outs[j], send_sem.at[j], recv_sem.at[j], (x, y, 1 - c))
                  for j in range(n)]
        for cp in copies:
            cp.start()
        for cp in copies:
            cp.wait()

    return pl.pallas_call(
        body, name="swap_halves", in_specs=[ANY] * n, out_specs=[ANY] * n,
        out_shape=[SDS((g.shape[0], 1) + g.shape[2:], g.dtype) for g in grads],
        scratch_shapes=[pltpu.SemaphoreType.DMA((n,)), pltpu.SemaphoreType.DMA((n,))],
    )(*grads)


def _scatter_to_chips(sums):
    n = len(sums)

    def body(*refs):
        ins, outs = refs[:n], refs[n:2 * n]
        send_sem, recv_sem = refs[2 * n:]
        x, y, c, k, chips = _place()
        copies = []
        for j in range(n):
            for ti, (cx, cy) in enumerate(chips):
                cp = _remote(ins[j].at[:, pl.ds(2 * cx + cy, 1)], outs[j].at[:, pl.ds(k, 1)],
                             send_sem.at[3 * j + ti], recv_sem.at[3 * j + ti], (cx, cy, c))
                cp.start()
                copies.append(cp)
        for j in range(n):
            for ti, (cx, cy) in enumerate(chips):
                slot = outs[j].at[:, pl.ds(2 * cx + cy, 1)]
                _remote(slot, slot, send_sem.at[3 * j + ti], recv_sem.at[3 * j + ti], (cx, cy, c)).wait_recv()
        for cp in copies:
            cp.wait_send()

    return pl.pallas_call(
        body, name="scatter_to_chips", in_specs=[ANY] * n, out_specs=[ANY] * n,
        out_shape=[SDS(s.shape, s.dtype) for s in sums],
        scratch_shapes=[pltpu.SemaphoreType.DMA((3 * n,)), pltpu.SemaphoreType.DMA((3 * n,))],
    )(*sums)


def _swap_pieces(pieces):
    n = len(pieces)

    def body(*refs):
        ins, outs = refs[:n], refs[n:2 * n]
        send_sem, recv_sem = refs[2 * n:]
        x, y, c, _, _ = _place()
        copies = [_remote(ins[j], outs[j], send_sem.at[j], recv_sem.at[j], (x, y, 1 - c)) for j in range(n)]
        for cp in copies:
            cp.start()
        for cp in copies:
            cp.wait()

    return pl.pallas_call(
        body, name="swap_pieces", in_specs=[ANY] * n, out_specs=[ANY] * n,
        out_shape=[SDS(p.shape, p.dtype) for p in pieces],
        scratch_shapes=[pltpu.SemaphoreType.DMA((n,)), pltpu.SemaphoreType.DMA((n,))],
    )(*pieces)


def _gather_pieces(piece):
    def body(in_ref, out_ref, send_sem, recv_sem):
        x, y, c, k, chips = _place()
        peers = [(x, y, 1 - c)] + [(cx, cy, pc) for (cx, cy) in chips for pc in (c, 1 - c)]
        copies = []
        for ti, peer in enumerate(peers):
            cp = _remote(in_ref, out_ref.at[pl.ds(k, 1), pl.ds(c, 1)], send_sem.at[ti], recv_sem.at[ti], peer)
            cp.start()
            copies.append(cp)
        for ti, (px, py, pc) in enumerate(peers):
            _remote(in_ref, out_ref.at[pl.ds(2 * px + py, 1), pl.ds(pc, 1)], send_sem.at[ti], recv_sem.at[ti],
                    (px, py, pc)).wait_recv()
        for cp in copies:
            cp.wait_send()

    n_peers = 2 * N_CHIPS - 1
    return pl.pallas_call(
        body, name="gather_pieces", in_specs=[ANY], out_specs=ANY,
        out_shape=SDS((N_CHIPS, 2) + piece.shape[2:], piece.dtype),
        scratch_shapes=[pltpu.SemaphoreType.DMA((n_peers,)), pltpu.SemaphoreType.DMA((n_peers,))],
    )(piece)


def _pack_rows(parts, width, total_rows=None):
    rows = []
    for a in parts:
        a2 = a.reshape(-1, width)
        pad = (-a2.shape[0]) % SUBLANES
        rows.append(jnp.pad(a2, ((0, pad), (0, 0))) if pad else a2)
    out = jnp.concatenate(rows, axis=0)
    if total_rows is not None and out.shape[0] < total_rows:
        out = jnp.pad(out, ((0, total_rows - out.shape[0]), (0, 0)))
    return out


def _unpack_rows(packed, shapes, width):
    out, r = [], 0
    for shp in shapes:
        size = 1
        for s in shp:
            size *= s
        nr = size // width
        out.append(packed[r:r + nr].reshape(shp))
        r += nr + ((-nr) % SUBLANES)
    return out


def kernel(x, p, norm_g, w_in, w_out, conv_w, conv_b, conv_ln_g, conv_ln_b, sgu_ln_g, sgu_ln_b, sgu_w, sgu_b, pl_norm_g, pl_gate_w, pl_proj_w, final_g, loss_target, m_norm_g, m_w_in, m_w_out, m_conv_w, m_conv_b, m_conv_ln_g, m_conv_ln_b, m_sgu_ln_g, m_sgu_ln_b, m_sgu_w, m_sgu_b, m_pl_norm_g, m_pl_gate_w, m_pl_proj_w, m_final_g, v_norm_g, v_w_in, v_w_out, v_conv_w, v_conv_b, v_conv_ln_g, v_conv_ln_b, v_sgu_ln_g, v_sgu_ln_b, v_sgu_w, v_sgu_b, v_pl_norm_g, v_pl_gate_w, v_pl_proj_w, v_final_g):
    bsz, seq, d = x.shape
    depth = w_in.shape[0]
    e = w_out.shape[1] * N_CHIPS
    e3 = 3 * e
    n4 = w_in.shape[2]
    ple = p.shape[-1]
    dq = pl_proj_w.shape[2]
    k_taps = conv_w.shape[1]
    kp = k_taps + 1
    n_conv, n_sgu = conv_w.shape[0], sgu_ln_g.shape[0]
    t = bsz * seq
    tm_mm = min(512, seq)
    tm_mix = min(256, seq)
    cb, ncb = _col_blocks(e)
    my_c = lax.axis_index("c")
    my_k = 2 * lax.axis_index("x") + lax.axis_index("y")

    ec = e // N_CHIPS
    small_w = _pack_rows([conv_w.reshape(n_conv * k_taps, ec), sgu_ln_g, sgu_ln_b], ec)[None]
    big = [w_in.astype(BF16)[:, None], w_out.astype(BF16)[:, None], pl_gate_w.astype(BF16)[:, None],
           pl_proj_w.astype(BF16)[:, None]]
    w_in_f, w_out_f, gate_f, proj_f, small_f = _gather_weights(big, small_w)
    w_out_f = w_out_f.reshape(depth, e, d)
    gate_f = gate_f.reshape(depth, d, d)
    conv_w_rows, sgu_g_rows, sgu_b_rows = _unpack_rows(
        jnp.transpose(small_f, (1, 0, 2)).reshape(small_f.shape[1], e),
        [(n_conv * k_taps, e), (n_sgu, e), (n_sgu, e)], e)
    conv_w_full = conv_w_rows.reshape(n_conv, k_taps, e)
    conv_w_blk = jnp.transpose(jnp.pad(conv_w_full, ((0, 0), (0, 1), (0, 0))).reshape(n_conv, kp, ncb, cb), (0, 2, 1, 3))
    sgu_ln_g_full = sgu_g_rows.reshape(n_sgu, 1, e)
    sgu_ln_b_full = sgu_b_rows.reshape(n_sgu, 1, e)
    sgu_bt = jnp.transpose(sgu_b, (0, 2, 1))

    norm_g3 = norm_g[:, None]
    pl_norm_g3 = pl_norm_g[:, None]
    conv_b3, conv_ln_g3, conv_ln_b3 = conv_b[:, None], conv_ln_g[:, None], conv_ln_b[:, None]
    p3 = p.reshape(depth, t, ple)

    xs, hs, projs, us, x1s, y1s = [], [], [], [], [], {}
    xc = x.reshape(t, d)
    for l in range(depth):
        j = l // 2
        xs.append(xc)
        h, proj = _fwd_in(xc, norm_g3, w_in_f, l, tm_mm)
        if l % 2 == 0:
            u, y1 = _fwd_conv(proj, conv_w_blk, conv_b3, conv_ln_g3, conv_ln_b3, j, bsz, seq, tm_mix)
            y1s[l] = y1
        else:
            u = _fwd_sgu(proj, sgu_ln_g_full, sgu_ln_b_full, sgu_w, sgu_bt, j, tm_mix)
        x1, xc = _fwd_out(xc, u, w_out_f, pl_norm_g3, gate_f, p3, proj_f, l, tm_mm)
        hs.append(h)
        projs.append(proj)
        us.append(u)
        x1s.append(x1)

    loss_local, dx, d_final_g = _loss_head(xc, final_g[None], loss_target.reshape(t, d), tm_mm)
    loss = lax.psum(loss_local[0, 0], ("x", "y", "c"))

    g_in, g_out, g_gate, g_proj = [None] * depth, [None] * depth, [None] * depth, [None] * depth
    d_norm_g, d_pl_norm_g = [None] * depth, [None] * depth
    d_conv = [None] * n_conv
    d_sgu = [None] * n_sgu
    for l in reversed(range(depth)):
        j = l // 2
        dx1, du, rn, ds, dqv, d_pl_norm_g[l] = _bwd_out(dx, x1s[l], p3, pl_norm_g3, gate_f, proj_f, w_out_f, l, tm_mm)
        g_proj[l] = _wgrad(p3, dqv, 1, N_CHIPS, tm_mm, "wgrad_proj", a_layer=l)
        g_gate[l] = _wgrad(rn, ds, 1, 1, tm_mm, "wgrad_gate")
        g_out[l] = _wgrad(us[l], dx1, N_CHIPS, 1, tm_mm, "wgrad_out")
        if l % 2 == 0:
            dproj, dcw, dcb, dlg, dlb = _bwd_conv(du, projs[l], y1s[l], conv_w_blk, conv_ln_g3, conv_ln_b3, j, bsz, seq, tm_mix)
            d_conv[j] = (dcw, dcb, dlg, dlb)
        else:
            dproj, dsw, dsbt, dlg, dlb = _bwd_sgu(du, projs[l], sgu_ln_g_full, sgu_ln_b_full, sgu_w, sgu_bt, j, tm_mix)
            d_sgu[j] = (dsw, dsbt, dlg, dlb)
        g_in[l] = _wgrad(hs[l], dproj, 1, N_CHIPS, tm_mm, "wgrad_in")
        dx, d_norm_g[l] = _bwd_in(dproj, dx1, xs[l], norm_g3, w_in_f, l, tm_mm)
    grad_x = dx.reshape(bsz, seq, d)

    gl_in = jnp.stack(g_in).reshape(depth * N_CHIPS, 2, d // 2, n4)
    gl_out = jnp.stack(g_out).reshape(depth * N_CHIPS, 2, e // (2 * N_CHIPS), d)
    gl_gate = jnp.stack(g_gate).reshape(depth * N_CHIPS, 2, d // (2 * N_CHIPS), d)
    gl_proj = jnp.stack(g_proj).reshape(depth * N_CHIPS, 2, ple // 2, dq)

    d_conv_w = jnp.stack([jnp.transpose(dc[0], (1, 0, 2)).reshape(kp, e)[:k_taps] for dc in d_conv])
    d_conv_b = jnp.stack([dc[1][0] for dc in d_conv])
    d_conv_ln_g = jnp.stack([dc[2][0] for dc in d_conv])
    d_conv_ln_b = jnp.stack([dc[3][0] for dc in d_conv])
    d_sgu_w = jnp.stack([dsg[0] for dsg in d_sgu])
    d_sgu_b = jnp.stack([jnp.transpose(dsg[1]) for dsg in d_sgu])
    d_sgu_ln_g = jnp.stack([dsg[2][0] for dsg in d_sgu])
    d_sgu_ln_b = jnp.stack([dsg[3][0] for dsg in d_sgu])
    small_grads = [jnp.concatenate(d_norm_g), d_conv_w, d_conv_b, d_conv_ln_g, d_conv_ln_b, d_sgu_ln_g, d_sgu_ln_b,
                   d_sgu_w, d_sgu_b, jnp.concatenate(d_pl_norm_g), d_final_g]
    small_shapes = [a.shape for a in small_grads]
    packed = _pack_rows(small_grads, d)
    pack_rows = packed.shape[0] + ((-packed.shape[0]) % (8 * SUBLANES))
    packed = _pack_rows(small_grads, d, pack_rows)
    gl_small = packed.reshape(N_CHIPS, 2, pack_rows // 8, d)

    locals_ = [gl_in, gl_out, gl_gate, gl_proj, gl_small]
    recv = _swap_halves(locals_)
    c_arr = my_c.astype(jnp.int32).reshape(1)
    k_arr = my_k.astype(jnp.int32).reshape(1)
    wire = [BF16, BF16, BF16, BF16, F32]
    by_chip = lambda a: a.reshape((a.shape[0] // N_CHIPS, N_CHIPS) + a.shape[1:])
    pair, pair_wire = zip(*[_pair_sum(gl, rc, c_arr, wd) for gl, rc, wd in zip(locals_, recv, wire)])
    from_chips = _scatter_to_chips([by_chip(pw) for pw in pair_wire])
    reduced = [_chip_sum(by_chip(pr), fc, k_arr) for pr, fc in zip(pair, from_chips)]
    theirs = _swap_pieces(reduced[:4])
    small_all = _gather_pieces(reduced[4][:, None])
    small_all = lax.dynamic_update_slice(small_all, reduced[4][:, None], (my_k, my_c, 0, 0)).reshape(pack_rows, d)
    small_red = _unpack_rows(small_all, small_shapes, d)
    (gr_norm_g, gr_conv_w, gr_conv_b, gr_conv_ln_g, gr_conv_ln_b, gr_sgu_ln_g, gr_sgu_ln_b, gr_sgu_w, gr_sgu_b,
     gr_pl_norm_g, gr_final_g) = small_red
    gr_final_g = gr_final_g.reshape(d)
    gr_conv_w = lax.dynamic_slice_in_dim(gr_conv_w, my_k * ec, ec, axis=2)
    gr_sgu_ln_g = lax.dynamic_slice_in_dim(gr_sgu_ln_g, my_k * ec, ec, axis=1)
    gr_sgu_ln_b = lax.dynamic_slice_in_dim(gr_sgu_ln_b, my_k * ec, ec, axis=1)

    up_in = _adamw_halves(w_in, reduced[0], theirs[0], m_w_in, v_w_in, c_arr)
    up_out = _adamw_halves(w_out, reduced[1], theirs[1], m_w_out, v_w_out, c_arr)
    up_gate = _adamw_halves(pl_gate_w, reduced[2], theirs[2], m_pl_gate_w, v_pl_gate_w, c_arr)
    up_proj = _adamw_halves(pl_proj_w, reduced[3], theirs[3], m_pl_proj_w, v_pl_proj_w, c_arr)

    small_names = ["norm_g", "conv_w", "conv_b", "conv_ln_g", "conv_ln_b", "sgu_ln_g", "sgu_ln_b", "sgu_w", "sgu_b",
                   "pl_norm_g", "final_g"]
    small_w_list = [norm_g, conv_w, conv_b, conv_ln_g, conv_ln_b, sgu_ln_g, sgu_ln_b, sgu_w, sgu_b, pl_norm_g, final_g]
    small_m_list = [m_norm_g, m_conv_w, m_conv_b, m_conv_ln_g, m_conv_ln_b, m_sgu_ln_g, m_sgu_ln_b, m_sgu_w, m_sgu_b,
                    m_pl_norm_g, m_final_g]
    small_v_list = [v_norm_g, v_conv_w, v_conv_b, v_conv_ln_g, v_conv_ln_b, v_sgu_ln_g, v_sgu_ln_b, v_sgu_w, v_sgu_b,
                    v_pl_norm_g, v_final_g]
    small_g_list = [gr_norm_g, gr_conv_w, gr_conv_b, gr_conv_ln_g, gr_conv_ln_b, gr_sgu_ln_g, gr_sgu_ln_b, gr_sgu_w,
                    gr_sgu_b, gr_pl_norm_g, gr_final_g]
    width = ec
    shapes_local = [a.shape for a in small_w_list]
    outs_small = _adamw(_pack_rows(small_w_list, width), _pack_rows(small_g_list, width),
                        _pack_rows(small_m_list, width), _pack_rows(small_v_list, width))
    unpacked = [_unpack_rows(o, shapes_local, width) for o in outs_small]
    ups = {name: [unpacked[kind][i] for kind in range(4)] for i, name in enumerate(small_names)}
    ups["w_in"], ups["w_out"], ups["pl_gate_w"], ups["pl_proj_w"] = up_in, up_out, up_gate, up_proj

    order = ["norm_g", "w_in", "w_out", "conv_w", "conv_b", "conv_ln_g", "conv_ln_b", "sgu_ln_g", "sgu_ln_b", "sgu_w",
             "sgu_b", "pl_norm_g", "pl_gate_w", "pl_proj_w", "final_g"]
    result = [loss, grad_x]
    for kind in range(4):
        result.extend(ups[name][kind] for name in order)
    return tuple(result)
```

```python
import functools

import jax
import jax.numpy as jnp
from jax import lax
from jax.experimental import pallas as pl
from jax.experimental.pallas import tpu as pltpu

F32 = jnp.float32
BF16 = jnp.bfloat16
SDS = jax.ShapeDtypeStruct

EPS = 1e-6
CHUNK = 128
GROUPS = 8
HALO = 32
N_CHIPS = 4
LANES = 128
SUBLANES = 8
V7X_VMEM_LIMIT = 56 << 20

ADAM_LR = 0.001
ADAM_B1 = 0.9
ADAM_B2 = 0.999
ADAM_EPS = 1e-08
ADAM_WD = 0.01
ADAM_STEP = 10

MESH_IDS = pl.DeviceIdType.MESH
ANY = pl.BlockSpec(memory_space=pl.ANY)


def _params(n_axes):
    return pltpu.CompilerParams(dimension_semantics=("arbitrary",) * n_axes, vmem_limit_bytes=V7X_VMEM_LIMIT)


def _const(shape):
    zeros = (0,) * len(shape)
    return pl.BlockSpec(shape, lambda *_: zeros)


def _layer(shape, layer):
    zeros = (0,) * len(shape)
    return pl.BlockSpec((None,) + tuple(shape), lambda *_: (layer,) + zeros)


class _Comm:
    def __init__(self, operands, out_shape, sems, start, finish):
        self.operands, self.out_shape, self.sems, self.start, self.finish = operands, out_shape, sems, start, finish


def _call(body, *, name, grid, in_specs, out_specs, out_shape, operands, scratch_shapes=(), comm=None):
    in_specs, out_specs, out_shape, scratch_shapes = list(in_specs), list(out_specs), list(out_shape), list(scratch_shapes)
    if comm is None:
        return pl.pallas_call(body, name=name, grid=grid, in_specs=in_specs, out_specs=out_specs, out_shape=out_shape,
                              scratch_shapes=scratch_shapes, compiler_params=_params(len(grid)))(*operands)
    n_in, n_out, n_sc = len(in_specs), len(out_specs), len(scratch_shapes)
    ci, co = len(comm.operands), len(comm.out_shape)

    def hosted(*refs):
        ins, cins = refs[:n_in], refs[n_in:n_in + ci]
        outs, couts = refs[n_in + ci:n_in + ci + n_out], refs[n_in + ci + n_out:n_in + ci + n_out + co]
        scratch = refs[n_in + ci + n_out + co:n_in + ci + n_out + co + n_sc]
        sems = refs[n_in + ci + n_out + co + n_sc:]
        first = functools.reduce(jnp.logical_and, [pl.program_id(a) == 0 for a in range(len(grid))])
        last = functools.reduce(jnp.logical_and, [pl.program_id(a) == g - 1 for a, g in enumerate(grid)])

        @pl.when(first)
        def _():
            comm.start(cins, couts, sems)

        body(*ins, *outs, *scratch)

        @pl.when(last)
        def _():
            comm.finish(cins, couts, sems)

    res = pl.pallas_call(
        hosted, name=name, grid=grid, in_specs=in_specs + [ANY] * ci, out_specs=out_specs + [ANY] * co,
        out_shape=out_shape + list(comm.out_shape), scratch_shapes=scratch_shapes + list(comm.sems),
        compiler_params=_params(len(grid)))(*operands, *comm.operands)
    return res[:n_out], res[n_out:]


def _run_comm(comm, name):
    ci, co = len(comm.operands), len(comm.out_shape)

    def body(*refs):
        comm.start(refs[:ci], refs[ci:ci + co], refs[ci + co:])
        comm.finish(refs[:ci], refs[ci:ci + co], refs[ci + co:])

    return pl.pallas_call(body, name=name, in_specs=[ANY] * ci, out_specs=[ANY] * co, out_shape=list(comm.out_shape),
                          scratch_shapes=list(comm.sems))(*comm.operands)


def _sigmoid(v):
    return jax.nn.sigmoid(v)


def _dsilu(v, s):
    return s * (1.0 + v * (1.0 - s))


def _gelu_parts(v):
    cdf = 0.5 * (1.0 + lax.erf(v * 0.7071067811865476))
    pdf = jnp.exp(-0.5 * v * v) * 0.3989422804014327
    return v * cdf, cdf + v * pdf


def _gelu(v):
    return 0.5 * v * (1.0 + lax.erf(v * 0.7071067811865476))


def _rms_stats(x):
    r = lax.rsqrt(jnp.mean(x * x, axis=-1, keepdims=True) + EPS)
    return r, x * r


def _rms_bwd(dy, g, r, xh):
    gdy = dy * g
    return r * (gdy - xh * jnp.mean(xh * gdy, axis=-1, keepdims=True))


def _ln_stats(x):
    mu = jnp.mean(x, axis=-1, keepdims=True)
    xc = x - mu
    rs = lax.rsqrt(jnp.mean(xc * xc, axis=-1, keepdims=True) + EPS)
    return rs, xc * rs


def _ln_bwd(dy, g, rs, xh):
    dxh = dy * g
    return rs * (dxh - jnp.mean(dxh, axis=-1, keepdims=True) - xh * jnp.mean(dxh * xh, axis=-1, keepdims=True))


def _dot(a, b):
    return jnp.dot(a, b, preferred_element_type=F32)


def _dot_nt(a, b):
    return lax.dot_general(a, b, (((1,), (1,)), ((), ())), preferred_element_type=F32)


def _dot_tn(a, b):
    return lax.dot_general(a, b, (((0,), (0,)), ((), ())), preferred_element_type=F32)


def _fwd_in(x, norm_g, w_in_full, layer, tm, comm=None):
    t, d = x.shape
    _, nk, _, n4 = w_in_full.shape

    def body(x_ref, g_ref, w_ref, h_ref, proj_ref):
        r, xh = _rms_stats(x_ref[...])
        h = (xh * g_ref[...]).astype(BF16)
        h_ref[...] = h
        for k in range(nk):
            proj_ref[:, k * n4:(k + 1) * n4] = _dot(h, w_ref[k]).astype(BF16)

    return _call(
        body, name="fwd_in", grid=(t // tm,),
        in_specs=[pl.BlockSpec((tm, d), lambda i: (i, 0)), _layer((1, d), layer), _layer((nk, d, n4), 0)],
        out_specs=[pl.BlockSpec((tm, d), lambda i: (i, 0)), pl.BlockSpec((tm, nk * n4), lambda i: (i, 0))],
        out_shape=[SDS((t, d), BF16), SDS((t, nk * n4), BF16)],
        operands=(x, norm_g, w_in_full), comm=comm)


def _halo_maps(nt, hb, n_halo_blocks):
    def prev(b, i):
        return (jnp.maximum((b * nt + i) * hb - 1, 0), 0)

    def nxt(b, i):
        return (jnp.minimum((b * nt + i + 1) * hb, n_halo_blocks - 1), 0)

    return prev, nxt


def _col_blocks(e):
    cb = min(2 * LANES, e)
    return cb, e // cb


def _conv_taps(src_ref, w_ref, dst_ref, cb_idx, n_rows, rb, first, reverse):
    k_taps = w_ref.shape[1] - 1
    for r0 in range(0, n_rows, rb):
        acc = None
        for res in range(SUBLANES):
            rows = rb + (SUBLANES if res else 0)
            group = None
            for k in range(k_taps):
                off = first + k
                if off % SUBLANES != res:
                    continue
                wk = w_ref[cb_idx, pl.ds((k_taps - 1 - k) if reverse else k, 1), :]
                term = wk * src_ref[cb_idx, pl.ds(r0 + off - res, rows), :]
                group = term if group is None else group + term
            if group is None:
                continue
            part = group[res:res + rb] if res else group
            acc = part if acc is None else acc + part
        dst_ref[cb_idx, pl.ds(r0, rb), :] = acc


def _fwd_conv(proj, conv_w_blk, conv_b, ln_g, ln_b, layer, bsz, seq, tm):
    t, e3 = proj.shape
    e = e3 // 3
    nt = seq // tm
    hb = tm // HALO
    cb, ncb = _col_blocks(e)
    rb = min(64, tm)
    kp = conv_w_blk.shape[2]
    prev, _ = _halo_maps(nt, hb, t // HALO)

    def body(proj_ref, halo_ref, w_ref, b_ref, g_ref, lb_ref, u_ref, y1_ref, y0s, y1s):
        i = pl.program_id(1)
        a = proj_ref[:, 0:e].astype(F32)
        b = proj_ref[:, e:2 * e].astype(F32)
        y0 = a * _sigmoid(b)
        ah = halo_ref[:, 0:e].astype(F32)
        bh = halo_ref[:, e:2 * e].astype(F32)
        y0h = jnp.where(i > 0, ah * _sigmoid(bh), 0.0)
        for c in range(ncb):
            y0s[c, 0:HALO, :] = y0h[:, c * cb:(c + 1) * cb]
            y0s[c, HALO:HALO + tm, :] = y0[:, c * cb:(c + 1) * cb]

        def per_block(c, carry):
            _conv_taps(y0s, w_ref, y1s, c, tm, rb, HALO - (kp - 2), False)
            return carry

        lax.fori_loop(0, ncb, per_block, 0)
        y1 = jnp.concatenate([y1s[c] for c in range(ncb)], axis=1) + b_ref[...]
        y1_ref[...] = y1
        rs, xh = _ln_stats(y1)
        y2 = xh * g_ref[...] + lb_ref[...]
        y = y2 * _sigmoid(y2)
        z = proj_ref[:, 2 * e:3 * e].astype(F32)
        u_ref[...] = (y * (z * _sigmoid(z))).astype(BF16)

    return pl.pallas_call(
        body, name="fwd_conv", grid=(bsz, nt),
        in_specs=[pl.BlockSpec((tm, e3), lambda b, i: (b * nt + i, 0)),
                  pl.BlockSpec((HALO, 2 * e), prev),
                  _layer((ncb, kp, cb), layer), _layer((1, e), layer), _layer((1, e), layer), _layer((1, e), layer)],
        out_specs=[pl.BlockSpec((tm, e), lambda b, i: (b * nt + i, 0)), pl.BlockSpec((tm, e), lambda b, i: (b * nt + i, 0))],
        out_shape=[SDS((t, e), BF16), SDS((t, e), F32)],
        scratch_shapes=[pltpu.VMEM((ncb, HALO + tm, cb), F32), pltpu.VMEM((ncb, tm, cb), F32)],
        compiler_params=_params(2),
    )(proj, proj, conv_w_blk, conv_b, ln_g, ln_b)


def _tril_mask():
    rows = lax.broadcasted_iota(jnp.int32, (CHUNK, CHUNK), 0)
    cols = lax.broadcasted_iota(jnp.int32, (CHUNK, CHUNK), 1)
    return rows >= cols


def _fwd_sgu(proj, ln_g, ln_b, sgu_w, sgu_bt, layer, tm):
    t, e3 = proj.shape
    e = e3 // 3
    gw = e // GROUPS
    nch = tm // CHUNK

    def body(proj_ref, g_ref, lb_ref, w_ref, bt_ref, u_ref, mixed):
        a = proj_ref[:, 0:e].astype(F32)
        b = proj_ref[:, e:2 * e].astype(F32)
        z = proj_ref[:, 2 * e:3 * e].astype(F32)
        rs, xh = _ln_stats(_gelu(b))
        v = (xh * g_ref[...] + lb_ref[...]).astype(BF16)
        mask = _tril_mask()
        for g in range(GROUPS):
            wm = jnp.where(mask, w_ref[g], 0.0).astype(BF16)
            bias = bt_ref[:, g:g + 1]
            for n in range(nch):
                blk = v[n * CHUNK:(n + 1) * CHUNK, g * gw:(g + 1) * gw]
                mixed[n * CHUNK:(n + 1) * CHUNK, g * gw:(g + 1) * gw] = _dot(wm, blk) + bias
        y = _gelu(a) * mixed[...]
        u_ref[...] = (y * (z * _sigmoid(z))).astype(BF16)

    return pl.pallas_call(
        body, name="fwd_sgu", grid=(t // tm,),
        in_specs=[pl.BlockSpec((tm, e3), lambda i: (i, 0)), _layer((1, e), layer), _layer((1, e), layer),
                  _layer((GROUPS, CHUNK, CHUNK), layer), _layer((CHUNK, GROUPS), layer)],
        out_specs=pl.BlockSpec((tm, e), lambda i: (i, 0)),
        out_shape=SDS((t, e), BF16),
        scratch_shapes=[pltpu.VMEM((tm, e), F32)],
        compiler_params=_params(1),
    )(proj, ln_g, ln_b, sgu_w, sgu_bt)


def _ple_forward(x1, p_ref, plg_ref, gw_ref, pw_ref):
    nk, _, dq = pw_ref.shape
    r, xh = _rms_stats(x1)
    rn = (xh * plg_ref[...]).astype(BF16)
    gate = _sigmoid(_dot(rn, gw_ref[...]))
    pb = p_ref[...].astype(BF16)
    q = jnp.concatenate([_dot(pb, pw_ref[k]) for k in range(nk)], axis=1)
    return r, xh, rn, gate, q


def _fwd_out(x, u, w_out_full, pl_norm_g, gate_w_full, p, proj_w_full, layer, tm):
    t, d = x.shape
    e = u.shape[1]
    ple = p.shape[-1]
    nk, dq = proj_w_full.shape[1], proj_w_full.shape[3]

    def body(x_ref, u_ref, wo_ref, plg_ref, gw_ref, p_ref, pw_ref, x1_ref, x2_ref):
        x1 = x_ref[...] + _dot(u_ref[...], wo_ref[...])
        x1_ref[...] = x1
        _, _, _, gate, q = _ple_forward(x1, p_ref, plg_ref, gw_ref, pw_ref)
        x2_ref[...] = x1 + gate * q

    return pl.pallas_call(
        body, name="fwd_out", grid=(t // tm,),
        in_specs=[pl.BlockSpec((tm, d), lambda i: (i, 0)), pl.BlockSpec((tm, e), lambda i: (i, 0)),
                  _layer((e, d), 0), _layer((1, d), layer), _layer((d, d), 0),
                  pl.BlockSpec((None, tm, ple), lambda i: (layer, i, 0)), _layer((nk, ple, dq), 0)],
        out_specs=[pl.BlockSpec((tm, d), lambda i: (i, 0)), pl.BlockSpec((tm, d), lambda i: (i, 0))],
        out_shape=[SDS((t, d), F32), SDS((t, d), F32)],
        compiler_params=_params(1),
    )(x, u, w_out_full, pl_norm_g, gate_w_full, p, proj_w_full)


def _loss_head(x, final_g, target, tm):
    t, d = x.shape
    n_steps = t // tm

    def body(x_ref, g_ref, tgt_ref, loss_ref, dx_ref, dg_ref, sq_acc):
        i = pl.program_id(0)

        @pl.when(i == 0)
        def _():
            sq_acc[...] = jnp.zeros_like(sq_acc)
            dg_ref[...] = jnp.zeros_like(dg_ref)

        g = g_ref[...]
        r, xh = _rms_stats(x_ref[...])
        diff = xh * g - tgt_ref[...]
        sq_acc[...] += jnp.sum(diff * diff, axis=0, keepdims=True)
        dout = diff * (1.0 / d)
        dg_ref[...] += jnp.sum(dout * xh, axis=0, keepdims=True)
        dx_ref[...] = _rms_bwd(dout, g, r, xh)

        @pl.when(i == n_steps - 1)
        def _():
            loss_ref[...] = jnp.sum(sq_acc[...], axis=1, keepdims=True) * (0.5 / d)

    return pl.pallas_call(
        body, name="loss_head", grid=(n_steps,),
        in_specs=[pl.BlockSpec((tm, d), lambda i: (i, 0)), _const((1, d)), pl.BlockSpec((tm, d), lambda i: (i, 0))],
        out_specs=[_const((1, 1)), pl.BlockSpec((tm, d), lambda i: (i, 0)), _const((1, d))],
        out_shape=[SDS((1, 1), F32), SDS((t, d), F32), SDS((1, d), F32)],
        scratch_shapes=[pltpu.VMEM((1, d), F32)],
        compiler_params=_params(1),
    )(x, final_g, target)


def _bwd_out(dx2, x1, p, pl_norm_g, gate_w_full, proj_w_full, w_out_full, layer, tm, comm=None):
    t, d = dx2.shape
    e = w_out_full.shape[1]
    ple = p.shape[-1]
    nk, dq_w = proj_w_full.shape[1], proj_w_full.shape[3]

    def body(dx2_ref, x1_ref, p_ref, plg_ref, gw_ref, pw_ref, wo_ref, dx1_ref, du_ref, rn_ref, ds_ref, dq_ref, dplg_ref):
        @pl.when(pl.program_id(0) == 0)
        def _():
            dplg_ref[...] = jnp.zeros_like(dplg_ref)

        dx2v = dx2_ref[...]
        r, xh, rn, gate, q = _ple_forward(x1_ref[...], p_ref, plg_ref, gw_ref, pw_ref)
        rn_ref[...] = rn
        dq_ref[...] = (dx2v * gate).astype(BF16)
        ds = (dx2v * q * gate * (1.0 - gate)).astype(BF16)
        ds_ref[...] = ds
        dr = _dot_nt(ds, gw_ref[...])
        dplg_ref[...] += jnp.sum(dr * xh, axis=0, keepdims=True)
        dx1 = dx2v + _rms_bwd(dr, plg_ref[...], r, xh)
        dx1_ref[...] = dx1
        du_ref[...] = _dot_nt(dx1.astype(BF16), wo_ref[...]).astype(BF16)

    row = lambda w: pl.BlockSpec((tm, w), lambda i: (i, 0))
    return _call(
        body, name="bwd_out", grid=(t // tm,),
        in_specs=[row(d), row(d), pl.BlockSpec((None, tm, ple), lambda i: (layer, i, 0)),
                  _layer((1, d), layer), _layer((d, d), 0), _layer((nk, ple, dq_w), 0), _layer((e, d), 0)],
        out_specs=[row(d), row(e), row(d), row(d), row(d), _const((1, d))],
        out_shape=[SDS((t, d), F32), SDS((t, e), BF16), SDS((t, d), BF16), SDS((t, d), BF16), SDS((t, d), BF16),
                   SDS((1, d), F32)],
        operands=(dx2, x1, p, pl_norm_g, gate_w_full, proj_w_full, w_out_full), comm=comm)


def _bwd_conv(du, proj, y1, conv_w_blk, ln_g, ln_b, layer, bsz, seq, tm):
    t, e3 = proj.shape
    e = e3 // 3
    nt = seq // tm
    hb = tm // HALO
    cb, ncb = _col_blocks(e)
    rb = min(64, tm)
    kp = conv_w_blk.shape[2]
    k_taps = kp - 1
    prev, nxt = _halo_maps(nt, hb, t // HALO)
    z_halo = lambda b, i: (nxt(b, i)[0], 2)

    def ln_silu_bwd(du_v, z_v, y1_v, g, lb):
        rs, xh = _ln_stats(y1_v)
        y2 = xh * g + lb
        sg = _sigmoid(y2)
        sz = _sigmoid(z_v)
        dy = du_v * (z_v * sz)
        dy2 = dy * _dsilu(y2, sg)
        return _ln_bwd(dy2, g, rs, xh), dy2, xh, du_v * (y2 * sg) * _dsilu(z_v, sz)

    def body(du_ref, proj_ref, y1_ref, duh_ref, zh_ref, y1h_ref, abh_ref, w_ref, g_ref, lb_ref,
             dproj_ref, dw_ref, dcb_ref, dg_ref, dlb_ref, y0s, dy1s, dy0s, ysh):
        b_id, i = pl.program_id(0), pl.program_id(1)

        @pl.when((b_id == 0) & (i == 0))
        def _():
            dw_ref[...] = jnp.zeros_like(dw_ref)
            dcb_ref[...] = jnp.zeros_like(dcb_ref)
            dg_ref[...] = jnp.zeros_like(dg_ref)
            dlb_ref[...] = jnp.zeros_like(dlb_ref)

        g, lb = g_ref[...], lb_ref[...]
        a = proj_ref[:, 0:e].astype(F32)
        b = proj_ref[:, e:2 * e].astype(F32)
        z = proj_ref[:, 2 * e:3 * e].astype(F32)
        sb = _sigmoid(b)
        y0 = a * sb
        dy1, dy2, xh, dz = ln_silu_bwd(du_ref[...].astype(F32), z, y1_ref[...], g, lb)
        dproj_ref[:, 2 * e:3 * e] = dz.astype(BF16)
        dg_ref[...] += jnp.sum(dy2 * xh, axis=0, keepdims=True)
        dlb_ref[...] += jnp.sum(dy2, axis=0, keepdims=True)
        dcb_ref[...] += jnp.sum(dy1, axis=0, keepdims=True)
        dy1h, _, _, _ = ln_silu_bwd(duh_ref[...].astype(F32), zh_ref[...].astype(F32), y1h_ref[...], g, lb)
        dy1h = jnp.where(i < nt - 1, dy1h, 0.0)
        ah = abh_ref[:, 0:e].astype(F32)
        bh = abh_ref[:, e:2 * e].astype(F32)
        y0h = jnp.where(i > 0, ah * _sigmoid(bh), 0.0)
        for c in range(ncb):
            cols = slice(c * cb, (c + 1) * cb)
            y0s[c, 0:HALO, :] = y0h[:, cols]
            y0s[c, HALO:HALO + tm, :] = y0[:, cols]
            dy1s[c, 0:tm, :] = dy1[:, cols]
            dy1s[c, tm:tm + HALO, :] = dy1h[:, cols]

        def per_block(c, carry):
            _conv_taps(dy1s, w_ref, dy0s, c, tm, rb, 0, True)
            for res in range(1, SUBLANES):
                ysh[res - 1] = y0s[c, pl.ds(res, tm + HALO - SUBLANES), :]
            for k in range(k_taps):
                off = HALO - (k_taps - 1) + k
                res = off % SUBLANES
                acc = None
                for r0 in range(0, tm, rb):
                    rows = pl.ds(r0 + off - res, rb)
                    shifted = ysh[res - 1, rows, :] if res else y0s[c, rows, :]
                    term = dy1s[c, pl.ds(r0, rb), :] * shifted
                    acc = term if acc is None else acc + term
                dw_ref[c, pl.ds(k, 1), :] += jnp.sum(acc, axis=0, keepdims=True)
            return carry

        lax.fori_loop(0, ncb, per_block, 0)
        dy0 = jnp.concatenate([dy0s[c] for c in range(ncb)], axis=1)
        dproj_ref[:, 0:e] = (dy0 * sb).astype(BF16)
        dproj_ref[:, e:2 * e] = (dy0 * a * sb * (1.0 - sb)).astype(BF16)

    tile = lambda w: pl.BlockSpec((tm, w), lambda b, i: (b * nt + i, 0))
    return pl.pallas_call(
        body, name="bwd_conv", grid=(bsz, nt),
        in_specs=[tile(e), tile(e3), tile(e),
                  pl.BlockSpec((HALO, e), nxt), pl.BlockSpec((HALO, e), z_halo), pl.BlockSpec((HALO, e), nxt),
                  pl.BlockSpec((HALO, 2 * e), prev),
                  _layer((ncb, kp, cb), layer), _layer((1, e), layer), _layer((1, e), layer)],
        out_specs=[tile(e3), _const((ncb, kp, cb)), _const((1, e)), _const((1, e)), _const((1, e))],
        out_shape=[SDS((t, e3), BF16), SDS((ncb, kp, cb), F32), SDS((1, e), F32), SDS((1, e), F32), SDS((1, e), F32)],
        scratch_shapes=[pltpu.VMEM((ncb, HALO + tm, cb), F32), pltpu.VMEM((ncb, tm + HALO, cb), F32),
                        pltpu.VMEM((ncb, tm, cb), F32), pltpu.VMEM((SUBLANES - 1, tm + HALO - SUBLANES, cb), F32)],
        compiler_params=_params(2),
    )(du, proj, y1, du, proj, y1, proj, conv_w_blk, ln_g, ln_b)


def _bwd_sgu(du, proj, ln_g, ln_b, sgu_w, sgu_bt, layer, tm):
    t, e3 = proj.shape
    e = e3 // 3
    gw = e // GROUPS
    nch = tm // CHUNK

    def body(du_ref, proj_ref, g_ref, lb_ref, w_ref, bt_ref, dproj_ref, dw_ref, dbt_ref, dg_ref, dlb_ref, mixed, dmix, dv):
        @pl.when(pl.program_id(0) == 0)
        def _():
            dw_ref[...] = jnp.zeros_like(dw_ref)
            dbt_ref[...] = jnp.zeros_like(dbt_ref)
            dg_ref[...] = jnp.zeros_like(dg_ref)
            dlb_ref[...] = jnp.zeros_like(dlb_ref)

        g, lb = g_ref[...], lb_ref[...]
        a = proj_ref[:, 0:e].astype(F32)
        b = proj_ref[:, e:2 * e].astype(F32)
        z = proj_ref[:, 2 * e:3 * e].astype(F32)
        ug, dug = _gelu_parts(a)
        vb, dvb_db = _gelu_parts(b)
        rs, xh = _ln_stats(vb)
        v = (xh * g + lb).astype(BF16)
        mask = _tril_mask()
        for gi in range(GROUPS):
            wm = jnp.where(mask, w_ref[gi], 0.0).astype(BF16)
            bias = bt_ref[:, gi:gi + 1]
            for n in range(nch):
                blk = v[n * CHUNK:(n + 1) * CHUNK, gi * gw:(gi + 1) * gw]
                mixed[n * CHUNK:(n + 1) * CHUNK, gi * gw:(gi + 1) * gw] = _dot(wm, blk) + bias
        mx = mixed[...]
        sz = _sigmoid(z)
        duv = du_ref[...].astype(F32)
        dy = duv * (z * sz)
        dproj_ref[:, 2 * e:3 * e] = (duv * (ug * mx) * _dsilu(z, sz)).astype(BF16)
        dproj_ref[:, 0:e] = (dy * mx * dug).astype(BF16)
        dmix[...] = dy * ug
        for gi in range(GROUPS):
            wm = jnp.where(mask, w_ref[gi], 0.0).astype(BF16)
            dw_acc = None
            db_acc = None
            for n in range(nch):
                rows, cols = slice(n * CHUNK, (n + 1) * CHUNK), slice(gi * gw, (gi + 1) * gw)
                dm = dmix[rows, cols]
                dmb = dm.astype(BF16)
                dw_n = _dot_nt(dmb, v[rows, cols])
                db_n = jnp.sum(dm, axis=1, keepdims=True)
                dw_acc = dw_n if dw_acc is None else dw_acc + dw_n
                db_acc = db_n if db_acc is None else db_acc + db_n
                dv[rows, cols] = _dot_tn(wm, dmb)
            dw_ref[gi] += jnp.where(mask, dw_acc, 0.0)
            dbt_ref[:, gi:gi + 1] += db_acc
        dvv = dv[...]
        dg_ref[...] += jnp.sum(dvv * xh, axis=0, keepdims=True)
        dlb_ref[...] += jnp.sum(dvv, axis=0, keepdims=True)
        dproj_ref[:, e:2 * e] = (_ln_bwd(dvv, g, rs, xh) * dvb_db).astype(BF16)

    return pl.pallas_call(
        body, name="bwd_sgu", grid=(t // tm,),
        in_specs=[pl.BlockSpec((tm, e), lambda i: (i, 0)), pl.BlockSpec((tm, e3), lambda i: (i, 0)),
                  _layer((1, e), layer), _layer((1, e), layer),
                  _layer((GROUPS, CHUNK, CHUNK), layer), _layer((CHUNK, GROUPS), layer)],
        out_specs=[pl.BlockSpec((tm, e3), lambda i: (i, 0)), _const((GROUPS, CHUNK, CHUNK)), _const((CHUNK, GROUPS)),
                   _const((1, e)), _const((1, e))],
        out_shape=[SDS((t, e3), BF16), SDS((GROUPS, CHUNK, CHUNK), F32), SDS((CHUNK, GROUPS), F32),
                   SDS((1, e), F32), SDS((1, e), F32)],
        scratch_shapes=[pltpu.VMEM((tm, e), F32), pltpu.VMEM((tm, e), F32), pltpu.VMEM((tm, e), F32)],
        compiler_params=_params(1),
    )(du, proj, ln_g, ln_b, sgu_w, sgu_bt)


def _bwd_in(dproj, dx1, x, norm_g, w_in_full, layer, tm, comm=None):
    t, d = x.shape
    _, nk, _, n4 = w_in_full.shape

    def body(dproj_ref, dx1_ref, x_ref, g_ref, w_ref, dx_ref, dg_ref):
        @pl.when(pl.program_id(0) == 0)
        def _():
            dg_ref[...] = jnp.zeros_like(dg_ref)

        dh = None
        for k in range(nk):
            part = _dot_nt(dproj_ref[:, k * n4:(k + 1) * n4], w_ref[k])
            dh = part if dh is None else dh + part
        r, xh = _rms_stats(x_ref[...])
        dg_ref[...] += jnp.sum(dh * xh, axis=0, keepdims=True)
        dx_ref[...] = dx1_ref[...] + _rms_bwd(dh, g_ref[...], r, xh)

    row = lambda w: pl.BlockSpec((tm, w), lambda i: (i, 0))
    return _call(
        body, name="bwd_in", grid=(t // tm,),
        in_specs=[row(nk * n4), row(d), row(d), _layer((1, d), layer), _layer((nk, d, n4), 0)],
        out_specs=[row(d), _const((1, d))],
        out_shape=[SDS((t, d), F32), SDS((1, d), F32)],
        operands=(dproj, dx1, x, norm_g, w_in_full), comm=comm)


def _wgrad(a, b, kblk, nblk, tm, name, a_layer=None):
    t, n = b.shape
    k = a.shape[-1]
    kw, nw = k // kblk, n // nblk
    n_steps = t // tm

    def body(a_ref, b_ref, o_ref):
        @pl.when(pl.program_id(2) == 0)
        def _():
            o_ref[...] = jnp.zeros_like(o_ref)

        o_ref[...] += _dot_tn(a_ref[...].astype(BF16), b_ref[...].astype(BF16))

    if a_layer is None:
        a_spec = pl.BlockSpec((tm, kw), lambda kb, nb, i: (i, kb))
    else:
        a_spec = pl.BlockSpec((None, tm, kw), lambda kb, nb, i: (a_layer, i, kb))
    return pl.pallas_call(
        body, name=name, grid=(kblk, nblk, n_steps),
        in_specs=[a_spec, pl.BlockSpec((tm, nw), lambda kb, nb, i: (i, nb))],
        out_specs=pl.BlockSpec((None, None, kw, nw), lambda kb, nb, i: (kb, nb, 0, 0)),
        out_shape=SDS((kblk, nblk, kw, nw), F32),
        compiler_params=_params(3),
    )(a, b)


def _row_tile(rows, cols, budget_bytes=1 << 20):
    best = None
    for cand in range(SUBLANES, rows + 1, SUBLANES):
        if rows % cand == 0 and cand * cols * 4 <= budget_bytes:
            best = cand
    return best if best is not None else rows


def _pair_sum(grads, recv, my_c, wire_dtype):
    n, _, h, c = grads.shape
    th = _row_tile(h, c)
    two = wire_dtype != F32

    def body(c_ref, g_ref, r_ref, o_ref, *wire_ref):
        total = g_ref[...] + r_ref[...]
        o_ref[...] = total
        if two:
            wire_ref[0][...] = total.astype(wire_dtype)

    out_spec = pl.BlockSpec((None, th, c), lambda j, i, c_ref: (j, i, 0))
    grid_spec = pltpu.PrefetchScalarGridSpec(
        num_scalar_prefetch=1, grid=(n, h // th),
        in_specs=[pl.BlockSpec((None, None, th, c), lambda j, i, c_ref: (j, c_ref[0], i, 0)),
                  pl.BlockSpec((None, None, th, c), lambda j, i, c_ref: (j, 0, i, 0))],
        out_specs=[out_spec, out_spec] if two else [out_spec])
    out_shape = [SDS((n, h, c), F32)] + ([SDS((n, h, c), wire_dtype)] if two else [])
    outs = pl.pallas_call(body, name="pair_sum", grid_spec=grid_spec, out_shape=out_shape,
                          compiler_params=_params(2))(my_c, grads, recv)
    return outs[0], outs[-1]


def _chip_sum(pair, recv, my_k, stacked, layer):
    _, _, h, c = pair.shape
    th = _row_tile(h, c)

    def body(k_ref, own_ref, r1_ref, r2_ref, r3_ref, acc_ref, o_ref):
        acc = own_ref[...]
        for ref in (r1_ref, r2_ref, r3_ref):
            acc = acc + ref[...].astype(F32)
        o_ref[...] = acc

    def slot(flip):
        return pl.BlockSpec((None, None, th, c), lambda i, k_ref: (0, jnp.bitwise_xor(k_ref[0], flip), i, 0))

    grid_spec = pltpu.PrefetchScalarGridSpec(
        num_scalar_prefetch=1, grid=(h // th,),
        in_specs=[slot(0), slot(1), slot(2), slot(3), ANY],
        out_specs=pl.BlockSpec((None, th, c), lambda i, k_ref: (layer, i, 0)))
    return pl.pallas_call(body, name="chip_sum", grid_spec=grid_spec, out_shape=SDS(stacked.shape, F32),
                          input_output_aliases={5: 0}, compiler_params=_params(1))(my_k, pair, recv, recv, recv, stacked)


def _adam_math(w, gv, m, v):
    c1 = 1.0 - ADAM_B1 ** ADAM_STEP
    c2 = 1.0 - ADAM_B2 ** ADAM_STEP
    mn = ADAM_B1 * m + (1.0 - ADAM_B1) * gv
    vn = ADAM_B2 * v + (1.0 - ADAM_B2) * (gv * gv)
    m_hat = mn / c1
    v_hat = vn / c2
    return -ADAM_LR * (m_hat / (jnp.sqrt(v_hat) + ADAM_EPS) + ADAM_WD * w), mn, vn


def _adamw_halves(w, g_mine, g_theirs, m, v, my_c):
    nl, rows, cols = w.shape
    h = rows // 2
    th = _row_tile(h, cols, 512 << 10)
    as4 = lambda a: a.reshape(nl, 2, h, cols)

    def body(c_ref, w_ref, gm_ref, gt_ref, m_ref, v_ref, go_ref, d_ref, mo_ref, vo_ref):
        gv = jnp.where(pl.program_id(1) == c_ref[0], gm_ref[...], gt_ref[...])
        go_ref[...] = gv
        d_ref[...], mo_ref[...], vo_ref[...] = _adam_math(w_ref[...], gv, m_ref[...], v_ref[...])

    full = pl.BlockSpec((None, None, th, cols), lambda j, s, i, c_ref: (j, s, i, 0))
    half = pl.BlockSpec((None, th, cols), lambda j, s, i, c_ref: (j, i, 0))
    grid_spec = pltpu.PrefetchScalarGridSpec(
        num_scalar_prefetch=1, grid=(nl, 2, h // th),
        in_specs=[full, half, half, full, full], out_specs=[full] * 4)
    outs = pl.pallas_call(body, name="adamw_halves", grid_spec=grid_spec, out_shape=[SDS((nl, 2, h, cols), F32)] * 4,
                          compiler_params=_params(3))(my_c, as4(w), g_mine, g_theirs, as4(m), as4(v))
    return [o.reshape(nl, rows, cols) for o in outs]


def _adamw(w, g, m, v):
    rows, cols = w.shape
    tr = _row_tile(rows, cols, 512 << 10)

    def body(w_ref, g_ref, m_ref, v_ref, go_ref, d_ref, mo_ref, vo_ref):
        gv = g_ref[...]
        go_ref[...] = gv
        d_ref[...], mo_ref[...], vo_ref[...] = _adam_math(w_ref[...], gv, m_ref[...], v_ref[...])

    spec = pl.BlockSpec((tr, cols), lambda i: (i, 0))
    return pl.pallas_call(
        body, name="adamw", grid=(rows // tr,), in_specs=[spec] * 4, out_specs=[spec] * 4,
        out_shape=[SDS((rows, cols), F32)] * 4, compiler_params=_params(1))(w, g, m, v)


def _place():
    x, y, c = lax.axis_index("x"), lax.axis_index("y"), lax.axis_index("c")
    chips = [(1 - x, y), (x, 1 - y), (1 - x, 1 - y)]
    return x, y, c, 2 * x + y, chips


def _remote(src, dst, send_sem, recv_sem, device):
    return pltpu.make_async_remote_copy(src_ref=src, dst_ref=dst, send_sem=send_sem, recv_sem=recv_sem,
                                        device_id=device, device_id_type=MESH_IDS)


def _gather_comm(shards, layer, small=None):
    n = len(shards)
    extra = 0 if small is None else 1

    def copies(ins, outs, sems):
        ici_send, ici_recv, d2d_send, d2d_recv, own_send, own_recv = sems
        x, y, c, k, chips = _place()
        sibling = (x, y, 1 - c)
        own, ici_out, ici_in, fwd_out, fwd_in = [], [], [], [], []
        for j in range(n):
            h = ins[j].shape[2] // 2
            mine, theirs = pl.ds(c * h, h), pl.ds((1 - c) * h, h)
            own.append(_remote(ins[j].at[pl.ds(layer, 1)], outs[j].at[:, pl.ds(k, 1)], own_send.at[j], own_recv.at[j], sibling))
            for ti, (cx, cy) in enumerate(chips):
                s = 3 * j + ti
                ici_out.append(_remote(ins[j].at[pl.ds(layer, 1), :, mine], outs[j].at[:, pl.ds(k, 1), mine],
                                       ici_send.at[s], ici_recv.at[s], (cx, cy, c)))
                landed = outs[j].at[:, pl.ds(2 * cx + cy, 1), mine]
                ici_in.append(_remote(landed, landed, ici_send.at[s], ici_recv.at[s], (cx, cy, c)))
                fwd_out.append(_remote(landed, landed, d2d_send.at[s], d2d_recv.at[s], sibling))
                passed = outs[j].at[:, pl.ds(2 * cx + cy, 1), theirs]
                fwd_in.append(_remote(passed, passed, d2d_send.at[s], d2d_recv.at[s], sibling))
        if extra:
            own.append(_remote(ins[n], outs[n].at[pl.ds(k, 1)], own_send.at[n], own_recv.at[n], sibling))
            for ti, (cx, cy) in enumerate(chips):
                s = 3 * n + ti
                ici_out.append(_remote(ins[n], outs[n].at[pl.ds(k, 1)], ici_send.at[s], ici_recv.at[s], (cx, cy, c)))
                slot = outs[n].at[pl.ds(2 * cx + cy, 1)]
                ici_in.append(_remote(slot, slot, ici_send.at[s], ici_recv.at[s], (cx, cy, c)))
        return own, ici_out, ici_in, fwd_out, fwd_in

    def start(ins, outs, sems):
        own, ici_out, _, _, _ = copies(ins, outs, sems)
        for cp in own + ici_out:
            cp.start()

    def finish(ins, outs, sems):
        own, ici_out, ici_in, fwd_out, fwd_in = copies(ins, outs, sems)
        for idx, cp in enumerate(ici_in):
            cp.wait_recv()
            if idx < len(fwd_out):
                fwd_out[idx].start()
        for cp in fwd_in:
            cp.wait_recv()
        for cp in ici_out + fwd_out:
            cp.wait_send()
        for cp in own:
            cp.wait()

    operands = list(shards) + ([small] if extra else [])
    out_shape = [SDS((1, N_CHIPS) + s.shape[2:], s.dtype) for s in shards]
    if extra:
        out_shape.append(SDS((N_CHIPS,) + small.shape[1:], small.dtype))
    sems = [pltpu.SemaphoreType.DMA((3 * (n + extra),)), pltpu.SemaphoreType.DMA((3 * (n + extra),)),
            pltpu.SemaphoreType.DMA((3 * n,)), pltpu.SemaphoreType.DMA((3 * n,)),
            pltpu.SemaphoreType.DMA((n + extra,)), pltpu.SemaphoreType.DMA((n + extra,))]
    return _Comm(operands, out_shape, sems, start, finish)


def _swap_comm(grads):
    n = len(grads)

    def copies(ins, outs, sems):
        send_sem, recv_sem = sems
        x, y, c, _, _ = _place()
        return [_remote(ins[j].at[:, pl.ds(1 - c, 1)], outs[j], send_sem.at[j], recv_sem.at[j], (x, y, 1 - c))
                for j in range(n)]

    def start(ins, outs, sems):
        for cp in copies(ins, outs, sems):
            cp.start()

    def finish(ins, outs, sems):
        for cp in copies(ins, outs, sems):
            cp.wait()

    out_shape = [SDS((g.shape[0], 1) + g.shape[2:], g.dtype) for g in grads]
    return _Comm(list(grads), out_shape, [pltpu.SemaphoreType.DMA((n,)), pltpu.SemaphoreType.DMA((n,))], start, finish)


def _scatter_comm(sums):
    n = len(sums)

    def copies(ins, outs, sems):
        send_sem, recv_sem = sems
        x, y, c, k, chips = _place()
        out, landing = [], []
        for j in range(n):
            for ti, (cx, cy) in enumerate(chips):
                s = 3 * j + ti
                out.append(_remote(ins[j].at[:, pl.ds(2 * cx + cy, 1)], outs[j].at[:, pl.ds(k, 1)],
                                   send_sem.at[s], recv_sem.at[s], (cx, cy, c)))
                slot = outs[j].at[:, pl.ds(2 * cx + cy, 1)]
                landing.append(_remote(slot, slot, send_sem.at[s], recv_sem.at[s], (cx, cy, c)))
        return out, landing

    def start(ins, outs, sems):
        for cp in copies(ins, outs, sems)[0]:
            cp.start()

    def finish(ins, outs, sems):
        out, landing = copies(ins, outs, sems)
        for cp in landing:
            cp.wait_recv()
        for cp in out:
            cp.wait_send()

    out_shape = [SDS(s.shape, s.dtype) for s in sums]
    return _Comm(list(sums), out_shape, [pltpu.SemaphoreType.DMA((3 * n,)), pltpu.SemaphoreType.DMA((3 * n,))], start, finish)


def _swap_pieces(pieces):
    n = len(pieces)

    def body(*refs):
        ins, outs = refs[:n], refs[n:2 * n]
        send_sem, recv_sem = refs[2 * n:]
        x, y, c, _, _ = _place()
        copies = [_remote(ins[j], outs[j], send_sem.at[j], recv_sem.at[j], (x, y, 1 - c)) for j in range(n)]
        for cp in copies:
            cp.start()
        for cp in copies:
            cp.wait()

    return pl.pallas_call(
        body, name="swap_pieces", in_specs=[ANY] * n, out_specs=[ANY] * n,
        out_shape=[SDS(p.shape, p.dtype) for p in pieces],
        scratch_shapes=[pltpu.SemaphoreType.DMA((n,)), pltpu.SemaphoreType.DMA((n,))],
    )(*pieces)


def _gather_pieces(piece):
    def body(in_ref, out_ref, send_sem, recv_sem):
        x, y, c, k, chips = _place()
        peers = [(x, y, 1 - c)] + [(cx, cy, pc) for (cx, cy) in chips for pc in (c, 1 - c)]
        copies = []
        for ti, peer in enumerate(peers):
            cp = _remote(in_ref, out_ref.at[pl.ds(k, 1), pl.ds(c, 1)], send_sem.at[ti], recv_sem.at[ti], peer)
            cp.start()
            copies.append(cp)
        for ti, (px, py, pc) in enumerate(peers):
            _remote(in_ref, out_ref.at[pl.ds(2 * px + py, 1), pl.ds(pc, 1)], send_sem.at[ti], recv_sem.at[ti],
                    (px, py, pc)).wait_recv()
        for cp in copies:
            cp.wait_send()

    n_peers = 2 * N_CHIPS - 1
    return pl.pallas_call(
        body, name="gather_pieces", in_specs=[ANY], out_specs=ANY,
        out_shape=SDS((N_CHIPS, 2) + piece.shape[2:], piece.dtype),
        scratch_shapes=[pltpu.SemaphoreType.DMA((n_peers,)), pltpu.SemaphoreType.DMA((n_peers,))],
    )(piece)


def _pack_rows(parts, width, total_rows=None):
    rows = []
    for a in parts:
        a2 = a.reshape(-1, width)
        pad = (-a2.shape[0]) % SUBLANES
        rows.append(jnp.pad(a2, ((0, pad), (0, 0))) if pad else a2)
    out = jnp.concatenate(rows, axis=0)
    if total_rows is not None and out.shape[0] < total_rows:
        out = jnp.pad(out, ((0, total_rows - out.shape[0]), (0, 0)))
    return out


def _unpack_rows(packed, shapes, width):
    out, r = [], 0
    for shp in shapes:
        size = 1
        for s in shp:
            size *= s
        nr = size // width
        out.append(packed[r:r + nr].reshape(shp))
        r += nr + ((-nr) % SUBLANES)
    return out


def kernel(x, p, norm_g, w_in, w_out, conv_w, conv_b, conv_ln_g, conv_ln_b, sgu_ln_g, sgu_ln_b, sgu_w, sgu_b, pl_norm_g, pl_gate_w, pl_proj_w, final_g, loss_target, m_norm_g, m_w_in, m_w_out, m_conv_w, m_conv_b, m_conv_ln_g, m_conv_ln_b, m_sgu_ln_g, m_sgu_ln_b, m_sgu_w, m_sgu_b, m_pl_norm_g, m_pl_gate_w, m_pl_proj_w, m_final_g, v_norm_g, v_w_in, v_w_out, v_conv_w, v_conv_b, v_conv_ln_g, v_conv_ln_b, v_sgu_ln_g, v_sgu_ln_b, v_sgu_w, v_sgu_b, v_pl_norm_g, v_pl_gate_w, v_pl_proj_w, v_final_g):
    bsz, seq, d = x.shape
    depth = w_in.shape[0]
    e = w_out.shape[1] * N_CHIPS
    e3 = 3 * e
    n4 = w_in.shape[2]
    ple = p.shape[-1]
    dq = pl_proj_w.shape[2]
    k_taps = conv_w.shape[1]
    kp = k_taps + 1
    n_conv, n_sgu = conv_w.shape[0], sgu_ln_g.shape[0]
    t = bsz * seq
    tm_mm = min(512, seq)
    tm_mix = min(256, seq)
    cb, ncb = _col_blocks(e)
    my_c = lax.axis_index("c")
    my_k = 2 * lax.axis_index("x") + lax.axis_index("y")

    ec = e // N_CHIPS
    small_w = _pack_rows([conv_w.reshape(n_conv * k_taps, ec), sgu_ln_g, sgu_ln_b], ec)[None]
    shards = [w_in.astype(BF16)[:, None], w_out.astype(BF16)[:, None], pl_gate_w.astype(BF16)[:, None],
              pl_proj_w.astype(BF16)[:, None]]
    *layer_w, small_f = _run_comm(_gather_comm(shards, 0, small_w), "gather_first")
    conv_w_rows, sgu_g_rows, sgu_b_rows = _unpack_rows(
        jnp.transpose(small_f, (1, 0, 2)).reshape(small_f.shape[1], e),
        [(n_conv * k_taps, e), (n_sgu, e), (n_sgu, e)], e)
    conv_w_full = conv_w_rows.reshape(n_conv, k_taps, e)
    conv_w_blk = jnp.transpose(jnp.pad(conv_w_full, ((0, 0), (0, 1), (0, 0))).reshape(n_conv, kp, ncb, cb), (0, 2, 1, 3))
    sgu_ln_g_full = sgu_g_rows.reshape(n_sgu, 1, e)
    sgu_ln_b_full = sgu_b_rows.reshape(n_sgu, 1, e)
    sgu_bt = jnp.transpose(sgu_b, (0, 2, 1))

    norm_g3 = norm_g[:, None]
    pl_norm_g3 = pl_norm_g[:, None]
    conv_b3, conv_ln_g3, conv_ln_b3 = conv_b[:, None], conv_ln_g[:, None], conv_ln_b[:, None]
    p3 = p.reshape(depth, t, ple)

    def weights_of(gathered):
        w_in_l, w_out_l, gate_l, proj_l = gathered
        return w_in_l, w_out_l.reshape(1, e, d), gate_l.reshape(1, d, d), proj_l

    xs, hs, projs, us, x1s, y1s, weights = [], [], [], [], [], {}, []
    xc = x.reshape(t, d)
    for l in range(depth):
        j = l // 2
        w_in_l, w_out_l, gate_l, proj_l = weights_of(layer_w)
        weights.append((w_in_l, w_out_l, gate_l, proj_l))
        xs.append(xc)
        if l + 1 < depth:
            (h, proj), layer_w = _fwd_in(xc, norm_g3, w_in_l, l, tm_mm, _gather_comm(shards, l + 1))
        else:
            h, proj = _fwd_in(xc, norm_g3, w_in_l, l, tm_mm)
        if l % 2 == 0:
            u, y1 = _fwd_conv(proj, conv_w_blk, conv_b3, conv_ln_g3, conv_ln_b3, j, bsz, seq, tm_mix)
            y1s[l] = y1
        else:
            u = _fwd_sgu(proj, sgu_ln_g_full, sgu_ln_b_full, sgu_w, sgu_bt, j, tm_mix)
        x1, xc = _fwd_out(xc, u, w_out_l, pl_norm_g3, gate_l, p3, proj_l, l, tm_mm)
        hs.append(h)
        projs.append(proj)
        us.append(u)
        x1s.append(x1)

    loss_local, dx, d_final_g = _loss_head(xc, final_g[None], loss_target.reshape(t, d), tm_mm)
    loss = lax.psum(loss_local[0, 0], ("x", "y", "c"))

    c_arr = my_c.astype(jnp.int32).reshape(1)
    k_arr = my_k.astype(jnp.int32).reshape(1)
    by_chip = lambda a: a.reshape((1, N_CHIPS) + a.shape[1:])
    d_norm_g, d_pl_norm_g = [None] * depth, [None] * depth
    d_conv = [None] * n_conv
    d_sgu = [None] * n_sgu
    pairs, arrived = [None] * depth, [None] * depth
    in_flight = None
    for l in reversed(range(depth)):
        j = l // 2
        w_in_l, w_out_l, gate_l, proj_l = weights[l]
        if in_flight is None:
            dx1, du, rn, ds, dqv, d_pl_norm_g[l] = _bwd_out(dx, x1s[l], p3, pl_norm_g3, gate_l, proj_l, w_out_l, l, tm_mm)
        else:
            (dx1, du, rn, ds, dqv, d_pl_norm_g[l]), arrived[in_flight[0]] = _bwd_out(
                dx, x1s[l], p3, pl_norm_g3, gate_l, proj_l, w_out_l, l, tm_mm, _scatter_comm(in_flight[1]))
        g_proj = _wgrad(p3, dqv, 1, N_CHIPS, tm_mm, "wgrad_proj", a_layer=l)
        g_gate = _wgrad(rn, ds, 1, 1, tm_mm, "wgrad_gate")
        g_out = _wgrad(us[l], dx1, N_CHIPS, 1, tm_mm, "wgrad_out")
        if l % 2 == 0:
            dproj, dcw, dcb, dlg, dlb = _bwd_conv(du, projs[l], y1s[l], conv_w_blk, conv_ln_g3, conv_ln_b3, j, bsz, seq, tm_mix)
            d_conv[j] = (dcw, dcb, dlg, dlb)
        else:
            dproj, dsw, dsbt, dlg, dlb = _bwd_sgu(du, projs[l], sgu_ln_g_full, sgu_ln_b_full, sgu_w, sgu_bt, j, tm_mix)
            d_sgu[j] = (dsw, dsbt, dlg, dlb)
        g_in = _wgrad(hs[l], dproj, 1, N_CHIPS, tm_mm, "wgrad_in")
        local = [g_in.reshape(N_CHIPS, 2, d // 2, n4), g_out.reshape(N_CHIPS, 2, e // (2 * N_CHIPS), d),
                 g_gate.reshape(N_CHIPS, 2, d // (2 * N_CHIPS), d), g_proj.reshape(N_CHIPS, 2, ple // 2, dq)]
        (dx, d_norm_g[l]), from_sibling = _bwd_in(dproj, dx1, xs[l], norm_g3, w_in_l, l, tm_mm, _swap_comm(local))
        sums = [_pair_sum(gl, rc, c_arr, BF16) for gl, rc in zip(local, from_sibling)]
        pairs[l] = [by_chip(s32) for s32, _ in sums]
        in_flight = (l, [by_chip(wire) for _, wire in sums])
    grad_x = dx.reshape(bsz, seq, d)

    d_conv_w = jnp.stack([jnp.transpose(dc[0], (1, 0, 2)).reshape(kp, e)[:k_taps] for dc in d_conv])
    d_conv_b = jnp.stack([dc[1][0] for dc in d_conv])
    d_conv_ln_g = jnp.stack([dc[2][0] for dc in d_conv])
    d_conv_ln_b = jnp.stack([dc[3][0] for dc in d_conv])
    d_sgu_w = jnp.stack([dsg[0] for dsg in d_sgu])
    d_sgu_b = jnp.stack([jnp.transpose(dsg[1]) for dsg in d_sgu])
    d_sgu_ln_g = jnp.stack([dsg[2][0] for dsg in d_sgu])
    d_sgu_ln_b = jnp.stack([dsg[3][0] for dsg in d_sgu])
    small_grads = [jnp.concatenate(d_norm_g), d_conv_w, d_conv_b, d_conv_ln_g, d_conv_ln_b, d_sgu_ln_g, d_sgu_ln_b,
                   d_sgu_w, d_sgu_b, jnp.concatenate(d_pl_norm_g), d_final_g]
    small_shapes = [a.shape for a in small_grads]
    packed = _pack_rows(small_grads, d)
    pack_rows = packed.shape[0] + ((-packed.shape[0]) % (8 * SUBLANES))
    packed = _pack_rows(small_grads, d, pack_rows)
    gl_small = packed.reshape(N_CHIPS, 2, pack_rows // 8, d)
    (small_sibling,) = _run_comm(_swap_comm([gl_small]), "swap_small")
    small_pair, _ = _pair_sum(gl_small, small_sibling, c_arr, F32)
    small_pair = by_chip(small_pair)

    *arrived[in_flight[0]], small_arrived = _run_comm(_scatter_comm(in_flight[1] + [small_pair]), "scatter_last")
    reduced = [lax.empty((depth,) + pr.shape[2:], F32) for pr in pairs[0]]
    for l in range(depth):
        reduced = [_chip_sum(pr, ar, k_arr, acc, l) for pr, ar, acc in zip(pairs[l], arrived[l], reduced)]
    small_mine = _chip_sum(small_pair, small_arrived, k_arr, lax.empty((1,) + small_pair.shape[2:], F32), 0)
    theirs = _swap_pieces(reduced)
    small_all = _gather_pieces(small_mine[:, None])
    small_all = lax.dynamic_update_slice(small_all, small_mine[:, None], (my_k, my_c, 0, 0)).reshape(pack_rows, d)
    small_red = _unpack_rows(small_all, small_shapes, d)
    (gr_norm_g, gr_conv_w, gr_conv_b, gr_conv_ln_g, gr_conv_ln_b, gr_sgu_ln_g, gr_sgu_ln_b, gr_sgu_w, gr_sgu_b,
     gr_pl_norm_g, gr_final_g) = small_red
    gr_final_g = gr_final_g.reshape(d)
    gr_conv_w = lax.dynamic_slice_in_dim(gr_conv_w, my_k * ec, ec, axis=2)
    gr_sgu_ln_g = lax.dynamic_slice_in_dim(gr_sgu_ln_g, my_k * ec, ec, axis=1)
    gr_sgu_ln_b = lax.dynamic_slice_in_dim(gr_sgu_ln_b, my_k * ec, ec, axis=1)

    up_in = _adamw_halves(w_in, reduced[0], theirs[0], m_w_in, v_w_in, c_arr)
    up_out = _adamw_halves(w_out, reduced[1], theirs[1], m_w_out, v_w_out, c_arr)
    up_gate = _adamw_halves(pl_gate_w, reduced[2], theirs[2], m_pl_gate_w, v_pl_gate_w, c_arr)
    up_proj = _adamw_halves(pl_proj_w, reduced[3], theirs[3], m_pl_proj_w, v_pl_proj_w, c_arr)

    small_names = ["norm_g", "conv_w", "conv_b", "conv_ln_g", "conv_ln_b", "sgu_ln_g", "sgu_ln_b", "sgu_w", "sgu_b",
                   "pl_norm_g", "final_g"]
    small_w_list = [norm_g, conv_w, conv_b, conv_ln_g, conv_ln_b, sgu_ln_g, sgu_ln_b, sgu_w, sgu_b, pl_norm_g, final_g]
    small_m_list = [m_norm_g, m_conv_w, m_conv_b, m_conv_ln_g, m_conv_ln_b, m_sgu_ln_g, m_sgu_ln_b, m_sgu_w, m_sgu_b,
                    m_pl_norm_g, m_final_g]
    small_v_list = [v_norm_g, v_conv_w, v_conv_b, v_conv_ln_g, v_conv_ln_b, v_sgu_ln_g, v_sgu_ln_b, v_sgu_w, v_sgu_b,
                    v_pl_norm_g, v_final_g]
    small_g_list = [gr_norm_g, gr_conv_w, gr_conv_b, gr_conv_ln_g, gr_conv_ln_b, gr_sgu_ln_g, gr_sgu_ln_b, gr_sgu_w,
                    gr_sgu_b, gr_pl_norm_g, gr_final_g]
    width = ec
    shapes_local = [a.shape for a in small_w_list]
    outs_small = _adamw(_pack_rows(small_w_list, width), _pack_rows(small_g_list, width),
                        _pack_rows(small_m_list, width), _pack_rows(small_v_list, width))
    unpacked = [_unpack_rows(o, shapes_local, width) for o in outs_small]
    ups = {name: [unpacked[kind][i] for kind in range(4)] for i, name in enumerate(small_names)}
    ups["w_in"], ups["w_out"], ups["pl_gate_w"], ups["pl_proj_w"] = up_in, up_out, up_gate, up_proj

    order = ["norm_g", "w_in", "w_out", "conv_w", "conv_b", "conv_ln_g", "conv_ln_b", "sgu_ln_g", "sgu_ln_b", "sgu_w",
             "sgu_b", "pl_norm_g", "pl_gate_w", "pl_proj_w", "final_g"]
    result = [loss, grad_x]
    for kind in range(4):
        result.extend(ups[name][kind] for name in order)
    return tuple(result)
```

```python
import functools

import jax
import jax.numpy as jnp
from jax import lax
from jax.experimental import pallas as pl
from jax.experimental.pallas import tpu as pltpu

F32 = jnp.float32
BF16 = jnp.bfloat16
SDS = jax.ShapeDtypeStruct

EPS = 1e-6
CHUNK = 128
GROUPS = 8
HALO = 32
N_CHIPS = 4
LANES = 128
SUBLANES = 8
V7X_VMEM_LIMIT = 56 << 20

ADAM_LR = 0.001
ADAM_B1 = 0.9
ADAM_B2 = 0.999
ADAM_EPS = 1e-08
ADAM_WD = 0.01
ADAM_STEP = 10

MESH_IDS = pl.DeviceIdType.MESH
ANY = pl.BlockSpec(memory_space=pl.ANY)


def _params(n_axes):
    return pltpu.CompilerParams(dimension_semantics=("arbitrary",) * n_axes, vmem_limit_bytes=V7X_VMEM_LIMIT)


def _const(shape):
    zeros = (0,) * len(shape)
    return pl.BlockSpec(shape, lambda *_: zeros)


def _layer(shape, layer):
    zeros = (0,) * len(shape)
    return pl.BlockSpec((None,) + tuple(shape), lambda *_: (layer,) + zeros)


class _Comm:
    def __init__(self, operands, out_shape, sems, start, finish):
        self.operands, self.out_shape, self.sems, self.start, self.finish = operands, out_shape, sems, start, finish


def _call(body, *, name, grid, in_specs, out_specs, out_shape, operands, scratch_shapes=(), comm=None):
    in_specs, out_specs, out_shape, scratch_shapes = list(in_specs), list(out_specs), list(out_shape), list(scratch_shapes)
    if comm is None:
        return pl.pallas_call(body, name=name, grid=grid, in_specs=in_specs, out_specs=out_specs, out_shape=out_shape,
                              scratch_shapes=scratch_shapes, compiler_params=_params(len(grid)))(*operands)
    n_in, n_out, n_sc = len(in_specs), len(out_specs), len(scratch_shapes)
    ci, co = len(comm.operands), len(comm.out_shape)

    def hosted(*refs):
        ins, cins = refs[:n_in], refs[n_in:n_in + ci]
        outs, couts = refs[n_in + ci:n_in + ci + n_out], refs[n_in + ci + n_out:n_in + ci + n_out + co]
        scratch = refs[n_in + ci + n_out + co:n_in + ci + n_out + co + n_sc]
        sems = refs[n_in + ci + n_out + co + n_sc:]
        first = functools.reduce(jnp.logical_and, [pl.program_id(a) == 0 for a in range(len(grid))])
        last = functools.reduce(jnp.logical_and, [pl.program_id(a) == g - 1 for a, g in enumerate(grid)])

        @pl.when(first)
        def _():
            comm.start(cins, couts, sems)

        body(*ins, *outs, *scratch)

        @pl.when(last)
        def _():
            comm.finish(cins, couts, sems)

    res = pl.pallas_call(
        hosted, name=name, grid=grid, in_specs=in_specs + [ANY] * ci, out_specs=out_specs + [ANY] * co,
        out_shape=out_shape + list(comm.out_shape), scratch_shapes=scratch_shapes + list(comm.sems),
        compiler_params=_params(len(grid)))(*operands, *comm.operands)
    return res[:n_out], res[n_out:]


def _run_comm(comm, name):
    ci, co = len(comm.operands), len(comm.out_shape)

    def body(*refs):
        comm.start(refs[:ci], refs[ci:ci + co], refs[ci + co:])
        comm.finish(refs[:ci], refs[ci:ci + co], refs[ci + co:])

    return pl.pallas_call(body, name=name, in_specs=[ANY] * ci, out_specs=[ANY] * co, out_shape=list(comm.out_shape),
                          scratch_shapes=list(comm.sems))(*comm.operands)


def _sigmoid(v):
    return jax.nn.sigmoid(v)


def _dsilu(v, s):
    return s * (1.0 + v * (1.0 - s))


def _gelu_parts(v):
    cdf = 0.5 * (1.0 + lax.erf(v * 0.7071067811865476))
    pdf = jnp.exp(-0.5 * v * v) * 0.3989422804014327
    return v * cdf, cdf + v * pdf


def _gelu(v):
    return 0.5 * v * (1.0 + lax.erf(v * 0.7071067811865476))


def _rms_stats(x):
    r = lax.rsqrt(jnp.mean(x * x, axis=-1, keepdims=True) + EPS)
    return r, x * r


def _rms_bwd(dy, g, r, xh):
    gdy = dy * g
    return r * (gdy - xh * jnp.mean(xh * gdy, axis=-1, keepdims=True))


def _ln_stats(x):
    mu = jnp.mean(x, axis=-1, keepdims=True)
    xc = x - mu
    rs = lax.rsqrt(jnp.mean(xc * xc, axis=-1, keepdims=True) + EPS)
    return rs, xc * rs


def _ln_bwd(dy, g, rs, xh):
    dxh = dy * g
    return rs * (dxh - jnp.mean(dxh, axis=-1, keepdims=True) - xh * jnp.mean(dxh * xh, axis=-1, keepdims=True))


def _dot(a, b):
    return jnp.dot(a, b, preferred_element_type=F32)


def _dot_nt(a, b):
    return lax.dot_general(a, b, (((1,), (1,)), ((), ())), preferred_element_type=F32)


def _dot_tn(a, b):
    return lax.dot_general(a, b, (((0,), (0,)), ((), ())), preferred_element_type=F32)


def _fwd_in(x, norm_g, w_in_full, layer, tm, comm=None):
    t, d = x.shape
    _, nk, _, n4 = w_in_full.shape

    def body(x_ref, g_ref, w_ref, h_ref, proj_ref):
        r, xh = _rms_stats(x_ref[...])
        h = (xh * g_ref[...]).astype(BF16)
        h_ref[...] = h
        for k in range(nk):
            proj_ref[:, k * n4:(k + 1) * n4] = _dot(h, w_ref[k]).astype(BF16)

    return _call(
        body, name="fwd_in", grid=(t // tm,),
        in_specs=[pl.BlockSpec((tm, d), lambda i: (i, 0)), _layer((1, d), layer), _layer((nk, d, n4), 0)],
        out_specs=[pl.BlockSpec((tm, d), lambda i: (i, 0)), pl.BlockSpec((tm, nk * n4), lambda i: (i, 0))],
        out_shape=[SDS((t, d), BF16), SDS((t, nk * n4), BF16)],
        operands=(x, norm_g, w_in_full), comm=comm)


def _halo_maps(nt, hb, n_halo_blocks):
    def prev(b, i):
        return (jnp.maximum((b * nt + i) * hb - 1, 0), 0)

    def nxt(b, i):
        return (jnp.minimum((b * nt + i + 1) * hb, n_halo_blocks - 1), 0)

    return prev, nxt


def _col_blocks(e):
    cb = min(2 * LANES, e)
    return cb, e // cb


def _conv_taps(src_ref, w_ref, dst_ref, cb_idx, n_rows, rb, first, reverse):
    k_taps = w_ref.shape[1] - 1
    for r0 in range(0, n_rows, rb):
        acc = None
        for res in range(SUBLANES):
            rows = rb + (SUBLANES if res else 0)
            group = None
            for k in range(k_taps):
                off = first + k
                if off % SUBLANES != res:
                    continue
                wk = w_ref[cb_idx, pl.ds((k_taps - 1 - k) if reverse else k, 1), :]
                term = wk * src_ref[cb_idx, pl.ds(r0 + off - res, rows), :]
                group = term if group is None else group + term
            if group is None:
                continue
            part = group[res:res + rb] if res else group
            acc = part if acc is None else acc + part
        dst_ref[cb_idx, pl.ds(r0, rb), :] = acc


def _fwd_conv(proj, conv_w_blk, conv_b, ln_g, ln_b, layer, bsz, seq, tm):
    t, e3 = proj.shape
    e = e3 // 3
    nt = seq // tm
    hb = tm // HALO
    cb, ncb = _col_blocks(e)
    rb = min(64, tm)
    kp = conv_w_blk.shape[2]
    prev, _ = _halo_maps(nt, hb, t // HALO)

    def body(proj_ref, halo_ref, w_ref, b_ref, g_ref, lb_ref, u_ref, y1_ref, y0s, y1s):
        i = pl.program_id(1)
        a = proj_ref[:, 0:e].astype(F32)
        b = proj_ref[:, e:2 * e].astype(F32)
        y0 = a * _sigmoid(b)
        ah = halo_ref[:, 0:e].astype(F32)
        bh = halo_ref[:, e:2 * e].astype(F32)
        y0h = jnp.where(i > 0, ah * _sigmoid(bh), 0.0)
        for c in range(ncb):
            y0s[c, 0:HALO, :] = y0h[:, c * cb:(c + 1) * cb]
            y0s[c, HALO:HALO + tm, :] = y0[:, c * cb:(c + 1) * cb]

        def per_block(c, carry):
            _conv_taps(y0s, w_ref, y1s, c, tm, rb, HALO - (kp - 2), False)
            return carry

        lax.fori_loop(0, ncb, per_block, 0)
        y1 = jnp.concatenate([y1s[c] for c in range(ncb)], axis=1) + b_ref[...]
        y1_ref[...] = y1
        rs, xh = _ln_stats(y1)
        y2 = xh * g_ref[...] + lb_ref[...]
        y = y2 * _sigmoid(y2)
        z = proj_ref[:, 2 * e:3 * e].astype(F32)
        u_ref[...] = (y * (z * _sigmoid(z))).astype(BF16)

    return pl.pallas_call(
        body, name="fwd_conv", grid=(bsz, nt),
        in_specs=[pl.BlockSpec((tm, e3), lambda b, i: (b * nt + i, 0)),
                  pl.BlockSpec((HALO, 2 * e), prev),
                  _layer((ncb, kp, cb), layer), _layer((1, e), layer), _layer((1, e), layer), _layer((1, e), layer)],
        out_specs=[pl.BlockSpec((tm, e), lambda b, i: (b * nt + i, 0)), pl.BlockSpec((tm, e), lambda b, i: (b * nt + i, 0))],
        out_shape=[SDS((t, e), BF16), SDS((t, e), F32)],
        scratch_shapes=[pltpu.VMEM((ncb, HALO + tm, cb), F32), pltpu.VMEM((ncb, tm, cb), F32)],
        compiler_params=_params(2),
    )(proj, proj, conv_w_blk, conv_b, ln_g, ln_b)


def _tril_mask():
    rows = lax.broadcasted_iota(jnp.int32, (CHUNK, CHUNK), 0)
    cols = lax.broadcasted_iota(jnp.int32, (CHUNK, CHUNK), 1)
    return rows >= cols


def _fwd_sgu(proj, ln_g, ln_b, sgu_w, sgu_bt, layer, tm):
    t, e3 = proj.shape
    e = e3 // 3
    gw = e // GROUPS
    nch = tm // CHUNK

    def body(proj_ref, g_ref, lb_ref, w_ref, bt_ref, u_ref, mixed):
        a = proj_ref[:, 0:e].astype(F32)
        b = proj_ref[:, e:2 * e].astype(F32)
        z = proj_ref[:, 2 * e:3 * e].astype(F32)
        rs, xh = _ln_stats(_gelu(b))
        v = (xh * g_ref[...] + lb_ref[...]).astype(BF16)
        mask = _tril_mask()
        for g in range(GROUPS):
            wm = jnp.where(mask, w_ref[g], 0.0).astype(BF16)
            bias = bt_ref[:, g:g + 1]
            for n in range(nch):
                blk = v[n * CHUNK:(n + 1) * CHUNK, g * gw:(g + 1) * gw]
                mixed[n * CHUNK:(n + 1) * CHUNK, g * gw:(g + 1) * gw] = _dot(wm, blk) + bias
        y = _gelu(a) * mixed[...]
        u_ref[...] = (y * (z * _sigmoid(z))).astype(BF16)

    return pl.pallas_call(
        body, name="fwd_sgu", grid=(t // tm,),
        in_specs=[pl.BlockSpec((tm, e3), lambda i: (i, 0)), _layer((1, e), layer), _layer((1, e), layer),
                  _layer((GROUPS, CHUNK, CHUNK), layer), _layer((CHUNK, GROUPS), layer)],
        out_specs=pl.BlockSpec((tm, e), lambda i: (i, 0)),
        out_shape=SDS((t, e), BF16),
        scratch_shapes=[pltpu.VMEM((tm, e), F32)],
        compiler_params=_params(1),
    )(proj, ln_g, ln_b, sgu_w, sgu_bt)


def _ple_forward(x1, p_ref, plg_ref, gw_ref, pw_ref):
    nk, _, dq = pw_ref.shape
    r, xh = _rms_stats(x1)
    rn = (xh * plg_ref[...]).astype(BF16)
    gate = _sigmoid(_dot(rn, gw_ref[...]))
    pb = p_ref[...].astype(BF16)
    q = jnp.concatenate([_dot(pb, pw_ref[k]) for k in range(nk)], axis=1)
    return r, xh, rn, gate, q


def _fwd_out(x, u, w_out_full, pl_norm_g, gate_w_full, p, proj_w_full, layer, tm, comm=None):
    t, d = x.shape
    e = u.shape[1]
    ple = p.shape[-1]
    nk, dq = proj_w_full.shape[1], proj_w_full.shape[3]

    def body(x_ref, u_ref, wo_ref, plg_ref, gw_ref, p_ref, pw_ref, x1_ref, x2_ref):
        x1 = x_ref[...] + _dot(u_ref[...], wo_ref[...])
        x1_ref[...] = x1
        _, _, _, gate, q = _ple_forward(x1, p_ref, plg_ref, gw_ref, pw_ref)
        x2_ref[...] = x1 + gate * q

    return _call(
        body, name="fwd_out", grid=(t // tm,),
        in_specs=[pl.BlockSpec((tm, d), lambda i: (i, 0)), pl.BlockSpec((tm, e), lambda i: (i, 0)),
                  _layer((e, d), 0), _layer((1, d), layer), _layer((d, d), 0),
                  pl.BlockSpec((None, tm, ple), lambda i: (layer, i, 0)), _layer((nk, ple, dq), 0)],
        out_specs=[pl.BlockSpec((tm, d), lambda i: (i, 0)), pl.BlockSpec((tm, d), lambda i: (i, 0))],
        out_shape=[SDS((t, d), F32), SDS((t, d), F32)],
        operands=(x, u, w_out_full, pl_norm_g, gate_w_full, p, proj_w_full), comm=comm)


def _loss_head(x, final_g, target, tm):
    t, d = x.shape
    n_steps = t // tm

    def body(x_ref, g_ref, tgt_ref, loss_ref, dx_ref, dg_ref, sq_acc):
        i = pl.program_id(0)

        @pl.when(i == 0)
        def _():
            sq_acc[...] = jnp.zeros_like(sq_acc)
            dg_ref[...] = jnp.zeros_like(dg_ref)

        g = g_ref[...]
        r, xh = _rms_stats(x_ref[...])
        diff = xh * g - tgt_ref[...]
        sq_acc[...] += jnp.sum(diff * diff, axis=0, keepdims=True)
        dout = diff * (1.0 / d)
        dg_ref[...] += jnp.sum(dout * xh, axis=0, keepdims=True)
        dx_ref[...] = _rms_bwd(dout, g, r, xh)

        @pl.when(i == n_steps - 1)
        def _():
            loss_ref[...] = jnp.sum(sq_acc[...], axis=1, keepdims=True) * (0.5 / d)

    return pl.pallas_call(
        body, name="loss_head", grid=(n_steps,),
        in_specs=[pl.BlockSpec((tm, d), lambda i: (i, 0)), _const((1, d)), pl.BlockSpec((tm, d), lambda i: (i, 0))],
        out_specs=[_const((1, 1)), pl.BlockSpec((tm, d), lambda i: (i, 0)), _const((1, d))],
        out_shape=[SDS((1, 1), F32), SDS((t, d), F32), SDS((1, d), F32)],
        scratch_shapes=[pltpu.VMEM((1, d), F32)],
        compiler_params=_params(1),
    )(x, final_g, target)


def _bwd_out(dx2, x1, p, pl_norm_g, gate_w_full, proj_w_full, w_out_full, layer, tm, comm=None):
    t, d = dx2.shape
    e = w_out_full.shape[1]
    ple = p.shape[-1]
    nk, dq_w = proj_w_full.shape[1], proj_w_full.shape[3]

    def body(dx2_ref, x1_ref, p_ref, plg_ref, gw_ref, pw_ref, wo_ref, dx1_ref, du_ref, rn_ref, ds_ref, dq_ref, dplg_ref):
        @pl.when(pl.program_id(0) == 0)
        def _():
            dplg_ref[...] = jnp.zeros_like(dplg_ref)

        dx2v = dx2_ref[...]
        r, xh, rn, gate, q = _ple_forward(x1_ref[...], p_ref, plg_ref, gw_ref, pw_ref)
        rn_ref[...] = rn
        dq_ref[...] = (dx2v * gate).astype(BF16)
        ds = (dx2v * q * gate * (1.0 - gate)).astype(BF16)
        ds_ref[...] = ds
        dr = _dot_nt(ds, gw_ref[...])
        dplg_ref[...] += jnp.sum(dr * xh, axis=0, keepdims=True)
        dx1 = dx2v + _rms_bwd(dr, plg_ref[...], r, xh)
        dx1_ref[...] = dx1
        du_ref[...] = _dot_nt(dx1.astype(BF16), wo_ref[...]).astype(BF16)

    row = lambda w: pl.BlockSpec((tm, w), lambda i: (i, 0))
    return _call(
        body, name="bwd_out", grid=(t // tm,),
        in_specs=[row(d), row(d), pl.BlockSpec((None, tm, ple), lambda i: (layer, i, 0)),
                  _layer((1, d), layer), _layer((d, d), 0), _layer((nk, ple, dq_w), 0), _layer((e, d), 0)],
        out_specs=[row(d), row(e), row(d), row(d), row(d), _const((1, d))],
        out_shape=[SDS((t, d), F32), SDS((t, e), BF16), SDS((t, d), BF16), SDS((t, d), BF16), SDS((t, d), BF16),
                   SDS((1, d), F32)],
        operands=(dx2, x1, p, pl_norm_g, gate_w_full, proj_w_full, w_out_full), comm=comm)


def _bwd_conv(du, proj, y1, conv_w_blk, ln_g, ln_b, layer, bsz, seq, tm):
    t, e3 = proj.shape
    e = e3 // 3
    nt = seq // tm
    hb = tm // HALO
    cb, ncb = _col_blocks(e)
    rb = min(64, tm)
    kp = conv_w_blk.shape[2]
    k_taps = kp - 1
    prev, nxt = _halo_maps(nt, hb, t // HALO)
    z_halo = lambda b, i: (nxt(b, i)[0], 2)

    def ln_silu_bwd(du_v, z_v, y1_v, g, lb):
        rs, xh = _ln_stats(y1_v)
        y2 = xh * g + lb
        sg = _sigmoid(y2)
        sz = _sigmoid(z_v)
        dy = du_v * (z_v * sz)
        dy2 = dy * _dsilu(y2, sg)
        return _ln_bwd(dy2, g, rs, xh), dy2, xh, du_v * (y2 * sg) * _dsilu(z_v, sz)

    def body(du_ref, proj_ref, y1_ref, duh_ref, zh_ref, y1h_ref, abh_ref, w_ref, g_ref, lb_ref,
             dproj_ref, dw_ref, dcb_ref, dg_ref, dlb_ref, y0s, dy1s, dy0s, ysh):
        b_id, i = pl.program_id(0), pl.program_id(1)

        @pl.when((b_id == 0) & (i == 0))
        def _():
            dw_ref[...] = jnp.zeros_like(dw_ref)
            dcb_ref[...] = jnp.zeros_like(dcb_ref)
            dg_ref[...] = jnp.zeros_like(dg_ref)
            dlb_ref[...] = jnp.zeros_like(dlb_ref)

        g, lb = g_ref[...], lb_ref[...]
        a = proj_ref[:, 0:e].astype(F32)
        b = proj_ref[:, e:2 * e].astype(F32)
        z = proj_ref[:, 2 * e:3 * e].astype(F32)
        sb = _sigmoid(b)
        y0 = a * sb
        dy1, dy2, xh, dz = ln_silu_bwd(du_ref[...].astype(F32), z, y1_ref[...], g, lb)
        dproj_ref[:, 2 * e:3 * e] = dz.astype(BF16)
        dg_ref[...] += jnp.sum(dy2 * xh, axis=0, keepdims=True)
        dlb_ref[...] += jnp.sum(dy2, axis=0, keepdims=True)
        dcb_ref[...] += jnp.sum(dy1, axis=0, keepdims=True)
        dy1h, _, _, _ = ln_silu_bwd(duh_ref[...].astype(F32), zh_ref[...].astype(F32), y1h_ref[...], g, lb)
        dy1h = jnp.where(i < nt - 1, dy1h, 0.0)
        ah = abh_ref[:, 0:e].astype(F32)
        bh = abh_ref[:, e:2 * e].astype(F32)
        y0h = jnp.where(i > 0, ah * _sigmoid(bh), 0.0)
        for c in range(ncb):
            cols = slice(c * cb, (c + 1) * cb)
            y0s[c, 0:HALO, :] = y0h[:, cols]
            y0s[c, HALO:HALO + tm, :] = y0[:, cols]
            dy1s[c, 0:tm, :] = dy1[:, cols]
            dy1s[c, tm:tm + HALO, :] = dy1h[:, cols]

        def per_block(c, carry):
            _conv_taps(dy1s, w_ref, dy0s, c, tm, rb, 0, True)
            for res in range(1, SUBLANES):
                ysh[res - 1] = y0s[c, pl.ds(res, tm + HALO - SUBLANES), :]
            for k in range(k_taps):
                off = HALO - (k_taps - 1) + k
                res = off % SUBLANES
                acc = None
                for r0 in range(0, tm, rb):
                    rows = pl.ds(r0 + off - res, rb)
                    shifted = ysh[res - 1, rows, :] if res else y0s[c, rows, :]
                    term = dy1s[c, pl.ds(r0, rb), :] * shifted
                    acc = term if acc is None else acc + term
                dw_ref[c, pl.ds(k, 1), :] += jnp.sum(acc, axis=0, keepdims=True)
            return carry

        lax.fori_loop(0, ncb, per_block, 0)
        dy0 = jnp.concatenate([dy0s[c] for c in range(ncb)], axis=1)
        dproj_ref[:, 0:e] = (dy0 * sb).astype(BF16)
        dproj_ref[:, e:2 * e] = (dy0 * a * sb * (1.0 - sb)).astype(BF16)

    tile = lambda w: pl.BlockSpec((tm, w), lambda b, i: (b * nt + i, 0))
    return pl.pallas_call(
        body, name="bwd_conv", grid=(bsz, nt),
        in_specs=[tile(e), tile(e3), tile(e),
                  pl.BlockSpec((HALO, e), nxt), pl.BlockSpec((HALO, e), z_halo), pl.BlockSpec((HALO, e), nxt),
                  pl.BlockSpec((HALO, 2 * e), prev),
                  _layer((ncb, kp, cb), layer), _layer((1, e), layer), _layer((1, e), layer)],
        out_specs=[tile(e3), _const((ncb, kp, cb)), _const((1, e)), _const((1, e)), _const((1, e))],
        out_shape=[SDS((t, e3), BF16), SDS((ncb, kp, cb), F32), SDS((1, e), F32), SDS((1, e), F32), SDS((1, e), F32)],
        scratch_shapes=[pltpu.VMEM((ncb, HALO + tm, cb), F32), pltpu.VMEM((ncb, tm + HALO, cb), F32),
                        pltpu.VMEM((ncb, tm, cb), F32), pltpu.VMEM((SUBLANES - 1, tm + HALO - SUBLANES, cb), F32)],
        compiler_params=_params(2),
    )(du, proj, y1, du, proj, y1, proj, conv_w_blk, ln_g, ln_b)


def _bwd_sgu(du, proj, ln_g, ln_b, sgu_w, sgu_bt, layer, tm):
    t, e3 = proj.shape
    e = e3 // 3
    gw = e // GROUPS
    nch = tm // CHUNK

    def body(du_ref, proj_ref, g_ref, lb_ref, w_ref, bt_ref, dproj_ref, dw_ref, dbt_ref, dg_ref, dlb_ref, mixed, dmix, dv):
        @pl.when(pl.program_id(0) == 0)
        def _():
            dw_ref[...] = jnp.zeros_like(dw_ref)
            dbt_ref[...] = jnp.zeros_like(dbt_ref)
            dg_ref[...] = jnp.zeros_like(dg_ref)
            dlb_ref[...] = jnp.zeros_like(dlb_ref)

        g, lb = g_ref[...], lb_ref[...]
        a = proj_ref[:, 0:e].astype(F32)
        b = proj_ref[:, e:2 * e].astype(F32)
        z = proj_ref[:, 2 * e:3 * e].astype(F32)
        ug, dug = _gelu_parts(a)
        vb, dvb_db = _gelu_parts(b)
        rs, xh = _ln_stats(vb)
        v = (xh * g + lb).astype(BF16)
        mask = _tril_mask()
        for gi in range(GROUPS):
            wm = jnp.where(mask, w_ref[gi], 0.0).astype(BF16)
            bias = bt_ref[:, gi:gi + 1]
            for n in range(nch):
                blk = v[n * CHUNK:(n + 1) * CHUNK, gi * gw:(gi + 1) * gw]
                mixed[n * CHUNK:(n + 1) * CHUNK, gi * gw:(gi + 1) * gw] = _dot(wm, blk) + bias
        mx = mixed[...]
        sz = _sigmoid(z)
        duv = du_ref[...].astype(F32)
        dy = duv * (z * sz)
        dproj_ref[:, 2 * e:3 * e] = (duv * (ug * mx) * _dsilu(z, sz)).astype(BF16)
        dproj_ref[:, 0:e] = (dy * mx * dug).astype(BF16)
        dmix[...] = dy * ug
        for gi in range(GROUPS):
            wm = jnp.where(mask, w_ref[gi], 0.0).astype(BF16)
            dw_acc = None
            db_acc = None
            for n in range(nch):
                rows, cols = slice(n * CHUNK, (n + 1) * CHUNK), slice(gi * gw, (gi + 1) * gw)
                dm = dmix[rows, cols]
                dmb = dm.astype(BF16)
                dw_n = _dot_nt(dmb, v[rows, cols])
                db_n = jnp.sum(dm, axis=1, keepdims=True)
                dw_acc = dw_n if dw_acc is None else dw_acc + dw_n
                db_acc = db_n if db_acc is None else db_acc + db_n
                dv[rows, cols] = _dot_tn(wm, dmb)
            dw_ref[gi] += jnp.where(mask, dw_acc, 0.0)
            dbt_ref[:, gi:gi + 1] += db_acc
        dvv = dv[...]
        dg_ref[...] += jnp.sum(dvv * xh, axis=0, keepdims=True)
        dlb_ref[...] += jnp.sum(dvv, axis=0, keepdims=True)
        dproj_ref[:, e:2 * e] = (_ln_bwd(dvv, g, rs, xh) * dvb_db).astype(BF16)

    return pl.pallas_call(
        body, name="bwd_sgu", grid=(t // tm,),
        in_specs=[pl.BlockSpec((tm, e), lambda i: (i, 0)), pl.BlockSpec((tm, e3), lambda i: (i, 0)),
                  _layer((1, e), layer), _layer((1, e), layer),
                  _layer((GROUPS, CHUNK, CHUNK), layer), _layer((CHUNK, GROUPS), layer)],
        out_specs=[pl.BlockSpec((tm, e3), lambda i: (i, 0)), _const((GROUPS, CHUNK, CHUNK)), _const((CHUNK, GROUPS)),
                   _const((1, e)), _const((1, e))],
        out_shape=[SDS((t, e3), BF16), SDS((GROUPS, CHUNK, CHUNK), F32), SDS((CHUNK, GROUPS), F32),
                   SDS((1, e), F32), SDS((1, e), F32)],
        scratch_shapes=[pltpu.VMEM((tm, e), F32), pltpu.VMEM((tm, e), F32), pltpu.VMEM((tm, e), F32)],
        compiler_params=_params(1),
    )(du, proj, ln_g, ln_b, sgu_w, sgu_bt)


def _bwd_in(dproj, dx1, x, norm_g, w_in_full, layer, tm, comm=None):
    t, d = x.shape
    _, nk, _, n4 = w_in_full.shape

    def body(dproj_ref, dx1_ref, x_ref, g_ref, w_ref, dx_ref, dg_ref):
        @pl.when(pl.program_id(0) == 0)
        def _():
            dg_ref[...] = jnp.zeros_like(dg_ref)

        dh = None
        for k in range(nk):
            part = _dot_nt(dproj_ref[:, k * n4:(k + 1) * n4], w_ref[k])
            dh = part if dh is None else dh + part
        r, xh = _rms_stats(x_ref[...])
        dg_ref[...] += jnp.sum(dh * xh, axis=0, keepdims=True)
        dx_ref[...] = dx1_ref[...] + _rms_bwd(dh, g_ref[...], r, xh)

    row = lambda w: pl.BlockSpec((tm, w), lambda i: (i, 0))
    return _call(
        body, name="bwd_in", grid=(t // tm,),
        in_specs=[row(nk * n4), row(d), row(d), _layer((1, d), layer), _layer((nk, d, n4), 0)],
        out_specs=[row(d), _const((1, d))],
        out_shape=[SDS((t, d), F32), SDS((1, d), F32)],
        operands=(dproj, dx1, x, norm_g, w_in_full), comm=comm)


def _wgrad(a, b, kblk, nblk, n_split, tm, name, a_layer=None, comm=None):
    t, n = b.shape
    k = a.shape[-1]
    kw, nw = k // kblk, n // nblk
    nws = nw // n_split
    n_steps = t // tm

    def body(a_ref, b_ref, o_ref):
        @pl.when(pl.program_id(2) == 0)
        def _():
            o_ref[...] = jnp.zeros_like(o_ref)

        res = _dot_tn(a_ref[...].astype(BF16), b_ref[...].astype(BF16))
        for s in range(n_split):
            o_ref[s] += res[:, s * nws:(s + 1) * nws]

    if a_layer is None:
        a_spec = pl.BlockSpec((tm, kw), lambda kb, nb, i: (i, kb))
    else:
        a_spec = pl.BlockSpec((None, tm, kw), lambda kb, nb, i: (a_layer, i, kb))
    return _call(
        body, name=name, grid=(kblk, nblk, n_steps),
        in_specs=[a_spec, pl.BlockSpec((tm, nw), lambda kb, nb, i: (i, nb))],
        out_specs=[pl.BlockSpec((None, n_split, kw, nws), lambda kb, nb, i: (kb, nb, 0, 0))],
        out_shape=[SDS((kblk, nblk * n_split, kw, nws), F32)],
        operands=(a, b), comm=comm)


def _row_tile(rows, cols, budget_bytes=1 << 20):
    best = None
    for cand in range(SUBLANES, rows + 1, SUBLANES):
        if rows % cand == 0 and cand * cols * 4 <= budget_bytes:
            best = cand
    return best if best is not None else rows


def _pair_sum(grads, recv, my_c, wire_dtype):
    n, _, h, c = grads.shape
    th = _row_tile(h, c)
    two = wire_dtype != F32

    def body(c_ref, g_ref, r_ref, o_ref, *wire_ref):
        total = g_ref[...] + r_ref[...]
        o_ref[...] = total
        if two:
            wire_ref[0][...] = total.astype(wire_dtype)

    out_spec = pl.BlockSpec((None, th, c), lambda j, i, c_ref: (j, i, 0))
    grid_spec = pltpu.PrefetchScalarGridSpec(
        num_scalar_prefetch=1, grid=(n, h // th),
        in_specs=[pl.BlockSpec((None, None, th, c), lambda j, i, c_ref: (j, c_ref[0], i, 0)),
                  pl.BlockSpec((None, None, th, c), lambda j, i, c_ref: (j, 0, i, 0))],
        out_specs=[out_spec, out_spec] if two else [out_spec])
    out_shape = [SDS((n, h, c), F32)] + ([SDS((n, h, c), wire_dtype)] if two else [])
    outs = pl.pallas_call(body, name="pair_sum", grid_spec=grid_spec, out_shape=out_shape,
                          compiler_params=_params(2))(my_c, grads, recv)
    return outs[0], outs[-1]


def _chip_sum(pair, recv, my_k, stacked, layer):
    _, _, h, c = pair.shape
    th = _row_tile(h, c)

    def body(k_ref, own_ref, r1_ref, r2_ref, r3_ref, acc_ref, o_ref):
        acc = own_ref[...]
        for ref in (r1_ref, r2_ref, r3_ref):
            acc = acc + ref[...].astype(F32)
        o_ref[...] = acc

    def slot(flip):
        return pl.BlockSpec((None, None, th, c), lambda i, k_ref: (0, jnp.bitwise_xor(k_ref[0], flip), i, 0))

    grid_spec = pltpu.PrefetchScalarGridSpec(
        num_scalar_prefetch=1, grid=(h // th,),
        in_specs=[slot(0), slot(1), slot(2), slot(3), ANY],
        out_specs=pl.BlockSpec((None, th, c), lambda i, k_ref: (layer, i, 0)))
    return pl.pallas_call(body, name="chip_sum", grid_spec=grid_spec, out_shape=SDS(stacked.shape, F32),
                          input_output_aliases={5: 0}, compiler_params=_params(1))(my_k, pair, recv, recv, recv, stacked)


def _adam_math(w, gv, m, v):
    c1 = 1.0 - ADAM_B1 ** ADAM_STEP
    c2 = 1.0 - ADAM_B2 ** ADAM_STEP
    mn = ADAM_B1 * m + (1.0 - ADAM_B1) * gv
    vn = ADAM_B2 * v + (1.0 - ADAM_B2) * (gv * gv)
    m_hat = mn / c1
    v_hat = vn / c2
    return -ADAM_LR * (m_hat / (jnp.sqrt(v_hat) + ADAM_EPS) + ADAM_WD * w), mn, vn


def _adamw_halves(w, g_mine, g_theirs, m, v, my_c):
    nl, rows, cols = w.shape
    h = rows // 2
    th = _row_tile(h, cols, 512 << 10)
    as4 = lambda a: a.reshape(nl, 2, h, cols)

    def body(c_ref, w_ref, gm_ref, gt_ref, m_ref, v_ref, go_ref, d_ref, mo_ref, vo_ref):
        gv = jnp.where(pl.program_id(1) == c_ref[0], gm_ref[...], gt_ref[...])
        go_ref[...] = gv
        d_ref[...], mo_ref[...], vo_ref[...] = _adam_math(w_ref[...], gv, m_ref[...], v_ref[...])

    full = pl.BlockSpec((None, None, th, cols), lambda j, s, i, c_ref: (j, s, i, 0))
    half = pl.BlockSpec((None, th, cols), lambda j, s, i, c_ref: (j, i, 0))
    grid_spec = pltpu.PrefetchScalarGridSpec(
        num_scalar_prefetch=1, grid=(nl, 2, h // th),
        in_specs=[full, half, half, full, full], out_specs=[full] * 4)
    outs = pl.pallas_call(body, name="adamw_halves", grid_spec=grid_spec, out_shape=[SDS((nl, 2, h, cols), F32)] * 4,
                          compiler_params=_params(3))(my_c, as4(w), g_mine, g_theirs, as4(m), as4(v))
    return [o.reshape(nl, rows, cols) for o in outs]


def _adamw(w, g, m, v):
    rows, cols = w.shape
    tr = _row_tile(rows, cols, 512 << 10)

    def body(w_ref, g_ref, m_ref, v_ref, go_ref, d_ref, mo_ref, vo_ref):
        gv = g_ref[...]
        go_ref[...] = gv
        d_ref[...], mo_ref[...], vo_ref[...] = _adam_math(w_ref[...], gv, m_ref[...], v_ref[...])

    spec = pl.BlockSpec((tr, cols), lambda i: (i, 0))
    return pl.pallas_call(
        body, name="adamw", grid=(rows // tr,), in_specs=[spec] * 4, out_specs=[spec] * 4,
        out_shape=[SDS((rows, cols), F32)] * 4, compiler_params=_params(1))(w, g, m, v)


def _place():
    x, y, c = lax.axis_index("x"), lax.axis_index("y"), lax.axis_index("c")
    chips = [(1 - x, y), (x, 1 - y), (1 - x, 1 - y)]
    return x, y, c, 2 * x + y, chips


def _remote(src, dst, send_sem, recv_sem, device):
    return pltpu.make_async_remote_copy(src_ref=src, dst_ref=dst, send_sem=send_sem, recv_sem=recv_sem,
                                        device_id=device, device_id_type=MESH_IDS)


def _gather_comm(items, small=None):
    shards = [arr for arr, _ in items]
    layers = [layer for _, layer in items]
    n = len(shards)
    extra = 0 if small is None else 1

    def copies(ins, outs, sems):
        ici_send, ici_recv, d2d_send, d2d_recv, own_send, own_recv = sems
        x, y, c, k, chips = _place()
        sibling = (x, y, 1 - c)
        own, ici_out, ici_in, fwd_out, fwd_in = [], [], [], [], []
        for j in range(n):
            h = ins[j].shape[2] // 2
            mine, theirs = pl.ds(c * h, h), pl.ds((1 - c) * h, h)
            own.append(_remote(ins[j].at[pl.ds(layers[j], 1)], outs[j].at[:, pl.ds(k, 1)], own_send.at[j], own_recv.at[j], sibling))
            for ti, (cx, cy) in enumerate(chips):
                s = 3 * j + ti
                ici_out.append(_remote(ins[j].at[pl.ds(layers[j], 1), :, mine], outs[j].at[:, pl.ds(k, 1), mine],
                                       ici_send.at[s], ici_recv.at[s], (cx, cy, c)))
                landed = outs[j].at[:, pl.ds(2 * cx + cy, 1), mine]
                ici_in.append(_remote(landed, landed, ici_send.at[s], ici_recv.at[s], (cx, cy, c)))
                fwd_out.append(_remote(landed, landed, d2d_send.at[s], d2d_recv.at[s], sibling))
                passed = outs[j].at[:, pl.ds(2 * cx + cy, 1), theirs]
                fwd_in.append(_remote(passed, passed, d2d_send.at[s], d2d_recv.at[s], sibling))
        if extra:
            own.append(_remote(ins[n], outs[n].at[pl.ds(k, 1)], own_send.at[n], own_recv.at[n], sibling))
            for ti, (cx, cy) in enumerate(chips):
                s = 3 * n + ti
                ici_out.append(_remote(ins[n], outs[n].at[pl.ds(k, 1)], ici_send.at[s], ici_recv.at[s], (cx, cy, c)))
                slot = outs[n].at[pl.ds(2 * cx + cy, 1)]
                ici_in.append(_remote(slot, slot, ici_send.at[s], ici_recv.at[s], (cx, cy, c)))
        return own, ici_out, ici_in, fwd_out, fwd_in

    def start(ins, outs, sems):
        own, ici_out, _, _, _ = copies(ins, outs, sems)
        for cp in own + ici_out:
            cp.start()

    def finish(ins, outs, sems):
        own, ici_out, ici_in, fwd_out, fwd_in = copies(ins, outs, sems)
        for idx, cp in enumerate(ici_in):
            cp.wait_recv()
            if idx < len(fwd_out):
                fwd_out[idx].start()
        for cp in fwd_in:
            cp.wait_recv()
        for cp in ici_out + fwd_out:
            cp.wait_send()
        for cp in own:
            cp.wait()

    operands = list(shards) + ([small] if extra else [])
    out_shape = [SDS((1, N_CHIPS) + s.shape[2:], s.dtype) for s in shards]
    if extra:
        out_shape.append(SDS((N_CHIPS,) + small.shape[1:], small.dtype))
    sems = [pltpu.SemaphoreType.DMA((3 * (n + extra),)), pltpu.SemaphoreType.DMA((3 * (n + extra),)),
            pltpu.SemaphoreType.DMA((3 * n,)), pltpu.SemaphoreType.DMA((3 * n,)),
            pltpu.SemaphoreType.DMA((n + extra,)), pltpu.SemaphoreType.DMA((n + extra,))]
    return _Comm(operands, out_shape, sems, start, finish)


def _swap_comm(grads):
    n = len(grads)

    def copies(ins, outs, sems):
        send_sem, recv_sem = sems
        x, y, c, _, _ = _place()
        return [_remote(ins[j].at[:, pl.ds(1 - c, 1)], outs[j], send_sem.at[j], recv_sem.at[j], (x, y, 1 - c))
                for j in range(n)]

    def start(ins, outs, sems):
        for cp in copies(ins, outs, sems):
            cp.start()

    def finish(ins, outs, sems):
        for cp in copies(ins, outs, sems):
            cp.wait()

    out_shape = [SDS((g.shape[0], 1) + g.shape[2:], g.dtype) for g in grads]
    return _Comm(list(grads), out_shape, [pltpu.SemaphoreType.DMA((n,)), pltpu.SemaphoreType.DMA((n,))], start, finish)


def _scatter_comm(sums):
    n = len(sums)

    def copies(ins, outs, sems):
        send_sem, recv_sem = sems
        x, y, c, k, chips = _place()
        out, landing = [], []
        for j in range(n):
            for ti, (cx, cy) in enumerate(chips):
                s = 3 * j + ti
                out.append(_remote(ins[j].at[:, pl.ds(2 * cx + cy, 1)], outs[j].at[:, pl.ds(k, 1)],
                                   send_sem.at[s], recv_sem.at[s], (cx, cy, c)))
                slot = outs[j].at[:, pl.ds(2 * cx + cy, 1)]
                landing.append(_remote(slot, slot, send_sem.at[s], recv_sem.at[s], (cx, cy, c)))
        return out, landing

    def start(ins, outs, sems):
        for cp in copies(ins, outs, sems)[0]:
            cp.start()

    def finish(ins, outs, sems):
        out, landing = copies(ins, outs, sems)
        for cp in landing:
            cp.wait_recv()
        for cp in out:
            cp.wait_send()

    out_shape = [SDS(s.shape, s.dtype) for s in sums]
    return _Comm(list(sums), out_shape, [pltpu.SemaphoreType.DMA((3 * n,)), pltpu.SemaphoreType.DMA((3 * n,))], start, finish)


def _swap_pieces(pieces):
    n = len(pieces)

    def body(*refs):
        ins, outs = refs[:n], refs[n:2 * n]
        send_sem, recv_sem = refs[2 * n:]
        x, y, c, _, _ = _place()
        copies = [_remote(ins[j], outs[j], send_sem.at[j], recv_sem.at[j], (x, y, 1 - c)) for j in range(n)]
        for cp in copies:
            cp.start()
        for cp in copies:
            cp.wait()

    return pl.pallas_call(
        body, name="swap_pieces", in_specs=[ANY] * n, out_specs=[ANY] * n,
        out_shape=[SDS(p.shape, p.dtype) for p in pieces],
        scratch_shapes=[pltpu.SemaphoreType.DMA((n,)), pltpu.SemaphoreType.DMA((n,))],
    )(*pieces)


def _gather_pieces(piece):
    def body(in_ref, out_ref, send_sem, recv_sem):
        x, y, c, k, chips = _place()
        peers = [(x, y, 1 - c)] + [(cx, cy, pc) for (cx, cy) in chips for pc in (c, 1 - c)]
        copies = []
        for ti, peer in enumerate(peers):
            cp = _remote(in_ref, out_ref.at[pl.ds(k, 1), pl.ds(c, 1)], send_sem.at[ti], recv_sem.at[ti], peer)
            cp.start()
            copies.append(cp)
        for ti, (px, py, pc) in enumerate(peers):
            _remote(in_ref, out_ref.at[pl.ds(2 * px + py, 1), pl.ds(pc, 1)], send_sem.at[ti], recv_sem.at[ti],
                    (px, py, pc)).wait_recv()
        for cp in copies:
            cp.wait_send()

    n_peers = 2 * N_CHIPS - 1
    return pl.pallas_call(
        body, name="gather_pieces", in_specs=[ANY], out_specs=ANY,
        out_shape=SDS((N_CHIPS, 2) + piece.shape[2:], piece.dtype),
        scratch_shapes=[pltpu.SemaphoreType.DMA((n_peers,)), pltpu.SemaphoreType.DMA((n_peers,))],
    )(piece)


def _pack_rows(parts, width, total_rows=None):
    rows = []
    for a in parts:
        a2 = a.reshape(-1, width)
        pad = (-a2.shape[0]) % SUBLANES
        rows.append(jnp.pad(a2, ((0, pad), (0, 0))) if pad else a2)
    out = jnp.concatenate(rows, axis=0)
    if total_rows is not None and out.shape[0] < total_rows:
        out = jnp.pad(out, ((0, total_rows - out.shape[0]), (0, 0)))
    return out


def _unpack_rows(packed, shapes, width):
    out, r = [], 0
    for shp in shapes:
        size = 1
        for s in shp:
            size *= s
        nr = size // width
        out.append(packed[r:r + nr].reshape(shp))
        r += nr + ((-nr) % SUBLANES)
    return out


def kernel(x, p, norm_g, w_in, w_out, conv_w, conv_b, conv_ln_g, conv_ln_b, sgu_ln_g, sgu_ln_b, sgu_w, sgu_b, pl_norm_g, pl_gate_w, pl_proj_w, final_g, loss_target, m_norm_g, m_w_in, m_w_out, m_conv_w, m_conv_b, m_conv_ln_g, m_conv_ln_b, m_sgu_ln_g, m_sgu_ln_b, m_sgu_w, m_sgu_b, m_pl_norm_g, m_pl_gate_w, m_pl_proj_w, m_final_g, v_norm_g, v_w_in, v_w_out, v_conv_w, v_conv_b, v_conv_ln_g, v_conv_ln_b, v_sgu_ln_g, v_sgu_ln_b, v_sgu_w, v_sgu_b, v_pl_norm_g, v_pl_gate_w, v_pl_proj_w, v_final_g):
    bsz, seq, d = x.shape
    depth = w_in.shape[0]
    e = w_out.shape[1] * N_CHIPS
    e3 = 3 * e
    n4 = w_in.shape[2]
    ple = p.shape[-1]
    dq = pl_proj_w.shape[2]
    k_taps = conv_w.shape[1]
    kp = k_taps + 1
    n_conv, n_sgu = conv_w.shape[0], sgu_ln_g.shape[0]
    t = bsz * seq
    tm_mm = min(512, seq)
    tm_mix = min(256, seq)
    cb, ncb = _col_blocks(e)
    my_c = lax.axis_index("c")
    my_k = 2 * lax.axis_index("x") + lax.axis_index("y")

    ec = e // N_CHIPS
    small_w = _pack_rows([conv_w.reshape(n_conv * k_taps, ec), sgu_ln_g, sgu_ln_b], ec)[None]
    shards = {"in": w_in.astype(BF16)[:, None], "out": w_out.astype(BF16)[:, None],
              "gate": pl_gate_w.astype(BF16)[:, None], "proj": pl_proj_w.astype(BF16)[:, None]}
    rest = ("out", "gate", "proj")
    gathered = {}
    gathered["in", 0], small_f = _run_comm(_gather_comm([(shards["in"], 0)], small_w), "gather_first")
    carried_by_in = {0: [(nm, 0) for nm in rest] + ([("in", 1)] if depth > 1 else [])}
    carried_by_out = {0: [(nm, 1) for nm in rest] if depth > 1 else []}
    for l in range(1, depth - 1):
        carried_by_in[l] = [(nm, l + 1) for nm in ("in",) + rest]

    def carried(keys):
        return _gather_comm([(shards[nm], ly) for nm, ly in keys]) if keys else None

    def unpack(res, keys):
        if not keys:
            return res
        gathered.update(zip(keys, res[1]))
        return res[0]

    conv_w_rows, sgu_g_rows, sgu_b_rows = _unpack_rows(
        jnp.transpose(small_f, (1, 0, 2)).reshape(small_f.shape[1], e),
        [(n_conv * k_taps, e), (n_sgu, e), (n_sgu, e)], e)
    conv_w_full = conv_w_rows.reshape(n_conv, k_taps, e)
    conv_w_blk = jnp.transpose(jnp.pad(conv_w_full, ((0, 0), (0, 1), (0, 0))).reshape(n_conv, kp, ncb, cb), (0, 2, 1, 3))
    sgu_ln_g_full = sgu_g_rows.reshape(n_sgu, 1, e)
    sgu_ln_b_full = sgu_b_rows.reshape(n_sgu, 1, e)
    sgu_bt = jnp.transpose(sgu_b, (0, 2, 1))

    norm_g3 = norm_g[:, None]
    pl_norm_g3 = pl_norm_g[:, None]
    conv_b3, conv_ln_g3, conv_ln_b3 = conv_b[:, None], conv_ln_g[:, None], conv_ln_b[:, None]
    p3 = p.reshape(depth, t, ple)

    xs, hs, projs, us, x1s, y1s, weights = [], [], [], [], [], {}, []
    xc = x.reshape(t, d)
    for l in range(depth):
        j = l // 2
        xs.append(xc)
        keys = carried_by_in.get(l, [])
        h, proj = unpack(_fwd_in(xc, norm_g3, gathered["in", l], l, tm_mm, carried(keys)), keys)
        if l % 2 == 0:
            u, y1 = _fwd_conv(proj, conv_w_blk, conv_b3, conv_ln_g3, conv_ln_b3, j, bsz, seq, tm_mix)
            y1s[l] = y1
        else:
            u = _fwd_sgu(proj, sgu_ln_g_full, sgu_ln_b_full, sgu_w, sgu_bt, j, tm_mix)
        w_out_l, gate_l, proj_l = gathered["out", l].reshape(1, e, d), gathered["gate", l].reshape(1, d, d), gathered["proj", l]
        weights.append((gathered["in", l], w_out_l, gate_l, proj_l))
        keys = carried_by_out.get(l, [])
        x1, xc = unpack(_fwd_out(xc, u, w_out_l, pl_norm_g3, gate_l, p3, proj_l, l, tm_mm, carried(keys)), keys)
        hs.append(h)
        projs.append(proj)
        us.append(u)
        x1s.append(x1)

    loss_local, dx, d_final_g = _loss_head(xc, final_g[None], loss_target.reshape(t, d), tm_mm)
    loss = lax.psum(loss_local[0, 0], ("x", "y", "c"))

    c_arr = my_c.astype(jnp.int32).reshape(1)
    k_arr = my_k.astype(jnp.int32).reshape(1)
    by_chip = lambda a: a.reshape((1, N_CHIPS) + a.shape[1:])
    d_norm_g, d_pl_norm_g = [None] * depth, [None] * depth
    d_conv = [None] * n_conv
    d_sgu = [None] * n_sgu
    pairs, arrived = [None] * depth, [None] * depth
    in_flight = None
    for l in reversed(range(depth)):
        j = l // 2
        w_in_l, w_out_l, gate_l, proj_l = weights[l]
        if in_flight is None:
            dx1, du, rn, ds, dqv, d_pl_norm_g[l] = _bwd_out(dx, x1s[l], p3, pl_norm_g3, gate_l, proj_l, w_out_l, l, tm_mm)
        else:
            (dx1, du, rn, ds, dqv, d_pl_norm_g[l]), arrived[in_flight[0]] = _bwd_out(
                dx, x1s[l], p3, pl_norm_g3, gate_l, proj_l, w_out_l, l, tm_mm, _scatter_comm(in_flight[1]))
        (g_proj,) = _wgrad(p3, dqv, 1, 1, N_CHIPS, tm_mm, "wgrad_proj", a_layer=l)
        (g_gate,) = _wgrad(rn, ds, 1, 1, 1, tm_mm, "wgrad_gate")
        (g_out,) = _wgrad(us[l], dx1, 1, 1, 1, tm_mm, "wgrad_out")
        if l % 2 == 0:
            dproj, dcw, dcb, dlg, dlb = _bwd_conv(du, projs[l], y1s[l], conv_w_blk, conv_ln_g3, conv_ln_b3, j, bsz, seq, tm_mix)
            d_conv[j] = (dcw, dcb, dlg, dlb)
        else:
            dproj, dsw, dsbt, dlg, dlb = _bwd_sgu(du, projs[l], sgu_ln_g_full, sgu_ln_b_full, sgu_w, sgu_bt, j, tm_mix)
            d_sgu[j] = (dsw, dsbt, dlg, dlb)
        (g_in,) = _wgrad(hs[l], dproj, 1, N_CHIPS, 1, tm_mm, "wgrad_in")
        local = [g_in.reshape(N_CHIPS, 2, d // 2, n4), g_out.reshape(N_CHIPS, 2, e // (2 * N_CHIPS), d),
                 g_gate.reshape(N_CHIPS, 2, d // (2 * N_CHIPS), d), g_proj.reshape(N_CHIPS, 2, ple // 2, dq)]
        if l > 0:
            (dx, d_norm_g[l]), from_sibling = _bwd_in(dproj, dx1, xs[l], norm_g3, w_in_l, l, tm_mm, _swap_comm(local))
            sums = [_pair_sum(gl, rc, c_arr, BF16) for gl, rc in zip(local, from_sibling)]
            in_flight = (l, [by_chip(wire) for _, wire in sums])
        else:
            from_sibling = _run_comm(_swap_comm(local), "swap_last")
            sums = [_pair_sum(gl, rc, c_arr, BF16) for gl, rc in zip(local, from_sibling)]
            (dx, d_norm_g[l]), arrived[l] = _bwd_in(dproj, dx1, xs[l], norm_g3, w_in_l, l, tm_mm,
                                                    _scatter_comm([by_chip(wire) for _, wire in sums]))
        pairs[l] = [by_chip(s32) for s32, _ in sums]
    grad_x = dx.reshape(bsz, seq, d)

    d_conv_w = jnp.stack([jnp.transpose(dc[0], (1, 0, 2)).reshape(kp, e)[:k_taps] for dc in d_conv])
    d_conv_b = jnp.stack([dc[1][0] for dc in d_conv])
    d_conv_ln_g = jnp.stack([dc[2][0] for dc in d_conv])
    d_conv_ln_b = jnp.stack([dc[3][0] for dc in d_conv])
    d_sgu_w = jnp.stack([dsg[0] for dsg in d_sgu])
    d_sgu_b = jnp.stack([jnp.transpose(dsg[1]) for dsg in d_sgu])
    d_sgu_ln_g = jnp.stack([dsg[2][0] for dsg in d_sgu])
    d_sgu_ln_b = jnp.stack([dsg[3][0] for dsg in d_sgu])
    small_grads = [jnp.concatenate(d_norm_g), d_conv_w, d_conv_b, d_conv_ln_g, d_conv_ln_b, d_sgu_ln_g, d_sgu_ln_b,
                   d_sgu_w, d_sgu_b, jnp.concatenate(d_pl_norm_g), d_final_g]
    small_shapes = [a.shape for a in small_grads]
    packed = _pack_rows(small_grads, d)
    pack_rows = packed.shape[0] + ((-packed.shape[0]) % (8 * SUBLANES))
    packed = _pack_rows(small_grads, d, pack_rows)
    gl_small = packed.reshape(N_CHIPS, 2, pack_rows // 8, d)
    (small_sibling,) = _run_comm(_swap_comm([gl_small]), "swap_small")
    small_pair, _ = _pair_sum(gl_small, small_sibling, c_arr, F32)
    small_pair = by_chip(small_pair)

    (small_arrived,) = _run_comm(_scatter_comm([small_pair]), "scatter_small")
    reduced = [lax.empty((depth,) + pr.shape[2:], F32) for pr in pairs[0]]
    for l in range(depth):
        reduced = [_chip_sum(pr, ar, k_arr, acc, l) for pr, ar, acc in zip(pairs[l], arrived[l], reduced)]
    small_mine = _chip_sum(small_pair, small_arrived, k_arr, lax.empty((1,) + small_pair.shape[2:], F32), 0)
    theirs = _swap_pieces(reduced)
    small_all = _gather_pieces(small_mine[:, None])
    small_all = lax.dynamic_update_slice(small_all, small_mine[:, None], (my_k, my_c, 0, 0)).reshape(pack_rows, d)
    small_red = _unpack_rows(small_all, small_shapes, d)
    (gr_norm_g, gr_conv_w, gr_conv_b, gr_conv_ln_g, gr_conv_ln_b, gr_sgu_ln_g, gr_sgu_ln_b, gr_sgu_w, gr_sgu_b,
     gr_pl_norm_g, gr_final_g) = small_red
    gr_final_g = gr_final_g.reshape(d)
    gr_conv_w = lax.dynamic_slice_in_dim(gr_conv_w, my_k * ec, ec, axis=2)
    gr_sgu_ln_g = lax.dynamic_slice_in_dim(gr_sgu_ln_g, my_k * ec, ec, axis=1)
    gr_sgu_ln_b = lax.dynamic_slice_in_dim(gr_sgu_ln_b, my_k * ec, ec, axis=1)

    up_in = _adamw_halves(w_in, reduced[0], theirs[0], m_w_in, v_w_in, c_arr)
    up_out = _adamw_halves(w_out, reduced[1], theirs[1], m_w_out, v_w_out, c_arr)
    up_gate = _adamw_halves(pl_gate_w, reduced[2], theirs[2], m_pl_gate_w, v_pl_gate_w, c_arr)
    up_proj = _adamw_halves(pl_proj_w, reduced[3], theirs[3], m_pl_proj_w, v_pl_proj_w, c_arr)

    small_names = ["norm_g", "conv_w", "conv_b", "conv_ln_g", "conv_ln_b", "sgu_ln_g", "sgu_ln_b", "sgu_w", "sgu_b",
                   "pl_norm_g", "final_g"]
    small_w_list = [norm_g, conv_w, conv_b, conv_ln_g, conv_ln_b, sgu_ln_g, sgu_ln_b, sgu_w, sgu_b, pl_norm_g, final_g]
    small_m_list = [m_norm_g, m_conv_w, m_conv_b, m_conv_ln_g, m_conv_ln_b, m_sgu_ln_g, m_sgu_ln_b, m_sgu_w, m_sgu_b,
                    m_pl_norm_g, m_final_g]
    small_v_list = [v_norm_g, v_conv_w, v_conv_b, v_conv_ln_g, v_conv_ln_b, v_sgu_ln_g, v_sgu_ln_b, v_sgu_w, v_sgu_b,
                    v_pl_norm_g, v_final_g]
    small_g_list = [gr_norm_g, gr_conv_w, gr_conv_b, gr_conv_ln_g, gr_conv_ln_b, gr_sgu_ln_g, gr_sgu_ln_b, gr_sgu_w,
                    gr_sgu_b, gr_pl_norm_g, gr_final_g]
    width = ec
    shapes_local = [a.shape for a in small_w_list]
    outs_small = _adamw(_pack_rows(small_w_list, width), _pack_rows(small_g_list, width),
                        _pack_rows(small_m_list, width), _pack_rows(small_v_list, width))
    unpacked = [_unpack_rows(o, shapes_local, width) for o in outs_small]
    ups = {name: [unpacked[kind][i] for kind in range(4)] for i, name in enumerate(small_names)}
    ups["w_in"], ups["w_out"], ups["pl_gate_w"], ups["pl_proj_w"] = up_in, up_out, up_gate, up_proj

    order = ["norm_g", "w_in", "w_out", "conv_w", "conv_b", "conv_ln_g", "conv_ln_b", "sgu_ln_g", "sgu_ln_b", "sgu_w",
             "sgu_b", "pl_norm_g", "pl_gate_w", "pl_proj_w", "final_g"]
    result = [loss, grad_x]
    for kind in range(4):
        result.extend(ups[name][kind] for name in order)
    return tuple(result)
```

```python
import functools

import jax
import jax.numpy as jnp
from jax import lax
from jax.experimental import pallas as pl
from jax.experimental.pallas import tpu as pltpu

F32 = jnp.float32
BF16 = jnp.bfloat16
SDS = jax.ShapeDtypeStruct

EPS = 1e-6
CHUNK = 128
GROUPS = 8
HALO = 32
N_CHIPS = 4
LANES = 128
SUBLANES = 8
V7X_VMEM_LIMIT = 56 << 20

ADAM_LR = 0.001
ADAM_B1 = 0.9
ADAM_B2 = 0.999
ADAM_EPS = 1e-08
ADAM_WD = 0.01
ADAM_STEP = 10

MESH_IDS = pl.DeviceIdType.MESH
ANY = pl.BlockSpec(memory_space=pl.ANY)


def _params(n_axes):
    return pltpu.CompilerParams(dimension_semantics=("arbitrary",) * n_axes, vmem_limit_bytes=V7X_VMEM_LIMIT)


def _const(shape):
    zeros = (0,) * len(shape)
    return pl.BlockSpec(shape, lambda *_: zeros)


def _layer(shape, layer):
    zeros = (0,) * len(shape)
    return pl.BlockSpec((None,) + tuple(shape), lambda *_: (layer,) + zeros, pipeline_mode=pl.Buffered(1))


class _Comm:
    def __init__(self, operands, out_shape, sems, start, finish):
        self.operands, self.out_shape, self.sems, self.start, self.finish = operands, out_shape, sems, start, finish


def _call(body, *, name, grid, in_specs, out_specs, out_shape, operands, scratch_shapes=(), comm=None):
    in_specs, out_specs, out_shape, scratch_shapes = list(in_specs), list(out_specs), list(out_shape), list(scratch_shapes)
    if comm is None:
        return pl.pallas_call(body, name=name, grid=grid, in_specs=in_specs, out_specs=out_specs, out_shape=out_shape,
                              scratch_shapes=scratch_shapes, compiler_params=_params(len(grid)))(*operands)
    n_in, n_out, n_sc = len(in_specs), len(out_specs), len(scratch_shapes)
    ci, co = len(comm.operands), len(comm.out_shape)

    def hosted(*refs):
        ins, cins = refs[:n_in], refs[n_in:n_in + ci]
        outs, couts = refs[n_in + ci:n_in + ci + n_out], refs[n_in + ci + n_out:n_in + ci + n_out + co]
        scratch = refs[n_in + ci + n_out + co:n_in + ci + n_out + co + n_sc]
        sems = refs[n_in + ci + n_out + co + n_sc:]
        first = functools.reduce(jnp.logical_and, [pl.program_id(a) == 0 for a in range(len(grid))])
        last = functools.reduce(jnp.logical_and, [pl.program_id(a) == g - 1 for a, g in enumerate(grid)])

        @pl.when(first)
        def _():
            comm.start(cins, couts, sems)

        body(*ins, *outs, *scratch)

        @pl.when(last)
        def _():
            comm.finish(cins, couts, sems)

    res = pl.pallas_call(
        hosted, name=name, grid=grid, in_specs=in_specs + [ANY] * ci, out_specs=out_specs + [ANY] * co,
        out_shape=out_shape + list(comm.out_shape), scratch_shapes=scratch_shapes + list(comm.sems),
        compiler_params=_params(len(grid)))(*operands, *comm.operands)
    return res[:n_out], res[n_out:]


def _run_comm(comm, name):
    ci, co = len(comm.operands), len(comm.out_shape)

    def body(*refs):
        comm.start(refs[:ci], refs[ci:ci + co], refs[ci + co:])
        comm.finish(refs[:ci], refs[ci:ci + co], refs[ci + co:])

    return pl.pallas_call(body, name=name, in_specs=[ANY] * ci, out_specs=[ANY] * co, out_shape=list(comm.out_shape),
                          scratch_shapes=list(comm.sems))(*comm.operands)


def _sigmoid(v):
    return jax.nn.sigmoid(v)


def _dsilu(v, s):
    return s * (1.0 + v * (1.0 - s))


def _gelu_parts(v):
    cdf = 0.5 * (1.0 + lax.erf(v * 0.7071067811865476))
    pdf = jnp.exp(-0.5 * v * v) * 0.3989422804014327
    return v * cdf, cdf + v * pdf


def _gelu(v):
    return 0.5 * v * (1.0 + lax.erf(v * 0.7071067811865476))


def _rms_stats(x):
    r = lax.rsqrt(jnp.mean(x * x, axis=-1, keepdims=True) + EPS)
    return r, x * r


def _rms_bwd(dy, g, r, xh):
    gdy = dy * g
    return r * (gdy - xh * jnp.mean(xh * gdy, axis=-1, keepdims=True))


def _ln_stats(x):
    mu = jnp.mean(x, axis=-1, keepdims=True)
    xc = x - mu
    rs = lax.rsqrt(jnp.mean(xc * xc, axis=-1, keepdims=True) + EPS)
    return rs, xc * rs


def _ln_bwd(dy, g, rs, xh):
    dxh = dy * g
    return rs * (dxh - jnp.mean(dxh, axis=-1, keepdims=True) - xh * jnp.mean(dxh * xh, axis=-1, keepdims=True))


def _dot(a, b):
    return jnp.dot(a, b, preferred_element_type=F32)


def _dot_nt(a, b):
    return lax.dot_general(a, b, (((1,), (1,)), ((), ())), preferred_element_type=F32)


def _dot_tn(a, b):
    return lax.dot_general(a, b, (((0,), (0,)), ((), ())), preferred_element_type=F32)


def _fwd_in(x, norm_g, w_in_full, layer, tm, comm=None):
    t, d = x.shape
    _, nk, _, n4 = w_in_full.shape

    def body(x_ref, g_ref, w_ref, h_ref, proj_ref):
        r, xh = _rms_stats(x_ref[...])
        h = (xh * g_ref[...]).astype(BF16)
        h_ref[...] = h
        for k in range(nk):
            proj_ref[:, k * n4:(k + 1) * n4] = _dot(h, w_ref[k]).astype(BF16)

    return _call(
        body, name="fwd_in", grid=(t // tm,),
        in_specs=[pl.BlockSpec((tm, d), lambda i: (i, 0)), _layer((1, d), layer), _layer((nk, d, n4), 0)],
        out_specs=[pl.BlockSpec((tm, d), lambda i: (i, 0)), pl.BlockSpec((tm, nk * n4), lambda i: (i, 0))],
        out_shape=[SDS((t, d), BF16), SDS((t, nk * n4), BF16)],
        operands=(x, norm_g, w_in_full), comm=comm)


def _halo_maps(nt, hb, n_halo_blocks):
    def prev(b, i):
        return (jnp.maximum((b * nt + i) * hb - 1, 0), 0)

    def nxt(b, i):
        return (jnp.minimum((b * nt + i + 1) * hb, n_halo_blocks - 1), 0)

    return prev, nxt


def _col_blocks(e):
    cb = min(2 * LANES, e)
    return cb, e // cb


def _conv_taps(src_ref, w_ref, dst_ref, cb_idx, n_rows, rb, first, reverse):
    k_taps = w_ref.shape[1] - 1
    for r0 in range(0, n_rows, rb):
        acc = None
        for res in range(SUBLANES):
            rows = rb + (SUBLANES if res else 0)
            group = None
            for k in range(k_taps):
                off = first + k
                if off % SUBLANES != res:
                    continue
                wk = w_ref[cb_idx, pl.ds((k_taps - 1 - k) if reverse else k, 1), :]
                term = wk * src_ref[cb_idx, pl.ds(r0 + off - res, rows), :]
                group = term if group is None else group + term
            if group is None:
                continue
            part = group[res:res + rb] if res else group
            acc = part if acc is None else acc + part
        dst_ref[cb_idx, pl.ds(r0, rb), :] = acc


def _fwd_conv(proj, conv_w_blk, conv_b, ln_g, ln_b, layer, bsz, seq, tm, comm=None):
    t, e3 = proj.shape
    e = e3 // 3
    nt = seq // tm
    hb = tm // HALO
    cb, ncb = _col_blocks(e)
    rb = min(64, tm)
    kp = conv_w_blk.shape[2]
    prev, _ = _halo_maps(nt, hb, t // HALO)

    def body(proj_ref, halo_ref, w_ref, b_ref, g_ref, lb_ref, u_ref, y1_ref, y0s, y1s):
        i = pl.program_id(1)
        a = proj_ref[:, 0:e].astype(F32)
        b = proj_ref[:, e:2 * e].astype(F32)
        y0 = a * _sigmoid(b)
        ah = halo_ref[:, 0:e].astype(F32)
        bh = halo_ref[:, e:2 * e].astype(F32)
        y0h = jnp.where(i > 0, ah * _sigmoid(bh), 0.0)
        for c in range(ncb):
            y0s[c, 0:HALO, :] = y0h[:, c * cb:(c + 1) * cb]
            y0s[c, HALO:HALO + tm, :] = y0[:, c * cb:(c + 1) * cb]

        def per_block(c, carry):
            _conv_taps(y0s, w_ref, y1s, c, tm, rb, HALO - (kp - 2), False)
            return carry

        lax.fori_loop(0, ncb, per_block, 0)
        y1 = jnp.concatenate([y1s[c] for c in range(ncb)], axis=1) + b_ref[...]
        y1_ref[...] = y1
        rs, xh = _ln_stats(y1)
        y2 = xh * g_ref[...] + lb_ref[...]
        y = y2 * _sigmoid(y2)
        z = proj_ref[:, 2 * e:3 * e].astype(F32)
        u_ref[...] = (y * (z * _sigmoid(z))).astype(BF16)

    return _call(
        body, name="fwd_conv", grid=(bsz, nt),
        in_specs=[pl.BlockSpec((tm, e3), lambda b, i: (b * nt + i, 0)),
                  pl.BlockSpec((HALO, 2 * e), prev),
                  _layer((ncb, kp, cb), layer), _layer((1, e), layer), _layer((1, e), layer), _layer((1, e), layer)],
        out_specs=[pl.BlockSpec((tm, e), lambda b, i: (b * nt + i, 0)), pl.BlockSpec((tm, e), lambda b, i: (b * nt + i, 0))],
        out_shape=[SDS((t, e), BF16), SDS((t, e), F32)],
        scratch_shapes=[pltpu.VMEM((ncb, HALO + tm, cb), F32), pltpu.VMEM((ncb, tm, cb), F32)],
        operands=(proj, proj, conv_w_blk, conv_b, ln_g, ln_b), comm=comm)


def _tril_mask():
    rows = lax.broadcasted_iota(jnp.int32, (CHUNK, CHUNK), 0)
    cols = lax.broadcasted_iota(jnp.int32, (CHUNK, CHUNK), 1)
    return rows >= cols


def _fwd_sgu(proj, ln_g, ln_b, sgu_w, sgu_bt, layer, tm, comm=None):
    t, e3 = proj.shape
    e = e3 // 3
    gw = e // GROUPS
    nch = tm // CHUNK

    def body(proj_ref, g_ref, lb_ref, w_ref, bt_ref, u_ref, mixed):
        a = proj_ref[:, 0:e].astype(F32)
        b = proj_ref[:, e:2 * e].astype(F32)
        z = proj_ref[:, 2 * e:3 * e].astype(F32)
        rs, xh = _ln_stats(_gelu(b))
        v = (xh * g_ref[...] + lb_ref[...]).astype(BF16)
        mask = _tril_mask()
        for g in range(GROUPS):
            wm = jnp.where(mask, w_ref[g], 0.0).astype(BF16)
            bias = bt_ref[:, g:g + 1]
            for n in range(nch):
                blk = v[n * CHUNK:(n + 1) * CHUNK, g * gw:(g + 1) * gw]
                mixed[n * CHUNK:(n + 1) * CHUNK, g * gw:(g + 1) * gw] = _dot(wm, blk) + bias
        y = _gelu(a) * mixed[...]
        u_ref[...] = (y * (z * _sigmoid(z))).astype(BF16)

    return _call(
        body, name="fwd_sgu", grid=(t // tm,),
        in_specs=[pl.BlockSpec((tm, e3), lambda i: (i, 0)), _layer((1, e), layer), _layer((1, e), layer),
                  _layer((GROUPS, CHUNK, CHUNK), layer), _layer((CHUNK, GROUPS), layer)],
        out_specs=[pl.BlockSpec((tm, e), lambda i: (i, 0))],
        out_shape=[SDS((t, e), BF16)],
        scratch_shapes=[pltpu.VMEM((tm, e), F32)],
        operands=(proj, ln_g, ln_b, sgu_w, sgu_bt), comm=comm)


def _ple_forward(x1, p_ref, plg_ref, gw_ref, pw_ref):
    nk, _, dq = pw_ref.shape
    r, xh = _rms_stats(x1)
    rn = (xh * plg_ref[...]).astype(BF16)
    gate = _sigmoid(_dot(rn, gw_ref[...]))
    pb = p_ref[...].astype(BF16)
    q = jnp.concatenate([_dot(pb, pw_ref[k]) for k in range(nk)], axis=1)
    return r, xh, rn, gate, q


def _fwd_out(x, u, w_out_full, pl_norm_g, gate_w_full, p, proj_w_full, layer, tm, comm=None):
    t, d = x.shape
    e = u.shape[1]
    ple = p.shape[-1]
    nk, dq = proj_w_full.shape[1], proj_w_full.shape[3]

    def body(x_ref, u_ref, wo_ref, plg_ref, gw_ref, p_ref, pw_ref, x1_ref, x2_ref):
        x1 = x_ref[...] + _dot(u_ref[...], wo_ref[...])
        x1_ref[...] = x1
        _, _, _, gate, q = _ple_forward(x1, p_ref, plg_ref, gw_ref, pw_ref)
        x2_ref[...] = x1 + gate * q

    return _call(
        body, name="fwd_out", grid=(t // tm,),
        in_specs=[pl.BlockSpec((tm, d), lambda i: (i, 0)), pl.BlockSpec((tm, e), lambda i: (i, 0)),
                  _layer((e, d), 0), _layer((1, d), layer), _layer((d, d), 0),
                  pl.BlockSpec((None, tm, ple), lambda i: (layer, i, 0)), _layer((nk, ple, dq), 0)],
        out_specs=[pl.BlockSpec((tm, d), lambda i: (i, 0)), pl.BlockSpec((tm, d), lambda i: (i, 0))],
        out_shape=[SDS((t, d), F32), SDS((t, d), F32)],
        operands=(x, u, w_out_full, pl_norm_g, gate_w_full, p, proj_w_full), comm=comm)


def _loss_head(x, final_g, target, tm):
    t, d = x.shape
    n_steps = t // tm

    def body(x_ref, g_ref, tgt_ref, loss_ref, dx_ref, dg_ref, sq_acc):
        i = pl.program_id(0)

        @pl.when(i == 0)
        def _():
            sq_acc[...] = jnp.zeros_like(sq_acc)
            dg_ref[...] = jnp.zeros_like(dg_ref)

        g = g_ref[...]
        r, xh = _rms_stats(x_ref[...])
        diff = xh * g - tgt_ref[...]
        sq_acc[...] += jnp.sum(diff * diff, axis=0, keepdims=True)
        dout = diff * (1.0 / d)
        dg_ref[...] += jnp.sum(dout * xh, axis=0, keepdims=True)
        dx_ref[...] = _rms_bwd(dout, g, r, xh)

        @pl.when(i == n_steps - 1)
        def _():
            loss_ref[...] = jnp.sum(sq_acc[...], axis=1, keepdims=True) * (0.5 / d)

    return pl.pallas_call(
        body, name="loss_head", grid=(n_steps,),
        in_specs=[pl.BlockSpec((tm, d), lambda i: (i, 0)), _const((1, d)), pl.BlockSpec((tm, d), lambda i: (i, 0))],
        out_specs=[_const((1, 1)), pl.BlockSpec((tm, d), lambda i: (i, 0)), _const((1, d))],
        out_shape=[SDS((1, 1), F32), SDS((t, d), F32), SDS((1, d), F32)],
        scratch_shapes=[pltpu.VMEM((1, d), F32)],
        compiler_params=_params(1),
    )(x, final_g, target)


def _bwd_out(dx2, x1, p, pl_norm_g, gate_w_full, proj_w_full, w_out_full, layer, tm, comm=None):
    t, d = dx2.shape
    e = w_out_full.shape[1]
    ple = p.shape[-1]
    nk, dq_w = proj_w_full.shape[1], proj_w_full.shape[3]

    def body(dx2_ref, x1_ref, p_ref, plg_ref, gw_ref, pw_ref, wo_ref, dx1_ref, du_ref, rn_ref, ds_ref, dq_ref, dplg_ref):
        @pl.when(pl.program_id(0) == 0)
        def _():
            dplg_ref[...] = jnp.zeros_like(dplg_ref)

        dx2v = dx2_ref[...]
        r, xh, rn, gate, q = _ple_forward(x1_ref[...], p_ref, plg_ref, gw_ref, pw_ref)
        rn_ref[...] = rn
        dq_ref[...] = (dx2v * gate).astype(BF16)
        ds = (dx2v * q * gate * (1.0 - gate)).astype(BF16)
        ds_ref[...] = ds
        dr = _dot_nt(ds, gw_ref[...])
        dplg_ref[...] += jnp.sum(dr * xh, axis=0, keepdims=True)
        dx1 = dx2v + _rms_bwd(dr, plg_ref[...], r, xh)
        dx1_ref[...] = dx1
        du_ref[...] = _dot_nt(dx1.astype(BF16), wo_ref[...]).astype(BF16)

    row = lambda w: pl.BlockSpec((tm, w), lambda i: (i, 0))
    return _call(
        body, name="bwd_out", grid=(t // tm,),
        in_specs=[row(d), row(d), pl.BlockSpec((None, tm, ple), lambda i: (layer, i, 0)),
                  _layer((1, d), layer), _layer((d, d), 0), _layer((nk, ple, dq_w), 0), _layer((e, d), 0)],
        out_specs=[row(d), row(e), row(d), row(d), row(d), _const((1, d))],
        out_shape=[SDS((t, d), F32), SDS((t, e), BF16), SDS((t, d), BF16), SDS((t, d), BF16), SDS((t, d), BF16),
                   SDS((1, d), F32)],
        operands=(dx2, x1, p, pl_norm_g, gate_w_full, proj_w_full, w_out_full), comm=comm)


def _bwd_conv(du, proj, y1, conv_w_blk, ln_g, ln_b, layer, bsz, seq, tm, comm=None):
    t, e3 = proj.shape
    e = e3 // 3
    nt = seq // tm
    hb = tm // HALO
    cb, ncb = _col_blocks(e)
    rb = min(64, tm)
    kp = conv_w_blk.shape[2]
    k_taps = kp - 1
    prev, nxt = _halo_maps(nt, hb, t // HALO)
    z_halo = lambda b, i: (nxt(b, i)[0], 2)

    def ln_silu_bwd(du_v, z_v, y1_v, g, lb):
        rs, xh = _ln_stats(y1_v)
        y2 = xh * g + lb
        sg = _sigmoid(y2)
        sz = _sigmoid(z_v)
        dy = du_v * (z_v * sz)
        dy2 = dy * _dsilu(y2, sg)
        return _ln_bwd(dy2, g, rs, xh), dy2, xh, du_v * (y2 * sg) * _dsilu(z_v, sz)

    def body(du_ref, proj_ref, y1_ref, duh_ref, zh_ref, y1h_ref, abh_ref, w_ref, g_ref, lb_ref,
             dproj_ref, dw_ref, dcb_ref, dg_ref, dlb_ref, y0s, dy1s, dy0s, ysh):
        b_id, i = pl.program_id(0), pl.program_id(1)

        @pl.when((b_id == 0) & (i == 0))
        def _():
            dw_ref[...] = jnp.zeros_like(dw_ref)
            dcb_ref[...] = jnp.zeros_like(dcb_ref)
            dg_ref[...] = jnp.zeros_like(dg_ref)
            dlb_ref[...] = jnp.zeros_like(dlb_ref)

        g, lb = g_ref[...], lb_ref[...]
        a = proj_ref[:, 0:e].astype(F32)
        b = proj_ref[:, e:2 * e].astype(F32)
        z = proj_ref[:, 2 * e:3 * e].astype(F32)
        sb = _sigmoid(b)
        y0 = a * sb
        dy1, dy2, xh, dz = ln_silu_bwd(du_ref[...].astype(F32), z, y1_ref[...], g, lb)
        dproj_ref[:, 2 * e:3 * e] = dz.astype(BF16)
        dg_ref[...] += jnp.sum(dy2 * xh, axis=0, keepdims=True)
        dlb_ref[...] += jnp.sum(dy2, axis=0, keepdims=True)
        dcb_ref[...] += jnp.sum(dy1, axis=0, keepdims=True)
        dy1h, _, _, _ = ln_silu_bwd(duh_ref[...].astype(F32), zh_ref[...].astype(F32), y1h_ref[...], g, lb)
        dy1h = jnp.where(i < nt - 1, dy1h, 0.0)
        ah = abh_ref[:, 0:e].astype(F32)
        bh = abh_ref[:, e:2 * e].astype(F32)
        y0h = jnp.where(i > 0, ah * _sigmoid(bh), 0.0)
        for c in range(ncb):
            cols = slice(c * cb, (c + 1) * cb)
            y0s[c, 0:HALO, :] = y0h[:, cols]
            y0s[c, HALO:HALO + tm, :] = y0[:, cols]
            dy1s[c, 0:tm, :] = dy1[:, cols]
            dy1s[c, tm:tm + HALO, :] = dy1h[:, cols]

        def per_block(c, carry):
            _conv_taps(dy1s, w_ref, dy0s, c, tm, rb, 0, True)
            for res in range(1, SUBLANES):
                ysh[res - 1] = y0s[c, pl.ds(res, tm + HALO - SUBLANES), :]
            for k in range(k_taps):
                off = HALO - (k_taps - 1) + k
                res = off % SUBLANES
                acc = None
                for r0 in range(0, tm, rb):
                    rows = pl.ds(r0 + off - res, rb)
                    shifted = ysh[res - 1, rows, :] if res else y0s[c, rows, :]
                    term = dy1s[c, pl.ds(r0, rb), :] * shifted
                    acc = term if acc is None else acc + term
                dw_ref[c, pl.ds(k, 1), :] += jnp.sum(acc, axis=0, keepdims=True)
            return carry

        lax.fori_loop(0, ncb, per_block, 0)
        dy0 = jnp.concatenate([dy0s[c] for c in range(ncb)], axis=1)
        dproj_ref[:, 0:e] = (dy0 * sb).astype(BF16)
        dproj_ref[:, e:2 * e] = (dy0 * a * sb * (1.0 - sb)).astype(BF16)

    tile = lambda w: pl.BlockSpec((tm, w), lambda b, i: (b * nt + i, 0))
    return _call(
        body, name="bwd_conv", grid=(bsz, nt),
        in_specs=[tile(e), tile(e3), tile(e),
                  pl.BlockSpec((HALO, e), nxt), pl.BlockSpec((HALO, e), z_halo), pl.BlockSpec((HALO, e), nxt),
                  pl.BlockSpec((HALO, 2 * e), prev),
                  _layer((ncb, kp, cb), layer), _layer((1, e), layer), _layer((1, e), layer)],
        out_specs=[tile(e3), _const((ncb, kp, cb)), _const((1, e)), _const((1, e)), _const((1, e))],
        out_shape=[SDS((t, e3), BF16), SDS((ncb, kp, cb), F32), SDS((1, e), F32), SDS((1, e), F32), SDS((1, e), F32)],
        scratch_shapes=[pltpu.VMEM((ncb, HALO + tm, cb), F32), pltpu.VMEM((ncb, tm + HALO, cb), F32),
                        pltpu.VMEM((ncb, tm, cb), F32), pltpu.VMEM((SUBLANES - 1, tm + HALO - SUBLANES, cb), F32)],
        operands=(du, proj, y1, du, proj, y1, proj, conv_w_blk, ln_g, ln_b), comm=comm)


def _bwd_sgu(du, proj, ln_g, ln_b, sgu_w, sgu_bt, layer, tm, comm=None):
    t, e3 = proj.shape
    e = e3 // 3
    gw = e // GROUPS
    nch = tm // CHUNK

    def body(du_ref, proj_ref, g_ref, lb_ref, w_ref, bt_ref, dproj_ref, dw_ref, dbt_ref, dg_ref, dlb_ref, mixed, dmix, dv):
        @pl.when(pl.program_id(0) == 0)
        def _():
            dw_ref[...] = jnp.zeros_like(dw_ref)
            dbt_ref[...] = jnp.zeros_like(dbt_ref)
            dg_ref[...] = jnp.zeros_like(dg_ref)
            dlb_ref[...] = jnp.zeros_like(dlb_ref)

        g, lb = g_ref[...], lb_ref[...]
        a = proj_ref[:, 0:e].astype(F32)
        b = proj_ref[:, e:2 * e].astype(F32)
        z = proj_ref[:, 2 * e:3 * e].astype(F32)
        ug, dug = _gelu_parts(a)
        vb, dvb_db = _gelu_parts(b)
        rs, xh = _ln_stats(vb)
        v = (xh * g + lb).astype(BF16)
        mask = _tril_mask()
        for gi in range(GROUPS):
            wm = jnp.where(mask, w_ref[gi], 0.0).astype(BF16)
            bias = bt_ref[:, gi:gi + 1]
            for n in range(nch):
                blk = v[n * CHUNK:(n + 1) * CHUNK, gi * gw:(gi + 1) * gw]
                mixed[n * CHUNK:(n + 1) * CHUNK, gi * gw:(gi + 1) * gw] = _dot(wm, blk) + bias
        mx = mixed[...]
        sz = _sigmoid(z)
        duv = du_ref[...].astype(F32)
        dy = duv * (z * sz)
        dproj_ref[:, 2 * e:3 * e] = (duv * (ug * mx) * _dsilu(z, sz)).astype(BF16)
        dproj_ref[:, 0:e] = (dy * mx * dug).astype(BF16)
        dmix[...] = dy * ug
        for gi in range(GROUPS):
            wm = jnp.where(mask, w_ref[gi], 0.0).astype(BF16)
            dw_acc = None
            db_acc = None
            for n in range(nch):
                rows, cols = slice(n * CHUNK, (n + 1) * CHUNK), slice(gi * gw, (gi + 1) * gw)
                dm = dmix[rows, cols]
                dmb = dm.astype(BF16)
                dw_n = _dot_nt(dmb, v[rows, cols])
                db_n = jnp.sum(dm, axis=1, keepdims=True)
                dw_acc = dw_n if dw_acc is None else dw_acc + dw_n
                db_acc = db_n if db_acc is None else db_acc + db_n
                dv[rows, cols] = _dot_tn(wm, dmb)
            dw_ref[gi] += jnp.where(mask, dw_acc, 0.0)
            dbt_ref[:, gi:gi + 1] += db_acc
        dvv = dv[...]
        dg_ref[...] += jnp.sum(dvv * xh, axis=0, keepdims=True)
        dlb_ref[...] += jnp.sum(dvv, axis=0, keepdims=True)
        dproj_ref[:, e:2 * e] = (_ln_bwd(dvv, g, rs, xh) * dvb_db).astype(BF16)

    return _call(
        body, name="bwd_sgu", grid=(t // tm,),
        in_specs=[pl.BlockSpec((tm, e), lambda i: (i, 0)), pl.BlockSpec((tm, e3), lambda i: (i, 0)),
                  _layer((1, e), layer), _layer((1, e), layer),
                  _layer((GROUPS, CHUNK, CHUNK), layer), _layer((CHUNK, GROUPS), layer)],
        out_specs=[pl.BlockSpec((tm, e3), lambda i: (i, 0)), _const((GROUPS, CHUNK, CHUNK)), _const((CHUNK, GROUPS)),
                   _const((1, e)), _const((1, e))],
        out_shape=[SDS((t, e3), BF16), SDS((GROUPS, CHUNK, CHUNK), F32), SDS((CHUNK, GROUPS), F32),
                   SDS((1, e), F32), SDS((1, e), F32)],
        scratch_shapes=[pltpu.VMEM((tm, e), F32), pltpu.VMEM((tm, e), F32), pltpu.VMEM((tm, e), F32)],
        operands=(du, proj, ln_g, ln_b, sgu_w, sgu_bt), comm=comm)


def _bwd_in(dproj, dx1, x, norm_g, w_in_full, layer, tm, comm=None):
    t, d = x.shape
    _, nk, _, n4 = w_in_full.shape

    def body(dproj_ref, dx1_ref, x_ref, g_ref, w_ref, dx_ref, dg_ref):
        @pl.when(pl.program_id(0) == 0)
        def _():
            dg_ref[...] = jnp.zeros_like(dg_ref)

        dh = None
        for k in range(nk):
            part = _dot_nt(dproj_ref[:, k * n4:(k + 1) * n4], w_ref[k])
            dh = part if dh is None else dh + part
        r, xh = _rms_stats(x_ref[...])
        dg_ref[...] += jnp.sum(dh * xh, axis=0, keepdims=True)
        dx_ref[...] = dx1_ref[...] + _rms_bwd(dh, g_ref[...], r, xh)

    row = lambda w: pl.BlockSpec((tm, w), lambda i: (i, 0))
    return _call(
        body, name="bwd_in", grid=(t // tm,),
        in_specs=[row(nk * n4), row(d), row(d), _layer((1, d), layer), _layer((nk, d, n4), 0)],
        out_specs=[row(d), _const((1, d))],
        out_shape=[SDS((t, d), F32), SDS((1, d), F32)],
        operands=(dproj, dx1, x, norm_g, w_in_full), comm=comm)


def _wgrad(a, b, kblk, nblk, n_split, tm, name, a_layer=None, comm=None):
    t, n = b.shape
    k = a.shape[-1]
    kw, nw = k // kblk, n // nblk
    nws = nw // n_split
    n_steps = t // tm

    def body(a_ref, b_ref, o_ref):
        @pl.when(pl.program_id(2) == 0)
        def _():
            o_ref[...] = jnp.zeros_like(o_ref)

        res = _dot_tn(a_ref[...].astype(BF16), b_ref[...].astype(BF16))
        for s in range(n_split):
            o_ref[s] += res[:, s * nws:(s + 1) * nws]

    if a_layer is None:
        a_spec = pl.BlockSpec((tm, kw), lambda kb, nb, i: (i, kb))
    else:
        a_spec = pl.BlockSpec((None, tm, kw), lambda kb, nb, i: (a_layer, i, kb))
    return _call(
        body, name=name, grid=(kblk, nblk, n_steps),
        in_specs=[a_spec, pl.BlockSpec((tm, nw), lambda kb, nb, i: (i, nb))],
        out_specs=[pl.BlockSpec((None, n_split, kw, nws), lambda kb, nb, i: (kb, nb, 0, 0))],
        out_shape=[SDS((kblk, nblk * n_split, kw, nws), F32)],
        operands=(a, b), comm=comm)


def _row_tile(rows, cols, budget_bytes=1 << 20):
    best = None
    for cand in range(SUBLANES, rows + 1, SUBLANES):
        if rows % cand == 0 and cand * cols * 4 <= budget_bytes:
            best = cand
    return best if best is not None else rows


def _pair_sum(grads, recv, my_c, wire_dtype):
    n, _, h, c = grads.shape
    th = _row_tile(h, c)

    def body(c_ref, g_ref, r_ref, o_ref):
        o_ref[...] = (g_ref[...] + r_ref[...]).astype(wire_dtype)

    grid_spec = pltpu.PrefetchScalarGridSpec(
        num_scalar_prefetch=1, grid=(n, h // th),
        in_specs=[pl.BlockSpec((None, None, th, c), lambda j, i, c_ref: (j, c_ref[0], i, 0)),
                  pl.BlockSpec((None, None, th, c), lambda j, i, c_ref: (j, 0, i, 0))],
        out_specs=pl.BlockSpec((None, th, c), lambda j, i, c_ref: (j, i, 0)))
    return pl.pallas_call(body, name="pair_sum", grid_spec=grid_spec, out_shape=SDS((n, h, c), wire_dtype),
                          compiler_params=_params(2))(my_c, grads, recv)


def _chip_sum(grads, recv, arrived, my_ck, stacked, layer):
    _, _, h, c = grads.shape
    th = _row_tile(h, c)

    def body(ck_ref, g_ref, r_ref, a1_ref, a2_ref, a3_ref, acc_ref, o_ref):
        acc = g_ref[...] + r_ref[...]
        for ref in (a1_ref, a2_ref, a3_ref):
            acc = acc + ref[...].astype(F32)
        o_ref[...] = acc

    def slot(flip):
        return pl.BlockSpec((None, None, th, c), lambda i, ck: (0, jnp.bitwise_xor(ck[1], flip), i, 0))

    grid_spec = pltpu.PrefetchScalarGridSpec(
        num_scalar_prefetch=1, grid=(h // th,),
        in_specs=[pl.BlockSpec((None, None, th, c), lambda i, ck: (ck[1], ck[0], i, 0)),
                  pl.BlockSpec((None, None, th, c), lambda i, ck: (ck[1], 0, i, 0)),
                  slot(1), slot(2), slot(3), ANY],
        out_specs=pl.BlockSpec((None, th, c), lambda i, ck: (layer, i, 0)))
    return pl.pallas_call(body, name="chip_sum", grid_spec=grid_spec, out_shape=SDS(stacked.shape, F32),
                          input_output_aliases={6: 0}, compiler_params=_params(1))(
                              my_ck, grads, recv, arrived, arrived, arrived, stacked)


def _adam_math(w, gv, m, v):
    c1 = 1.0 - ADAM_B1 ** ADAM_STEP
    c2 = 1.0 - ADAM_B2 ** ADAM_STEP
    mn = ADAM_B1 * m + (1.0 - ADAM_B1) * gv
    vn = ADAM_B2 * v + (1.0 - ADAM_B2) * (gv * gv)
    m_hat = mn / c1
    v_hat = vn / c2
    return -ADAM_LR * (m_hat / (jnp.sqrt(v_hat) + ADAM_EPS) + ADAM_WD * w), mn, vn


def _adamw_halves(w, g_mine, g_theirs, m, v, my_c):
    nl, rows, cols = w.shape
    h = rows // 2
    th = _row_tile(h, cols, 512 << 10)
    as4 = lambda a: a.reshape(nl, 2, h, cols)

    def body(c_ref, w_ref, gm_ref, gt_ref, m_ref, v_ref, go_ref, d_ref, mo_ref, vo_ref):
        gv = jnp.where(pl.program_id(1) == c_ref[0], gm_ref[...], gt_ref[...])
        go_ref[...] = gv
        d_ref[...], mo_ref[...], vo_ref[...] = _adam_math(w_ref[...], gv, m_ref[...], v_ref[...])

    full = pl.BlockSpec((None, None, th, cols), lambda j, s, i, c_ref: (j, s, i, 0))
    half = pl.BlockSpec((None, th, cols), lambda j, s, i, c_ref: (j, i, 0))
    grid_spec = pltpu.PrefetchScalarGridSpec(
        num_scalar_prefetch=1, grid=(nl, 2, h // th),
        in_specs=[full, half, half, full, full], out_specs=[full] * 4)
    outs = pl.pallas_call(body, name="adamw_halves", grid_spec=grid_spec, out_shape=[SDS((nl, 2, h, cols), F32)] * 4,
                          compiler_params=_params(3))(my_c, as4(w), g_mine, g_theirs, as4(m), as4(v))
    return [o.reshape(nl, rows, cols) for o in outs]


def _adamw(w, g, m, v):
    rows, cols = w.shape
    tr = _row_tile(rows, cols, 512 << 10)

    def body(w_ref, g_ref, m_ref, v_ref, go_ref, d_ref, mo_ref, vo_ref):
        gv = g_ref[...]
        go_ref[...] = gv
        d_ref[...], mo_ref[...], vo_ref[...] = _adam_math(w_ref[...], gv, m_ref[...], v_ref[...])

    spec = pl.BlockSpec((tr, cols), lambda i: (i, 0))
    return pl.pallas_call(
        body, name="adamw", grid=(rows // tr,), in_specs=[spec] * 4, out_specs=[spec] * 4,
        out_shape=[SDS((rows, cols), F32)] * 4, compiler_params=_params(1))(w, g, m, v)


def _place():
    x, y, c = lax.axis_index("x"), lax.axis_index("y"), lax.axis_index("c")
    chips = [(1 - x, y), (x, 1 - y), (1 - x, 1 - y)]
    return x, y, c, 2 * x + y, chips


def _remote(src, dst, send_sem, recv_sem, device):
    return pltpu.make_async_remote_copy(src_ref=src, dst_ref=dst, send_sem=send_sem, recv_sem=recv_sem,
                                        device_id=device, device_id_type=MESH_IDS)


def _gather_comm(items, small=None):
    shards = [arr for arr, _ in items]
    layers = [layer for _, layer in items]
    n = len(shards)
    extra = 0 if small is None else 1

    def copies(ins, outs, sems):
        ici_send, ici_recv, d2d_send, d2d_recv, own_send, own_recv = sems
        x, y, c, k, chips = _place()
        sibling = (x, y, 1 - c)
        own, ici_out, ici_in, fwd_out, fwd_in = [], [], [], [], []
        for j in range(n):
            h = ins[j].shape[2] // 2
            mine, theirs = pl.ds(c * h, h), pl.ds((1 - c) * h, h)
            own.append(_remote(ins[j].at[pl.ds(layers[j], 1)], outs[j].at[:, pl.ds(k, 1)], own_send.at[j], own_recv.at[j], sibling))
            for ti, (cx, cy) in enumerate(chips):
                s = 3 * j + ti
                ici_out.append(_remote(ins[j].at[pl.ds(layers[j], 1), :, mine], outs[j].at[:, pl.ds(k, 1), mine],
                                       ici_send.at[s], ici_recv.at[s], (cx, cy, c)))
                landed = outs[j].at[:, pl.ds(2 * cx + cy, 1), mine]
                ici_in.append(_remote(landed, landed, ici_send.at[s], ici_recv.at[s], (cx, cy, c)))
                fwd_out.append(_remote(landed, landed, d2d_send.at[s], d2d_recv.at[s], sibling))
                passed = outs[j].at[:, pl.ds(2 * cx + cy, 1), theirs]
                fwd_in.append(_remote(passed, passed, d2d_send.at[s], d2d_recv.at[s], sibling))
        if extra:
            own.append(_remote(ins[n], outs[n].at[pl.ds(k, 1)], own_send.at[n], own_recv.at[n], sibling))
            for ti, (cx, cy) in enumerate(chips):
                s = 3 * n + ti
                ici_out.append(_remote(ins[n], outs[n].at[pl.ds(k, 1)], ici_send.at[s], ici_recv.at[s], (cx, cy, c)))
                slot = outs[n].at[pl.ds(2 * cx + cy, 1)]
                ici_in.append(_remote(slot, slot, ici_send.at[s], ici_recv.at[s], (cx, cy, c)))
        return own, ici_out, ici_in, fwd_out, fwd_in

    def start(ins, outs, sems):
        own, ici_out, _, _, _ = copies(ins, outs, sems)
        for cp in own + ici_out:
            cp.start()

    def finish(ins, outs, sems):
        own, ici_out, ici_in, fwd_out, fwd_in = copies(ins, outs, sems)
        for idx, cp in enumerate(ici_in):
            cp.wait_recv()
            if idx < len(fwd_out):
                fwd_out[idx].start()
        for cp in fwd_in:
            cp.wait_recv()
        for cp in ici_out + fwd_out:
            cp.wait_send()
        for cp in own:
            cp.wait()

    operands = list(shards) + ([small] if extra else [])
    out_shape = [SDS((1, N_CHIPS) + s.shape[2:], s.dtype) for s in shards]
    if extra:
        out_shape.append(SDS((N_CHIPS,) + small.shape[1:], small.dtype))
    sems = [pltpu.SemaphoreType.DMA((3 * (n + extra),)), pltpu.SemaphoreType.DMA((3 * (n + extra),)),
            pltpu.SemaphoreType.DMA((3 * n,)), pltpu.SemaphoreType.DMA((3 * n,)),
            pltpu.SemaphoreType.DMA((n + extra,)), pltpu.SemaphoreType.DMA((n + extra,))]
    return _Comm(operands, out_shape, sems, start, finish)


def _swap_comm(grads):
    n = len(grads)

    def copies(ins, outs, sems):
        send_sem, recv_sem = sems
        x, y, c, _, _ = _place()
        return [_remote(ins[j].at[:, pl.ds(1 - c, 1)], outs[j], send_sem.at[j], recv_sem.at[j], (x, y, 1 - c))
                for j in range(n)]

    def start(ins, outs, sems):
        for cp in copies(ins, outs, sems):
            cp.start()

    def finish(ins, outs, sems):
        for cp in copies(ins, outs, sems):
            cp.wait()

    out_shape = [SDS((g.shape[0], 1) + g.shape[2:], g.dtype) for g in grads]
    return _Comm(list(grads), out_shape, [pltpu.SemaphoreType.DMA((n,)), pltpu.SemaphoreType.DMA((n,))], start, finish)


def _scatter_comm(sums):
    n = len(sums)

    def copies(ins, outs, sems):
        send_sem, recv_sem = sems
        x, y, c, k, chips = _place()
        out, landing = [], []
        for j in range(n):
            for ti, (cx, cy) in enumerate(chips):
                s = 3 * j + ti
                out.append(_remote(ins[j].at[:, pl.ds(2 * cx + cy, 1)], outs[j].at[:, pl.ds(k, 1)],
                                   send_sem.at[s], recv_sem.at[s], (cx, cy, c)))
                slot = outs[j].at[:, pl.ds(2 * cx + cy, 1)]
                landing.append(_remote(slot, slot, send_sem.at[s], recv_sem.at[s], (cx, cy, c)))
        return out, landing

    def start(ins, outs, sems):
        for cp in copies(ins, outs, sems)[0]:
            cp.start()

    def finish(ins, outs, sems):
        out, landing = copies(ins, outs, sems)
        for cp in landing:
            cp.wait_recv()
        for cp in out:
            cp.wait_send()

    out_shape = [SDS(s.shape, s.dtype) for s in sums]
    return _Comm(list(sums), out_shape, [pltpu.SemaphoreType.DMA((3 * n,)), pltpu.SemaphoreType.DMA((3 * n,))], start, finish)


def _swap_pieces(pieces):
    n = len(pieces)

    def body(*refs):
        ins, outs = refs[:n], refs[n:2 * n]
        send_sem, recv_sem = refs[2 * n:]
        x, y, c, _, _ = _place()
        copies = [_remote(ins[j], outs[j], send_sem.at[j], recv_sem.at[j], (x, y, 1 - c)) for j in range(n)]
        for cp in copies:
            cp.start()
        for cp in copies:
            cp.wait()

    return pl.pallas_call(
        body, name="swap_pieces", in_specs=[ANY] * n, out_specs=[ANY] * n,
        out_shape=[SDS(p.shape, p.dtype) for p in pieces],
        scratch_shapes=[pltpu.SemaphoreType.DMA((n,)), pltpu.SemaphoreType.DMA((n,))],
    )(*pieces)


def _gather_pieces(piece):
    def body(in_ref, out_ref, send_sem, recv_sem):
        x, y, c, k, chips = _place()
        peers = [(x, y, 1 - c)] + [(cx, cy, pc) for (cx, cy) in chips for pc in (c, 1 - c)]
        copies = []
        for ti, peer in enumerate(peers):
            cp = _remote(in_ref, out_ref.at[pl.ds(k, 1), pl.ds(c, 1)], send_sem.at[ti], recv_sem.at[ti], peer)
            cp.start()
            copies.append(cp)
        for ti, (px, py, pc) in enumerate(peers):
            _remote(in_ref, out_ref.at[pl.ds(2 * px + py, 1), pl.ds(pc, 1)], send_sem.at[ti], recv_sem.at[ti],
                    (px, py, pc)).wait_recv()
        for cp in copies:
            cp.wait_send()

    n_peers = 2 * N_CHIPS - 1
    return pl.pallas_call(
        body, name="gather_pieces", in_specs=[ANY], out_specs=ANY,
        out_shape=SDS((N_CHIPS, 2) + piece.shape[2:], piece.dtype),
        scratch_shapes=[pltpu.SemaphoreType.DMA((n_peers,)), pltpu.SemaphoreType.DMA((n_peers,))],
    )(piece)


def _pack_rows(parts, width, total_rows=None):
    rows = []
    for a in parts:
        a2 = a.reshape(-1, width)
        pad = (-a2.shape[0]) % SUBLANES
        rows.append(jnp.pad(a2, ((0, pad), (0, 0))) if pad else a2)
    out = jnp.concatenate(rows, axis=0)
    if total_rows is not None and out.shape[0] < total_rows:
        out = jnp.pad(out, ((0, total_rows - out.shape[0]), (0, 0)))
    return out


def _unpack_rows(packed, shapes, width):
    out, r = [], 0
    for shp in shapes:
        size = 1
        for s in shp:
            size *= s
        nr = size // width
        out.append(packed[r:r + nr].reshape(shp))
        r += nr + ((-nr) % SUBLANES)
    return out


def kernel(x, p, norm_g, w_in, w_out, conv_w, conv_b, conv_ln_g, conv_ln_b, sgu_ln_g, sgu_ln_b, sgu_w, sgu_b, pl_norm_g, pl_gate_w, pl_proj_w, final_g, loss_target, m_norm_g, m_w_in, m_w_out, m_conv_w, m_conv_b, m_conv_ln_g, m_conv_ln_b, m_sgu_ln_g, m_sgu_ln_b, m_sgu_w, m_sgu_b, m_pl_norm_g, m_pl_gate_w, m_pl_proj_w, m_final_g, v_norm_g, v_w_in, v_w_out, v_conv_w, v_conv_b, v_conv_ln_g, v_conv_ln_b, v_sgu_ln_g, v_sgu_ln_b, v_sgu_w, v_sgu_b, v_pl_norm_g, v_pl_gate_w, v_pl_proj_w, v_final_g):
    bsz, seq, d = x.shape
    depth = w_in.shape[0]
    e = w_out.shape[1] * N_CHIPS
    e3 = 3 * e
    n4 = w_in.shape[2]
    ple = p.shape[-1]
    dq = pl_proj_w.shape[2]
    k_taps = conv_w.shape[1]
    kp = k_taps + 1
    n_conv, n_sgu = conv_w.shape[0], sgu_ln_g.shape[0]
    t = bsz * seq
    tm_mm = min(512, seq)
    tm_mix = min(256, seq)
    cb, ncb = _col_blocks(e)
    my_c = lax.axis_index("c")
    my_k = 2 * lax.axis_index("x") + lax.axis_index("y")

    ec = e // N_CHIPS
    small_w = _pack_rows([conv_w.reshape(n_conv * k_taps, ec), sgu_ln_g, sgu_ln_b], ec)[None]
    shards = {"in": w_in.astype(BF16)[:, None], "out": w_out.astype(BF16)[:, None],
              "gate": pl_gate_w.astype(BF16)[:, None], "proj": pl_proj_w.astype(BF16)[:, None]}
    rest = ("out", "gate", "proj")
    gathered = {}
    gathered["in", 0], small_f = _run_comm(_gather_comm([(shards["in"], 0)], small_w), "gather_first")
    carried_by_in = {l: [("in", l + 1)] for l in range(depth - 1)}
    carried_by_mix = {l: [(nm, l + 1) for nm in rest] for l in range(depth - 1)}
    carried_by_mix[0] = [(nm, 0) for nm in rest] + carried_by_mix.get(0, [])

    def carried(keys):
        return _gather_comm([(shards[nm], ly) for nm, ly in keys]) if keys else None

    def unpack(res, keys):
        if not keys:
            return res
        gathered.update(zip(keys, res[1]))
        return res[0]

    conv_w_rows, sgu_g_rows, sgu_b_rows = _unpack_rows(
        jnp.transpose(small_f, (1, 0, 2)).reshape(small_f.shape[1], e),
        [(n_conv * k_taps, e), (n_sgu, e), (n_sgu, e)], e)
    conv_w_full = conv_w_rows.reshape(n_conv, k_taps, e)
    conv_w_blk = jnp.transpose(jnp.pad(conv_w_full, ((0, 0), (0, 1), (0, 0))).reshape(n_conv, kp, ncb, cb), (0, 2, 1, 3))
    sgu_ln_g_full = sgu_g_rows.reshape(n_sgu, 1, e)
    sgu_ln_b_full = sgu_b_rows.reshape(n_sgu, 1, e)
    sgu_bt = jnp.transpose(sgu_b, (0, 2, 1))

    norm_g3 = norm_g[:, None]
    pl_norm_g3 = pl_norm_g[:, None]
    conv_b3, conv_ln_g3, conv_ln_b3 = conv_b[:, None], conv_ln_g[:, None], conv_ln_b[:, None]
    p3 = p.reshape(depth, t, ple)

    xs, hs, projs, us, x1s, y1s, weights = [], [], [], [], [], {}, []
    xc = x.reshape(t, d)
    for l in range(depth):
        j = l // 2
        xs.append(xc)
        keys = carried_by_in.get(l, [])
        h, proj = unpack(_fwd_in(xc, norm_g3, gathered["in", l], l, tm_mm, carried(keys)), keys)
        keys = carried_by_mix.get(l, [])
        if l % 2 == 0:
            u, y1s[l] = unpack(_fwd_conv(proj, conv_w_blk, conv_b3, conv_ln_g3, conv_ln_b3, j, bsz, seq, tm_mix,
                                         carried(keys)), keys)
        else:
            (u,) = unpack(_fwd_sgu(proj, sgu_ln_g_full, sgu_ln_b_full, sgu_w, sgu_bt, j, tm_mix, carried(keys)), keys)
        w_out_l, gate_l, proj_l = gathered["out", l].reshape(1, e, d), gathered["gate", l].reshape(1, d, d), gathered["proj", l]
        weights.append((gathered["in", l], w_out_l, gate_l, proj_l))
        x1, xc = _fwd_out(xc, u, w_out_l, pl_norm_g3, gate_l, p3, proj_l, l, tm_mm)
        hs.append(h)
        projs.append(proj)
        us.append(u)
        x1s.append(x1)

    loss_local, dx, d_final_g = _loss_head(xc, final_g[None], loss_target.reshape(t, d), tm_mm)
    loss = lax.psum(loss_local[0, 0], ("x", "y", "c"))

    c_arr = my_c.astype(jnp.int32).reshape(1)
    ck_arr = jnp.stack([my_c, my_k]).astype(jnp.int32)
    by_chip = lambda a: a.reshape((1, N_CHIPS) + a.shape[1:])
    d_norm_g, d_pl_norm_g = [None] * depth, [None] * depth
    d_conv = [None] * n_conv
    d_sgu = [None] * n_sgu
    locals_, siblings, arrived = [None] * depth, [None] * depth, [None] * depth
    in_flight = None

    def scatter_of(wires):
        return _scatter_comm(wires) if wires else None

    def landed(res, layer, comm):
        if comm is None:
            return res
        arrived[layer] = (arrived[layer] or []) + list(res[1])
        return res[0]

    for l in reversed(range(depth)):
        j = l // 2
        w_in_l, w_out_l, gate_l, proj_l = weights[l]
        comm = scatter_of(in_flight[1][:1]) if in_flight else None
        dx1, du, rn, ds, dqv, d_pl_norm_g[l] = landed(
            _bwd_out(dx, x1s[l], p3, pl_norm_g3, gate_l, proj_l, w_out_l, l, tm_mm, comm), in_flight and in_flight[0], comm)
        (g_proj,) = _wgrad(p3, dqv, 1, 1, N_CHIPS, tm_mm, "wgrad_proj", a_layer=l)
        (g_gate,) = _wgrad(rn, ds, 1, 1, 1, tm_mm, "wgrad_gate")
        (g_out,) = _wgrad(us[l], dx1, 1, 1, 1, tm_mm, "wgrad_out")
        comm = scatter_of(in_flight[1][1:]) if in_flight else None
        if l % 2 == 0:
            dproj, dcw, dcb, dlg, dlb = landed(
                _bwd_conv(du, projs[l], y1s[l], conv_w_blk, conv_ln_g3, conv_ln_b3, j, bsz, seq, tm_mix, comm),
                in_flight and in_flight[0], comm)
            d_conv[j] = (dcw, dcb, dlg, dlb)
        else:
            dproj, dsw, dsbt, dlg, dlb = landed(
                _bwd_sgu(du, projs[l], sgu_ln_g_full, sgu_ln_b_full, sgu_w, sgu_bt, j, tm_mix, comm),
                in_flight and in_flight[0], comm)
            d_sgu[j] = (dsw, dsbt, dlg, dlb)
        (g_in,) = _wgrad(hs[l], dproj, 1, N_CHIPS, 1, tm_mm, "wgrad_in")
        local = [g_in.reshape(N_CHIPS, 2, d // 2, n4), g_out.reshape(N_CHIPS, 2, e // (2 * N_CHIPS), d),
                 g_gate.reshape(N_CHIPS, 2, d // (2 * N_CHIPS), d), g_proj.reshape(N_CHIPS, 2, ple // 2, dq)]
        if l > 0:
            (dx, d_norm_g[l]), from_sibling = _bwd_in(dproj, dx1, xs[l], norm_g3, w_in_l, l, tm_mm, _swap_comm(local))
            in_flight = (l, [by_chip(_pair_sum(gl, rc, c_arr, BF16)) for gl, rc in zip(local, from_sibling)])
        else:
            from_sibling = _run_comm(_swap_comm(local), "swap_last")
            wires = [by_chip(_pair_sum(gl, rc, c_arr, BF16)) for gl, rc in zip(local, from_sibling)]
            (dx, d_norm_g[l]), arrived[l] = _bwd_in(dproj, dx1, xs[l], norm_g3, w_in_l, l, tm_mm, _scatter_comm(wires))
        locals_[l], siblings[l] = local, from_sibling
    grad_x = dx.reshape(bsz, seq, d)

    d_conv_w = jnp.stack([jnp.transpose(dc[0], (1, 0, 2)).reshape(kp, e)[:k_taps] for dc in d_conv])
    d_conv_b = jnp.stack([dc[1][0] for dc in d_conv])
    d_conv_ln_g = jnp.stack([dc[2][0] for dc in d_conv])
    d_conv_ln_b = jnp.stack([dc[3][0] for dc in d_conv])
    d_sgu_w = jnp.stack([dsg[0] for dsg in d_sgu])
    d_sgu_b = jnp.stack([jnp.transpose(dsg[1]) for dsg in d_sgu])
    d_sgu_ln_g = jnp.stack([dsg[2][0] for dsg in d_sgu])
    d_sgu_ln_b = jnp.stack([dsg[3][0] for dsg in d_sgu])
    small_grads = [jnp.concatenate(d_norm_g), d_conv_w, d_conv_b, d_conv_ln_g, d_conv_ln_b, d_sgu_ln_g, d_sgu_ln_b,
                   d_sgu_w, d_sgu_b, jnp.concatenate(d_pl_norm_g), d_final_g]
    small_shapes = [a.shape for a in small_grads]
    packed = _pack_rows(small_grads, d)
    pack_rows = packed.shape[0] + ((-packed.shape[0]) % (8 * SUBLANES))
    packed = _pack_rows(small_grads, d, pack_rows)
    gl_small = packed.reshape(N_CHIPS, 2, pack_rows // 8, d)
    (small_sibling,) = _run_comm(_swap_comm([gl_small]), "swap_small")
    small_pair = by_chip(_pair_sum(gl_small, small_sibling, c_arr, F32))

    (small_arrived,) = _run_comm(_scatter_comm([small_pair]), "scatter_small")
    reduced = [lax.empty((depth,) + gl.shape[2:], F32) for gl in locals_[0]]
    for l in range(depth):
        reduced = [_chip_sum(gl, rc, ar, ck_arr, acc, l)
                   for gl, rc, ar, acc in zip(locals_[l], siblings[l], arrived[l], reduced)]
    small_mine = _chip_sum(gl_small, small_sibling, small_arrived, ck_arr, lax.empty((1,) + gl_small.shape[2:], F32), 0)
    theirs = _swap_pieces(reduced)
    small_all = _gather_pieces(small_mine[:, None])
    small_all = lax.dynamic_update_slice(small_all, small_mine[:, None], (my_k, my_c, 0, 0)).reshape(pack_rows, d)
    small_red = _unpack_rows(small_all, small_shapes, d)
    (gr_norm_g, gr_conv_w, gr_conv_b, gr_conv_ln_g, gr_conv_ln_b, gr_sgu_ln_g, gr_sgu_ln_b, gr_sgu_w, gr_sgu_b,
     gr_pl_norm_g, gr_final_g) = small_red
    gr_final_g = gr_final_g.reshape(d)
    gr_conv_w = lax.dynamic_slice_in_dim(gr_conv_w, my_k * ec, ec, axis=2)
    gr_sgu_ln_g = lax.dynamic_slice_in_dim(gr_sgu_ln_g, my_k * ec, ec, axis=1)
    gr_sgu_ln_b = lax.dynamic_slice_in_dim(gr_sgu_ln_b, my_k * ec, ec, axis=1)

    up_in = _adamw_halves(w_in, reduced[0], theirs[0], m_w_in, v_w_in, c_arr)
    up_out = _adamw_halves(w_out, reduced[1], theirs[1], m_w_out, v_w_out, c_arr)
    up_gate = _adamw_halves(pl_gate_w, reduced[2], theirs[2], m_pl_gate_w, v_pl_gate_w, c_arr)
    up_proj = _adamw_halves(pl_proj_w, reduced[3], theirs[3], m_pl_proj_w, v_pl_proj_w, c_arr)

    small_names = ["norm_g", "conv_w", "conv_b", "conv_ln_g", "conv_ln_b", "sgu_ln_g", "sgu_ln_b", "sgu_w", "sgu_b",
                   "pl_norm_g", "final_g"]
    small_w_list = [norm_g, conv_w, conv_b, conv_ln_g, conv_ln_b, sgu_ln_g, sgu_ln_b, sgu_w, sgu_b, pl_norm_g, final_g]
    small_m_list = [m_norm_g, m_conv_w, m_conv_b, m_conv_ln_g, m_conv_ln_b, m_sgu_ln_g, m_sgu_ln_b, m_sgu_w, m_sgu_b,
                    m_pl_norm_g, m_final_g]
    small_v_list = [v_norm_g, v_conv_w, v_conv_b, v_conv_ln_g, v_conv_ln_b, v_sgu_ln_g, v_sgu_ln_b, v_sgu_w, v_sgu_b,
                    v_pl_norm_g, v_final_g]
    small_g_list = [gr_norm_g, gr_conv_w, gr_conv_b, gr_conv_ln_g, gr_conv_ln_b, gr_sgu_ln_g, gr_sgu_ln_b, gr_sgu_w,
                    gr_sgu_b, gr_pl_norm_g, gr_final_g]
    width = ec
    shapes_local = [a.shape for a in small_w_list]
    outs_small = _adamw(_pack_rows(small_w_list, width), _pack_rows(small_g_list, width),
                        _pack_rows(small_m_list, width), _pack_rows(small_v_list, width))
    unpacked = [_unpack_rows(o, shapes_local, width) for o in outs_small]
    ups = {name: [unpacked[kind][i] for kind in range(4)] for i, name in enumerate(small_names)}
    ups["w_in"], ups["w_out"], ups["pl_gate_w"], ups["pl_proj_w"] = up_in, up_out, up_gate, up_proj

    order = ["norm_g", "w_in", "w_out", "conv_w", "conv_b", "conv_ln_g", "conv_ln_b", "sgu_ln_g", "sgu_ln_b", "sgu_w",
             "sgu_b", "pl_norm_g", "pl_gate_w", "pl_proj_w", "final_g"]
    result = [loss, grad_x]
    for kind in range(4):
        result.extend(ups[name][kind] for name in order)
    return tuple(result)
```

```python
import functools

import jax
import jax.numpy as jnp
from jax import lax
from jax.experimental import pallas as pl
from jax.experimental.pallas import tpu as pltpu

F32 = jnp.float32
BF16 = jnp.bfloat16
SDS = jax.ShapeDtypeStruct

EPS = 1e-6
CHUNK = 128
GROUPS = 8
HALO = 32
N_CHIPS = 4
LANES = 128
SUBLANES = 8
V7X_VMEM_LIMIT = 56 << 20

ADAM_LR = 0.001
ADAM_B1 = 0.9
ADAM_B2 = 0.999
ADAM_EPS = 1e-08
ADAM_WD = 0.01
ADAM_STEP = 10

MESH_IDS = pl.DeviceIdType.MESH
ANY = pl.BlockSpec(memory_space=pl.ANY)


def _params(n_axes):
    return pltpu.CompilerParams(dimension_semantics=("arbitrary",) * n_axes, vmem_limit_bytes=V7X_VMEM_LIMIT)


def _const(shape):
    zeros = (0,) * len(shape)
    return pl.BlockSpec(shape, lambda *_: zeros)


def _layer(shape, layer):
    zeros = (0,) * len(shape)
    return pl.BlockSpec((None,) + tuple(shape), lambda *_: (layer,) + zeros, pipeline_mode=pl.Buffered(1))


class _Comm:
    def __init__(self, operands, out_shape, sems, start, finish):
        self.operands, self.out_shape, self.sems, self.start, self.finish = operands, out_shape, sems, start, finish


def _call(body, *, name, grid, in_specs, out_specs, out_shape, operands, scratch_shapes=(), comm=None):
    in_specs, out_specs, out_shape, scratch_shapes = list(in_specs), list(out_specs), list(out_shape), list(scratch_shapes)
    if comm is None:
        return pl.pallas_call(body, name=name, grid=grid, in_specs=in_specs, out_specs=out_specs, out_shape=out_shape,
                              scratch_shapes=scratch_shapes, compiler_params=_params(len(grid)))(*operands)
    n_in, n_out, n_sc = len(in_specs), len(out_specs), len(scratch_shapes)
    ci, co = len(comm.operands), len(comm.out_shape)

    def hosted(*refs):
        ins, cins = refs[:n_in], refs[n_in:n_in + ci]
        outs, couts = refs[n_in + ci:n_in + ci + n_out], refs[n_in + ci + n_out:n_in + ci + n_out + co]
        scratch = refs[n_in + ci + n_out + co:n_in + ci + n_out + co + n_sc]
        sems = refs[n_in + ci + n_out + co + n_sc:]
        first = functools.reduce(jnp.logical_and, [pl.program_id(a) == 0 for a in range(len(grid))])
        last = functools.reduce(jnp.logical_and, [pl.program_id(a) == g - 1 for a, g in enumerate(grid)])

        @pl.when(first)
        def _():
            comm.start(cins, couts, sems)

        body(*ins, *outs, *scratch)

        @pl.when(last)
        def _():
            comm.finish(cins, couts, sems)

    res = pl.pallas_call(
        hosted, name=name, grid=grid, in_specs=in_specs + [ANY] * ci, out_specs=out_specs + [ANY] * co,
        out_shape=out_shape + list(comm.out_shape), scratch_shapes=scratch_shapes + list(comm.sems),
        compiler_params=_params(len(grid)))(*operands, *comm.operands)
    return res[:n_out], res[n_out:]


def _run_comm(comm, name):
    ci, co = len(comm.operands), len(comm.out_shape)

    def body(*refs):
        comm.start(refs[:ci], refs[ci:ci + co], refs[ci + co:])
        comm.finish(refs[:ci], refs[ci:ci + co], refs[ci + co:])

    return pl.pallas_call(body, name=name, in_specs=[ANY] * ci, out_specs=[ANY] * co, out_shape=list(comm.out_shape),
                          scratch_shapes=list(comm.sems))(*comm.operands)


def _sigmoid(v):
    return jax.nn.sigmoid(v)


def _dsilu(v, s):
    return s * (1.0 + v * (1.0 - s))


def _gelu_parts(v):
    cdf = 0.5 * (1.0 + lax.erf(v * 0.7071067811865476))
    pdf = jnp.exp(-0.5 * v * v) * 0.3989422804014327
    return v * cdf, cdf + v * pdf


def _gelu(v):
    return 0.5 * v * (1.0 + lax.erf(v * 0.7071067811865476))


def _rms_stats(x):
    r = lax.rsqrt(jnp.mean(x * x, axis=-1, keepdims=True) + EPS)
    return r, x * r


def _rms_bwd(dy, g, r, xh):
    gdy = dy * g
    return r * (gdy - xh * jnp.mean(xh * gdy, axis=-1, keepdims=True))


def _ln_stats(x):
    mu = jnp.mean(x, axis=-1, keepdims=True)
    xc = x - mu
    rs = lax.rsqrt(jnp.mean(xc * xc, axis=-1, keepdims=True) + EPS)
    return rs, xc * rs


def _ln_bwd(dy, g, rs, xh):
    dxh = dy * g
    return rs * (dxh - jnp.mean(dxh, axis=-1, keepdims=True) - xh * jnp.mean(dxh * xh, axis=-1, keepdims=True))


def _dot(a, b):
    return jnp.dot(a, b, preferred_element_type=F32)


def _dot_nt(a, b):
    return lax.dot_general(a, b, (((1,), (1,)), ((), ())), preferred_element_type=F32)


def _dot_tn(a, b):
    return lax.dot_general(a, b, (((0,), (0,)), ((), ())), preferred_element_type=F32)


def _fwd_in(x, norm_g, w_in_full, layer, tm, comm=None):
    t, d = x.shape
    _, nk, _, n4 = w_in_full.shape

    def body(x_ref, g_ref, w_ref, h_ref, proj_ref):
        r, xh = _rms_stats(x_ref[...])
        h = (xh * g_ref[...]).astype(BF16)
        h_ref[...] = h
        for k in range(nk):
            proj_ref[:, k * n4:(k + 1) * n4] = _dot(h, w_ref[k]).astype(BF16)

    return _call(
        body, name="fwd_in", grid=(t // tm,),
        in_specs=[pl.BlockSpec((tm, d), lambda i: (i, 0)), _layer((1, d), layer), _layer((nk, d, n4), 0)],
        out_specs=[pl.BlockSpec((tm, d), lambda i: (i, 0)), pl.BlockSpec((tm, nk * n4), lambda i: (i, 0))],
        out_shape=[SDS((t, d), BF16), SDS((t, nk * n4), BF16)],
        operands=(x, norm_g, w_in_full), comm=comm)


def _halo_maps(nt, hb, n_halo_blocks):
    def prev(b, i):
        return (jnp.maximum((b * nt + i) * hb - 1, 0), 0)

    def nxt(b, i):
        return (jnp.minimum((b * nt + i + 1) * hb, n_halo_blocks - 1), 0)

    return prev, nxt


TAP_TILE_VREGS = 16


def _tap_rows(cb, tm):
    return min(TAP_TILE_VREGS * SUBLANES * LANES // cb, tm)


def _conv_taps(src_ref, w_ref, dst_ref, cb_idx, n_rows, rb, first, reverse):
    k_taps = w_ref.shape[1] - 1
    for r0 in range(0, n_rows, rb):
        acc = None
        for res in range(SUBLANES):
            rows = rb + (SUBLANES if res else 0)
            group = None
            for k in range(k_taps):
                off = first + k
                if off % SUBLANES != res:
                    continue
                wk = w_ref[cb_idx, pl.ds((k_taps - 1 - k) if reverse else k, 1), :]
                term = wk * src_ref[cb_idx, pl.ds(r0 + off - res, rows), :]
                group = term if group is None else group + term
            if group is None:
                continue
            part = group[res:res + rb] if res else group
            acc = part if acc is None else acc + part
        dst_ref[cb_idx, pl.ds(r0, rb), :] = acc


def _fwd_conv(proj, conv_w_blk, conv_b, ln_g, ln_b, layer, bsz, seq, tm, comm=None):
    t, e3 = proj.shape
    e = e3 // 3
    nt = seq // tm
    hb = tm // HALO
    _, ncb, kp, cb = conv_w_blk.shape
    rb = _tap_rows(cb, tm)
    prev, _ = _halo_maps(nt, hb, t // HALO)

    def body(proj_ref, halo_ref, w_ref, b_ref, g_ref, lb_ref, u_ref, y1_ref, y0s, y1s):
        i = pl.program_id(1)
        a = proj_ref[:, 0:e].astype(F32)
        b = proj_ref[:, e:2 * e].astype(F32)
        y0 = a * _sigmoid(b)
        ah = halo_ref[:, 0:e].astype(F32)
        bh = halo_ref[:, e:2 * e].astype(F32)
        y0h = jnp.where(i > 0, ah * _sigmoid(bh), 0.0)
        for c in range(ncb):
            y0s[c, 0:HALO, :] = y0h[:, c * cb:(c + 1) * cb]
            y0s[c, HALO:HALO + tm, :] = y0[:, c * cb:(c + 1) * cb]

        def per_block(c, carry):
            _conv_taps(y0s, w_ref, y1s, c, tm, rb, HALO - (kp - 2), False)
            return carry

        lax.fori_loop(0, ncb, per_block, 0)
        y1 = jnp.concatenate([y1s[c] for c in range(ncb)], axis=1) + b_ref[...]
        y1_ref[...] = y1
        rs, xh = _ln_stats(y1)
        y2 = xh * g_ref[...] + lb_ref[...]
        y = y2 * _sigmoid(y2)
        z = proj_ref[:, 2 * e:3 * e].astype(F32)
        u_ref[...] = (y * (z * _sigmoid(z))).astype(BF16)

    return _call(
        body, name="fwd_conv", grid=(bsz, nt),
        in_specs=[pl.BlockSpec((tm, e3), lambda b, i: (b * nt + i, 0)),
                  pl.BlockSpec((HALO, 2 * e), prev),
                  _layer((ncb, kp, cb), layer), _layer((1, e), layer), _layer((1, e), layer), _layer((1, e), layer)],
        out_specs=[pl.BlockSpec((tm, e), lambda b, i: (b * nt + i, 0)), pl.BlockSpec((tm, e), lambda b, i: (b * nt + i, 0))],
        out_shape=[SDS((t, e), BF16), SDS((t, e), F32)],
        scratch_shapes=[pltpu.VMEM((ncb, HALO + tm, cb), F32), pltpu.VMEM((ncb, tm, cb), F32)],
        operands=(proj, proj, conv_w_blk, conv_b, ln_g, ln_b), comm=comm)


def _tril_mask():
    rows = lax.broadcasted_iota(jnp.int32, (CHUNK, CHUNK), 0)
    cols = lax.broadcasted_iota(jnp.int32, (CHUNK, CHUNK), 1)
    return rows >= cols


def _fwd_sgu(proj, ln_g, ln_b, sgu_w, sgu_bt, layer, tm, comm=None):
    t, e3 = proj.shape
    e = e3 // 3
    gw = e // GROUPS
    nch = tm // CHUNK

    def body(proj_ref, g_ref, lb_ref, w_ref, bt_ref, u_ref, mixed):
        a = proj_ref[:, 0:e].astype(F32)
        b = proj_ref[:, e:2 * e].astype(F32)
        z = proj_ref[:, 2 * e:3 * e].astype(F32)
        rs, xh = _ln_stats(_gelu(b))
        v = (xh * g_ref[...] + lb_ref[...]).astype(BF16)
        mask = _tril_mask()
        for g in range(GROUPS):
            wm = jnp.where(mask, w_ref[g], 0.0).astype(BF16)
            bias = bt_ref[:, g:g + 1]
            for n in range(nch):
                blk = v[n * CHUNK:(n + 1) * CHUNK, g * gw:(g + 1) * gw]
                mixed[n * CHUNK:(n + 1) * CHUNK, g * gw:(g + 1) * gw] = _dot(wm, blk) + bias
        y = _gelu(a) * mixed[...]
        u_ref[...] = (y * (z * _sigmoid(z))).astype(BF16)

    return _call(
        body, name="fwd_sgu", grid=(t // tm,),
        in_specs=[pl.BlockSpec((tm, e3), lambda i: (i, 0)), _layer((1, e), layer), _layer((1, e), layer),
                  _layer((GROUPS, CHUNK, CHUNK), layer), _layer((CHUNK, GROUPS), layer)],
        out_specs=[pl.BlockSpec((tm, e), lambda i: (i, 0))],
        out_shape=[SDS((t, e), BF16)],
        scratch_shapes=[pltpu.VMEM((tm, e), F32)],
        operands=(proj, ln_g, ln_b, sgu_w, sgu_bt), comm=comm)


def _ple_forward(x1, p_ref, plg_ref, gw_ref, pw_ref):
    nk, _, dq = pw_ref.shape
    r, xh = _rms_stats(x1)
    rn = (xh * plg_ref[...]).astype(BF16)
    gate = _sigmoid(_dot(rn, gw_ref[...]))
    pb = p_ref[...].astype(BF16)
    q = jnp.concatenate([_dot(pb, pw_ref[k]) for k in range(nk)], axis=1)
    return r, xh, rn, gate, q


def _fwd_out(x, u, w_out_full, pl_norm_g, gate_w_full, p, proj_w_full, layer, tm, comm=None):
    t, d = x.shape
    e = u.shape[1]
    ple = p.shape[-1]
    nk, dq = proj_w_full.shape[1], proj_w_full.shape[3]

    def body(x_ref, u_ref, wo_ref, plg_ref, gw_ref, p_ref, pw_ref, x1_ref, x2_ref):
        x1 = x_ref[...] + _dot(u_ref[...], wo_ref[...])
        x1_ref[...] = x1
        _, _, _, gate, q = _ple_forward(x1, p_ref, plg_ref, gw_ref, pw_ref)
        x2_ref[...] = x1 + gate * q

    return _call(
        body, name="fwd_out", grid=(t // tm,),
        in_specs=[pl.BlockSpec((tm, d), lambda i: (i, 0)), pl.BlockSpec((tm, e), lambda i: (i, 0)),
                  _layer((e, d), 0), _layer((1, d), layer), _layer((d, d), 0),
                  pl.BlockSpec((None, tm, ple), lambda i: (layer, i, 0)), _layer((nk, ple, dq), 0)],
        out_specs=[pl.BlockSpec((tm, d), lambda i: (i, 0)), pl.BlockSpec((tm, d), lambda i: (i, 0))],
        out_shape=[SDS((t, d), F32), SDS((t, d), F32)],
        operands=(x, u, w_out_full, pl_norm_g, gate_w_full, p, proj_w_full), comm=comm)


def _loss_head(x, final_g, target, tm):
    t, d = x.shape
    n_steps = t // tm

    def body(x_ref, g_ref, tgt_ref, loss_ref, dx_ref, dg_ref, sq_acc):
        i = pl.program_id(0)

        @pl.when(i == 0)
        def _():
            sq_acc[...] = jnp.zeros_like(sq_acc)
            dg_ref[...] = jnp.zeros_like(dg_ref)

        g = g_ref[...]
        r, xh = _rms_stats(x_ref[...])
        diff = xh * g - tgt_ref[...]
        sq_acc[...] += jnp.sum(diff * diff, axis=0, keepdims=True)
        dout = diff * (1.0 / d)
        dg_ref[...] += jnp.sum(dout * xh, axis=0, keepdims=True)
        dx_ref[...] = _rms_bwd(dout, g, r, xh)

        @pl.when(i == n_steps - 1)
        def _():
            loss_ref[...] = jnp.sum(sq_acc[...], axis=1, keepdims=True) * (0.5 / d)

    return pl.pallas_call(
        body, name="loss_head", grid=(n_steps,),
        in_specs=[pl.BlockSpec((tm, d), lambda i: (i, 0)), _const((1, d)), pl.BlockSpec((tm, d), lambda i: (i, 0))],
        out_specs=[_const((1, 1)), pl.BlockSpec((tm, d), lambda i: (i, 0)), _const((1, d))],
        out_shape=[SDS((1, 1), F32), SDS((t, d), F32), SDS((1, d), F32)],
        scratch_shapes=[pltpu.VMEM((1, d), F32)],
        compiler_params=_params(1),
    )(x, final_g, target)


def _bwd_out(dx2, x1, p, pl_norm_g, gate_w_full, proj_w_full, w_out_full, layer, tm, comm=None):
    t, d = dx2.shape
    e = w_out_full.shape[1]
    ple = p.shape[-1]
    nk, dq_w = proj_w_full.shape[1], proj_w_full.shape[3]

    def body(dx2_ref, x1_ref, p_ref, plg_ref, gw_ref, pw_ref, wo_ref, dx1_ref, du_ref, rn_ref, ds_ref, dq_ref, dplg_ref):
        @pl.when(pl.program_id(0) == 0)
        def _():
            dplg_ref[...] = jnp.zeros_like(dplg_ref)

        dx2v = dx2_ref[...]
        r, xh, rn, gate, q = _ple_forward(x1_ref[...], p_ref, plg_ref, gw_ref, pw_ref)
        rn_ref[...] = rn
        dq_ref[...] = (dx2v * gate).astype(BF16)
        ds = (dx2v * q * gate * (1.0 - gate)).astype(BF16)
        ds_ref[...] = ds
        dr = _dot_nt(ds, gw_ref[...])
        dplg_ref[...] += jnp.sum(dr * xh, axis=0, keepdims=True)
        dx1 = dx2v + _rms_bwd(dr, plg_ref[...], r, xh)
        dx1_ref[...] = dx1
        du_ref[...] = _dot_nt(dx1.astype(BF16), wo_ref[...]).astype(BF16)

    row = lambda w: pl.BlockSpec((tm, w), lambda i: (i, 0))
    return _call(
        body, name="bwd_out", grid=(t // tm,),
        in_specs=[row(d), row(d), pl.BlockSpec((None, tm, ple), lambda i: (layer, i, 0)),
                  _layer((1, d), layer), _layer((d, d), 0), _layer((nk, ple, dq_w), 0), _layer((e, d), 0)],
        out_specs=[row(d), row(e), row(d), row(d), row(d), _const((1, d))],
        out_shape=[SDS((t, d), F32), SDS((t, e), BF16), SDS((t, d), BF16), SDS((t, d), BF16), SDS((t, d), BF16),
                   SDS((1, d), F32)],
        operands=(dx2, x1, p, pl_norm_g, gate_w_full, proj_w_full, w_out_full), comm=comm)


def _bwd_conv(du, proj, y1, conv_w_blk, ln_g, ln_b, layer, bsz, seq, tm, comm=None):
    t, e3 = proj.shape
    e = e3 // 3
    nt = seq // tm
    hb = tm // HALO
    _, ncb, kp, cb = conv_w_blk.shape
    rb = _tap_rows(cb, tm)
    k_taps = kp - 1
    prev, nxt = _halo_maps(nt, hb, t // HALO)
    z_halo = lambda b, i: (nxt(b, i)[0], 2)

    def ln_silu_bwd(du_v, z_v, y1_v, g, lb):
        rs, xh = _ln_stats(y1_v)
        y2 = xh * g + lb
        sg = _sigmoid(y2)
        sz = _sigmoid(z_v)
        dy = du_v * (z_v * sz)
        dy2 = dy * _dsilu(y2, sg)
        return _ln_bwd(dy2, g, rs, xh), dy2, xh, du_v * (y2 * sg) * _dsilu(z_v, sz)

    def body(du_ref, proj_ref, y1_ref, duh_ref, zh_ref, y1h_ref, abh_ref, w_ref, g_ref, lb_ref,
             dproj_ref, dw_ref, dcb_ref, dg_ref, dlb_ref, y0s, dy1s, dy0s, ysh):
        b_id, i = pl.program_id(0), pl.program_id(1)

        @pl.when((b_id == 0) & (i == 0))
        def _():
            dw_ref[...] = jnp.zeros_like(dw_ref)
            dcb_ref[...] = jnp.zeros_like(dcb_ref)
            dg_ref[...] = jnp.zeros_like(dg_ref)
            dlb_ref[...] = jnp.zeros_like(dlb_ref)

        g, lb = g_ref[...], lb_ref[...]
        a = proj_ref[:, 0:e].astype(F32)
        b = proj_ref[:, e:2 * e].astype(F32)
        z = proj_ref[:, 2 * e:3 * e].astype(F32)
        sb = _sigmoid(b)
        y0 = a * sb
        dy1, dy2, xh, dz = ln_silu_bwd(du_ref[...].astype(F32), z, y1_ref[...], g, lb)
        dproj_ref[:, 2 * e:3 * e] = dz.astype(BF16)
        dg_ref[...] += jnp.sum(dy2 * xh, axis=0, keepdims=True)
        dlb_ref[...] += jnp.sum(dy2, axis=0, keepdims=True)
        dcb_ref[...] += jnp.sum(dy1, axis=0, keepdims=True)
        dy1h, _, _, _ = ln_silu_bwd(duh_ref[...].astype(F32), zh_ref[...].astype(F32), y1h_ref[...], g, lb)
        dy1h = jnp.where(i < nt - 1, dy1h, 0.0)
        ah = abh_ref[:, 0:e].astype(F32)
        bh = abh_ref[:, e:2 * e].astype(F32)
        y0h = jnp.where(i > 0, ah * _sigmoid(bh), 0.0)
        for c in range(ncb):
            cols = slice(c * cb, (c + 1) * cb)
            y0s[c, 0:HALO, :] = y0h[:, cols]
            y0s[c, HALO:HALO + tm, :] = y0[:, cols]
            dy1s[c, 0:tm, :] = dy1[:, cols]
            dy1s[c, tm:tm + HALO, :] = dy1h[:, cols]

        def per_block(c, carry):
            _conv_taps(dy1s, w_ref, dy0s, c, tm, rb, 0, True)
            for res in range(1, SUBLANES):
                ysh[res - 1] = y0s[c, pl.ds(res, tm + HALO - SUBLANES), :]
            for k in range(k_taps):
                off = HALO - (k_taps - 1) + k
                res = off % SUBLANES
                acc = None
                for r0 in range(0, tm, rb):
                    rows = pl.ds(r0 + off - res, rb)
                    shifted = ysh[res - 1, rows, :] if res else y0s[c, rows, :]
                    term = dy1s[c, pl.ds(r0, rb), :] * shifted
                    acc = term if acc is None else acc + term
                dw_ref[c, pl.ds(k, 1), :] += jnp.sum(acc, axis=0, keepdims=True)
            return carry

        lax.fori_loop(0, ncb, per_block, 0)
        dy0 = jnp.concatenate([dy0s[c] for c in range(ncb)], axis=1)
        dproj_ref[:, 0:e] = (dy0 * sb).astype(BF16)
        dproj_ref[:, e:2 * e] = (dy0 * a * sb * (1.0 - sb)).astype(BF16)

    tile = lambda w: pl.BlockSpec((tm, w), lambda b, i: (b * nt + i, 0))
    return _call(
        body, name="bwd_conv", grid=(bsz, nt),
        in_specs=[tile(e), tile(e3), tile(e),
                  pl.BlockSpec((HALO, e), nxt), pl.BlockSpec((HALO, e), z_halo), pl.BlockSpec((HALO, e), nxt),
                  pl.BlockSpec((HALO, 2 * e), prev),
                  _layer((ncb, kp, cb), layer), _layer((1, e), layer), _layer((1, e), layer)],
        out_specs=[tile(e3), _const((ncb, kp, cb)), _const((1, e)), _const((1, e)), _const((1, e))],
        out_shape=[SDS((t, e3), BF16), SDS((ncb, kp, cb), F32), SDS((1, e), F32), SDS((1, e), F32), SDS((1, e), F32)],
        scratch_shapes=[pltpu.VMEM((ncb, HALO + tm, cb), F32), pltpu.VMEM((ncb, tm + HALO, cb), F32),
                        pltpu.VMEM((ncb, tm, cb), F32), pltpu.VMEM((SUBLANES - 1, tm + HALO - SUBLANES, cb), F32)],
        operands=(du, proj, y1, du, proj, y1, proj, conv_w_blk, ln_g, ln_b), comm=comm)


def _bwd_sgu(du, proj, ln_g, ln_b, sgu_w, sgu_bt, layer, tm, comm=None):
    t, e3 = proj.shape
    e = e3 // 3
    gw = e // GROUPS
    nch = tm // CHUNK

    def body(du_ref, proj_ref, g_ref, lb_ref, w_ref, bt_ref, dproj_ref, dw_ref, dbt_ref, dg_ref, dlb_ref, mixed, dmix, dv):
        @pl.when(pl.program_id(0) == 0)
        def _():
            dw_ref[...] = jnp.zeros_like(dw_ref)
            dbt_ref[...] = jnp.zeros_like(dbt_ref)
            dg_ref[...] = jnp.zeros_like(dg_ref)
            dlb_ref[...] = jnp.zeros_like(dlb_ref)

        g, lb = g_ref[...], lb_ref[...]
        a = proj_ref[:, 0:e].astype(F32)
        b = proj_ref[:, e:2 * e].astype(F32)
        z = proj_ref[:, 2 * e:3 * e].astype(F32)
        ug, dug = _gelu_parts(a)
        vb, dvb_db = _gelu_parts(b)
        rs, xh = _ln_stats(vb)
        v = (xh * g + lb).astype(BF16)
        mask = _tril_mask()
        for gi in range(GROUPS):
            wm = jnp.where(mask, w_ref[gi], 0.0).astype(BF16)
            bias = bt_ref[:, gi:gi + 1]
            for c in range(nch):
                blk = v[c * CHUNK:(c + 1) * CHUNK, gi * gw:(gi + 1) * gw]
                mixed[c * CHUNK:(c + 1) * CHUNK, gi * gw:(gi + 1) * gw] = _dot(wm, blk) + bias
        mx = mixed[...]
        sz = _sigmoid(z)
        duv = du_ref[...].astype(F32)
        dy = duv * (z * sz)
        dproj_ref[:, 2 * e:3 * e] = (duv * (ug * mx) * _dsilu(z, sz)).astype(BF16)
        dproj_ref[:, 0:e] = (dy * mx * dug).astype(BF16)
        dmix[...] = dy * ug
        for gi in range(GROUPS):
            wm = jnp.where(mask, w_ref[gi], 0.0).astype(BF16)
            dw_acc = None
            db_acc = None
            for c in range(nch):
                rows, cols = slice(c * CHUNK, (c + 1) * CHUNK), slice(gi * gw, (gi + 1) * gw)
                dm = dmix[rows, cols]
                dmb = dm.astype(BF16)
                dw_n = _dot_nt(dmb, v[rows, cols])
                db_n = jnp.sum(dm, axis=1, keepdims=True)
                dw_acc = dw_n if dw_acc is None else dw_acc + dw_n
                db_acc = db_n if db_acc is None else db_acc + db_n
                dv[rows, cols] = _dot_tn(wm, dmb)
            dw_ref[gi] += jnp.where(mask, dw_acc, 0.0)
            dbt_ref[:, gi:gi + 1] += db_acc
        dvv = dv[...]
        dg_ref[...] += jnp.sum(dvv * xh, axis=0, keepdims=True)
        dlb_ref[...] += jnp.sum(dvv, axis=0, keepdims=True)
        dproj_ref[:, e:2 * e] = (_ln_bwd(dvv, g, rs, xh) * dvb_db).astype(BF16)

    return _call(
        body, name="bwd_sgu", grid=(t // tm,),
        in_specs=[pl.BlockSpec((tm, e), lambda i: (i, 0)), pl.BlockSpec((tm, e3), lambda i: (i, 0)),
                  _layer((1, e), layer), _layer((1, e), layer),
                  _layer((GROUPS, CHUNK, CHUNK), layer), _layer((CHUNK, GROUPS), layer)],
        out_specs=[pl.BlockSpec((tm, e3), lambda i: (i, 0)), _const((GROUPS, CHUNK, CHUNK)), _const((CHUNK, GROUPS)),
                   _const((1, e)), _const((1, e))],
        out_shape=[SDS((t, e3), BF16), SDS((GROUPS, CHUNK, CHUNK), F32), SDS((CHUNK, GROUPS), F32),
                   SDS((1, e), F32), SDS((1, e), F32)],
        scratch_shapes=[pltpu.VMEM((tm, e), F32), pltpu.VMEM((tm, e), F32), pltpu.VMEM((tm, e), F32)],
        operands=(du, proj, ln_g, ln_b, sgu_w, sgu_bt), comm=comm)


def _bwd_in(dproj, dx1, x, norm_g, w_in_full, layer, tm, comm=None):
    t, d = x.shape
    _, nk, _, n4 = w_in_full.shape

    def body(dproj_ref, dx1_ref, x_ref, g_ref, w_ref, dx_ref, dg_ref):
        @pl.when(pl.program_id(0) == 0)
        def _():
            dg_ref[...] = jnp.zeros_like(dg_ref)

        dh = None
        for k in range(nk):
            part = _dot_nt(dproj_ref[:, k * n4:(k + 1) * n4], w_ref[k])
            dh = part if dh is None else dh + part
        r, xh = _rms_stats(x_ref[...])
        dg_ref[...] += jnp.sum(dh * xh, axis=0, keepdims=True)
        dx_ref[...] = dx1_ref[...] + _rms_bwd(dh, g_ref[...], r, xh)

    row = lambda w: pl.BlockSpec((tm, w), lambda i: (i, 0))
    return _call(
        body, name="bwd_in", grid=(t // tm,),
        in_specs=[row(nk * n4), row(d), row(d), _layer((1, d), layer), _layer((nk, d, n4), 0)],
        out_specs=[row(d), _const((1, d))],
        out_shape=[SDS((t, d), F32), SDS((1, d), F32)],
        operands=(dproj, dx1, x, norm_g, w_in_full), comm=comm)


def _wgrad(a, b, kblk, nblk, n_split, tm, name, a_layer=None, comm=None):
    t, n = b.shape
    k = a.shape[-1]
    kw, nw = k // kblk, n // nblk
    nws = nw // n_split
    n_steps = t // tm

    def body(a_ref, b_ref, o_ref):
        @pl.when(pl.program_id(2) == 0)
        def _():
            o_ref[...] = jnp.zeros_like(o_ref)

        res = _dot_tn(a_ref[...].astype(BF16), b_ref[...].astype(BF16))
        for s in range(n_split):
            o_ref[s] += res[:, s * nws:(s + 1) * nws]

    if a_layer is None:
        a_spec = pl.BlockSpec((tm, kw), lambda kb, nb, i: (i, kb))
    else:
        a_spec = pl.BlockSpec((None, tm, kw), lambda kb, nb, i: (a_layer, i, kb))
    return _call(
        body, name=name, grid=(kblk, nblk, n_steps),
        in_specs=[a_spec, pl.BlockSpec((tm, nw), lambda kb, nb, i: (i, nb))],
        out_specs=[pl.BlockSpec((None, n_split, kw, nws), lambda kb, nb, i: (kb, nb, 0, 0))],
        out_shape=[SDS((kblk, nblk * n_split, kw, nws), F32)],
        operands=(a, b), comm=comm)


def _row_tile(rows, cols, budget_bytes=1 << 20):
    best = None
    for cand in range(SUBLANES, rows + 1, SUBLANES):
        if rows % cand == 0 and cand * cols * 4 <= budget_bytes:
            best = cand
    return best if best is not None else rows


def _pair_sum(grads, recv, my_c, wire_dtype):
    n, _, h, c = grads.shape
    th = _row_tile(h, c)

    def body(c_ref, g_ref, r_ref, o_ref):
        o_ref[...] = (g_ref[...] + r_ref[...]).astype(wire_dtype)

    grid_spec = pltpu.PrefetchScalarGridSpec(
        num_scalar_prefetch=1, grid=(n, h // th),
        in_specs=[pl.BlockSpec((None, None, th, c), lambda j, i, c_ref: (j, c_ref[0], i, 0)),
                  pl.BlockSpec((None, None, th, c), lambda j, i, c_ref: (j, 0, i, 0))],
        out_specs=pl.BlockSpec((None, th, c), lambda j, i, c_ref: (j, i, 0)))
    return pl.pallas_call(body, name="pair_sum", grid_spec=grid_spec, out_shape=SDS((n, h, c), wire_dtype),
                          compiler_params=_params(2))(my_c, grads, recv)


def _chip_sum(grads, recv, arrived, my_ck, stacked, layer):
    _, _, h, c = grads.shape
    th = _row_tile(h, c)

    def body(ck_ref, g_ref, r_ref, a1_ref, a2_ref, a3_ref, acc_ref, o_ref):
        acc = g_ref[...] + r_ref[...]
        for ref in (a1_ref, a2_ref, a3_ref):
            acc = acc + ref[...].astype(F32)
        o_ref[...] = acc

    def slot(flip):
        return pl.BlockSpec((None, None, th, c), lambda i, ck: (0, jnp.bitwise_xor(ck[1], flip), i, 0))

    grid_spec = pltpu.PrefetchScalarGridSpec(
        num_scalar_prefetch=1, grid=(h // th,),
        in_specs=[pl.BlockSpec((None, None, th, c), lambda i, ck: (ck[1], ck[0], i, 0)),
                  pl.BlockSpec((None, None, th, c), lambda i, ck: (ck[1], 0, i, 0)),
                  slot(1), slot(2), slot(3), ANY],
        out_specs=pl.BlockSpec((None, None, th, c), lambda i, ck: (layer, ck[0], i, 0)))
    return pl.pallas_call(body, name="chip_sum", grid_spec=grid_spec, out_shape=SDS(stacked.shape, F32),
                          input_output_aliases={6: 0}, compiler_params=_params(1))(
                              my_ck, grads, recv, arrived, arrived, arrived, stacked)


def _adam_math(w, gv, m, v):
    c1 = 1.0 - ADAM_B1 ** ADAM_STEP
    c2 = 1.0 - ADAM_B2 ** ADAM_STEP
    mn = ADAM_B1 * m + (1.0 - ADAM_B1) * gv
    vn = ADAM_B2 * v + (1.0 - ADAM_B2) * (gv * gv)
    m_hat = mn / c1
    v_hat = vn / c2
    return -ADAM_LR * (m_hat / (jnp.sqrt(v_hat) + ADAM_EPS) + ADAM_WD * w), mn, vn


def _adamw(w, g, m, v):
    rows, cols = w.shape
    tr = _row_tile(rows, cols, 512 << 10)

    def body(w_ref, g_ref, m_ref, v_ref, go_ref, d_ref, mo_ref, vo_ref):
        gv = g_ref[...]
        go_ref[...] = gv
        d_ref[...], mo_ref[...], vo_ref[...] = _adam_math(w_ref[...], gv, m_ref[...], v_ref[...])

    spec = pl.BlockSpec((tr, cols), lambda i: (i, 0))
    return pl.pallas_call(
        body, name="adamw", grid=(rows // tr,), in_specs=[spec] * 4, out_specs=[spec] * 4,
        out_shape=[SDS((rows, cols), F32)] * 4, compiler_params=_params(1))(w, g, m, v)


def _place():
    x, y, c = lax.axis_index("x"), lax.axis_index("y"), lax.axis_index("c")
    chips = [(1 - x, y), (x, 1 - y), (1 - x, 1 - y)]
    return x, y, c, 2 * x + y, chips


def _remote(src, dst, send_sem, recv_sem, device):
    return pltpu.make_async_remote_copy(src_ref=src, dst_ref=dst, send_sem=send_sem, recv_sem=recv_sem,
                                        device_id=device, device_id_type=MESH_IDS)


def _gather_comm(items, small=None):
    shards = [arr for arr, _ in items]
    layers = [layer for _, layer in items]
    n = len(shards)
    extra = 0 if small is None else 1

    def copies(ins, outs, sems):
        ici_send, ici_recv, d2d_send, d2d_recv, own_send, own_recv = sems
        x, y, c, k, chips = _place()
        sibling = (x, y, 1 - c)
        own, ici_out, ici_in, fwd_out, fwd_in = [], [], [], [], []
        for j in range(n):
            h = ins[j].shape[2] // 2
            mine, theirs = pl.ds(c * h, h), pl.ds((1 - c) * h, h)
            own.append(_remote(ins[j].at[pl.ds(layers[j], 1)], outs[j].at[:, pl.ds(k, 1)], own_send.at[j], own_recv.at[j], sibling))
            for ti, (cx, cy) in enumerate(chips):
                s = 3 * j + ti
                ici_out.append(_remote(ins[j].at[pl.ds(layers[j], 1), :, mine], outs[j].at[:, pl.ds(k, 1), mine],
                                       ici_send.at[s], ici_recv.at[s], (cx, cy, c)))
                landed = outs[j].at[:, pl.ds(2 * cx + cy, 1), mine]
                ici_in.append(_remote(landed, landed, ici_send.at[s], ici_recv.at[s], (cx, cy, c)))
                fwd_out.append(_remote(landed, landed, d2d_send.at[s], d2d_recv.at[s], sibling))
                passed = outs[j].at[:, pl.ds(2 * cx + cy, 1), theirs]
                fwd_in.append(_remote(passed, passed, d2d_send.at[s], d2d_recv.at[s], sibling))
        if extra:
            own.append(_remote(ins[n], outs[n].at[pl.ds(k, 1)], own_send.at[n], own_recv.at[n], sibling))
            for ti, (cx, cy) in enumerate(chips):
                s = 3 * n + ti
                ici_out.append(_remote(ins[n], outs[n].at[pl.ds(k, 1)], ici_send.at[s], ici_recv.at[s], (cx, cy, c)))
                slot = outs[n].at[pl.ds(2 * cx + cy, 1)]
                ici_in.append(_remote(slot, slot, ici_send.at[s], ici_recv.at[s], (cx, cy, c)))
        return own, ici_out, ici_in, fwd_out, fwd_in

    def start(ins, outs, sems):
        own, ici_out, _, _, _ = copies(ins, outs, sems)
        for cp in own + ici_out:
            cp.start()

    def finish(ins, outs, sems):
        own, ici_out, ici_in, fwd_out, fwd_in = copies(ins, outs, sems)
        for idx, cp in enumerate(ici_in):
            cp.wait_recv()
            if idx < len(fwd_out):
                fwd_out[idx].start()
        for cp in fwd_in:
            cp.wait_recv()
        for cp in ici_out + fwd_out:
            cp.wait_send()
        for cp in own:
            cp.wait()

    operands = list(shards) + ([small] if extra else [])
    out_shape = [SDS((1, N_CHIPS) + s.shape[2:], s.dtype) for s in shards]
    if extra:
        out_shape.append(SDS((N_CHIPS,) + small.shape[1:], small.dtype))
    sems = [pltpu.SemaphoreType.DMA((3 * (n + extra),)), pltpu.SemaphoreType.DMA((3 * (n + extra),)),
            pltpu.SemaphoreType.DMA((3 * n,)), pltpu.SemaphoreType.DMA((3 * n,)),
            pltpu.SemaphoreType.DMA((n + extra,)), pltpu.SemaphoreType.DMA((n + extra,))]
    return _Comm(operands, out_shape, sems, start, finish)


def _swap_comm(grads):
    n = len(grads)

    def copies(ins, outs, sems):
        send_sem, recv_sem = sems
        x, y, c, _, _ = _place()
        return [_remote(ins[j].at[:, pl.ds(1 - c, 1)], outs[j], send_sem.at[j], recv_sem.at[j], (x, y, 1 - c))
                for j in range(n)]

    def start(ins, outs, sems):
        for cp in copies(ins, outs, sems):
            cp.start()

    def finish(ins, outs, sems):
        for cp in copies(ins, outs, sems):
            cp.wait()

    out_shape = [SDS((g.shape[0], 1) + g.shape[2:], g.dtype) for g in grads]
    return _Comm(list(grads), out_shape, [pltpu.SemaphoreType.DMA((n,)), pltpu.SemaphoreType.DMA((n,))], start, finish)


def _scatter_comm(sums):
    n = len(sums)

    def copies(ins, outs, sems):
        send_sem, recv_sem = sems
        x, y, c, k, chips = _place()
        out, landing = [], []
        for j in range(n):
            for ti, (cx, cy) in enumerate(chips):
                s = 3 * j + ti
                out.append(_remote(ins[j].at[:, pl.ds(2 * cx + cy, 1)], outs[j].at[:, pl.ds(k, 1)],
                                   send_sem.at[s], recv_sem.at[s], (cx, cy, c)))
                slot = outs[j].at[:, pl.ds(2 * cx + cy, 1)]
                landing.append(_remote(slot, slot, send_sem.at[s], recv_sem.at[s], (cx, cy, c)))
        return out, landing

    def start(ins, outs, sems):
        for cp in copies(ins, outs, sems)[0]:
            cp.start()

    def finish(ins, outs, sems):
        out, landing = copies(ins, outs, sems)
        for cp in landing:
            cp.wait_recv()
        for cp in out:
            cp.wait_send()

    out_shape = [SDS(s.shape, s.dtype) for s in sums]
    return _Comm(list(sums), out_shape, [pltpu.SemaphoreType.DMA((3 * n,)), pltpu.SemaphoreType.DMA((3 * n,))], start, finish)


def _swap_pieces(pieces):
    n = len(pieces)

    def body(*refs):
        bufs = refs[n:2 * n]
        send_sem, recv_sem = refs[2 * n:]
        x, y, c, _, _ = _place()
        for j in range(n):
            mine = bufs[j].at[:, pl.ds(c, 1)]
            _remote(mine, mine, send_sem.at[j], recv_sem.at[j], (x, y, 1 - c)).start()
        for j in range(n):
            mine, theirs = bufs[j].at[:, pl.ds(c, 1)], bufs[j].at[:, pl.ds(1 - c, 1)]
            _remote(mine, theirs, send_sem.at[j], recv_sem.at[j], (x, y, 1 - c)).wait()

    return pl.pallas_call(
        body, name="swap_pieces", in_specs=[ANY] * n, out_specs=[ANY] * n,
        out_shape=[SDS(p.shape, p.dtype) for p in pieces], input_output_aliases={j: j for j in range(n)},
        scratch_shapes=[pltpu.SemaphoreType.DMA((n,)), pltpu.SemaphoreType.DMA((n,))],
    )(*pieces)


def _gather_pieces(piece):
    def body(in_ref, out_ref, send_sem, recv_sem):
        x, y, c, k, chips = _place()
        peers = [(x, y, 1 - c)] + [(cx, cy, pc) for (cx, cy) in chips for pc in (c, 1 - c)]
        copies = []
        for ti, peer in enumerate(peers):
            cp = _remote(in_ref, out_ref.at[pl.ds(k, 1), pl.ds(c, 1)], send_sem.at[ti], recv_sem.at[ti], peer)
            cp.start()
            copies.append(cp)
        for ti, (px, py, pc) in enumerate(peers):
            _remote(in_ref, out_ref.at[pl.ds(2 * px + py, 1), pl.ds(pc, 1)], send_sem.at[ti], recv_sem.at[ti],
                    (px, py, pc)).wait_recv()
        for cp in copies:
            cp.wait_send()

    n_peers = 2 * N_CHIPS - 1
    return pl.pallas_call(
        body, name="gather_pieces", in_specs=[ANY], out_specs=ANY,
        out_shape=SDS((N_CHIPS, 2) + piece.shape[2:], piece.dtype),
        scratch_shapes=[pltpu.SemaphoreType.DMA((n_peers,)), pltpu.SemaphoreType.DMA((n_peers,))],
    )(piece)


def _pack_rows(parts, width, total_rows=None):
    rows = []
    for a in parts:
        a2 = a.reshape(-1, width)
        pad = (-a2.shape[0]) % SUBLANES
        rows.append(jnp.pad(a2, ((0, pad), (0, 0))) if pad else a2)
    out = jnp.concatenate(rows, axis=0)
    if total_rows is not None and out.shape[0] < total_rows:
        out = jnp.pad(out, ((0, total_rows - out.shape[0]), (0, 0)))
    return out


def _unpack_rows(packed, shapes, width):
    out, r = [], 0
    for shp in shapes:
        size = 1
        for s in shp:
            size *= s
        nr = size // width
        out.append(packed[r:r + nr].reshape(shp))
        r += nr + ((-nr) % SUBLANES)
    return out


def kernel(x, p, norm_g, w_in, w_out, conv_w, conv_b, conv_ln_g, conv_ln_b, sgu_ln_g, sgu_ln_b, sgu_w, sgu_b, pl_norm_g, pl_gate_w, pl_proj_w, final_g, loss_target, m_norm_g, m_w_in, m_w_out, m_conv_w, m_conv_b, m_conv_ln_g, m_conv_ln_b, m_sgu_ln_g, m_sgu_ln_b, m_sgu_w, m_sgu_b, m_pl_norm_g, m_pl_gate_w, m_pl_proj_w, m_final_g, v_norm_g, v_w_in, v_w_out, v_conv_w, v_conv_b, v_conv_ln_g, v_conv_ln_b, v_sgu_ln_g, v_sgu_ln_b, v_sgu_w, v_sgu_b, v_pl_norm_g, v_pl_gate_w, v_pl_proj_w, v_final_g):
    bsz, seq, d = x.shape
    depth = w_in.shape[0]
    e = w_out.shape[1] * N_CHIPS
    e3 = 3 * e
    n4 = w_in.shape[2]
    ple = p.shape[-1]
    dq = pl_proj_w.shape[2]
    k_taps = conv_w.shape[1]
    kp = k_taps + 1
    n_conv, n_sgu = conv_w.shape[0], sgu_ln_g.shape[0]
    t = bsz * seq
    tm_mm = min(512, seq)
    tm_mix = min(256, seq)
    my_c = lax.axis_index("c")
    my_k = 2 * lax.axis_index("x") + lax.axis_index("y")

    ec = e // N_CHIPS
    small_w = _pack_rows([conv_w.reshape(n_conv * k_taps, ec), sgu_ln_g, sgu_ln_b], ec)[None]
    shards = {"in": w_in.astype(BF16)[:, None], "out": w_out.astype(BF16)[:, None],
              "gate": pl_gate_w.astype(BF16)[:, None], "proj": pl_proj_w.astype(BF16)[:, None]}
    rest = ("out", "gate", "proj")
    gathered = {}
    gathered["in", 0], small_f = _run_comm(_gather_comm([(shards["in"], 0)], small_w), "gather_first")
    carried_by_in = {l: [("in", l + 1)] for l in range(depth - 1)}
    carried_by_mix = {l: [(nm, l + 1) for nm in rest] for l in range(depth - 1)}
    carried_by_mix[0] = [(nm, 0) for nm in rest] + carried_by_mix.get(0, [])

    def carried(keys):
        return _gather_comm([(shards[nm], ly) for nm, ly in keys]) if keys else None

    def unpack(res, keys):
        if not keys:
            return res
        gathered.update(zip(keys, res[1]))
        return res[0]

    conv_w_rows, sgu_g_rows, sgu_b_rows = _unpack_rows(
        jnp.transpose(small_f, (1, 0, 2)).reshape(small_f.shape[1], e),
        [(n_conv * k_taps, e), (n_sgu, e), (n_sgu, e)], e)
    conv_w_full = conv_w_rows.reshape(n_conv, k_taps, e)
    conv_w_pad = jnp.pad(conv_w_full, ((0, 0), (0, 1), (0, 0)))
    conv_w_fwd, conv_w_bwd = [
        jnp.transpose(conv_w_pad.reshape(n_conv, kp, e // min(cb, e), min(cb, e)), (0, 2, 1, 3)) for cb in (2 * LANES, LANES)]
    sgu_ln_g_full = sgu_g_rows.reshape(n_sgu, 1, e)
    sgu_ln_b_full = sgu_b_rows.reshape(n_sgu, 1, e)
    sgu_bt = jnp.transpose(sgu_b, (0, 2, 1))

    norm_g3 = norm_g[:, None]
    pl_norm_g3 = pl_norm_g[:, None]
    conv_b3, conv_ln_g3, conv_ln_b3 = conv_b[:, None], conv_ln_g[:, None], conv_ln_b[:, None]
    p3 = p.reshape(depth, t, ple)

    xs, hs, projs, us, x1s, y1s, weights = [], [], [], [], [], {}, []
    xc = x.reshape(t, d)
    for l in range(depth):
        j = l // 2
        xs.append(xc)
        keys = carried_by_in.get(l, [])
        h, proj = unpack(_fwd_in(xc, norm_g3, gathered["in", l], l, tm_mm, carried(keys)), keys)
        keys = carried_by_mix.get(l, [])
        if l % 2 == 0:
            u, y1s[l] = unpack(_fwd_conv(proj, conv_w_fwd, conv_b3, conv_ln_g3, conv_ln_b3, j, bsz, seq, tm_mix,
                                         carried(keys)), keys)
        else:
            (u,) = unpack(_fwd_sgu(proj, sgu_ln_g_full, sgu_ln_b_full, sgu_w, sgu_bt, j, tm_mix, carried(keys)), keys)
        w_out_l, gate_l, proj_l = gathered["out", l].reshape(1, e, d), gathered["gate", l].reshape(1, d, d), gathered["proj", l]
        weights.append((gathered["in", l], w_out_l, gate_l, proj_l))
        x1, xc = _fwd_out(xc, u, w_out_l, pl_norm_g3, gate_l, p3, proj_l, l, tm_mm)
        hs.append(h)
        projs.append(proj)
        us.append(u)
        x1s.append(x1)

    loss_local, dx, d_final_g = _loss_head(xc, final_g[None], loss_target.reshape(t, d), tm_mm)
    loss = lax.psum(loss_local[0, 0], ("x", "y", "c"))

    c_arr = my_c.astype(jnp.int32).reshape(1)
    ck_arr = jnp.stack([my_c, my_k]).astype(jnp.int32)
    by_chip = lambda a: a.reshape((1, N_CHIPS) + a.shape[1:])
    d_norm_g, d_pl_norm_g = [None] * depth, [None] * depth
    d_conv = [None] * n_conv
    d_sgu = [None] * n_sgu
    locals_, siblings, arrived = [None] * depth, [None] * depth, [None] * depth
    pending = []

    def take():
        if not pending:
            return None
        kind, _, payload = pending[0]
        return _swap_comm(payload) if kind == "swap" else _scatter_comm(payload)

    def settle(res, comm):
        if comm is None:
            return res
        outs, brought = res
        kind, layer, payload = pending.pop()
        if kind == "swap":
            locals_[layer], siblings[layer] = payload, brought
            pending.append(("scatter", layer, [by_chip(_pair_sum(gl, rc, c_arr, BF16)) for gl, rc in zip(payload, brought)]))
        else:
            arrived[layer] = brought
        return outs

    def run_alone(name):
        comm = take()
        settle(([], _run_comm(comm, name)), comm)

    for l in reversed(range(depth)):
        j = l // 2
        w_in_l, w_out_l, gate_l, proj_l = weights[l]
        comm = take()
        dx1, du, rn, ds, dqv, d_pl_norm_g[l] = settle(
            _bwd_out(dx, x1s[l], p3, pl_norm_g3, gate_l, proj_l, w_out_l, l, tm_mm, comm), comm)
        (g_proj,) = _wgrad(p3, dqv, 1, 1, N_CHIPS, tm_mm, "wgrad_proj", a_layer=l)
        (g_gate,) = _wgrad(rn, ds, 1, 1, 1, tm_mm, "wgrad_gate")
        (g_out,) = _wgrad(us[l], dx1, 1, 1, 1, tm_mm, "wgrad_out")
        comm = take()
        if l % 2 == 0:
            dproj, dcw, dcb, dlg, dlb = settle(
                _bwd_conv(du, projs[l], y1s[l], conv_w_bwd, conv_ln_g3, conv_ln_b3, j, bsz, seq, tm_mix, comm), comm)
            d_conv[j] = (dcw, dcb, dlg, dlb)
        else:
            dproj, dsw, dsbt, dlg, dlb = settle(
                _bwd_sgu(du, projs[l], sgu_ln_g_full, sgu_ln_b_full, sgu_w, sgu_bt, j, tm_mix, comm), comm)
            d_sgu[j] = (dsw, dsbt, dlg, dlb)
        (g_in,) = _wgrad(hs[l], dproj, 1, N_CHIPS, 1, tm_mm, "wgrad_in")
        local = [g_in.reshape(N_CHIPS, 2, d // 2, n4), g_out.reshape(N_CHIPS, 2, e // (2 * N_CHIPS), d),
                 g_gate.reshape(N_CHIPS, 2, d // (2 * N_CHIPS), d), g_proj.reshape(N_CHIPS, 2, ple // 2, dq)]
        while pending:
            run_alone("reduce_step")
        pending.append(("swap", l, local))
        if l == 0:
            run_alone("swap_last")
        comm = take()
        dx, d_norm_g[l] = settle(_bwd_in(dproj, dx1, xs[l], norm_g3, w_in_l, l, tm_mm, comm), comm)
    while pending:
        run_alone("reduce_tail")
    grad_x = dx.reshape(bsz, seq, d)

    d_conv_w = jnp.stack([jnp.transpose(dc[0], (1, 0, 2)).reshape(kp, e)[:k_taps] for dc in d_conv])
    d_conv_b = jnp.stack([dc[1][0] for dc in d_conv])
    d_conv_ln_g = jnp.stack([dc[2][0] for dc in d_conv])
    d_conv_ln_b = jnp.stack([dc[3][0] for dc in d_conv])
    d_sgu_w = jnp.stack([dsg[0] for dsg in d_sgu])
    d_sgu_b = jnp.stack([jnp.transpose(dsg[1]) for dsg in d_sgu])
    d_sgu_ln_g = jnp.stack([dsg[2][0] for dsg in d_sgu])
    d_sgu_ln_b = jnp.stack([dsg[3][0] for dsg in d_sgu])
    small_grads = [jnp.concatenate(d_norm_g), d_conv_w, d_conv_b, d_conv_ln_g, d_conv_ln_b, d_sgu_ln_g, d_sgu_ln_b,
                   d_sgu_w, d_sgu_b, jnp.concatenate(d_pl_norm_g), d_final_g]
    small_shapes = [a.shape for a in small_grads]
    packed = _pack_rows(small_grads, d)
    pack_rows = packed.shape[0] + ((-packed.shape[0]) % (8 * SUBLANES))
    packed = _pack_rows(small_grads, d, pack_rows)
    gl_small = packed.reshape(N_CHIPS, 2, pack_rows // 8, d)
    (small_sibling,) = _run_comm(_swap_comm([gl_small]), "swap_small")
    small_pair = by_chip(_pair_sum(gl_small, small_sibling, c_arr, F32))

    (small_arrived,) = _run_comm(_scatter_comm([small_pair]), "scatter_small")
    reduced = [lax.empty((depth, 2) + gl.shape[2:], F32) for gl in locals_[0]]
    for l in range(depth):
        reduced = [_chip_sum(gl, rc, ar, ck_arr, acc, l)
                   for gl, rc, ar, acc in zip(locals_[l], siblings[l], arrived[l], reduced)]
    small_both = _chip_sum(gl_small, small_sibling, small_arrived, ck_arr, lax.empty((1, 2) + gl_small.shape[2:], F32), 0)
    small_mine = lax.dynamic_slice_in_dim(small_both, my_c, 1, axis=1)
    reduced = _swap_pieces(reduced)
    small_all = _gather_pieces(small_mine)
    small_all = lax.dynamic_update_slice(small_all, small_mine, (my_k, my_c, 0, 0)).reshape(pack_rows, d)
    small_red = _unpack_rows(small_all, small_shapes, d)
    (gr_norm_g, gr_conv_w, gr_conv_b, gr_conv_ln_g, gr_conv_ln_b, gr_sgu_ln_g, gr_sgu_ln_b, gr_sgu_w, gr_sgu_b,
     gr_pl_norm_g, gr_final_g) = small_red
    gr_final_g = gr_final_g.reshape(d)
    gr_conv_w = lax.dynamic_slice_in_dim(gr_conv_w, my_k * ec, ec, axis=2)
    gr_sgu_ln_g = lax.dynamic_slice_in_dim(gr_sgu_ln_g, my_k * ec, ec, axis=1)
    gr_sgu_ln_b = lax.dynamic_slice_in_dim(gr_sgu_ln_b, my_k * ec, ec, axis=1)

    def shard_update(w, g, m, v):
        flat = lambda a: a.reshape(-1, w.shape[-1])
        return [o.reshape(w.shape) for o in _adamw(flat(w), flat(g), flat(m), flat(v))]

    up_in = shard_update(w_in, reduced[0], m_w_in, v_w_in)
    up_out = shard_update(w_out, reduced[1], m_w_out, v_w_out)
    up_gate = shard_update(pl_gate_w, reduced[2], m_pl_gate_w, v_pl_gate_w)
    up_proj = shard_update(pl_proj_w, reduced[3], m_pl_proj_w, v_pl_proj_w)

    small_names = ["norm_g", "conv_w", "conv_b", "conv_ln_g", "conv_ln_b", "sgu_ln_g", "sgu_ln_b", "sgu_w", "sgu_b",
                   "pl_norm_g", "final_g"]
    small_w_list = [norm_g, conv_w, conv_b, conv_ln_g, conv_ln_b, sgu_ln_g, sgu_ln_b, sgu_w, sgu_b, pl_norm_g, final_g]
    small_m_list = [m_norm_g, m_conv_w, m_conv_b, m_conv_ln_g, m_conv_ln_b, m_sgu_ln_g, m_sgu_ln_b, m_sgu_w, m_sgu_b,
                    m_pl_norm_g, m_final_g]
    small_v_list = [v_norm_g, v_conv_w, v_conv_b, v_conv_ln_g, v_conv_ln_b, v_sgu_ln_g, v_sgu_ln_b, v_sgu_w, v_sgu_b,
                    v_pl_norm_g, v_final_g]
    small_g_list = [gr_norm_g, gr_conv_w, gr_conv_b, gr_conv_ln_g, gr_conv_ln_b, gr_sgu_ln_g, gr_sgu_ln_b, gr_sgu_w,
                    gr_sgu_b, gr_pl_norm_g, gr_final_g]
    width = ec
    shapes_local = [a.shape for a in small_w_list]
    outs_small = _adamw(_pack_rows(small_w_list, width), _pack_rows(small_g_list, width),
                        _pack_rows(small_m_list, width), _pack_rows(small_v_list, width))
    unpacked = [_unpack_rows(o, shapes_local, width) for o in outs_small]
    ups = {name: [unpacked[kind][i] for kind in range(4)] for i, name in enumerate(small_names)}
    ups["w_in"], ups["w_out"], ups["pl_gate_w"], ups["pl_proj_w"] = up_in, up_out, up_gate, up_proj

    order = ["norm_g", "w_in", "w_out", "conv_w", "conv_b", "conv_ln_g", "conv_ln_b", "sgu_ln_g", "sgu_ln_b", "sgu_w",
             "sgu_b", "pl_norm_g", "pl_gate_w", "pl_proj_w", "final_g"]
    result = [loss, grad_x]
    for kind in range(4):
        result.extend(ups[name][kind] for name in order)
    return tuple(result)
```

```python
import functools

import jax
import jax.numpy as jnp
from jax import lax
from jax.experimental import pallas as pl
from jax.experimental.pallas import tpu as pltpu

F32 = jnp.float32
BF16 = jnp.bfloat16
SDS = jax.ShapeDtypeStruct

EPS = 1e-6
CHUNK = 128
GROUPS = 8
HALO = 32
N_CHIPS = 4
LANES = 128
SUBLANES = 8
V7X_VMEM_LIMIT = 56 << 20

ADAM_LR = 0.001
ADAM_B1 = 0.9
ADAM_B2 = 0.999
ADAM_EPS = 1e-08
ADAM_WD = 0.01
ADAM_STEP = 10

MESH_IDS = pl.DeviceIdType.MESH
ANY = pl.BlockSpec(memory_space=pl.ANY)


def _params(n_axes):
    return pltpu.CompilerParams(dimension_semantics=("arbitrary",) * n_axes, vmem_limit_bytes=V7X_VMEM_LIMIT)


def _const(shape):
    zeros = (0,) * len(shape)
    return pl.BlockSpec(shape, lambda *_: zeros)


def _layer(shape, layer):
    zeros = (0,) * len(shape)
    return pl.BlockSpec((None,) + tuple(shape), lambda *_: (layer,) + zeros, pipeline_mode=pl.Buffered(1))


class _Comm:
    def __init__(self, operands, out_shape, sems, start, finish):
        self.operands, self.out_shape, self.sems, self.start, self.finish = operands, out_shape, sems, start, finish


def _call(body, *, name, grid, in_specs, out_specs, out_shape, operands, scratch_shapes=(), comm=None):
    in_specs, out_specs, out_shape, scratch_shapes = list(in_specs), list(out_specs), list(out_shape), list(scratch_shapes)
    if comm is None:
        return pl.pallas_call(body, name=name, grid=grid, in_specs=in_specs, out_specs=out_specs, out_shape=out_shape,
                              scratch_shapes=scratch_shapes, compiler_params=_params(len(grid)))(*operands)
    n_in, n_out, n_sc = len(in_specs), len(out_specs), len(scratch_shapes)
    ci, co = len(comm.operands), len(comm.out_shape)

    def hosted(*refs):
        ins, cins = refs[:n_in], refs[n_in:n_in + ci]
        outs, couts = refs[n_in + ci:n_in + ci + n_out], refs[n_in + ci + n_out:n_in + ci + n_out + co]
        scratch = refs[n_in + ci + n_out + co:n_in + ci + n_out + co + n_sc]
        sems = refs[n_in + ci + n_out + co + n_sc:]
        first = functools.reduce(jnp.logical_and, [pl.program_id(a) == 0 for a in range(len(grid))])
        last = functools.reduce(jnp.logical_and, [pl.program_id(a) == g - 1 for a, g in enumerate(grid)])

        @pl.when(first)
        def _():
            comm.start(cins, couts, sems)

        body(*ins, *outs, *scratch)

        @pl.when(last)
        def _():
            comm.finish(cins, couts, sems)

    res = pl.pallas_call(
        hosted, name=name, grid=grid, in_specs=in_specs + [ANY] * ci, out_specs=out_specs + [ANY] * co,
        out_shape=out_shape + list(comm.out_shape), scratch_shapes=scratch_shapes + list(comm.sems),
        compiler_params=_params(len(grid)))(*operands, *comm.operands)
    return res[:n_out], res[n_out:]


def _run_comm(comm, name):
    ci, co = len(comm.operands), len(comm.out_shape)

    def body(*refs):
        comm.start(refs[:ci], refs[ci:ci + co], refs[ci + co:])
        comm.finish(refs[:ci], refs[ci:ci + co], refs[ci + co:])

    return pl.pallas_call(body, name=name, in_specs=[ANY] * ci, out_specs=[ANY] * co, out_shape=list(comm.out_shape),
                          scratch_shapes=list(comm.sems))(*comm.operands)


def _sigmoid(v):
    return jax.nn.sigmoid(v)


def _dsilu(v, s):
    return s * (1.0 + v * (1.0 - s))


def _gelu_parts(v):
    cdf = 0.5 * (1.0 + lax.erf(v * 0.7071067811865476))
    pdf = jnp.exp(-0.5 * v * v) * 0.3989422804014327
    return v * cdf, cdf + v * pdf


def _gelu(v):
    return 0.5 * v * (1.0 + lax.erf(v * 0.7071067811865476))


def _rms_stats(x):
    r = lax.rsqrt(jnp.mean(x * x, axis=-1, keepdims=True) + EPS)
    return r, x * r


def _rms_bwd(dy, g, r, xh):
    gdy = dy * g
    return r * (gdy - xh * jnp.mean(xh * gdy, axis=-1, keepdims=True))


def _ln_stats(x):
    mu = jnp.mean(x, axis=-1, keepdims=True)
    xc = x - mu
    rs = lax.rsqrt(jnp.mean(xc * xc, axis=-1, keepdims=True) + EPS)
    return rs, xc * rs


def _ln_bwd(dy, g, rs, xh):
    dxh = dy * g
    return rs * (dxh - jnp.mean(dxh, axis=-1, keepdims=True) - xh * jnp.mean(dxh * xh, axis=-1, keepdims=True))


def _dot(a, b):
    return jnp.dot(a, b, preferred_element_type=F32)


def _dot_nt(a, b):
    return lax.dot_general(a, b, (((1,), (1,)), ((), ())), preferred_element_type=F32)


def _dot_tn(a, b):
    return lax.dot_general(a, b, (((0,), (0,)), ((), ())), preferred_element_type=F32)


def _fwd_in(x, norm_g, w_in_full, layer, tm, comm=None):
    t, d = x.shape
    _, nk, _, n4 = w_in_full.shape

    def body(x_ref, g_ref, w_ref, h_ref, proj_ref):
        r, xh = _rms_stats(x_ref[...])
        h = (xh * g_ref[...]).astype(BF16)
        h_ref[...] = h
        for k in range(nk):
            proj_ref[:, k * n4:(k + 1) * n4] = _dot(h, w_ref[k]).astype(BF16)

    return _call(
        body, name="fwd_in", grid=(t // tm,),
        in_specs=[pl.BlockSpec((tm, d), lambda i: (i, 0)), _layer((1, d), layer), _layer((nk, d, n4), 0)],
        out_specs=[pl.BlockSpec((tm, d), lambda i: (i, 0)), pl.BlockSpec((tm, nk * n4), lambda i: (i, 0))],
        out_shape=[SDS((t, d), BF16), SDS((t, nk * n4), BF16)],
        operands=(x, norm_g, w_in_full), comm=comm)


def _halo_maps(nt, hb, n_halo_blocks):
    def prev(b, i):
        return (jnp.maximum((b * nt + i) * hb - 1, 0), 0)

    def nxt(b, i):
        return (jnp.minimum((b * nt + i + 1) * hb, n_halo_blocks - 1), 0)

    return prev, nxt


TAP_TILE_VREGS = 16


def _tap_rows(cb, tm):
    return min(TAP_TILE_VREGS * SUBLANES * LANES // cb, tm)


def _conv_taps(src_ref, w_ref, dst_ref, cb_idx, n_rows, rb, first, reverse):
    k_taps = w_ref.shape[1] - 1
    for r0 in range(0, n_rows, rb):
        acc = None
        for res in range(SUBLANES):
            rows = rb + (SUBLANES if res else 0)
            group = None
            for k in range(k_taps):
                off = first + k
                if off % SUBLANES != res:
                    continue
                wk = w_ref[cb_idx, pl.ds((k_taps - 1 - k) if reverse else k, 1), :]
                term = wk * src_ref[cb_idx, pl.ds(r0 + off - res, rows), :]
                group = term if group is None else group + term
            if group is None:
                continue
            part = group[res:res + rb] if res else group
            acc = part if acc is None else acc + part
        dst_ref[cb_idx, pl.ds(r0, rb), :] = acc


def _fwd_conv(proj, conv_w_blk, conv_b, ln_g, ln_b, layer, bsz, seq, tm, comm=None):
    t, e3 = proj.shape
    e = e3 // 3
    nt = seq // tm
    hb = tm // HALO
    _, ncb, kp, cb = conv_w_blk.shape
    rb = _tap_rows(cb, tm)
    prev, _ = _halo_maps(nt, hb, t // HALO)

    def body(proj_ref, halo_ref, w_ref, b_ref, g_ref, lb_ref, u_ref, y1_ref, y0s, y1s):
        i = pl.program_id(1)
        a = proj_ref[:, 0:e].astype(F32)
        b = proj_ref[:, e:2 * e].astype(F32)
        y0 = a * _sigmoid(b)
        ah = halo_ref[:, 0:e].astype(F32)
        bh = halo_ref[:, e:2 * e].astype(F32)
        y0h = jnp.where(i > 0, ah * _sigmoid(bh), 0.0)
        for c in range(ncb):
            y0s[c, 0:HALO, :] = y0h[:, c * cb:(c + 1) * cb]
            y0s[c, HALO:HALO + tm, :] = y0[:, c * cb:(c + 1) * cb]

        def per_block(c, carry):
            _conv_taps(y0s, w_ref, y1s, c, tm, rb, HALO - (kp - 2), False)
            return carry

        lax.fori_loop(0, ncb, per_block, 0)
        y1 = jnp.concatenate([y1s[c] for c in range(ncb)], axis=1) + b_ref[...]
        y1_ref[...] = y1
        rs, xh = _ln_stats(y1)
        y2 = xh * g_ref[...] + lb_ref[...]
        y = y2 * _sigmoid(y2)
        z = proj_ref[:, 2 * e:3 * e].astype(F32)
        u_ref[...] = (y * (z * _sigmoid(z))).astype(BF16)

    return _call(
        body, name="fwd_conv", grid=(bsz, nt),
        in_specs=[pl.BlockSpec((tm, e3), lambda b, i: (b * nt + i, 0)),
                  pl.BlockSpec((HALO, 2 * e), prev),
                  _layer((ncb, kp, cb), layer), _layer((1, e), layer), _layer((1, e), layer), _layer((1, e), layer)],
        out_specs=[pl.BlockSpec((tm, e), lambda b, i: (b * nt + i, 0)), pl.BlockSpec((tm, e), lambda b, i: (b * nt + i, 0))],
        out_shape=[SDS((t, e), BF16), SDS((t, e), F32)],
        scratch_shapes=[pltpu.VMEM((ncb, HALO + tm, cb), F32), pltpu.VMEM((ncb, tm, cb), F32)],
        operands=(proj, proj, conv_w_blk, conv_b, ln_g, ln_b), comm=comm)


def _tril_mask():
    rows = lax.broadcasted_iota(jnp.int32, (CHUNK, CHUNK), 0)
    cols = lax.broadcasted_iota(jnp.int32, (CHUNK, CHUNK), 1)
    return rows >= cols


def _fwd_sgu(proj, ln_g, ln_b, sgu_w, sgu_bt, layer, tm, comm=None):
    t, e3 = proj.shape
    e = e3 // 3
    gw = e // GROUPS
    nch = tm // CHUNK

    def body(proj_ref, g_ref, lb_ref, w_ref, bt_ref, u_ref, mixed):
        a = proj_ref[:, 0:e].astype(F32)
        b = proj_ref[:, e:2 * e].astype(F32)
        z = proj_ref[:, 2 * e:3 * e].astype(F32)
        rs, xh = _ln_stats(_gelu(b))
        v = (xh * g_ref[...] + lb_ref[...]).astype(BF16)
        mask = _tril_mask()
        for g in range(GROUPS):
            wm = jnp.where(mask, w_ref[g], 0.0).astype(BF16)
            bias = bt_ref[:, g:g + 1]
            for n in range(nch):
                blk = v[n * CHUNK:(n + 1) * CHUNK, g * gw:(g + 1) * gw]
                mixed[n * CHUNK:(n + 1) * CHUNK, g * gw:(g + 1) * gw] = _dot(wm, blk) + bias
        y = _gelu(a) * mixed[...]
        u_ref[...] = (y * (z * _sigmoid(z))).astype(BF16)

    return _call(
        body, name="fwd_sgu", grid=(t // tm,),
        in_specs=[pl.BlockSpec((tm, e3), lambda i: (i, 0)), _layer((1, e), layer), _layer((1, e), layer),
                  _layer((GROUPS, CHUNK, CHUNK), layer), _layer((CHUNK, GROUPS), layer)],
        out_specs=[pl.BlockSpec((tm, e), lambda i: (i, 0))],
        out_shape=[SDS((t, e), BF16)],
        scratch_shapes=[pltpu.VMEM((tm, e), F32)],
        operands=(proj, ln_g, ln_b, sgu_w, sgu_bt), comm=comm)


def _ple_forward(x1, p_ref, plg_ref, gw_ref, pw_ref):
    nk, _, dq = pw_ref.shape
    r, xh = _rms_stats(x1)
    rn = (xh * plg_ref[...]).astype(BF16)
    gate = _sigmoid(_dot(rn, gw_ref[...]))
    pb = p_ref[...].astype(BF16)
    q = jnp.concatenate([_dot(pb, pw_ref[k]) for k in range(nk)], axis=1)
    return r, xh, rn, gate, q


def _fwd_out(x, u, w_out_full, pl_norm_g, gate_w_full, p, proj_w_full, layer, tm, comm=None):
    t, d = x.shape
    e = u.shape[1]
    ple = p.shape[-1]
    nk, dq = proj_w_full.shape[1], proj_w_full.shape[3]

    def body(x_ref, u_ref, wo_ref, plg_ref, gw_ref, p_ref, pw_ref, x1_ref, x2_ref):
        x1 = x_ref[...] + _dot(u_ref[...], wo_ref[...])
        x1_ref[...] = x1
        _, _, _, gate, q = _ple_forward(x1, p_ref, plg_ref, gw_ref, pw_ref)
        x2_ref[...] = x1 + gate * q

    return _call(
        body, name="fwd_out", grid=(t // tm,),
        in_specs=[pl.BlockSpec((tm, d), lambda i: (i, 0)), pl.BlockSpec((tm, e), lambda i: (i, 0)),
                  _layer((e, d), 0), _layer((1, d), layer), _layer((d, d), 0),
                  pl.BlockSpec((None, tm, ple), lambda i: (layer, i, 0)), _layer((nk, ple, dq), 0)],
        out_specs=[pl.BlockSpec((tm, d), lambda i: (i, 0)), pl.BlockSpec((tm, d), lambda i: (i, 0))],
        out_shape=[SDS((t, d), F32), SDS((t, d), F32)],
        operands=(x, u, w_out_full, pl_norm_g, gate_w_full, p, proj_w_full), comm=comm)


def _loss_head(x, final_g, target, tm):
    t, d = x.shape
    n_steps = t // tm

    def body(x_ref, g_ref, tgt_ref, loss_ref, dx_ref, dg_ref, sq_acc):
        i = pl.program_id(0)

        @pl.when(i == 0)
        def _():
            sq_acc[...] = jnp.zeros_like(sq_acc)
            dg_ref[...] = jnp.zeros_like(dg_ref)

        g = g_ref[...]
        r, xh = _rms_stats(x_ref[...])
        diff = xh * g - tgt_ref[...]
        sq_acc[...] += jnp.sum(diff * diff, axis=0, keepdims=True)
        dout = diff * (1.0 / d)
        dg_ref[...] += jnp.sum(dout * xh, axis=0, keepdims=True)
        dx_ref[...] = _rms_bwd(dout, g, r, xh)

        @pl.when(i == n_steps - 1)
        def _():
            loss_ref[...] = jnp.sum(sq_acc[...], axis=1, keepdims=True) * (0.5 / d)

    return pl.pallas_call(
        body, name="loss_head", grid=(n_steps,),
        in_specs=[pl.BlockSpec((tm, d), lambda i: (i, 0)), _const((1, d)), pl.BlockSpec((tm, d), lambda i: (i, 0))],
        out_specs=[_const((1, 1)), pl.BlockSpec((tm, d), lambda i: (i, 0)), _const((1, d))],
        out_shape=[SDS((1, 1), F32), SDS((t, d), F32), SDS((1, d), F32)],
        scratch_shapes=[pltpu.VMEM((1, d), F32)],
        compiler_params=_params(1),
    )(x, final_g, target)


def _bwd_out(dx2, x1, p, pl_norm_g, gate_w_full, proj_w_full, w_out_full, layer, tm, comm=None):
    t, d = dx2.shape
    e = w_out_full.shape[1]
    ple = p.shape[-1]
    nk, dq_w = proj_w_full.shape[1], proj_w_full.shape[3]

    def body(dx2_ref, x1_ref, p_ref, plg_ref, gw_ref, pw_ref, wo_ref, dx1_ref, du_ref, rn_ref, ds_ref, dq_ref, dplg_ref):
        @pl.when(pl.program_id(0) == 0)
        def _():
            dplg_ref[...] = jnp.zeros_like(dplg_ref)

        dx2v = dx2_ref[...]
        r, xh, rn, gate, q = _ple_forward(x1_ref[...], p_ref, plg_ref, gw_ref, pw_ref)
        rn_ref[...] = rn
        dq_ref[...] = (dx2v * gate).astype(BF16)
        ds = (dx2v * q * gate * (1.0 - gate)).astype(BF16)
        ds_ref[...] = ds
        dr = _dot_nt(ds, gw_ref[...])
        dplg_ref[...] += jnp.sum(dr * xh, axis=0, keepdims=True)
        dx1 = dx2v + _rms_bwd(dr, plg_ref[...], r, xh)
        dx1_ref[...] = dx1
        du_ref[...] = _dot_nt(dx1.astype(BF16), wo_ref[...]).astype(BF16)

    row = lambda w: pl.BlockSpec((tm, w), lambda i: (i, 0))
    return _call(
        body, name="bwd_out", grid=(t // tm,),
        in_specs=[row(d), row(d), pl.BlockSpec((None, tm, ple), lambda i: (layer, i, 0)),
                  _layer((1, d), layer), _layer((d, d), 0), _layer((nk, ple, dq_w), 0), _layer((e, d), 0)],
        out_specs=[row(d), row(e), row(d), row(d), row(d), _const((1, d))],
        out_shape=[SDS((t, d), F32), SDS((t, e), BF16), SDS((t, d), BF16), SDS((t, d), BF16), SDS((t, d), BF16),
                   SDS((1, d), F32)],
        operands=(dx2, x1, p, pl_norm_g, gate_w_full, proj_w_full, w_out_full), comm=comm)


def _bwd_conv(du, proj, y1, conv_w_blk, ln_g, ln_b, layer, bsz, seq, tm, comm=None):
    t, e3 = proj.shape
    e = e3 // 3
    nt = seq // tm
    hb = tm // HALO
    _, ncb, kp, cb = conv_w_blk.shape
    rb = _tap_rows(cb, tm)
    k_taps = kp - 1
    prev, nxt = _halo_maps(nt, hb, t // HALO)
    z_halo = lambda b, i: (nxt(b, i)[0], 2)

    def ln_silu_bwd(du_v, z_v, y1_v, g, lb):
        rs, xh = _ln_stats(y1_v)
        y2 = xh * g + lb
        sg = _sigmoid(y2)
        sz = _sigmoid(z_v)
        dy = du_v * (z_v * sz)
        dy2 = dy * _dsilu(y2, sg)
        return _ln_bwd(dy2, g, rs, xh), dy2, xh, du_v * (y2 * sg) * _dsilu(z_v, sz)

    def body(du_ref, proj_ref, y1_ref, duh_ref, zh_ref, y1h_ref, abh_ref, w_ref, g_ref, lb_ref,
             dproj_ref, dw_ref, dcb_ref, dg_ref, dlb_ref, y0s, dy1s, dy0s, ysh):
        b_id, i = pl.program_id(0), pl.program_id(1)

        @pl.when((b_id == 0) & (i == 0))
        def _():
            dw_ref[...] = jnp.zeros_like(dw_ref)
            dcb_ref[...] = jnp.zeros_like(dcb_ref)
            dg_ref[...] = jnp.zeros_like(dg_ref)
            dlb_ref[...] = jnp.zeros_like(dlb_ref)

        g, lb = g_ref[...], lb_ref[...]
        a = proj_ref[:, 0:e].astype(F32)
        b = proj_ref[:, e:2 * e].astype(F32)
        z = proj_ref[:, 2 * e:3 * e].astype(F32)
        sb = _sigmoid(b)
        y0 = a * sb
        dy1, dy2, xh, dz = ln_silu_bwd(du_ref[...].astype(F32), z, y1_ref[...], g, lb)
        dproj_ref[:, 2 * e:3 * e] = dz.astype(BF16)
        dg_ref[...] += jnp.sum(dy2 * xh, axis=0, keepdims=True)
        dlb_ref[...] += jnp.sum(dy2, axis=0, keepdims=True)
        dcb_ref[...] += jnp.sum(dy1, axis=0, keepdims=True)
        dy1h, _, _, _ = ln_silu_bwd(duh_ref[...].astype(F32), zh_ref[...].astype(F32), y1h_ref[...], g, lb)
        dy1h = jnp.where(i < nt - 1, dy1h, 0.0)
        ah = abh_ref[:, 0:e].astype(F32)
        bh = abh_ref[:, e:2 * e].astype(F32)
        y0h = jnp.where(i > 0, ah * _sigmoid(bh), 0.0)
        for c in range(ncb):
            cols = slice(c * cb, (c + 1) * cb)
            y0s[c, 0:HALO, :] = y0h[:, cols]
            y0s[c, HALO:HALO + tm, :] = y0[:, cols]
            dy1s[c, 0:tm, :] = dy1[:, cols]
            dy1s[c, tm:tm + HALO, :] = dy1h[:, cols]

        def per_block(c, carry):
            _conv_taps(dy1s, w_ref, dy0s, c, tm, rb, 0, True)
            for res in range(1, SUBLANES):
                ysh[res - 1] = y0s[c, pl.ds(res, tm + HALO - SUBLANES), :]
            for k in range(k_taps):
                off = HALO - (k_taps - 1) + k
                res = off % SUBLANES
                acc = None
                for r0 in range(0, tm, rb):
                    rows = pl.ds(r0 + off - res, rb)
                    shifted = ysh[res - 1, rows, :] if res else y0s[c, rows, :]
                    term = dy1s[c, pl.ds(r0, rb), :] * shifted
                    acc = term if acc is None else acc + term
                dw_ref[c, pl.ds(k, 1), :] += jnp.sum(acc, axis=0, keepdims=True)
            return carry

        lax.fori_loop(0, ncb, per_block, 0)
        dy0 = jnp.concatenate([dy0s[c] for c in range(ncb)], axis=1)
        dproj_ref[:, 0:e] = (dy0 * sb).astype(BF16)
        dproj_ref[:, e:2 * e] = (dy0 * a * sb * (1.0 - sb)).astype(BF16)

    tile = lambda w: pl.BlockSpec((tm, w), lambda b, i: (b * nt + i, 0))
    return _call(
        body, name="bwd_conv", grid=(bsz, nt),
        in_specs=[tile(e), tile(e3), tile(e),
                  pl.BlockSpec((HALO, e), nxt), pl.BlockSpec((HALO, e), z_halo), pl.BlockSpec((HALO, e), nxt),
                  pl.BlockSpec((HALO, 2 * e), prev),
                  _layer((ncb, kp, cb), layer), _layer((1, e), layer), _layer((1, e), layer)],
        out_specs=[tile(e3), _const((ncb, kp, cb)), _const((1, e)), _const((1, e)), _const((1, e))],
        out_shape=[SDS((t, e3), BF16), SDS((ncb, kp, cb), F32), SDS((1, e), F32), SDS((1, e), F32), SDS((1, e), F32)],
        scratch_shapes=[pltpu.VMEM((ncb, HALO + tm, cb), F32), pltpu.VMEM((ncb, tm + HALO, cb), F32),
                        pltpu.VMEM((ncb, tm, cb), F32), pltpu.VMEM((SUBLANES - 1, tm + HALO - SUBLANES, cb), F32)],
        operands=(du, proj, y1, du, proj, y1, proj, conv_w_blk, ln_g, ln_b), comm=comm)


def _bwd_sgu(du, proj, ln_g, ln_b, sgu_w, sgu_bt, layer, tm, comm=None):
    t, e3 = proj.shape
    e = e3 // 3
    gw = e // GROUPS
    nch = tm // CHUNK

    def body(du_ref, proj_ref, g_ref, lb_ref, w_ref, bt_ref, dproj_ref, dw_ref, dbt_ref, dg_ref, dlb_ref, mixed, dmix, dv):
        @pl.when(pl.program_id(0) == 0)
        def _():
            dw_ref[...] = jnp.zeros_like(dw_ref)
            dbt_ref[...] = jnp.zeros_like(dbt_ref)
            dg_ref[...] = jnp.zeros_like(dg_ref)
            dlb_ref[...] = jnp.zeros_like(dlb_ref)

        g, lb = g_ref[...], lb_ref[...]
        a = proj_ref[:, 0:e].astype(F32)
        b = proj_ref[:, e:2 * e].astype(F32)
        z = proj_ref[:, 2 * e:3 * e].astype(F32)
        ug, dug = _gelu_parts(a)
        vb, dvb_db = _gelu_parts(b)
        rs, xh = _ln_stats(vb)
        v = (xh * g + lb).astype(BF16)
        mask = _tril_mask()
        for gi in range(GROUPS):
            wm = jnp.where(mask, w_ref[gi], 0.0).astype(BF16)
            bias = bt_ref[:, gi:gi + 1]
            for c in range(nch):
                blk = v[c * CHUNK:(c + 1) * CHUNK, gi * gw:(gi + 1) * gw]
                mixed[c * CHUNK:(c + 1) * CHUNK, gi * gw:(gi + 1) * gw] = _dot(wm, blk) + bias
        mx = mixed[...]
        sz = _sigmoid(z)
        duv = du_ref[...].astype(F32)
        dy = duv * (z * sz)
        dproj_ref[:, 2 * e:3 * e] = (duv * (ug * mx) * _dsilu(z, sz)).astype(BF16)
        dproj_ref[:, 0:e] = (dy * mx * dug).astype(BF16)
        dmix[...] = dy * ug
        for gi in range(GROUPS):
            wm = jnp.where(mask, w_ref[gi], 0.0).astype(BF16)
            dw_acc = None
            db_acc = None
            for c in range(nch):
                rows, cols = slice(c * CHUNK, (c + 1) * CHUNK), slice(gi * gw, (gi + 1) * gw)
                dm = dmix[rows, cols]
                dmb = dm.astype(BF16)
                dw_n = _dot_nt(dmb, v[rows, cols])
                db_n = jnp.sum(dm, axis=1, keepdims=True)
                dw_acc = dw_n if dw_acc is None else dw_acc + dw_n
                db_acc = db_n if db_acc is None else db_acc + db_n
                dv[rows, cols] = _dot_tn(wm, dmb)
            dw_ref[gi] += jnp.where(mask, dw_acc, 0.0)
            dbt_ref[:, gi:gi + 1] += db_acc
        dvv = dv[...]
        dg_ref[...] += jnp.sum(dvv * xh, axis=0, keepdims=True)
        dlb_ref[...] += jnp.sum(dvv, axis=0, keepdims=True)
        dproj_ref[:, e:2 * e] = (_ln_bwd(dvv, g, rs, xh) * dvb_db).astype(BF16)

    return _call(
        body, name="bwd_sgu", grid=(t // tm,),
        in_specs=[pl.BlockSpec((tm, e), lambda i: (i, 0)), pl.BlockSpec((tm, e3), lambda i: (i, 0)),
                  _layer((1, e), layer), _layer((1, e), layer),
                  _layer((GROUPS, CHUNK, CHUNK), layer), _layer((CHUNK, GROUPS), layer)],
        out_specs=[pl.BlockSpec((tm, e3), lambda i: (i, 0)), _const((GROUPS, CHUNK, CHUNK)), _const((CHUNK, GROUPS)),
                   _const((1, e)), _const((1, e))],
        out_shape=[SDS((t, e3), BF16), SDS((GROUPS, CHUNK, CHUNK), F32), SDS((CHUNK, GROUPS), F32),
                   SDS((1, e), F32), SDS((1, e), F32)],
        scratch_shapes=[pltpu.VMEM((tm, e), F32), pltpu.VMEM((tm, e), F32), pltpu.VMEM((tm, e), F32)],
        operands=(du, proj, ln_g, ln_b, sgu_w, sgu_bt), comm=comm)


def _bwd_in(dproj, dx1, x, norm_g, w_in_full, layer, tm, comm=None):
    t, d = x.shape
    _, nk, _, n4 = w_in_full.shape

    def body(dproj_ref, dx1_ref, x_ref, g_ref, w_ref, dx_ref, dg_ref):
        @pl.when(pl.program_id(0) == 0)
        def _():
            dg_ref[...] = jnp.zeros_like(dg_ref)

        dh = None
        for k in range(nk):
            part = _dot_nt(dproj_ref[:, k * n4:(k + 1) * n4], w_ref[k])
            dh = part if dh is None else dh + part
        r, xh = _rms_stats(x_ref[...])
        dg_ref[...] += jnp.sum(dh * xh, axis=0, keepdims=True)
        dx_ref[...] = dx1_ref[...] + _rms_bwd(dh, g_ref[...], r, xh)

    row = lambda w: pl.BlockSpec((tm, w), lambda i: (i, 0))
    return _call(
        body, name="bwd_in", grid=(t // tm,),
        in_specs=[row(nk * n4), row(d), row(d), _layer((1, d), layer), _layer((nk, d, n4), 0)],
        out_specs=[row(d), _const((1, d))],
        out_shape=[SDS((t, d), F32), SDS((1, d), F32)],
        operands=(dproj, dx1, x, norm_g, w_in_full), comm=comm)


def _wgrad(a, b, kblk, nblk, n_split, tm, name, a_layer=None, comm=None):
    t, n = b.shape
    k = a.shape[-1]
    kw, nw = k // kblk, n // nblk
    nws = nw // n_split
    n_steps = t // tm

    def body(a_ref, b_ref, o_ref):
        @pl.when(pl.program_id(2) == 0)
        def _():
            o_ref[...] = jnp.zeros_like(o_ref)

        res = _dot_tn(a_ref[...].astype(BF16), b_ref[...].astype(BF16))
        for s in range(n_split):
            o_ref[s] += res[:, s * nws:(s + 1) * nws]

    if a_layer is None:
        a_spec = pl.BlockSpec((tm, kw), lambda kb, nb, i: (i, kb))
    else:
        a_spec = pl.BlockSpec((None, tm, kw), lambda kb, nb, i: (a_layer, i, kb))
    return _call(
        body, name=name, grid=(kblk, nblk, n_steps),
        in_specs=[a_spec, pl.BlockSpec((tm, nw), lambda kb, nb, i: (i, nb))],
        out_specs=[pl.BlockSpec((None, n_split, kw, nws), lambda kb, nb, i: (kb, nb, 0, 0))],
        out_shape=[SDS((kblk, nblk * n_split, kw, nws), F32)],
        operands=(a, b), comm=comm)


def _row_tile(rows, cols, budget_bytes=1 << 20):
    best = None
    for cand in range(SUBLANES, rows + 1, SUBLANES):
        if rows % cand == 0 and cand * cols * 4 <= budget_bytes:
            best = cand
    return best if best is not None else rows


def _pair_sum(grads, recv, my_c, wire_dtype):
    n, _, h, c = grads.shape
    th = _row_tile(h, c)

    def body(c_ref, g_ref, r_ref, o_ref):
        o_ref[...] = (g_ref[...] + r_ref[...]).astype(wire_dtype)

    grid_spec = pltpu.PrefetchScalarGridSpec(
        num_scalar_prefetch=1, grid=(n, h // th),
        in_specs=[pl.BlockSpec((None, None, th, c), lambda j, i, c_ref: (j, c_ref[0], i, 0)),
                  pl.BlockSpec((None, None, th, c), lambda j, i, c_ref: (j, 0, i, 0))],
        out_specs=pl.BlockSpec((None, th, c), lambda j, i, c_ref: (j, i, 0)))
    return pl.pallas_call(body, name="pair_sum", grid_spec=grid_spec, out_shape=SDS((n, h, c), wire_dtype),
                          compiler_params=_params(2))(my_c, grads, recv)


def _chip_sum(grads, recv, arrived, my_ck, stacked, layer):
    _, _, h, c = grads.shape
    th = _row_tile(h, c)

    def body(ck_ref, g_ref, r_ref, a1_ref, a2_ref, a3_ref, acc_ref, o_ref):
        acc = g_ref[...] + r_ref[...]
        for ref in (a1_ref, a2_ref, a3_ref):
            acc = acc + ref[...].astype(F32)
        o_ref[...] = acc

    def slot(flip):
        return pl.BlockSpec((None, None, th, c), lambda i, ck: (0, jnp.bitwise_xor(ck[1], flip), i, 0))

    grid_spec = pltpu.PrefetchScalarGridSpec(
        num_scalar_prefetch=1, grid=(h // th,),
        in_specs=[pl.BlockSpec((None, None, th, c), lambda i, ck: (ck[1], ck[0], i, 0)),
                  pl.BlockSpec((None, None, th, c), lambda i, ck: (ck[1], 0, i, 0)),
                  slot(1), slot(2), slot(3), ANY],
        out_specs=pl.BlockSpec((None, None, th, c), lambda i, ck: (layer, ck[0], i, 0)))
    return pl.pallas_call(body, name="chip_sum", grid_spec=grid_spec, out_shape=SDS(stacked.shape, F32),
                          input_output_aliases={6: 0}, compiler_params=_params(1))(
                              my_ck, grads, recv, arrived, arrived, arrived, stacked)


def _adam_math(w, gv, m, v):
    c1 = 1.0 - ADAM_B1 ** ADAM_STEP
    c2 = 1.0 - ADAM_B2 ** ADAM_STEP
    mn = ADAM_B1 * m + (1.0 - ADAM_B1) * gv
    vn = ADAM_B2 * v + (1.0 - ADAM_B2) * (gv * gv)
    m_hat = mn / c1
    v_hat = vn / c2
    return -ADAM_LR * (m_hat / (jnp.sqrt(v_hat) + ADAM_EPS) + ADAM_WD * w), mn, vn


def _adamw(w, g, m, v):
    rows, cols = w.shape
    tr = _row_tile(rows, cols, 512 << 10)

    def body(w_ref, g_ref, m_ref, v_ref, go_ref, d_ref, mo_ref, vo_ref):
        gv = g_ref[...]
        go_ref[...] = gv
        d_ref[...], mo_ref[...], vo_ref[...] = _adam_math(w_ref[...], gv, m_ref[...], v_ref[...])

    spec = pl.BlockSpec((tr, cols), lambda i: (i, 0))
    return pl.pallas_call(
        body, name="adamw", grid=(rows // tr,), in_specs=[spec] * 4, out_specs=[spec] * 4,
        out_shape=[SDS((rows, cols), F32)] * 4, compiler_params=_params(1))(w, g, m, v)


def _place():
    x, y, c = lax.axis_index("x"), lax.axis_index("y"), lax.axis_index("c")
    chips = [(1 - x, y), (x, 1 - y), (1 - x, 1 - y)]
    return x, y, c, 2 * x + y, chips


def _remote(src, dst, send_sem, recv_sem, device):
    return pltpu.make_async_remote_copy(src_ref=src, dst_ref=dst, send_sem=send_sem, recv_sem=recv_sem,
                                        device_id=device, device_id_type=MESH_IDS)


def _gather_comm(items, small=None):
    shards = [arr for arr, _ in items]
    layers = [layer for _, layer in items]
    n = len(shards)
    extra = 0 if small is None else 1

    def copies(ins, outs, sems):
        ici_send, ici_recv, d2d_send, d2d_recv, own_send, own_recv = sems
        x, y, c, k, chips = _place()
        sibling = (x, y, 1 - c)
        own, ici_out, ici_in, fwd_out, fwd_in = [], [], [], [], []
        for j in range(n):
            h = ins[j].shape[2] // 2
            mine, theirs = pl.ds(c * h, h), pl.ds((1 - c) * h, h)
            own.append(_remote(ins[j].at[pl.ds(layers[j], 1)], outs[j].at[:, pl.ds(k, 1)], own_send.at[j], own_recv.at[j], sibling))
            for ti, (cx, cy) in enumerate(chips):
                s = 3 * j + ti
                ici_out.append(_remote(ins[j].at[pl.ds(layers[j], 1), :, mine], outs[j].at[:, pl.ds(k, 1), mine],
                                       ici_send.at[s], ici_recv.at[s], (cx, cy, c)))
                landed = outs[j].at[:, pl.ds(2 * cx + cy, 1), mine]
                ici_in.append(_remote(landed, landed, ici_send.at[s], ici_recv.at[s], (cx, cy, c)))
                fwd_out.append(_remote(landed, landed, d2d_send.at[s], d2d_recv.at[s], sibling))
                passed = outs[j].at[:, pl.ds(2 * cx + cy, 1), theirs]
                fwd_in.append(_remote(passed, passed, d2d_send.at[s], d2d_recv.at[s], sibling))
        if extra:
            own.append(_remote(ins[n], outs[n].at[pl.ds(k, 1)], own_send.at[n], own_recv.at[n], sibling))
            for ti, (cx, cy) in enumerate(chips):
                s = 3 * n + ti
                ici_out.append(_remote(ins[n], outs[n].at[pl.ds(k, 1)], ici_send.at[s], ici_recv.at[s], (cx, cy, c)))
                slot = outs[n].at[pl.ds(2 * cx + cy, 1)]
                ici_in.append(_remote(slot, slot, ici_send.at[s], ici_recv.at[s], (cx, cy, c)))
        return own, ici_out, ici_in, fwd_out, fwd_in

    def start(ins, outs, sems):
        own, ici_out, _, _, _ = copies(ins, outs, sems)
        for cp in own + ici_out:
            cp.start()

    def finish(ins, outs, sems):
        own, ici_out, ici_in, fwd_out, fwd_in = copies(ins, outs, sems)
        for idx, cp in enumerate(ici_in):
            cp.wait_recv()
            if idx < len(fwd_out):
                fwd_out[idx].start()
        for cp in fwd_in:
            cp.wait_recv()
        for cp in ici_out + fwd_out:
            cp.wait_send()
        for cp in own:
            cp.wait()

    operands = list(shards) + ([small] if extra else [])
    out_shape = [SDS((1, N_CHIPS) + s.shape[2:], s.dtype) for s in shards]
    if extra:
        out_shape.append(SDS((N_CHIPS,) + small.shape[1:], small.dtype))
    sems = [pltpu.SemaphoreType.DMA((3 * (n + extra),)), pltpu.SemaphoreType.DMA((3 * (n + extra),)),
            pltpu.SemaphoreType.DMA((3 * n,)), pltpu.SemaphoreType.DMA((3 * n,)),
            pltpu.SemaphoreType.DMA((n + extra,)), pltpu.SemaphoreType.DMA((n + extra,))]
    return _Comm(operands, out_shape, sems, start, finish)


def _swap_comm(grads):
    n = len(grads)

    def copies(ins, outs, sems):
        send_sem, recv_sem = sems
        x, y, c, _, _ = _place()
        return [_remote(ins[j].at[:, pl.ds(1 - c, 1)], outs[j], send_sem.at[j], recv_sem.at[j], (x, y, 1 - c))
                for j in range(n)]

    def start(ins, outs, sems):
        for cp in copies(ins, outs, sems):
            cp.start()

    def finish(ins, outs, sems):
        for cp in copies(ins, outs, sems):
            cp.wait()

    out_shape = [SDS((g.shape[0], 1) + g.shape[2:], g.dtype) for g in grads]
    return _Comm(list(grads), out_shape, [pltpu.SemaphoreType.DMA((n,)), pltpu.SemaphoreType.DMA((n,))], start, finish)


def _scatter_comm(sums):
    n = len(sums)

    def copies(ins, outs, sems):
        send_sem, recv_sem = sems
        x, y, c, k, chips = _place()
        out, landing = [], []
        for j in range(n):
            for ti, (cx, cy) in enumerate(chips):
                s = 3 * j + ti
                out.append(_remote(ins[j].at[:, pl.ds(2 * cx + cy, 1)], outs[j].at[:, pl.ds(k, 1)],
                                   send_sem.at[s], recv_sem.at[s], (cx, cy, c)))
                slot = outs[j].at[:, pl.ds(2 * cx + cy, 1)]
                landing.append(_remote(slot, slot, send_sem.at[s], recv_sem.at[s], (cx, cy, c)))
        return out, landing

    def start(ins, outs, sems):
        for cp in copies(ins, outs, sems)[0]:
            cp.start()

    def finish(ins, outs, sems):
        out, landing = copies(ins, outs, sems)
        for cp in landing:
            cp.wait_recv()
        for cp in out:
            cp.wait_send()

    out_shape = [SDS(s.shape, s.dtype) for s in sums]
    return _Comm(list(sums), out_shape, [pltpu.SemaphoreType.DMA((3 * n,)), pltpu.SemaphoreType.DMA((3 * n,))], start, finish)


def _swap_pieces(pieces):
    n = len(pieces)

    def body(*refs):
        bufs = refs[n:2 * n]
        send_sem, recv_sem = refs[2 * n:]
        x, y, c, _, _ = _place()
        for j in range(n):
            mine = bufs[j].at[:, pl.ds(c, 1)]
            _remote(mine, mine, send_sem.at[j], recv_sem.at[j], (x, y, 1 - c)).start()
        for j in range(n):
            mine, theirs = bufs[j].at[:, pl.ds(c, 1)], bufs[j].at[:, pl.ds(1 - c, 1)]
            _remote(mine, theirs, send_sem.at[j], recv_sem.at[j], (x, y, 1 - c)).wait()

    return pl.pallas_call(
        body, name="swap_pieces", in_specs=[ANY] * n, out_specs=[ANY] * n,
        out_shape=[SDS(p.shape, p.dtype) for p in pieces], input_output_aliases={j: j for j in range(n)},
        scratch_shapes=[pltpu.SemaphoreType.DMA((n,)), pltpu.SemaphoreType.DMA((n,))],
    )(*pieces)


def _gather_pieces(piece):
    def body(in_ref, out_ref, send_sem, recv_sem):
        x, y, c, k, chips = _place()
        peers = [(x, y, 1 - c)] + [(cx, cy, pc) for (cx, cy) in chips for pc in (c, 1 - c)]
        copies = []
        for ti, peer in enumerate(peers):
            cp = _remote(in_ref, out_ref.at[pl.ds(k, 1), pl.ds(c, 1)], send_sem.at[ti], recv_sem.at[ti], peer)
            cp.start()
            copies.append(cp)
        for ti, (px, py, pc) in enumerate(peers):
            _remote(in_ref, out_ref.at[pl.ds(2 * px + py, 1), pl.ds(pc, 1)], send_sem.at[ti], recv_sem.at[ti],
                    (px, py, pc)).wait_recv()
        for cp in copies:
            cp.wait_send()

    n_peers = 2 * N_CHIPS - 1
    return pl.pallas_call(
        body, name="gather_pieces", in_specs=[ANY], out_specs=ANY,
        out_shape=SDS((N_CHIPS, 2) + piece.shape[2:], piece.dtype),
        scratch_shapes=[pltpu.SemaphoreType.DMA((n_peers,)), pltpu.SemaphoreType.DMA((n_peers,))],
    )(piece)


def _pack_rows(parts, width, total_rows=None):
    rows = []
    for a in parts:
        a2 = a.reshape(-1, width)
        pad = (-a2.shape[0]) % SUBLANES
        rows.append(jnp.pad(a2, ((0, pad), (0, 0))) if pad else a2)
    out = jnp.concatenate(rows, axis=0)
    if total_rows is not None and out.shape[0] < total_rows:
        out = jnp.pad(out, ((0, total_rows - out.shape[0]), (0, 0)))
    return out


def _unpack_rows(packed, shapes, width):
    out, r = [], 0
    for shp in shapes:
        size = 1
        for s in shp:
            size *= s
        nr = size // width
        out.append(packed[r:r + nr].reshape(shp))
        r += nr + ((-nr) % SUBLANES)
    return out


def kernel(x, p, norm_g, w_in, w_out, conv_w, conv_b, conv_ln_g, conv_ln_b, sgu_ln_g, sgu_ln_b, sgu_w, sgu_b, pl_norm_g, pl_gate_w, pl_proj_w, final_g, loss_target, m_norm_g, m_w_in, m_w_out, m_conv_w, m_conv_b, m_conv_ln_g, m_conv_ln_b, m_sgu_ln_g, m_sgu_ln_b, m_sgu_w, m_sgu_b, m_pl_norm_g, m_pl_gate_w, m_pl_proj_w, m_final_g, v_norm_g, v_w_in, v_w_out, v_conv_w, v_conv_b, v_conv_ln_g, v_conv_ln_b, v_sgu_ln_g, v_sgu_ln_b, v_sgu_w, v_sgu_b, v_pl_norm_g, v_pl_gate_w, v_pl_proj_w, v_final_g):
    bsz, seq, d = x.shape
    depth = w_in.shape[0]
    e = w_out.shape[1] * N_CHIPS
    e3 = 3 * e
    n4 = w_in.shape[2]
    ple = p.shape[-1]
    dq = pl_proj_w.shape[2]
    k_taps = conv_w.shape[1]
    kp = k_taps + 1
    n_conv, n_sgu = conv_w.shape[0], sgu_ln_g.shape[0]
    t = bsz * seq
    tm_mm = min(512, seq)
    tm_wg, tm_wg_out = min(2048, t), min(1024, t)
    tm_mix = min(256, seq)
    my_c = lax.axis_index("c")
    my_k = 2 * lax.axis_index("x") + lax.axis_index("y")

    ec = e // N_CHIPS
    small_w = _pack_rows([conv_w.reshape(n_conv * k_taps, ec), sgu_ln_g, sgu_ln_b], ec)[None]
    shards = {"in": w_in.astype(BF16)[:, None], "out": w_out.astype(BF16)[:, None],
              "gate": pl_gate_w.astype(BF16)[:, None], "proj": pl_proj_w.astype(BF16)[:, None]}
    rest = ("out", "gate", "proj")
    gathered = {}
    gathered["in", 0], small_f = _run_comm(_gather_comm([(shards["in"], 0)], small_w), "gather_first")
    carried_by_in = {l: [("in", l + 1)] for l in range(depth - 1)}
    carried_by_mix = {l: [(nm, l + 1) for nm in rest] for l in range(depth - 1)}
    carried_by_mix[0] = [(nm, 0) for nm in rest] + carried_by_mix.get(0, [])

    def carried(keys):
        return _gather_comm([(shards[nm], ly) for nm, ly in keys]) if keys else None

    def unpack(res, keys):
        if not keys:
            return res
        gathered.update(zip(keys, res[1]))
        return res[0]

    conv_w_rows, sgu_g_rows, sgu_b_rows = _unpack_rows(
        jnp.transpose(small_f, (1, 0, 2)).reshape(small_f.shape[1], e),
        [(n_conv * k_taps, e), (n_sgu, e), (n_sgu, e)], e)
    conv_w_full = conv_w_rows.reshape(n_conv, k_taps, e)
    conv_w_pad = jnp.pad(conv_w_full, ((0, 0), (0, 1), (0, 0)))
    conv_w_fwd, conv_w_bwd = [
        jnp.transpose(conv_w_pad.reshape(n_conv, kp, e // min(cb, e), min(cb, e)), (0, 2, 1, 3)) for cb in (2 * LANES, LANES)]
    sgu_ln_g_full = sgu_g_rows.reshape(n_sgu, 1, e)
    sgu_ln_b_full = sgu_b_rows.reshape(n_sgu, 1, e)
    sgu_bt = jnp.transpose(sgu_b, (0, 2, 1))

    norm_g3 = norm_g[:, None]
    pl_norm_g3 = pl_norm_g[:, None]
    conv_b3, conv_ln_g3, conv_ln_b3 = conv_b[:, None], conv_ln_g[:, None], conv_ln_b[:, None]
    p3 = p.reshape(depth, t, ple)

    xs, hs, projs, us, x1s, y1s, weights = [], [], [], [], [], {}, []
    xc = x.reshape(t, d)
    for l in range(depth):
        j = l // 2
        xs.append(xc)
        keys = carried_by_in.get(l, [])
        h, proj = unpack(_fwd_in(xc, norm_g3, gathered["in", l], l, tm_mm, carried(keys)), keys)
        keys = carried_by_mix.get(l, [])
        if l % 2 == 0:
            u, y1s[l] = unpack(_fwd_conv(proj, conv_w_fwd, conv_b3, conv_ln_g3, conv_ln_b3, j, bsz, seq, tm_mix,
                                         carried(keys)), keys)
        else:
            (u,) = unpack(_fwd_sgu(proj, sgu_ln_g_full, sgu_ln_b_full, sgu_w, sgu_bt, j, tm_mix, carried(keys)), keys)
        w_out_l, gate_l, proj_l = gathered["out", l].reshape(1, e, d), gathered["gate", l].reshape(1, d, d), gathered["proj", l]
        weights.append((gathered["in", l], w_out_l, gate_l, proj_l))
        x1, xc = _fwd_out(xc, u, w_out_l, pl_norm_g3, gate_l, p3, proj_l, l, tm_mm)
        hs.append(h)
        projs.append(proj)
        us.append(u)
        x1s.append(x1)

    loss_local, dx, d_final_g = _loss_head(xc, final_g[None], loss_target.reshape(t, d), tm_mm)
    loss = lax.psum(loss_local[0, 0], ("x", "y", "c"))

    c_arr = my_c.astype(jnp.int32).reshape(1)
    ck_arr = jnp.stack([my_c, my_k]).astype(jnp.int32)
    by_chip = lambda a: a.reshape((1, N_CHIPS) + a.shape[1:])
    d_norm_g, d_pl_norm_g = [None] * depth, [None] * depth
    d_conv = [None] * n_conv
    d_sgu = [None] * n_sgu
    locals_, siblings, arrived = [None] * depth, [None] * depth, [None] * depth
    pending = []

    def take():
        if not pending:
            return None
        kind, _, payload = pending[0]
        return _swap_comm(payload) if kind == "swap" else _scatter_comm(payload)

    def settle(res, comm):
        if comm is None:
            return res
        outs, brought = res
        kind, layer, payload = pending.pop()
        if kind == "swap":
            locals_[layer], siblings[layer] = payload, brought
            pending.append(("scatter", layer, [by_chip(_pair_sum(gl, rc, c_arr, BF16)) for gl, rc in zip(payload, brought)]))
        else:
            arrived[layer] = brought
        return outs

    def run_alone(name):
        comm = take()
        settle(([], _run_comm(comm, name)), comm)

    for l in reversed(range(depth)):
        j = l // 2
        w_in_l, w_out_l, gate_l, proj_l = weights[l]
        comm = take()
        dx1, du, rn, ds, dqv, d_pl_norm_g[l] = settle(
            _bwd_out(dx, x1s[l], p3, pl_norm_g3, gate_l, proj_l, w_out_l, l, tm_mm, comm), comm)
        (g_proj,) = _wgrad(p3, dqv, 1, 1, N_CHIPS, tm_wg, "wgrad_proj", a_layer=l)
        (g_gate,) = _wgrad(rn, ds, 1, 1, 1, tm_wg, "wgrad_gate")
        (g_out,) = _wgrad(us[l], dx1, 1, 1, 1, tm_wg_out, "wgrad_out")
        comm = take()
        if l % 2 == 0:
            dproj, dcw, dcb, dlg, dlb = settle(
                _bwd_conv(du, projs[l], y1s[l], conv_w_bwd, conv_ln_g3, conv_ln_b3, j, bsz, seq, tm_mix, comm), comm)
            d_conv[j] = (dcw, dcb, dlg, dlb)
        else:
            dproj, dsw, dsbt, dlg, dlb = settle(
                _bwd_sgu(du, projs[l], sgu_ln_g_full, sgu_ln_b_full, sgu_w, sgu_bt, j, tm_mix, comm), comm)
            d_sgu[j] = (dsw, dsbt, dlg, dlb)
        (g_in,) = _wgrad(hs[l], dproj, 1, N_CHIPS, 1, tm_wg, "wgrad_in")
        local = [g_in.reshape(N_CHIPS, 2, d // 2, n4), g_out.reshape(N_CHIPS, 2, e // (2 * N_CHIPS), d),
                 g_gate.reshape(N_CHIPS, 2, d // (2 * N_CHIPS), d), g_proj.reshape(N_CHIPS, 2, ple // 2, dq)]
        while pending:
            run_alone("reduce_step")
        pending.append(("swap", l, local))
        if l == 0:
            run_alone("swap_last")
        comm = take()
        dx, d_norm_g[l] = settle(_bwd_in(dproj, dx1, xs[l], norm_g3, w_in_l, l, tm_mm, comm), comm)
    while pending:
        run_alone("reduce_tail")
    grad_x = dx.reshape(bsz, seq, d)

    d_conv_w = jnp.stack([jnp.transpose(dc[0], (1, 0, 2)).reshape(kp, e)[:k_taps] for dc in d_conv])
    d_conv_b = jnp.stack([dc[1][0] for dc in d_conv])
    d_conv_ln_g = jnp.stack([dc[2][0] for dc in d_conv])
    d_conv_ln_b = jnp.stack([dc[3][0] for dc in d_conv])
    d_sgu_w = jnp.stack([dsg[0] for dsg in d_sgu])
    d_sgu_b = jnp.stack([jnp.transpose(dsg[1]) for dsg in d_sgu])
    d_sgu_ln_g = jnp.stack([dsg[2][0] for dsg in d_sgu])
    d_sgu_ln_b = jnp.stack([dsg[3][0] for dsg in d_sgu])
    small_grads = [jnp.concatenate(d_norm_g), d_conv_w, d_conv_b, d_conv_ln_g, d_conv_ln_b, d_sgu_ln_g, d_sgu_ln_b,
                   d_sgu_w, d_sgu_b, jnp.concatenate(d_pl_norm_g), d_final_g]
    small_shapes = [a.shape for a in small_grads]
    packed = _pack_rows(small_grads, d)
    pack_rows = packed.shape[0] + ((-packed.shape[0]) % (8 * SUBLANES))
    packed = _pack_rows(small_grads, d, pack_rows)
    gl_small = packed.reshape(N_CHIPS, 2, pack_rows // 8, d)
    (small_sibling,) = _run_comm(_swap_comm([gl_small]), "swap_small")
    small_pair = by_chip(_pair_sum(gl_small, small_sibling, c_arr, F32))

    (small_arrived,) = _run_comm(_scatter_comm([small_pair]), "scatter_small")
    reduced = [lax.empty((depth, 2) + gl.shape[2:], F32) for gl in locals_[0]]
    for l in range(depth):
        reduced = [_chip_sum(gl, rc, ar, ck_arr, acc, l)
                   for gl, rc, ar, acc in zip(locals_[l], siblings[l], arrived[l], reduced)]
    small_both = _chip_sum(gl_small, small_sibling, small_arrived, ck_arr, lax.empty((1, 2) + gl_small.shape[2:], F32), 0)
    small_mine = lax.dynamic_slice_in_dim(small_both, my_c, 1, axis=1)
    reduced = _swap_pieces(reduced)
    small_all = _gather_pieces(small_mine)
    small_all = lax.dynamic_update_slice(small_all, small_mine, (my_k, my_c, 0, 0)).reshape(pack_rows, d)
    small_red = _unpack_rows(small_all, small_shapes, d)
    (gr_norm_g, gr_conv_w, gr_conv_b, gr_conv_ln_g, gr_conv_ln_b, gr_sgu_ln_g, gr_sgu_ln_b, gr_sgu_w, gr_sgu_b,
     gr_pl_norm_g, gr_final_g) = small_red
    gr_final_g = gr_final_g.reshape(d)
    gr_conv_w = lax.dynamic_slice_in_dim(gr_conv_w, my_k * ec, ec, axis=2)
    gr_sgu_ln_g = lax.dynamic_slice_in_dim(gr_sgu_ln_g, my_k * ec, ec, axis=1)
    gr_sgu_ln_b = lax.dynamic_slice_in_dim(gr_sgu_ln_b, my_k * ec, ec, axis=1)

    def shard_update(w, g, m, v):
        flat = lambda a: a.reshape(-1, w.shape[-1])
        return [o.reshape(w.shape) for o in _adamw(flat(w), flat(g), flat(m), flat(v))]

    up_in = shard_update(w_in, reduced[0], m_w_in, v_w_in)
    up_out = shard_update(w_out, reduced[1], m_w_out, v_w_out)
    up_gate = shard_update(pl_gate_w, reduced[2], m_pl_gate_w, v_pl_gate_w)
    up_proj = shard_update(pl_proj_w, reduced[3], m_pl_proj_w, v_pl_proj_w)

    small_names = ["norm_g", "conv_w", "conv_b", "conv_ln_g", "conv_ln_b", "sgu_ln_g", "sgu_ln_b", "sgu_w", "sgu_b",
                   "pl_norm_g", "final_g"]
    small_w_list = [norm_g, conv_w, conv_b, conv_ln_g, conv_ln_b, sgu_ln_g, sgu_ln_b, sgu_w, sgu_b, pl_norm_g, final_g]
    small_m_list = [m_norm_g, m_conv_w, m_conv_b, m_conv_ln_g, m_conv_ln_b, m_sgu_ln_g, m_sgu_ln_b, m_sgu_w, m_sgu_b,
                    m_pl_norm_g, m_final_g]
    small_v_list = [v_norm_g, v_conv_w, v_conv_b, v_conv_ln_g, v_conv_ln_b, v_sgu_ln_g, v_sgu_ln_b, v_sgu_w, v_sgu_b,
                    v_pl_norm_g, v_final_g]
    small_g_list = [gr_norm_g, gr_conv_w, gr_conv_b, gr_conv_ln_g, gr_conv_ln_b, gr_sgu_ln_g, gr_sgu_ln_b, gr_sgu_w,
                    gr_sgu_b, gr_pl_norm_g, gr_final_g]
    width = ec
    shapes_local = [a.shape for a in small_w_list]
    outs_small = _adamw(_pack_rows(small_w_list, width), _pack_rows(small_g_list, width),
                        _pack_rows(small_m_list, width), _pack_rows(small_v_list, width))
    unpacked = [_unpack_rows(o, shapes_local, width) for o in outs_small]
    ups = {name: [unpacked[kind][i] for kind in range(4)] for i, name in enumerate(small_names)}
    ups["w_in"], ups["w_out"], ups["pl_gate_w"], ups["pl_proj_w"] = up_in, up_out, up_gate, up_proj

    order = ["norm_g", "w_in", "w_out", "conv_w", "conv_b", "conv_ln_g", "conv_ln_b", "sgu_ln_g", "sgu_ln_b", "sgu_w",
             "sgu_b", "pl_norm_g", "pl_gate_w", "pl_proj_w", "final_g"]
    result = [loss, grad_x]
    for kind in range(4):
        result.extend(ups[name][kind] for name in order)
    return tuple(result)
```

```python
import functools

import jax
import jax.numpy as jnp
from jax import lax
from jax.experimental import pallas as pl
from jax.experimental.pallas import tpu as pltpu

F32 = jnp.float32
BF16 = jnp.bfloat16
SDS = jax.ShapeDtypeStruct

EPS = 1e-6
CHUNK = 128
GROUPS = 8
HALO = 32
N_CHIPS = 4
LANES = 128
SUBLANES = 8
V7X_VMEM_LIMIT = 56 << 20

ADAM_LR = 0.001
ADAM_B1 = 0.9
ADAM_B2 = 0.999
ADAM_EPS = 1e-08
ADAM_WD = 0.01
ADAM_STEP = 10

MESH_IDS = pl.DeviceIdType.MESH
ANY = pl.BlockSpec(memory_space=pl.ANY)


def _params(n_axes):
    return pltpu.CompilerParams(dimension_semantics=("arbitrary",) * n_axes, vmem_limit_bytes=V7X_VMEM_LIMIT)


def _const(shape):
    zeros = (0,) * len(shape)
    return pl.BlockSpec(shape, lambda *_: zeros)


def _layer(shape, layer):
    zeros = (0,) * len(shape)
    return pl.BlockSpec((None,) + tuple(shape), lambda *_: (layer,) + zeros, pipeline_mode=pl.Buffered(1))


class _Comm:
    def __init__(self, operands, out_shape, sems, start, finish, relay=None):
        self.operands, self.out_shape, self.sems, self.start, self.finish = operands, out_shape, sems, start, finish
        self.relay = relay


def _call(body, *, name, grid, in_specs, out_specs, out_shape, operands, scratch_shapes=(), comm=None):
    in_specs, out_specs, out_shape, scratch_shapes = list(in_specs), list(out_specs), list(out_shape), list(scratch_shapes)
    if comm is None:
        return pl.pallas_call(body, name=name, grid=grid, in_specs=in_specs, out_specs=out_specs, out_shape=out_shape,
                              scratch_shapes=scratch_shapes, compiler_params=_params(len(grid)))(*operands)
    n_in, n_out, n_sc = len(in_specs), len(out_specs), len(scratch_shapes)
    ci, co = len(comm.operands), len(comm.out_shape)

    def hosted(*refs):
        ins, cins = refs[:n_in], refs[n_in:n_in + ci]
        outs, couts = refs[n_in + ci:n_in + ci + n_out], refs[n_in + ci + n_out:n_in + ci + n_out + co]
        scratch = refs[n_in + ci + n_out + co:n_in + ci + n_out + co + n_sc]
        sems = refs[n_in + ci + n_out + co + n_sc:]
        first = functools.reduce(jnp.logical_and, [pl.program_id(a) == 0 for a in range(len(grid))])
        last = functools.reduce(jnp.logical_and, [pl.program_id(a) == g - 1 for a, g in enumerate(grid)])

        @pl.when(first)
        def _():
            comm.start(cins, couts, sems)

        if comm.relay is not None:
            @pl.when(last)
            def _():
                comm.relay(cins, couts, sems)

        body(*ins, *outs, *scratch)

        @pl.when(last)
        def _():
            comm.finish(cins, couts, sems)

    res = pl.pallas_call(
        hosted, name=name, grid=grid, in_specs=in_specs + [ANY] * ci, out_specs=out_specs + [ANY] * co,
        out_shape=out_shape + list(comm.out_shape), scratch_shapes=scratch_shapes + list(comm.sems),
        compiler_params=_params(len(grid)))(*operands, *comm.operands)
    return res[:n_out], res[n_out:]


def _run_comm(comm, name):
    ci, co = len(comm.operands), len(comm.out_shape)

    def body(*refs):
        comm.start(refs[:ci], refs[ci:ci + co], refs[ci + co:])
        if comm.relay is not None:
            comm.relay(refs[:ci], refs[ci:ci + co], refs[ci + co:])
        comm.finish(refs[:ci], refs[ci:ci + co], refs[ci + co:])

    return pl.pallas_call(body, name=name, in_specs=[ANY] * ci, out_specs=[ANY] * co, out_shape=list(comm.out_shape),
                          scratch_shapes=list(comm.sems))(*comm.operands)


def _sigmoid(v):
    return jax.nn.sigmoid(v)


def _dsilu(v, s):
    return s * (1.0 + v * (1.0 - s))


def _gelu_parts(v):
    cdf = 0.5 * (1.0 + lax.erf(v * 0.7071067811865476))
    pdf = jnp.exp(-0.5 * v * v) * 0.3989422804014327
    return v * cdf, cdf + v * pdf


def _gelu(v):
    return 0.5 * v * (1.0 + lax.erf(v * 0.7071067811865476))


def _rms_stats(x):
    r = lax.rsqrt(jnp.mean(x * x, axis=-1, keepdims=True) + EPS)
    return r, x * r


def _rms_bwd(dy, g, r, xh):
    gdy = dy * g
    return r * (gdy - xh * jnp.mean(xh * gdy, axis=-1, keepdims=True))


def _ln_stats(x):
    mu = jnp.mean(x, axis=-1, keepdims=True)
    xc = x - mu
    rs = lax.rsqrt(jnp.mean(xc * xc, axis=-1, keepdims=True) + EPS)
    return rs, xc * rs


def _ln_bwd(dy, g, rs, xh):
    dxh = dy * g
    return rs * (dxh - jnp.mean(dxh, axis=-1, keepdims=True) - xh * jnp.mean(dxh * xh, axis=-1, keepdims=True))


def _dot(a, b):
    return jnp.dot(a, b, preferred_element_type=F32)


def _dot_nt(a, b):
    return lax.dot_general(a, b, (((1,), (1,)), ((), ())), preferred_element_type=F32)


def _dot_tn(a, b):
    return lax.dot_general(a, b, (((0,), (0,)), ((), ())), preferred_element_type=F32)


def _fwd_in(x, norm_g, w_in_full, layer, tm, comm=None):
    t, d = x.shape
    _, nk, _, n4 = w_in_full.shape

    def body(x_ref, g_ref, w_ref, h_ref, proj_ref):
        r, xh = _rms_stats(x_ref[...])
        h = (xh * g_ref[...]).astype(BF16)
        h_ref[...] = h
        for k in range(nk):
            proj_ref[:, k * n4:(k + 1) * n4] = _dot(h, w_ref[k]).astype(BF16)

    return _call(
        body, name="fwd_in", grid=(t // tm,),
        in_specs=[pl.BlockSpec((tm, d), lambda i: (i, 0)), _layer((1, d), layer), _layer((nk, d, n4), 0)],
        out_specs=[pl.BlockSpec((tm, d), lambda i: (i, 0)), pl.BlockSpec((tm, nk * n4), lambda i: (i, 0))],
        out_shape=[SDS((t, d), BF16), SDS((t, nk * n4), BF16)],
        operands=(x, norm_g, w_in_full), comm=comm)


def _halo_maps(nt, hb, n_halo_blocks):
    def prev(b, i):
        return (jnp.maximum((b * nt + i) * hb - 1, 0), 0)

    def nxt(b, i):
        return (jnp.minimum((b * nt + i + 1) * hb, n_halo_blocks - 1), 0)

    return prev, nxt


TAP_TILE_VREGS = 16


def _tap_rows(cb, tm):
    return min(TAP_TILE_VREGS * SUBLANES * LANES // cb, tm)


def _conv_taps(src_ref, w_ref, dst_ref, cb_idx, n_rows, rb, first, reverse):
    k_taps = w_ref.shape[1] - 1
    for r0 in range(0, n_rows, rb):
        acc = None
        for res in range(SUBLANES):
            rows = rb + (SUBLANES if res else 0)
            group = None
            for k in range(k_taps):
                off = first + k
                if off % SUBLANES != res:
                    continue
                wk = w_ref[cb_idx, pl.ds((k_taps - 1 - k) if reverse else k, 1), :]
                term = wk * src_ref[cb_idx, pl.ds(r0 + off - res, rows), :]
                group = term if group is None else group + term
            if group is None:
                continue
            part = group[res:res + rb] if res else group
            acc = part if acc is None else acc + part
        dst_ref[cb_idx, pl.ds(r0, rb), :] = acc


def _fwd_conv(proj, conv_w_blk, conv_b, ln_g, ln_b, layer, bsz, seq, tm, comm=None):
    t, e3 = proj.shape
    e = e3 // 3
    nt = seq // tm
    hb = tm // HALO
    _, ncb, kp, cb = conv_w_blk.shape
    rb = _tap_rows(cb, tm)
    prev, _ = _halo_maps(nt, hb, t // HALO)

    def body(proj_ref, halo_ref, w_ref, b_ref, g_ref, lb_ref, u_ref, y1_ref, y0s, y1s):
        i = pl.program_id(1)
        a = proj_ref[:, 0:e].astype(F32)
        b = proj_ref[:, e:2 * e].astype(F32)
        y0 = a * _sigmoid(b)
        ah = halo_ref[:, 0:e].astype(F32)
        bh = halo_ref[:, e:2 * e].astype(F32)
        y0h = jnp.where(i > 0, ah * _sigmoid(bh), 0.0)
        for c in range(ncb):
            y0s[c, 0:HALO, :] = y0h[:, c * cb:(c + 1) * cb]
            y0s[c, HALO:HALO + tm, :] = y0[:, c * cb:(c + 1) * cb]

        def per_block(c, carry):
            _conv_taps(y0s, w_ref, y1s, c, tm, rb, HALO - (kp - 2), False)
            return carry

        lax.fori_loop(0, ncb, per_block, 0)
        y1 = jnp.concatenate([y1s[c] for c in range(ncb)], axis=1) + b_ref[...]
        y1_ref[...] = y1
        rs, xh = _ln_stats(y1)
        y2 = xh * g_ref[...] + lb_ref[...]
        y = y2 * _sigmoid(y2)
        z = proj_ref[:, 2 * e:3 * e].astype(F32)
        u_ref[...] = (y * (z * _sigmoid(z))).astype(BF16)

    return _call(
        body, name="fwd_conv", grid=(bsz, nt),
        in_specs=[pl.BlockSpec((tm, e3), lambda b, i: (b * nt + i, 0)),
                  pl.BlockSpec((HALO, 2 * e), prev),
                  _layer((ncb, kp, cb), layer), _layer((1, e), layer), _layer((1, e), layer), _layer((1, e), layer)],
        out_specs=[pl.BlockSpec((tm, e), lambda b, i: (b * nt + i, 0)), pl.BlockSpec((tm, e), lambda b, i: (b * nt + i, 0))],
        out_shape=[SDS((t, e), BF16), SDS((t, e), F32)],
        scratch_shapes=[pltpu.VMEM((ncb, HALO + tm, cb), F32), pltpu.VMEM((ncb, tm, cb), F32)],
        operands=(proj, proj, conv_w_blk, conv_b, ln_g, ln_b), comm=comm)


def _tril_mask():
    rows = lax.broadcasted_iota(jnp.int32, (CHUNK, CHUNK), 0)
    cols = lax.broadcasted_iota(jnp.int32, (CHUNK, CHUNK), 1)
    return rows >= cols


def _fwd_sgu(proj, ln_g, ln_b, sgu_w, sgu_bt, layer, tm, comm=None):
    t, e3 = proj.shape
    e = e3 // 3
    gw = e // GROUPS
    nch = tm // CHUNK

    def body(proj_ref, g_ref, lb_ref, w_ref, bt_ref, u_ref, mixed):
        a = proj_ref[:, 0:e].astype(F32)
        b = proj_ref[:, e:2 * e].astype(F32)
        z = proj_ref[:, 2 * e:3 * e].astype(F32)
        rs, xh = _ln_stats(_gelu(b))
        v = (xh * g_ref[...] + lb_ref[...]).astype(BF16)
        mask = _tril_mask()
        for g in range(GROUPS):
            wm = jnp.where(mask, w_ref[g], 0.0).astype(BF16)
            bias = bt_ref[:, g:g + 1]
            for n in range(nch):
                blk = v[n * CHUNK:(n + 1) * CHUNK, g * gw:(g + 1) * gw]
                mixed[n * CHUNK:(n + 1) * CHUNK, g * gw:(g + 1) * gw] = _dot(wm, blk) + bias
        y = _gelu(a) * mixed[...]
        u_ref[...] = (y * (z * _sigmoid(z))).astype(BF16)

    return _call(
        body, name="fwd_sgu", grid=(t // tm,),
        in_specs=[pl.BlockSpec((tm, e3), lambda i: (i, 0)), _layer((1, e), layer), _layer((1, e), layer),
                  _layer((GROUPS, CHUNK, CHUNK), layer), _layer((CHUNK, GROUPS), layer)],
        out_specs=[pl.BlockSpec((tm, e), lambda i: (i, 0))],
        out_shape=[SDS((t, e), BF16)],
        scratch_shapes=[pltpu.VMEM((tm, e), F32)],
        operands=(proj, ln_g, ln_b, sgu_w, sgu_bt), comm=comm)


def _ple_forward(x1, p_ref, plg_ref, gw_ref, pw_ref):
    nk, _, dq = pw_ref.shape
    r, xh = _rms_stats(x1)
    rn = (xh * plg_ref[...]).astype(BF16)
    gate = _sigmoid(_dot(rn, gw_ref[...]))
    pb = p_ref[...].astype(BF16)
    q = jnp.concatenate([_dot(pb, pw_ref[k]) for k in range(nk)], axis=1)
    return r, xh, rn, gate, q


def _fwd_out(x, u, w_out_full, pl_norm_g, gate_w_full, p, proj_w_full, layer, tm, comm=None):
    t, d = x.shape
    e = u.shape[1]
    ple = p.shape[-1]
    nk, dq = proj_w_full.shape[1], proj_w_full.shape[3]

    def body(x_ref, u_ref, wo_ref, plg_ref, gw_ref, p_ref, pw_ref, x1_ref, x2_ref):
        x1 = x_ref[...] + _dot(u_ref[...], wo_ref[...])
        x1_ref[...] = x1
        _, _, _, gate, q = _ple_forward(x1, p_ref, plg_ref, gw_ref, pw_ref)
        x2_ref[...] = x1 + gate * q

    return _call(
        body, name="fwd_out", grid=(t // tm,),
        in_specs=[pl.BlockSpec((tm, d), lambda i: (i, 0)), pl.BlockSpec((tm, e), lambda i: (i, 0)),
                  _layer((e, d), 0), _layer((1, d), layer), _layer((d, d), 0),
                  pl.BlockSpec((None, tm, ple), lambda i: (layer, i, 0)), _layer((nk, ple, dq), 0)],
        out_specs=[pl.BlockSpec((tm, d), lambda i: (i, 0)), pl.BlockSpec((tm, d), lambda i: (i, 0))],
        out_shape=[SDS((t, d), F32), SDS((t, d), F32)],
        operands=(x, u, w_out_full, pl_norm_g, gate_w_full, p, proj_w_full), comm=comm)


def _loss_head(x, final_g, target, tm):
    t, d = x.shape
    n_steps = t // tm

    def body(x_ref, g_ref, tgt_ref, loss_ref, dx_ref, dg_ref, sq_acc):
        i = pl.program_id(0)

        @pl.when(i == 0)
        def _():
            sq_acc[...] = jnp.zeros_like(sq_acc)
            dg_ref[...] = jnp.zeros_like(dg_ref)

        g = g_ref[...]
        r, xh = _rms_stats(x_ref[...])
        diff = xh * g - tgt_ref[...]
        sq_acc[...] += jnp.sum(diff * diff, axis=0, keepdims=True)
        dout = diff * (1.0 / d)
        dg_ref[...] += jnp.sum(dout * xh, axis=0, keepdims=True)
        dx_ref[...] = _rms_bwd(dout, g, r, xh)

        @pl.when(i == n_steps - 1)
        def _():
            loss_ref[...] = jnp.sum(sq_acc[...], axis=1, keepdims=True) * (0.5 / d)

    return pl.pallas_call(
        body, name="loss_head", grid=(n_steps,),
        in_specs=[pl.BlockSpec((tm, d), lambda i: (i, 0)), _const((1, d)), pl.BlockSpec((tm, d), lambda i: (i, 0))],
        out_specs=[_const((1, 1)), pl.BlockSpec((tm, d), lambda i: (i, 0)), _const((1, d))],
        out_shape=[SDS((1, 1), F32), SDS((t, d), F32), SDS((1, d), F32)],
        scratch_shapes=[pltpu.VMEM((1, d), F32)],
        compiler_params=_params(1),
    )(x, final_g, target)


def _bwd_out(dx2, x1, p, pl_norm_g, gate_w_full, proj_w_full, w_out_full, layer, tm, comm=None):
    t, d = dx2.shape
    e = w_out_full.shape[1]
    ple = p.shape[-1]
    nk, dq_w = proj_w_full.shape[1], proj_w_full.shape[3]

    def body(dx2_ref, x1_ref, p_ref, plg_ref, gw_ref, pw_ref, wo_ref, dx1_ref, du_ref, rn_ref, ds_ref, dq_ref, dplg_ref):
        @pl.when(pl.program_id(0) == 0)
        def _():
            dplg_ref[...] = jnp.zeros_like(dplg_ref)

        dx2v = dx2_ref[...]
        r, xh, rn, gate, q = _ple_forward(x1_ref[...], p_ref, plg_ref, gw_ref, pw_ref)
        rn_ref[...] = rn
        dq_ref[...] = (dx2v * gate).astype(BF16)
        ds = (dx2v * q * gate * (1.0 - gate)).astype(BF16)
        ds_ref[...] = ds
        dr = _dot_nt(ds, gw_ref[...])
        dplg_ref[...] += jnp.sum(dr * xh, axis=0, keepdims=True)
        dx1 = dx2v + _rms_bwd(dr, plg_ref[...], r, xh)
        dx1_ref[...] = dx1
        du_ref[...] = _dot_nt(dx1.astype(BF16), wo_ref[...]).astype(BF16)

    row = lambda w: pl.BlockSpec((tm, w), lambda i: (i, 0))
    return _call(
        body, name="bwd_out", grid=(t // tm,),
        in_specs=[row(d), row(d), pl.BlockSpec((None, tm, ple), lambda i: (layer, i, 0)),
                  _layer((1, d), layer), _layer((d, d), 0), _layer((nk, ple, dq_w), 0), _layer((e, d), 0)],
        out_specs=[row(d), row(e), row(d), row(d), row(d), _const((1, d))],
        out_shape=[SDS((t, d), F32), SDS((t, e), BF16), SDS((t, d), BF16), SDS((t, d), BF16), SDS((t, d), BF16),
                   SDS((1, d), F32)],
        operands=(dx2, x1, p, pl_norm_g, gate_w_full, proj_w_full, w_out_full), comm=comm)


def _bwd_conv(du, proj, y1, conv_w_blk, ln_g, ln_b, layer, bsz, seq, tm, comm=None):
    t, e3 = proj.shape
    e = e3 // 3
    nt = seq // tm
    hb = tm // HALO
    _, ncb, kp, cb = conv_w_blk.shape
    rb = _tap_rows(cb, tm)
    k_taps = kp - 1
    prev, nxt = _halo_maps(nt, hb, t // HALO)
    z_halo = lambda b, i: (nxt(b, i)[0], 2)

    def ln_silu_bwd(du_v, z_v, y1_v, g, lb):
        rs, xh = _ln_stats(y1_v)
        y2 = xh * g + lb
        sg = _sigmoid(y2)
        sz = _sigmoid(z_v)
        dy = du_v * (z_v * sz)
        dy2 = dy * _dsilu(y2, sg)
        return _ln_bwd(dy2, g, rs, xh), dy2, xh, du_v * (y2 * sg) * _dsilu(z_v, sz)

    def body(du_ref, proj_ref, y1_ref, duh_ref, zh_ref, y1h_ref, abh_ref, w_ref, g_ref, lb_ref,
             dproj_ref, dw_ref, dcb_ref, dg_ref, dlb_ref, y0s, dy1s, dy0s, ysh):
        b_id, i = pl.program_id(0), pl.program_id(1)

        @pl.when((b_id == 0) & (i == 0))
        def _():
            dw_ref[...] = jnp.zeros_like(dw_ref)
            dcb_ref[...] = jnp.zeros_like(dcb_ref)
            dg_ref[...] = jnp.zeros_like(dg_ref)
            dlb_ref[...] = jnp.zeros_like(dlb_ref)

        g, lb = g_ref[...], lb_ref[...]
        a = proj_ref[:, 0:e].astype(F32)
        b = proj_ref[:, e:2 * e].astype(F32)
        z = proj_ref[:, 2 * e:3 * e].astype(F32)
        sb = _sigmoid(b)
        y0 = a * sb
        dy1, dy2, xh, dz = ln_silu_bwd(du_ref[...].astype(F32), z, y1_ref[...], g, lb)
        dproj_ref[:, 2 * e:3 * e] = dz.astype(BF16)
        dg_ref[...] += jnp.sum(dy2 * xh, axis=0, keepdims=True)
        dlb_ref[...] += jnp.sum(dy2, axis=0, keepdims=True)
        dcb_ref[...] += jnp.sum(dy1, axis=0, keepdims=True)
        dy1h, _, _, _ = ln_silu_bwd(duh_ref[...].astype(F32), zh_ref[...].astype(F32), y1h_ref[...], g, lb)
        dy1h = jnp.where(i < nt - 1, dy1h, 0.0)
        ah = abh_ref[:, 0:e].astype(F32)
        bh = abh_ref[:, e:2 * e].astype(F32)
        y0h = jnp.where(i > 0, ah * _sigmoid(bh), 0.0)
        for c in range(ncb):
            cols = slice(c * cb, (c + 1) * cb)
            y0s[c, 0:HALO, :] = y0h[:, cols]
            y0s[c, HALO:HALO + tm, :] = y0[:, cols]
            dy1s[c, 0:tm, :] = dy1[:, cols]
            dy1s[c, tm:tm + HALO, :] = dy1h[:, cols]

        def per_block(c, carry):
            _conv_taps(dy1s, w_ref, dy0s, c, tm, rb, 0, True)
            for res in range(1, SUBLANES):
                ysh[res - 1] = y0s[c, pl.ds(res, tm + HALO - SUBLANES), :]
            for k in range(k_taps):
                off = HALO - (k_taps - 1) + k
                res = off % SUBLANES
                acc = None
                for r0 in range(0, tm, rb):
                    rows = pl.ds(r0 + off - res, rb)
                    shifted = ysh[res - 1, rows, :] if res else y0s[c, rows, :]
                    term = dy1s[c, pl.ds(r0, rb), :] * shifted
                    acc = term if acc is None else acc + term
                dw_ref[c, pl.ds(k, 1), :] += jnp.sum(acc, axis=0, keepdims=True)
            return carry

        lax.fori_loop(0, ncb, per_block, 0)
        dy0 = jnp.concatenate([dy0s[c] for c in range(ncb)], axis=1)
        dproj_ref[:, 0:e] = (dy0 * sb).astype(BF16)
        dproj_ref[:, e:2 * e] = (dy0 * a * sb * (1.0 - sb)).astype(BF16)

    tile = lambda w: pl.BlockSpec((tm, w), lambda b, i: (b * nt + i, 0))
    return _call(
        body, name="bwd_conv", grid=(bsz, nt),
        in_specs=[tile(e), tile(e3), tile(e),
                  pl.BlockSpec((HALO, e), nxt), pl.BlockSpec((HALO, e), z_halo), pl.BlockSpec((HALO, e), nxt),
                  pl.BlockSpec((HALO, 2 * e), prev),
                  _layer((ncb, kp, cb), layer), _layer((1, e), layer), _layer((1, e), layer)],
        out_specs=[tile(e3), _const((ncb, kp, cb)), _const((1, e)), _const((1, e)), _const((1, e))],
        out_shape=[SDS((t, e3), BF16), SDS((ncb, kp, cb), F32), SDS((1, e), F32), SDS((1, e), F32), SDS((1, e), F32)],
        scratch_shapes=[pltpu.VMEM((ncb, HALO + tm, cb), F32), pltpu.VMEM((ncb, tm + HALO, cb), F32),
                        pltpu.VMEM((ncb, tm, cb), F32), pltpu.VMEM((SUBLANES - 1, tm + HALO - SUBLANES, cb), F32)],
        operands=(du, proj, y1, du, proj, y1, proj, conv_w_blk, ln_g, ln_b), comm=comm)


def _bwd_sgu(du, proj, ln_g, ln_b, sgu_w, sgu_bt, layer, tm, comm=None):
    t, e3 = proj.shape
    e = e3 // 3
    gw = e // GROUPS
    nch = tm // CHUNK

    def body(du_ref, proj_ref, g_ref, lb_ref, w_ref, bt_ref, dproj_ref, dw_ref, dbt_ref, dg_ref, dlb_ref, mixed, dmix, dv):
        @pl.when(pl.program_id(0) == 0)
        def _():
            dw_ref[...] = jnp.zeros_like(dw_ref)
            dbt_ref[...] = jnp.zeros_like(dbt_ref)
            dg_ref[...] = jnp.zeros_like(dg_ref)
            dlb_ref[...] = jnp.zeros_like(dlb_ref)

        g, lb = g_ref[...], lb_ref[...]
        a = proj_ref[:, 0:e].astype(F32)
        b = proj_ref[:, e:2 * e].astype(F32)
        z = proj_ref[:, 2 * e:3 * e].astype(F32)
        ug, dug = _gelu_parts(a)
        vb, dvb_db = _gelu_parts(b)
        rs, xh = _ln_stats(vb)
        v = (xh * g + lb).astype(BF16)
        mask = _tril_mask()
        for gi in range(GROUPS):
            wm = jnp.where(mask, w_ref[gi], 0.0).astype(BF16)
            bias = bt_ref[:, gi:gi + 1]
            for c in range(nch):
                blk = v[c * CHUNK:(c + 1) * CHUNK, gi * gw:(gi + 1) * gw]
                mixed[c * CHUNK:(c + 1) * CHUNK, gi * gw:(gi + 1) * gw] = _dot(wm, blk) + bias
        mx = mixed[...]
        sz = _sigmoid(z)
        duv = du_ref[...].astype(F32)
        dy = duv * (z * sz)
        dproj_ref[:, 2 * e:3 * e] = (duv * (ug * mx) * _dsilu(z, sz)).astype(BF16)
        dproj_ref[:, 0:e] = (dy * mx * dug).astype(BF16)
        dmix[...] = dy * ug
        for gi in range(GROUPS):
            wm = jnp.where(mask, w_ref[gi], 0.0).astype(BF16)
            dw_acc = None
            db_acc = None
            for c in range(nch):
                rows, cols = slice(c * CHUNK, (c + 1) * CHUNK), slice(gi * gw, (gi + 1) * gw)
                dm = dmix[rows, cols]
                dmb = dm.astype(BF16)
                dw_n = _dot_nt(dmb, v[rows, cols])
                db_n = jnp.sum(dm, axis=1, keepdims=True)
                dw_acc = dw_n if dw_acc is None else dw_acc + dw_n
                db_acc = db_n if db_acc is None else db_acc + db_n
                dv[rows, cols] = _dot_tn(wm, dmb)
            dw_ref[gi] += jnp.where(mask, dw_acc, 0.0)
            dbt_ref[:, gi:gi + 1] += db_acc
        dvv = dv[...]
        dg_ref[...] += jnp.sum(dvv * xh, axis=0, keepdims=True)
        dlb_ref[...] += jnp.sum(dvv, axis=0, keepdims=True)
        dproj_ref[:, e:2 * e] = (_ln_bwd(dvv, g, rs, xh) * dvb_db).astype(BF16)

    return _call(
        body, name="bwd_sgu", grid=(t // tm,),
        in_specs=[pl.BlockSpec((tm, e), lambda i: (i, 0)), pl.BlockSpec((tm, e3), lambda i: (i, 0)),
                  _layer((1, e), layer), _layer((1, e), layer),
                  _layer((GROUPS, CHUNK, CHUNK), layer), _layer((CHUNK, GROUPS), layer)],
        out_specs=[pl.BlockSpec((tm, e3), lambda i: (i, 0)), _const((GROUPS, CHUNK, CHUNK)), _const((CHUNK, GROUPS)),
                   _const((1, e)), _const((1, e))],
        out_shape=[SDS((t, e3), BF16), SDS((GROUPS, CHUNK, CHUNK), F32), SDS((CHUNK, GROUPS), F32),
                   SDS((1, e), F32), SDS((1, e), F32)],
        scratch_shapes=[pltpu.VMEM((tm, e), F32), pltpu.VMEM((tm, e), F32), pltpu.VMEM((tm, e), F32)],
        operands=(du, proj, ln_g, ln_b, sgu_w, sgu_bt), comm=comm)


def _bwd_in(dproj, dx1, x, norm_g, w_in_full, layer, tm, comm=None):
    t, d = x.shape
    _, nk, _, n4 = w_in_full.shape

    def body(dproj_ref, dx1_ref, x_ref, g_ref, w_ref, dx_ref, dg_ref):
        @pl.when(pl.program_id(0) == 0)
        def _():
            dg_ref[...] = jnp.zeros_like(dg_ref)

        dh = None
        for k in range(nk):
            part = _dot_nt(dproj_ref[:, k * n4:(k + 1) * n4], w_ref[k])
            dh = part if dh is None else dh + part
        r, xh = _rms_stats(x_ref[...])
        dg_ref[...] += jnp.sum(dh * xh, axis=0, keepdims=True)
        dx_ref[...] = dx1_ref[...] + _rms_bwd(dh, g_ref[...], r, xh)

    row = lambda w: pl.BlockSpec((tm, w), lambda i: (i, 0))
    return _call(
        body, name="bwd_in", grid=(t // tm,),
        in_specs=[row(nk * n4), row(d), row(d), _layer((1, d), layer), _layer((nk, d, n4), 0)],
        out_specs=[row(d), _const((1, d))],
        out_shape=[SDS((t, d), F32), SDS((1, d), F32)],
        operands=(dproj, dx1, x, norm_g, w_in_full), comm=comm)


def _wgrad(a, b, kblk, nblk, n_split, tm, name, a_layer=None, comm=None):
    t, n = b.shape
    k = a.shape[-1]
    kw, nw = k // kblk, n // nblk
    nws = nw // n_split
    n_steps = t // tm

    def body(a_ref, b_ref, o_ref):
        @pl.when(pl.program_id(2) == 0)
        def _():
            o_ref[...] = jnp.zeros_like(o_ref)

        res = _dot_tn(a_ref[...].astype(BF16), b_ref[...].astype(BF16))
        for s in range(n_split):
            o_ref[s] += res[:, s * nws:(s + 1) * nws]

    if a_layer is None:
        a_spec = pl.BlockSpec((tm, kw), lambda kb, nb, i: (i, kb))
    else:
        a_spec = pl.BlockSpec((None, tm, kw), lambda kb, nb, i: (a_layer, i, kb))
    return _call(
        body, name=name, grid=(kblk, nblk, n_steps),
        in_specs=[a_spec, pl.BlockSpec((tm, nw), lambda kb, nb, i: (i, nb))],
        out_specs=[pl.BlockSpec((None, n_split, kw, nws), lambda kb, nb, i: (kb, nb, 0, 0))],
        out_shape=[SDS((kblk, nblk * n_split, kw, nws), F32)],
        operands=(a, b), comm=comm)


def _row_tile(rows, cols, budget_bytes=2 << 20):
    best = None
    for cand in range(SUBLANES, rows + 1, SUBLANES):
        if rows % cand == 0 and cand * cols * 4 <= budget_bytes:
            best = cand
    return best if best is not None else rows


def _pair_sum(grads, recv, my_c, wire_dtype):
    n, _, h, c = grads.shape
    th = _row_tile(h, c)

    def body(c_ref, g_ref, r_ref, o_ref):
        o_ref[...] = (g_ref[...] + r_ref[...]).astype(wire_dtype)

    grid_spec = pltpu.PrefetchScalarGridSpec(
        num_scalar_prefetch=1, grid=(n, h // th),
        in_specs=[pl.BlockSpec((None, None, th, c), lambda j, i, c_ref: (j, c_ref[0], i, 0)),
                  pl.BlockSpec((None, None, th, c), lambda j, i, c_ref: (j, 0, i, 0))],
        out_specs=pl.BlockSpec((None, th, c), lambda j, i, c_ref: (j, i, 0)))
    return pl.pallas_call(body, name="pair_sum", grid_spec=grid_spec, out_shape=SDS((n, h, c), wire_dtype),
                          compiler_params=_params(2))(my_c, grads, recv)


def _chip_sum(grads, recv, arrived, my_ck, stacked, layer):
    _, _, h, c = grads.shape
    th = _row_tile(h, c)

    def body(ck_ref, g_ref, r_ref, a1_ref, a2_ref, a3_ref, acc_ref, o_ref):
        acc = g_ref[...] + r_ref[...]
        for ref in (a1_ref, a2_ref, a3_ref):
            acc = acc + ref[...].astype(F32)
        o_ref[...] = acc

    def slot(flip):
        return pl.BlockSpec((None, None, th, c), lambda i, ck: (0, jnp.bitwise_xor(ck[1], flip), i, 0))

    grid_spec = pltpu.PrefetchScalarGridSpec(
        num_scalar_prefetch=1, grid=(h // th,),
        in_specs=[pl.BlockSpec((None, None, th, c), lambda i, ck: (ck[1], ck[0], i, 0)),
                  pl.BlockSpec((None, None, th, c), lambda i, ck: (ck[1], 0, i, 0)),
                  slot(1), slot(2), slot(3), ANY],
        out_specs=pl.BlockSpec((None, None, th, c), lambda i, ck: (layer, ck[0], i, 0)))
    return pl.pallas_call(body, name="chip_sum", grid_spec=grid_spec, out_shape=SDS(stacked.shape, F32),
                          input_output_aliases={6: 0}, compiler_params=_params(1))(
                              my_ck, grads, recv, arrived, arrived, arrived, stacked)


def _adam_math(w, gv, m, v):
    c1 = 1.0 - ADAM_B1 ** ADAM_STEP
    c2 = 1.0 - ADAM_B2 ** ADAM_STEP
    mn = ADAM_B1 * m + (1.0 - ADAM_B1) * gv
    vn = ADAM_B2 * v + (1.0 - ADAM_B2) * (gv * gv)
    m_hat = mn / c1
    v_hat = vn / c2
    return -ADAM_LR * (m_hat / (jnp.sqrt(v_hat) + ADAM_EPS) + ADAM_WD * w), mn, vn


def _adamw(w, g, m, v):
    rows, cols = w.shape
    tr = _row_tile(rows, cols)

    def body(w_ref, g_ref, m_ref, v_ref, go_ref, d_ref, mo_ref, vo_ref):
        gv = g_ref[...]
        go_ref[...] = gv
        d_ref[...], mo_ref[...], vo_ref[...] = _adam_math(w_ref[...], gv, m_ref[...], v_ref[...])

    spec = pl.BlockSpec((tr, cols), lambda i: (i, 0))
    return pl.pallas_call(
        body, name="adamw", grid=(rows // tr,), in_specs=[spec] * 4, out_specs=[spec] * 4,
        out_shape=[SDS((rows, cols), F32)] * 4, compiler_params=_params(1))(w, g, m, v)


def _place():
    x, y, c = lax.axis_index("x"), lax.axis_index("y"), lax.axis_index("c")
    chips = [(1 - x, y), (x, 1 - y), (1 - x, 1 - y)]
    return x, y, c, 2 * x + y, chips


def _remote(src, dst, send_sem, recv_sem, device):
    return pltpu.make_async_remote_copy(src_ref=src, dst_ref=dst, send_sem=send_sem, recv_sem=recv_sem,
                                        device_id=device, device_id_type=MESH_IDS)


def _gather_comm(items, small=None):
    shards = [arr for arr, _ in items]
    layers = [layer for _, layer in items]
    n = len(shards)
    extra = 0 if small is None else 1

    def copies(ins, outs, sems):
        ici_send, ici_recv, d2d_send, d2d_recv, own_send, own_recv = sems
        x, y, c, k, chips = _place()
        sibling = (x, y, 1 - c)
        own, ici_out, ici_in, fwd_out, fwd_in = [], [], [], [], []
        for j in range(n):
            h = ins[j].shape[2] // 2
            mine, theirs = pl.ds(c * h, h), pl.ds((1 - c) * h, h)
            own.append(_remote(ins[j].at[pl.ds(layers[j], 1)], outs[j].at[:, pl.ds(k, 1)], own_send.at[j], own_recv.at[j], sibling))
            for ti, (cx, cy) in enumerate(chips):
                s = 3 * j + ti
                ici_out.append(_remote(ins[j].at[pl.ds(layers[j], 1), :, mine], outs[j].at[:, pl.ds(k, 1), mine],
                                       ici_send.at[s], ici_recv.at[s], (cx, cy, c)))
                landed = outs[j].at[:, pl.ds(2 * cx + cy, 1), mine]
                ici_in.append(_remote(landed, landed, ici_send.at[s], ici_recv.at[s], (cx, cy, c)))
                fwd_out.append(_remote(landed, landed, d2d_send.at[s], d2d_recv.at[s], sibling))
                passed = outs[j].at[:, pl.ds(2 * cx + cy, 1), theirs]
                fwd_in.append(_remote(passed, passed, d2d_send.at[s], d2d_recv.at[s], sibling))
        if extra:
            own.append(_remote(ins[n], outs[n].at[pl.ds(k, 1)], own_send.at[n], own_recv.at[n], sibling))
            for ti, (cx, cy) in enumerate(chips):
                s = 3 * n + ti
                ici_out.append(_remote(ins[n], outs[n].at[pl.ds(k, 1)], ici_send.at[s], ici_recv.at[s], (cx, cy, c)))
                slot = outs[n].at[pl.ds(2 * cx + cy, 1)]
                ici_in.append(_remote(slot, slot, ici_send.at[s], ici_recv.at[s], (cx, cy, c)))
        return own, ici_out, ici_in, fwd_out, fwd_in

    def start(ins, outs, sems):
        own, ici_out, _, _, _ = copies(ins, outs, sems)
        for cp in own + ici_out:
            cp.start()

    def relay(ins, outs, sems):
        _, _, ici_in, fwd_out, _ = copies(ins, outs, sems)
        for idx, cp in enumerate(ici_in):
            cp.wait_recv()
            if idx < len(fwd_out):
                fwd_out[idx].start()

    def finish(ins, outs, sems):
        own, ici_out, _, fwd_out, fwd_in = copies(ins, outs, sems)
        for cp in fwd_in:
            cp.wait_recv()
        for cp in ici_out + fwd_out:
            cp.wait_send()
        for cp in own:
            cp.wait()

    operands = list(shards) + ([small] if extra else [])
    out_shape = [SDS((1, N_CHIPS) + s.shape[2:], s.dtype) for s in shards]
    if extra:
        out_shape.append(SDS((N_CHIPS,) + small.shape[1:], small.dtype))
    sems = [pltpu.SemaphoreType.DMA((3 * (n + extra),)), pltpu.SemaphoreType.DMA((3 * (n + extra),)),
            pltpu.SemaphoreType.DMA((3 * n,)), pltpu.SemaphoreType.DMA((3 * n,)),
            pltpu.SemaphoreType.DMA((n + extra,)), pltpu.SemaphoreType.DMA((n + extra,))]
    return _Comm(operands, out_shape, sems, start, finish, relay)


def _swap_comm(grads):
    n = len(grads)

    def copies(ins, outs, sems):
        send_sem, recv_sem = sems
        x, y, c, _, _ = _place()
        return [_remote(ins[j].at[:, pl.ds(1 - c, 1)], outs[j], send_sem.at[j], recv_sem.at[j], (x, y, 1 - c))
                for j in range(n)]

    def start(ins, outs, sems):
        for cp in copies(ins, outs, sems):
            cp.start()

    def finish(ins, outs, sems):
        for cp in copies(ins, outs, sems):
            cp.wait()

    out_shape = [SDS((g.shape[0], 1) + g.shape[2:], g.dtype) for g in grads]
    return _Comm(list(grads), out_shape, [pltpu.SemaphoreType.DMA((n,)), pltpu.SemaphoreType.DMA((n,))], start, finish)


def _scatter_comm(sums):
    n = len(sums)

    def copies(ins, outs, sems):
        send_sem, recv_sem = sems
        x, y, c, k, chips = _place()
        out, landing = [], []
        for j in range(n):
            for ti, (cx, cy) in enumerate(chips):
                s = 3 * j + ti
                out.append(_remote(ins[j].at[:, pl.ds(2 * cx + cy, 1)], outs[j].at[:, pl.ds(k, 1)],
                                   send_sem.at[s], recv_sem.at[s], (cx, cy, c)))
                slot = outs[j].at[:, pl.ds(2 * cx + cy, 1)]
                landing.append(_remote(slot, slot, send_sem.at[s], recv_sem.at[s], (cx, cy, c)))
        return out, landing

    def start(ins, outs, sems):
        for cp in copies(ins, outs, sems)[0]:
            cp.start()

    def finish(ins, outs, sems):
        out, landing = copies(ins, outs, sems)
        for cp in landing:
            cp.wait_recv()
        for cp in out:
            cp.wait_send()

    out_shape = [SDS(s.shape, s.dtype) for s in sums]
    return _Comm(list(sums), out_shape, [pltpu.SemaphoreType.DMA((3 * n,)), pltpu.SemaphoreType.DMA((3 * n,))], start, finish)


def _swap_pieces(pieces):
    n = len(pieces)

    def body(*refs):
        bufs = refs[n:2 * n]
        send_sem, recv_sem = refs[2 * n:]
        x, y, c, _, _ = _place()
        for j in range(n):
            mine = bufs[j].at[:, pl.ds(c, 1)]
            _remote(mine, mine, send_sem.at[j], recv_sem.at[j], (x, y, 1 - c)).start()
        for j in range(n):
            mine, theirs = bufs[j].at[:, pl.ds(c, 1)], bufs[j].at[:, pl.ds(1 - c, 1)]
            _remote(mine, theirs, send_sem.at[j], recv_sem.at[j], (x, y, 1 - c)).wait()

    return pl.pallas_call(
        body, name="swap_pieces", in_specs=[ANY] * n, out_specs=[ANY] * n,
        out_shape=[SDS(p.shape, p.dtype) for p in pieces], input_output_aliases={j: j for j in range(n)},
        scratch_shapes=[pltpu.SemaphoreType.DMA((n,)), pltpu.SemaphoreType.DMA((n,))],
    )(*pieces)


def _gather_pieces(piece):
    def body(in_ref, out_ref, send_sem, recv_sem):
        x, y, c, k, chips = _place()
        peers = [(x, y, 1 - c)] + [(cx, cy, pc) for (cx, cy) in chips for pc in (c, 1 - c)]
        copies = []
        for ti, peer in enumerate(peers):
            cp = _remote(in_ref, out_ref.at[pl.ds(k, 1), pl.ds(c, 1)], send_sem.at[ti], recv_sem.at[ti], peer)
            cp.start()
            copies.append(cp)
        for ti, (px, py, pc) in enumerate(peers):
            _remote(in_ref, out_ref.at[pl.ds(2 * px + py, 1), pl.ds(pc, 1)], send_sem.at[ti], recv_sem.at[ti],
                    (px, py, pc)).wait_recv()
        for cp in copies:
            cp.wait_send()

    n_peers = 2 * N_CHIPS - 1
    return pl.pallas_call(
        body, name="gather_pieces", in_specs=[ANY], out_specs=ANY,
        out_shape=SDS((N_CHIPS, 2) + piece.shape[2:], piece.dtype),
        scratch_shapes=[pltpu.SemaphoreType.DMA((n_peers,)), pltpu.SemaphoreType.DMA((n_peers,))],
    )(piece)


def _pack_rows(parts, width, total_rows=None):
    rows = []
    for a in parts:
        a2 = a.reshape(-1, width)
        pad = (-a2.shape[0]) % SUBLANES
        rows.append(jnp.pad(a2, ((0, pad), (0, 0))) if pad else a2)
    out = jnp.concatenate(rows, axis=0)
    if total_rows is not None and out.shape[0] < total_rows:
        out = jnp.pad(out, ((0, total_rows - out.shape[0]), (0, 0)))
    return out


def _unpack_rows(packed, shapes, width):
    out, r = [], 0
    for shp in shapes:
        size = 1
        for s in shp:
            size *= s
        nr = size // width
        out.append(packed[r:r + nr].reshape(shp))
        r += nr + ((-nr) % SUBLANES)
    return out


def kernel(x, p, norm_g, w_in, w_out, conv_w, conv_b, conv_ln_g, conv_ln_b, sgu_ln_g, sgu_ln_b, sgu_w, sgu_b, pl_norm_g, pl_gate_w, pl_proj_w, final_g, loss_target, m_norm_g, m_w_in, m_w_out, m_conv_w, m_conv_b, m_conv_ln_g, m_conv_ln_b, m_sgu_ln_g, m_sgu_ln_b, m_sgu_w, m_sgu_b, m_pl_norm_g, m_pl_gate_w, m_pl_proj_w, m_final_g, v_norm_g, v_w_in, v_w_out, v_conv_w, v_conv_b, v_conv_ln_g, v_conv_ln_b, v_sgu_ln_g, v_sgu_ln_b, v_sgu_w, v_sgu_b, v_pl_norm_g, v_pl_gate_w, v_pl_proj_w, v_final_g):
    bsz, seq, d = x.shape
    depth = w_in.shape[0]
    e = w_out.shape[1] * N_CHIPS
    e3 = 3 * e
    n4 = w_in.shape[2]
    ple = p.shape[-1]
    dq = pl_proj_w.shape[2]
    k_taps = conv_w.shape[1]
    kp = k_taps + 1
    n_conv, n_sgu = conv_w.shape[0], sgu_ln_g.shape[0]
    t = bsz * seq
    tm_mm = min(512, seq)
    tm_wg, tm_wg_out = min(2048, t), min(1024, t)
    tm_mix = min(256, seq)
    my_c = lax.axis_index("c")
    my_k = 2 * lax.axis_index("x") + lax.axis_index("y")

    ec = e // N_CHIPS
    small_w = _pack_rows([conv_w.reshape(n_conv * k_taps, ec), sgu_ln_g, sgu_ln_b], ec)[None]
    shards = {"in": w_in.astype(BF16)[:, None], "out": w_out.astype(BF16)[:, None],
              "gate": pl_gate_w.astype(BF16)[:, None], "proj": pl_proj_w.astype(BF16)[:, None]}
    rest = ("out", "gate", "proj")
    gathered = {}
    gathered["in", 0], small_f = _run_comm(_gather_comm([(shards["in"], 0)], small_w), "gather_first")
    todo = [item for l in range(depth) for item in [(nm, l) for nm in rest] + ([("in", l + 1)] if l + 1 < depth else [])]

    def take_until(key):
        if key not in todo:
            return []
        n_items = todo.index(key) + 1
        items, todo[:] = todo[:n_items], todo[n_items:]
        return items

    def mixer_limit(l):
        ahead = l + 3 if l % 2 == 0 else l + 1
        return ("in", ahead) if ahead < depth else (rest[-1], depth - 1)

    def carried(keys):
        return _gather_comm([(shards[nm], ly) for nm, ly in keys]) if keys else None

    def unpack(res, keys):
        if not keys:
            return res
        gathered.update(zip(keys, res[1]))
        return res[0]

    conv_w_rows, sgu_g_rows, sgu_b_rows = _unpack_rows(
        jnp.transpose(small_f, (1, 0, 2)).reshape(small_f.shape[1], e),
        [(n_conv * k_taps, e), (n_sgu, e), (n_sgu, e)], e)
    conv_w_full = conv_w_rows.reshape(n_conv, k_taps, e)
    conv_w_pad = jnp.pad(conv_w_full, ((0, 0), (0, 1), (0, 0)))
    conv_w_fwd, conv_w_bwd = [
        jnp.transpose(conv_w_pad.reshape(n_conv, kp, e // min(cb, e), min(cb, e)), (0, 2, 1, 3)) for cb in (2 * LANES, LANES)]
    sgu_ln_g_full = sgu_g_rows.reshape(n_sgu, 1, e)
    sgu_ln_b_full = sgu_b_rows.reshape(n_sgu, 1, e)
    sgu_bt = jnp.transpose(sgu_b, (0, 2, 1))

    norm_g3 = norm_g[:, None]
    pl_norm_g3 = pl_norm_g[:, None]
    conv_b3, conv_ln_g3, conv_ln_b3 = conv_b[:, None], conv_ln_g[:, None], conv_ln_b[:, None]
    p3 = p.reshape(depth, t, ple)

    xs, hs, projs, us, x1s, y1s, weights = [], [], [], [], [], {}, []
    xc = x.reshape(t, d)
    for l in range(depth):
        j = l // 2
        xs.append(xc)
        keys = take_until(("in", 1) if depth > 1 else (rest[-1], 0)) if l == 0 else []
        h, proj = unpack(_fwd_in(xc, norm_g3, gathered["in", l], l, tm_mm, carried(keys)), keys)
        keys = take_until(mixer_limit(l))
        if l % 2 == 0:
            u, y1s[l] = unpack(_fwd_conv(proj, conv_w_fwd, conv_b3, conv_ln_g3, conv_ln_b3, j, bsz, seq, tm_mix,
                                         carried(keys)), keys)
        else:
            (u,) = unpack(_fwd_sgu(proj, sgu_ln_g_full, sgu_ln_b_full, sgu_w, sgu_bt, j, tm_mix, carried(keys)), keys)
        w_out_l, gate_l, proj_l = gathered["out", l].reshape(1, e, d), gathered["gate", l].reshape(1, d, d), gathered["proj", l]
        weights.append((gathered["in", l], w_out_l, gate_l, proj_l))
        x1, xc = _fwd_out(xc, u, w_out_l, pl_norm_g3, gate_l, p3, proj_l, l, tm_mm)
        hs.append(h)
        projs.append(proj)
        us.append(u)
        x1s.append(x1)

    loss_local, dx, d_final_g = _loss_head(xc, final_g[None], loss_target.reshape(t, d), tm_mm)
    loss = lax.psum(loss_local[0, 0], ("x", "y", "c"))

    c_arr = my_c.astype(jnp.int32).reshape(1)
    ck_arr = jnp.stack([my_c, my_k]).astype(jnp.int32)
    by_chip = lambda a: a.reshape((1, N_CHIPS) + a.shape[1:])
    d_norm_g, d_pl_norm_g = [None] * depth, [None] * depth
    d_conv = [None] * n_conv
    d_sgu = [None] * n_sgu
    locals_, siblings, arrived = [None] * depth, [None] * depth, [None] * depth
    pending = []

    def take():
        if not pending:
            return None
        kind, _, payload = pending[0]
        return _swap_comm(payload) if kind == "swap" else _scatter_comm(payload)

    def settle(res, comm):
        if comm is None:
            return res
        outs, brought = res
        kind, layer, payload = pending.pop()
        if kind == "swap":
            locals_[layer], siblings[layer] = payload, brought
            pending.append(("scatter", layer, [by_chip(_pair_sum(gl, rc, c_arr, BF16)) for gl, rc in zip(payload, brought)]))
        else:
            arrived[layer] = brought
        return outs

    def run_alone(name):
        comm = take()
        settle(([], _run_comm(comm, name)), comm)

    for l in reversed(range(depth)):
        j = l // 2
        w_in_l, w_out_l, gate_l, proj_l = weights[l]
        comm = take()
        dx1, du, rn, ds, dqv, d_pl_norm_g[l] = settle(
            _bwd_out(dx, x1s[l], p3, pl_norm_g3, gate_l, proj_l, w_out_l, l, tm_mm, comm), comm)
        (g_proj,) = _wgrad(p3, dqv, 1, 1, N_CHIPS, tm_wg, "wgrad_proj", a_layer=l)
        (g_gate,) = _wgrad(rn, ds, 1, 1, 1, tm_wg, "wgrad_gate")
        (g_out,) = _wgrad(us[l], dx1, 1, 1, 1, tm_wg_out, "wgrad_out")
        comm = take()
        if l % 2 == 0:
            dproj, dcw, dcb, dlg, dlb = settle(
                _bwd_conv(du, projs[l], y1s[l], conv_w_bwd, conv_ln_g3, conv_ln_b3, j, bsz, seq, tm_mix, comm), comm)
            d_conv[j] = (dcw, dcb, dlg, dlb)
        else:
            dproj, dsw, dsbt, dlg, dlb = settle(
                _bwd_sgu(du, projs[l], sgu_ln_g_full, sgu_ln_b_full, sgu_w, sgu_bt, j, tm_mix, comm), comm)
            d_sgu[j] = (dsw, dsbt, dlg, dlb)
        (g_in,) = _wgrad(hs[l], dproj, 1, N_CHIPS, 1, tm_wg, "wgrad_in")
        local = [g_in.reshape(N_CHIPS, 2, d // 2, n4), g_out.reshape(N_CHIPS, 2, e // (2 * N_CHIPS), d),
                 g_gate.reshape(N_CHIPS, 2, d // (2 * N_CHIPS), d), g_proj.reshape(N_CHIPS, 2, ple // 2, dq)]
        while pending:
            run_alone("reduce_step")
        pending.append(("swap", l, local))
        if l == 0:
            run_alone("swap_last")
        comm = take()
        dx, d_norm_g[l] = settle(_bwd_in(dproj, dx1, xs[l], norm_g3, w_in_l, l, tm_mm, comm), comm)
    while pending:
        run_alone("reduce_tail")
    grad_x = dx.reshape(bsz, seq, d)

    d_conv_w = jnp.stack([jnp.transpose(dc[0], (1, 0, 2)).reshape(kp, e)[:k_taps] for dc in d_conv])
    d_conv_b = jnp.stack([dc[1][0] for dc in d_conv])
    d_conv_ln_g = jnp.stack([dc[2][0] for dc in d_conv])
    d_conv_ln_b = jnp.stack([dc[3][0] for dc in d_conv])
    d_sgu_w = jnp.stack([dsg[0] for dsg in d_sgu])
    d_sgu_b = jnp.stack([jnp.transpose(dsg[1]) for dsg in d_sgu])
    d_sgu_ln_g = jnp.stack([dsg[2][0] for dsg in d_sgu])
    d_sgu_ln_b = jnp.stack([dsg[3][0] for dsg in d_sgu])
    small_grads = [jnp.concatenate(d_norm_g), d_conv_w, d_conv_b, d_conv_ln_g, d_conv_ln_b, d_sgu_ln_g, d_sgu_ln_b,
                   d_sgu_w, d_sgu_b, jnp.concatenate(d_pl_norm_g), d_final_g]
    small_shapes = [a.shape for a in small_grads]
    packed = _pack_rows(small_grads, d)
    pack_rows = packed.shape[0] + ((-packed.shape[0]) % (8 * SUBLANES))
    packed = _pack_rows(small_grads, d, pack_rows)
    gl_small = packed.reshape(N_CHIPS, 2, pack_rows // 8, d)
    (small_sibling,) = _run_comm(_swap_comm([gl_small]), "swap_small")
    small_pair = by_chip(_pair_sum(gl_small, small_sibling, c_arr, F32))

    (small_arrived,) = _run_comm(_scatter_comm([small_pair]), "scatter_small")
    reduced = [lax.empty((depth, 2) + gl.shape[2:], F32) for gl in locals_[0]]
    for l in range(depth):
        reduced = [_chip_sum(gl, rc, ar, ck_arr, acc, l)
                   for gl, rc, ar, acc in zip(locals_[l], siblings[l], arrived[l], reduced)]
    small_both = _chip_sum(gl_small, small_sibling, small_arrived, ck_arr, lax.empty((1, 2) + gl_small.shape[2:], F32), 0)
    small_mine = lax.dynamic_slice_in_dim(small_both, my_c, 1, axis=1)
    reduced = _swap_pieces(reduced)
    small_all = _gather_pieces(small_mine)
    small_all = lax.dynamic_update_slice(small_all, small_mine, (my_k, my_c, 0, 0)).reshape(pack_rows, d)
    small_red = _unpack_rows(small_all, small_shapes, d)
    (gr_norm_g, gr_conv_w, gr_conv_b, gr_conv_ln_g, gr_conv_ln_b, gr_sgu_ln_g, gr_sgu_ln_b, gr_sgu_w, gr_sgu_b,
     gr_pl_norm_g, gr_final_g) = small_red
    gr_final_g = gr_final_g.reshape(d)
    gr_conv_w = lax.dynamic_slice_in_dim(gr_conv_w, my_k * ec, ec, axis=2)
    gr_sgu_ln_g = lax.dynamic_slice_in_dim(gr_sgu_ln_g, my_k * ec, ec, axis=1)
    gr_sgu_ln_b = lax.dynamic_slice_in_dim(gr_sgu_ln_b, my_k * ec, ec, axis=1)

    def shard_update(w, g, m, v):
        flat = lambda a: a.reshape(-1, w.shape[-1])
        return [o.reshape(w.shape) for o in _adamw(flat(w), flat(g), flat(m), flat(v))]

    up_in = shard_update(w_in, reduced[0], m_w_in, v_w_in)
    up_out = shard_update(w_out, reduced[1], m_w_out, v_w_out)
    up_gate = shard_update(pl_gate_w, reduced[2], m_pl_gate_w, v_pl_gate_w)
    up_proj = shard_update(pl_proj_w, reduced[3], m_pl_proj_w, v_pl_proj_w)

    small_names = ["norm_g", "conv_w", "conv_b", "conv_ln_g", "conv_ln_b", "sgu_ln_g", "sgu_ln_b", "sgu_w", "sgu_b",
                   "pl_norm_g", "final_g"]
    small_w_list = [norm_g, conv_w, conv_b, conv_ln_g, conv_ln_b, sgu_ln_g, sgu_ln_b, sgu_w, sgu_b, pl_norm_g, final_g]
    small_m_list = [m_norm_g, m_conv_w, m_conv_b, m_conv_ln_g, m_conv_ln_b, m_sgu_ln_g, m_sgu_ln_b, m_sgu_w, m_sgu_b,
                    m_pl_norm_g, m_final_g]
    small_v_list = [v_norm_g, v_conv_w, v_conv_b, v_conv_ln_g, v_conv_ln_b, v_sgu_ln_g, v_sgu_ln_b, v_sgu_w, v_sgu_b,
                    v_pl_norm_g, v_final_g]
    small_g_list = [gr_norm_g, gr_conv_w, gr_conv_b, gr_conv_ln_g, gr_conv_ln_b, gr_sgu_ln_g, gr_sgu_ln_b, gr_sgu_w,
                    gr_sgu_b, gr_pl_norm_g, gr_final_g]
    width = ec
    shapes_local = [a.shape for a in small_w_list]
    outs_small = _adamw(_pack_rows(small_w_list, width), _pack_rows(small_g_list, width),
                        _pack_rows(small_m_list, width), _pack_rows(small_v_list, width))
    unpacked = [_unpack_rows(o, shapes_local, width) for o in outs_small]
    ups = {name: [unpacked[kind][i] for kind in range(4)] for i, name in enumerate(small_names)}
    ups["w_in"], ups["w_out"], ups["pl_gate_w"], ups["pl_proj_w"] = up_in, up_out, up_gate, up_proj

    order = ["norm_g", "w_in", "w_out", "conv_w", "conv_b", "conv_ln_g", "conv_ln_b", "sgu_ln_g", "sgu_ln_b", "sgu_w",
             "sgu_b", "pl_norm_g", "pl_gate_w", "pl_proj_w", "final_g"]
    result = [loss, grad_x]
    for kind in range(4):
        result.extend(ups[name][kind] for name in order)
    return tuple(result)
```

```python
import functools

import jax
import jax.numpy as jnp
from jax import lax
from jax.experimental import pallas as pl
from jax.experimental.pallas import tpu as pltpu

F32 = jnp.float32
BF16 = jnp.bfloat16
SDS = jax.ShapeDtypeStruct

EPS = 1e-6
CHUNK = 128
GROUPS = 8
HALO = 32
N_CHIPS = 4
LANES = 128
SUBLANES = 8
V7X_VMEM_LIMIT = 56 << 20

ADAM_LR = 0.001
ADAM_B1 = 0.9
ADAM_B2 = 0.999
ADAM_EPS = 1e-08
ADAM_WD = 0.01
ADAM_STEP = 10

MESH_IDS = pl.DeviceIdType.MESH
ANY = pl.BlockSpec(memory_space=pl.ANY)


def _params(n_axes):
    return pltpu.CompilerParams(dimension_semantics=("arbitrary",) * n_axes, vmem_limit_bytes=V7X_VMEM_LIMIT)


def _const(shape):
    zeros = (0,) * len(shape)
    return pl.BlockSpec(shape, lambda *_: zeros)


def _layer(shape, layer):
    zeros = (0,) * len(shape)
    return pl.BlockSpec((None,) + tuple(shape), lambda *_: (layer,) + zeros, pipeline_mode=pl.Buffered(1))


class _Comm:
    def __init__(self, operands, out_shape, sems, start, finish, relay=None):
        self.operands, self.out_shape, self.sems, self.start, self.finish = operands, out_shape, sems, start, finish
        self.relay = relay


def _call(body, *, name, grid, in_specs, out_specs, out_shape, operands, scratch_shapes=(), comm=None):
    in_specs, out_specs, out_shape, scratch_shapes = list(in_specs), list(out_specs), list(out_shape), list(scratch_shapes)
    if comm is None:
        return pl.pallas_call(body, name=name, grid=grid, in_specs=in_specs, out_specs=out_specs, out_shape=out_shape,
                              scratch_shapes=scratch_shapes, compiler_params=_params(len(grid)))(*operands)
    n_in, n_out, n_sc = len(in_specs), len(out_specs), len(scratch_shapes)
    ci, co = len(comm.operands), len(comm.out_shape)

    def hosted(*refs):
        ins, cins = refs[:n_in], refs[n_in:n_in + ci]
        outs, couts = refs[n_in + ci:n_in + ci + n_out], refs[n_in + ci + n_out:n_in + ci + n_out + co]
        scratch = refs[n_in + ci + n_out + co:n_in + ci + n_out + co + n_sc]
        sems = refs[n_in + ci + n_out + co + n_sc:]
        first = functools.reduce(jnp.logical_and, [pl.program_id(a) == 0 for a in range(len(grid))])
        last = functools.reduce(jnp.logical_and, [pl.program_id(a) == g - 1 for a, g in enumerate(grid)])

        @pl.when(first)
        def _():
            comm.start(cins, couts, sems)

        if comm.relay is not None:
            @pl.when(last)
            def _():
                comm.relay(cins, couts, sems)

        body(*ins, *outs, *scratch)

        @pl.when(last)
        def _():
            comm.finish(cins, couts, sems)

    res = pl.pallas_call(
        hosted, name=name, grid=grid, in_specs=in_specs + [ANY] * ci, out_specs=out_specs + [ANY] * co,
        out_shape=out_shape + list(comm.out_shape), scratch_shapes=scratch_shapes + list(comm.sems),
        compiler_params=_params(len(grid)))(*operands, *comm.operands)
    return res[:n_out], res[n_out:]


def _run_comm(comm, name):
    ci, co = len(comm.operands), len(comm.out_shape)

    def body(*refs):
        comm.start(refs[:ci], refs[ci:ci + co], refs[ci + co:])
        if comm.relay is not None:
            comm.relay(refs[:ci], refs[ci:ci + co], refs[ci + co:])
        comm.finish(refs[:ci], refs[ci:ci + co], refs[ci + co:])

    return pl.pallas_call(body, name=name, in_specs=[ANY] * ci, out_specs=[ANY] * co, out_shape=list(comm.out_shape),
                          scratch_shapes=list(comm.sems))(*comm.operands)


def _sigmoid(v):
    return jax.nn.sigmoid(v)


def _dsilu(v, s):
    return s * (1.0 + v * (1.0 - s))


def _gelu_parts(v):
    cdf = 0.5 * (1.0 + lax.erf(v * 0.7071067811865476))
    pdf = jnp.exp(-0.5 * v * v) * 0.3989422804014327
    return v * cdf, cdf + v * pdf


def _gelu(v):
    return 0.5 * v * (1.0 + lax.erf(v * 0.7071067811865476))


def _rms_stats(x):
    r = lax.rsqrt(jnp.mean(x * x, axis=-1, keepdims=True) + EPS)
    return r, x * r


def _rms_bwd(dy, g, r, xh):
    gdy = dy * g
    return r * (gdy - xh * jnp.mean(xh * gdy, axis=-1, keepdims=True))


def _ln_stats(x):
    mu = jnp.mean(x, axis=-1, keepdims=True)
    xc = x - mu
    rs = lax.rsqrt(jnp.mean(xc * xc, axis=-1, keepdims=True) + EPS)
    return rs, xc * rs


def _ln_bwd(dy, g, rs, xh):
    dxh = dy * g
    return rs * (dxh - jnp.mean(dxh, axis=-1, keepdims=True) - xh * jnp.mean(dxh * xh, axis=-1, keepdims=True))


def _dot(a, b):
    return jnp.dot(a, b, preferred_element_type=F32)


def _dot_nt(a, b):
    return lax.dot_general(a, b, (((1,), (1,)), ((), ())), preferred_element_type=F32)


def _dot_tn(a, b):
    return lax.dot_general(a, b, (((0,), (0,)), ((), ())), preferred_element_type=F32)


def _fwd_in(x, norm_g, w_in_full, layer, tm, comm=None):
    t, d = x.shape
    _, nk, _, n4 = w_in_full.shape

    def body(x_ref, g_ref, w_ref, h_ref, proj_ref):
        r, xh = _rms_stats(x_ref[...])
        h = (xh * g_ref[...]).astype(BF16)
        h_ref[...] = h
        for k in range(nk):
            proj_ref[:, k * n4:(k + 1) * n4] = _dot(h, w_ref[k]).astype(BF16)

    return _call(
        body, name="fwd_in", grid=(t // tm,),
        in_specs=[pl.BlockSpec((tm, d), lambda i: (i, 0)), _layer((1, d), layer), _layer((nk, d, n4), 0)],
        out_specs=[pl.BlockSpec((tm, d), lambda i: (i, 0)), pl.BlockSpec((tm, nk * n4), lambda i: (i, 0))],
        out_shape=[SDS((t, d), BF16), SDS((t, nk * n4), BF16)],
        operands=(x, norm_g, w_in_full), comm=comm)


def _halo_maps(nt, hb, n_halo_blocks):
    def prev(b, i):
        return (jnp.maximum((b * nt + i) * hb - 1, 0), 0)

    def nxt(b, i):
        return (jnp.minimum((b * nt + i + 1) * hb, n_halo_blocks - 1), 0)

    return prev, nxt


TAP_TILE_VREGS = 16


def _tap_rows(cb, tm):
    return min(TAP_TILE_VREGS * SUBLANES * LANES // cb, tm)


def _conv_taps(src_ref, w_ref, dst_ref, cb_idx, n_rows, rb, first, reverse):
    k_taps = w_ref.shape[1] - 1
    for r0 in range(0, n_rows, rb):
        acc = None
        for res in range(SUBLANES):
            rows = rb + (SUBLANES if res else 0)
            group = None
            for k in range(k_taps):
                off = first + k
                if off % SUBLANES != res:
                    continue
                wk = w_ref[cb_idx, pl.ds((k_taps - 1 - k) if reverse else k, 1), :]
                term = wk * src_ref[cb_idx, pl.ds(r0 + off - res, rows), :]
                group = term if group is None else group + term
            if group is None:
                continue
            part = group[res:res + rb] if res else group
            acc = part if acc is None else acc + part
        dst_ref[cb_idx, pl.ds(r0, rb), :] = acc


def _fwd_conv(proj, conv_w_blk, conv_b, ln_g, ln_b, layer, bsz, seq, tm, comm=None):
    t, e3 = proj.shape
    e = e3 // 3
    nt = seq // tm
    hb = tm // HALO
    _, ncb, kp, cb = conv_w_blk.shape
    rb = _tap_rows(cb, tm)
    prev, _ = _halo_maps(nt, hb, t // HALO)

    def body(proj_ref, halo_ref, w_ref, b_ref, g_ref, lb_ref, u_ref, y1_ref, y0s, y1s):
        i = pl.program_id(1)
        a = proj_ref[:, 0:e].astype(F32)
        b = proj_ref[:, e:2 * e].astype(F32)
        y0 = a * _sigmoid(b)
        ah = halo_ref[:, 0:e].astype(F32)
        bh = halo_ref[:, e:2 * e].astype(F32)
        y0h = jnp.where(i > 0, ah * _sigmoid(bh), 0.0)
        for c in range(ncb):
            y0s[c, 0:HALO, :] = y0h[:, c * cb:(c + 1) * cb]
            y0s[c, HALO:HALO + tm, :] = y0[:, c * cb:(c + 1) * cb]

        def per_block(c, carry):
            _conv_taps(y0s, w_ref, y1s, c, tm, rb, HALO - (kp - 2), False)
            return carry

        lax.fori_loop(0, ncb, per_block, 0)
        y1 = jnp.concatenate([y1s[c] for c in range(ncb)], axis=1) + b_ref[...]
        y1_ref[...] = y1
        rs, xh = _ln_stats(y1)
        y2 = xh * g_ref[...] + lb_ref[...]
        y = y2 * _sigmoid(y2)
        z = proj_ref[:, 2 * e:3 * e].astype(F32)
        u_ref[...] = (y * (z * _sigmoid(z))).astype(BF16)

    return _call(
        body, name="fwd_conv", grid=(bsz, nt),
        in_specs=[pl.BlockSpec((tm, e3), lambda b, i: (b * nt + i, 0)),
                  pl.BlockSpec((HALO, 2 * e), prev),
                  _layer((ncb, kp, cb), layer), _layer((1, e), layer), _layer((1, e), layer), _layer((1, e), layer)],
        out_specs=[pl.BlockSpec((tm, e), lambda b, i: (b * nt + i, 0)), pl.BlockSpec((tm, e), lambda b, i: (b * nt + i, 0))],
        out_shape=[SDS((t, e), BF16), SDS((t, e), F32)],
        scratch_shapes=[pltpu.VMEM((ncb, HALO + tm, cb), F32), pltpu.VMEM((ncb, tm, cb), F32)],
        operands=(proj, proj, conv_w_blk, conv_b, ln_g, ln_b), comm=comm)


def _tril_mask():
    rows = lax.broadcasted_iota(jnp.int32, (CHUNK, CHUNK), 0)
    cols = lax.broadcasted_iota(jnp.int32, (CHUNK, CHUNK), 1)
    return rows >= cols


def _fwd_sgu(proj, ln_g, ln_b, sgu_w, sgu_bt, layer, tm, comm=None):
    t, e3 = proj.shape
    e = e3 // 3
    gw = e // GROUPS
    nch = tm // CHUNK

    def body(proj_ref, g_ref, lb_ref, w_ref, bt_ref, u_ref, mixed):
        a = proj_ref[:, 0:e].astype(F32)
        b = proj_ref[:, e:2 * e].astype(F32)
        z = proj_ref[:, 2 * e:3 * e].astype(F32)
        rs, xh = _ln_stats(_gelu(b))
        v = (xh * g_ref[...] + lb_ref[...]).astype(BF16)
        mask = _tril_mask()
        for g in range(GROUPS):
            wm = jnp.where(mask, w_ref[g], 0.0).astype(BF16)
            bias = bt_ref[:, g:g + 1]
            for n in range(nch):
                blk = v[n * CHUNK:(n + 1) * CHUNK, g * gw:(g + 1) * gw]
                mixed[n * CHUNK:(n + 1) * CHUNK, g * gw:(g + 1) * gw] = _dot(wm, blk) + bias
        y = _gelu(a) * mixed[...]
        u_ref[...] = (y * (z * _sigmoid(z))).astype(BF16)

    return _call(
        body, name="fwd_sgu", grid=(t // tm,),
        in_specs=[pl.BlockSpec((tm, e3), lambda i: (i, 0)), _layer((1, e), layer), _layer((1, e), layer),
                  _layer((GROUPS, CHUNK, CHUNK), layer), _layer((CHUNK, GROUPS), layer)],
        out_specs=[pl.BlockSpec((tm, e), lambda i: (i, 0))],
        out_shape=[SDS((t, e), BF16)],
        scratch_shapes=[pltpu.VMEM((tm, e), F32)],
        operands=(proj, ln_g, ln_b, sgu_w, sgu_bt), comm=comm)


def _ple_forward(x1, p_ref, plg_ref, gw_ref, pw_ref):
    nk, _, dq = pw_ref.shape
    r, xh = _rms_stats(x1)
    rn = (xh * plg_ref[...]).astype(BF16)
    gate = _sigmoid(_dot(rn, gw_ref[...]))
    pb = p_ref[...].astype(BF16)
    q = jnp.concatenate([_dot(pb, pw_ref[k]) for k in range(nk)], axis=1)
    return r, xh, rn, gate, q


def _fwd_out(x, u, w_out_full, pl_norm_g, gate_w_full, p, proj_w_full, layer, tm):
    t, d = x.shape
    e = u.shape[1]
    ple = p.shape[-1]
    nk, dq = proj_w_full.shape[1], proj_w_full.shape[3]

    def body(x_ref, u_ref, wo_ref, plg_ref, gw_ref, p_ref, pw_ref, x1_ref, x2_ref):
        x1 = x_ref[...] + _dot(u_ref[...], wo_ref[...])
        x1_ref[...] = x1
        _, _, _, gate, q = _ple_forward(x1, p_ref, plg_ref, gw_ref, pw_ref)
        x2_ref[...] = x1 + gate * q

    return _call(
        body, name="fwd_out", grid=(t // tm,),
        in_specs=[pl.BlockSpec((tm, d), lambda i: (i, 0)), pl.BlockSpec((tm, e), lambda i: (i, 0)),
                  _layer((e, d), 0), _layer((1, d), layer), _layer((d, d), 0),
                  pl.BlockSpec((None, tm, ple), lambda i: (layer, i, 0)), _layer((nk, ple, dq), 0)],
        out_specs=[pl.BlockSpec((tm, d), lambda i: (i, 0)), pl.BlockSpec((tm, d), lambda i: (i, 0))],
        out_shape=[SDS((t, d), F32), SDS((t, d), F32)],
        operands=(x, u, w_out_full, pl_norm_g, gate_w_full, p, proj_w_full))


def _loss_head(x, final_g, target, tm):
    t, d = x.shape
    n_steps = t // tm

    def body(x_ref, g_ref, tgt_ref, loss_ref, dx_ref, dg_ref, sq_acc):
        i = pl.program_id(0)

        @pl.when(i == 0)
        def _():
            sq_acc[...] = jnp.zeros_like(sq_acc)
            dg_ref[...] = jnp.zeros_like(dg_ref)

        g = g_ref[...]
        r, xh = _rms_stats(x_ref[...])
        diff = xh * g - tgt_ref[...]
        sq_acc[...] += jnp.sum(diff * diff, axis=0, keepdims=True)
        dout = diff * (1.0 / d)
        dg_ref[...] += jnp.sum(dout * xh, axis=0, keepdims=True)
        dx_ref[...] = _rms_bwd(dout, g, r, xh)

        @pl.when(i == n_steps - 1)
        def _():
            loss_ref[...] = jnp.sum(sq_acc[...], axis=1, keepdims=True) * (0.5 / d)

    return pl.pallas_call(
        body, name="loss_head", grid=(n_steps,),
        in_specs=[pl.BlockSpec((tm, d), lambda i: (i, 0)), _const((1, d)), pl.BlockSpec((tm, d), lambda i: (i, 0))],
        out_specs=[_const((1, 1)), pl.BlockSpec((tm, d), lambda i: (i, 0)), _const((1, d))],
        out_shape=[SDS((1, 1), F32), SDS((t, d), F32), SDS((1, d), F32)],
        scratch_shapes=[pltpu.VMEM((1, d), F32)],
        compiler_params=_params(1),
    )(x, final_g, target)


def _bwd_out(dx2, x1, p, pl_norm_g, gate_w_full, proj_w_full, w_out_full, layer, tm):
    t, d = dx2.shape
    e = w_out_full.shape[1]
    ple = p.shape[-1]
    nk, dq_w = proj_w_full.shape[1], proj_w_full.shape[3]

    def body(dx2_ref, x1_ref, p_ref, plg_ref, gw_ref, pw_ref, wo_ref, dx1_ref, du_ref, rn_ref, ds_ref, dq_ref, dplg_ref):
        @pl.when(pl.program_id(0) == 0)
        def _():
            dplg_ref[...] = jnp.zeros_like(dplg_ref)

        dx2v = dx2_ref[...]
        r, xh, rn, gate, q = _ple_forward(x1_ref[...], p_ref, plg_ref, gw_ref, pw_ref)
        rn_ref[...] = rn
        dq_ref[...] = (dx2v * gate).astype(BF16)
        ds = (dx2v * q * gate * (1.0 - gate)).astype(BF16)
        ds_ref[...] = ds
        dr = _dot_nt(ds, gw_ref[...])
        dplg_ref[...] += jnp.sum(dr * xh, axis=0, keepdims=True)
        dx1 = dx2v + _rms_bwd(dr, plg_ref[...], r, xh)
        dx1_ref[...] = dx1
        du_ref[...] = _dot_nt(dx1.astype(BF16), wo_ref[...]).astype(BF16)

    row = lambda w: pl.BlockSpec((tm, w), lambda i: (i, 0))
    return _call(
        body, name="bwd_out", grid=(t // tm,),
        in_specs=[row(d), row(d), pl.BlockSpec((None, tm, ple), lambda i: (layer, i, 0)),
                  _layer((1, d), layer), _layer((d, d), 0), _layer((nk, ple, dq_w), 0), _layer((e, d), 0)],
        out_specs=[row(d), row(e), row(d), row(d), row(d), _const((1, d))],
        out_shape=[SDS((t, d), F32), SDS((t, e), BF16), SDS((t, d), BF16), SDS((t, d), BF16), SDS((t, d), BF16),
                   SDS((1, d), F32)],
        operands=(dx2, x1, p, pl_norm_g, gate_w_full, proj_w_full, w_out_full))


def _bwd_conv(du, proj, y1, conv_w_blk, ln_g, ln_b, layer, bsz, seq, tm, comm=None):
    t, e3 = proj.shape
    e = e3 // 3
    nt = seq // tm
    hb = tm // HALO
    _, ncb, kp, cb = conv_w_blk.shape
    rb = _tap_rows(cb, tm)
    k_taps = kp - 1
    prev, nxt = _halo_maps(nt, hb, t // HALO)
    z_halo = lambda b, i: (nxt(b, i)[0], 2)

    def ln_silu_bwd(du_v, z_v, y1_v, g, lb):
        rs, xh = _ln_stats(y1_v)
        y2 = xh * g + lb
        sg = _sigmoid(y2)
        sz = _sigmoid(z_v)
        dy = du_v * (z_v * sz)
        dy2 = dy * _dsilu(y2, sg)
        return _ln_bwd(dy2, g, rs, xh), dy2, xh, du_v * (y2 * sg) * _dsilu(z_v, sz)

    def body(du_ref, proj_ref, y1_ref, duh_ref, zh_ref, y1h_ref, abh_ref, w_ref, g_ref, lb_ref,
             dproj_ref, dw_ref, dcb_ref, dg_ref, dlb_ref, y0s, dy1s, dy0s, ysh):
        b_id, i = pl.program_id(0), pl.program_id(1)

        @pl.when((b_id == 0) & (i == 0))
        def _():
            dw_ref[...] = jnp.zeros_like(dw_ref)
            dcb_ref[...] = jnp.zeros_like(dcb_ref)
            dg_ref[...] = jnp.zeros_like(dg_ref)
            dlb_ref[...] = jnp.zeros_like(dlb_ref)

        g, lb = g_ref[...], lb_ref[...]
        a = proj_ref[:, 0:e].astype(F32)
        b = proj_ref[:, e:2 * e].astype(F32)
        z = proj_ref[:, 2 * e:3 * e].astype(F32)
        sb = _sigmoid(b)
        y0 = a * sb
        dy1, dy2, xh, dz = ln_silu_bwd(du_ref[...].astype(F32), z, y1_ref[...], g, lb)
        dproj_ref[:, 2 * e:3 * e] = dz.astype(BF16)
        dg_ref[...] += jnp.sum(dy2 * xh, axis=0, keepdims=True)
        dlb_ref[...] += jnp.sum(dy2, axis=0, keepdims=True)
        dcb_ref[...] += jnp.sum(dy1, axis=0, keepdims=True)
        dy1h, _, _, _ = ln_silu_bwd(duh_ref[...].astype(F32), zh_ref[...].astype(F32), y1h_ref[...], g, lb)
        dy1h = jnp.where(i < nt - 1, dy1h, 0.0)
        ah = abh_ref[:, 0:e].astype(F32)
        bh = abh_ref[:, e:2 * e].astype(F32)
        y0h = jnp.where(i > 0, ah * _sigmoid(bh), 0.0)
        for c in range(ncb):
            cols = slice(c * cb, (c + 1) * cb)
            y0s[c, 0:HALO, :] = y0h[:, cols]
            y0s[c, HALO:HALO + tm, :] = y0[:, cols]
            dy1s[c, 0:tm, :] = dy1[:, cols]
            dy1s[c, tm:tm + HALO, :] = dy1h[:, cols]

        def per_block(c, carry):
            _conv_taps(dy1s, w_ref, dy0s, c, tm, rb, 0, True)
            for res in range(1, SUBLANES):
                ysh[res - 1] = y0s[c, pl.ds(res, tm + HALO - SUBLANES), :]
            for k in range(k_taps):
                off = HALO - (k_taps - 1) + k
                res = off % SUBLANES
                acc = None
                for r0 in range(0, tm, rb):
                    rows = pl.ds(r0 + off - res, rb)
                    shifted = ysh[res - 1, rows, :] if res else y0s[c, rows, :]
                    term = dy1s[c, pl.ds(r0, rb), :] * shifted
                    acc = term if acc is None else acc + term
                dw_ref[c, pl.ds(k, 1), :] += jnp.sum(acc, axis=0, keepdims=True)
            return carry

        lax.fori_loop(0, ncb, per_block, 0)
        dy0 = jnp.concatenate([dy0s[c] for c in range(ncb)], axis=1)
        dproj_ref[:, 0:e] = (dy0 * sb).astype(BF16)
        dproj_ref[:, e:2 * e] = (dy0 * a * sb * (1.0 - sb)).astype(BF16)

    tile = lambda w: pl.BlockSpec((tm, w), lambda b, i: (b * nt + i, 0))
    return _call(
        body, name="bwd_conv", grid=(bsz, nt),
        in_specs=[tile(e), tile(e3), tile(e),
                  pl.BlockSpec((HALO, e), nxt), pl.BlockSpec((HALO, e), z_halo), pl.BlockSpec((HALO, e), nxt),
                  pl.BlockSpec((HALO, 2 * e), prev),
                  _layer((ncb, kp, cb), layer), _layer((1, e), layer), _layer((1, e), layer)],
        out_specs=[tile(e3), _const((ncb, kp, cb)), _const((1, e)), _const((1, e)), _const((1, e))],
        out_shape=[SDS((t, e3), BF16), SDS((ncb, kp, cb), F32), SDS((1, e), F32), SDS((1, e), F32), SDS((1, e), F32)],
        scratch_shapes=[pltpu.VMEM((ncb, HALO + tm, cb), F32), pltpu.VMEM((ncb, tm + HALO, cb), F32),
                        pltpu.VMEM((ncb, tm, cb), F32), pltpu.VMEM((SUBLANES - 1, tm + HALO - SUBLANES, cb), F32)],
        operands=(du, proj, y1, du, proj, y1, proj, conv_w_blk, ln_g, ln_b), comm=comm)


def _bwd_sgu(du, proj, ln_g, ln_b, sgu_w, sgu_bt, layer, tm, comm=None):
    t, e3 = proj.shape
    e = e3 // 3
    gw = e // GROUPS
    nch = tm // CHUNK

    def body(du_ref, proj_ref, g_ref, lb_ref, w_ref, bt_ref, dproj_ref, dw_ref, dbt_ref, dg_ref, dlb_ref, mixed, dmix, dv):
        @pl.when(pl.program_id(0) == 0)
        def _():
            dw_ref[...] = jnp.zeros_like(dw_ref)
            dbt_ref[...] = jnp.zeros_like(dbt_ref)
            dg_ref[...] = jnp.zeros_like(dg_ref)
            dlb_ref[...] = jnp.zeros_like(dlb_ref)

        g, lb = g_ref[...], lb_ref[...]
        a = proj_ref[:, 0:e].astype(F32)
        b = proj_ref[:, e:2 * e].astype(F32)
        z = proj_ref[:, 2 * e:3 * e].astype(F32)
        ug, dug = _gelu_parts(a)
        vb, dvb_db = _gelu_parts(b)
        rs, xh = _ln_stats(vb)
        v = (xh * g + lb).astype(BF16)
        mask = _tril_mask()
        for gi in range(GROUPS):
            wm = jnp.where(mask, w_ref[gi], 0.0).astype(BF16)
            bias = bt_ref[:, gi:gi + 1]
            for c in range(nch):
                blk = v[c * CHUNK:(c + 1) * CHUNK, gi * gw:(gi + 1) * gw]
                mixed[c * CHUNK:(c + 1) * CHUNK, gi * gw:(gi + 1) * gw] = _dot(wm, blk) + bias
        mx = mixed[...]
        sz = _sigmoid(z)
        duv = du_ref[...].astype(F32)
        dy = duv * (z * sz)
        dproj_ref[:, 2 * e:3 * e] = (duv * (ug * mx) * _dsilu(z, sz)).astype(BF16)
        dproj_ref[:, 0:e] = (dy * mx * dug).astype(BF16)
        dmix[...] = dy * ug
        for gi in range(GROUPS):
            wm = jnp.where(mask, w_ref[gi], 0.0).astype(BF16)
            dw_acc = None
            db_acc = None
            for c in range(nch):
                rows, cols = slice(c * CHUNK, (c + 1) * CHUNK), slice(gi * gw, (gi + 1) * gw)
                dm = dmix[rows, cols]
                dmb = dm.astype(BF16)
                dw_n = _dot_nt(dmb, v[rows, cols])
                db_n = jnp.sum(dm, axis=1, keepdims=True)
                dw_acc = dw_n if dw_acc is None else dw_acc + dw_n
                db_acc = db_n if db_acc is None else db_acc + db_n
                dv[rows, cols] = _dot_tn(wm, dmb)
            dw_ref[gi] += jnp.where(mask, dw_acc, 0.0)
            dbt_ref[:, gi:gi + 1] += db_acc
        dvv = dv[...]
        dg_ref[...] += jnp.sum(dvv * xh, axis=0, keepdims=True)
        dlb_ref[...] += jnp.sum(dvv, axis=0, keepdims=True)
        dproj_ref[:, e:2 * e] = (_ln_bwd(dvv, g, rs, xh) * dvb_db).astype(BF16)

    return _call(
        body, name="bwd_sgu", grid=(t // tm,),
        in_specs=[pl.BlockSpec((tm, e), lambda i: (i, 0)), pl.BlockSpec((tm, e3), lambda i: (i, 0)),
                  _layer((1, e), layer), _layer((1, e), layer),
                  _layer((GROUPS, CHUNK, CHUNK), layer), _layer((CHUNK, GROUPS), layer)],
        out_specs=[pl.BlockSpec((tm, e3), lambda i: (i, 0)), _const((GROUPS, CHUNK, CHUNK)), _const((CHUNK, GROUPS)),
                   _const((1, e)), _const((1, e))],
        out_shape=[SDS((t, e3), BF16), SDS((GROUPS, CHUNK, CHUNK), F32), SDS((CHUNK, GROUPS), F32),
                   SDS((1, e), F32), SDS((1, e), F32)],
        scratch_shapes=[pltpu.VMEM((tm, e), F32), pltpu.VMEM((tm, e), F32), pltpu.VMEM((tm, e), F32)],
        operands=(du, proj, ln_g, ln_b, sgu_w, sgu_bt), comm=comm)


def _bwd_in(dproj, dx1, x, norm_g, w_in_full, layer, tm, comm=None):
    t, d = x.shape
    _, nk, _, n4 = w_in_full.shape

    def body(dproj_ref, dx1_ref, x_ref, g_ref, w_ref, dx_ref, dg_ref):
        @pl.when(pl.program_id(0) == 0)
        def _():
            dg_ref[...] = jnp.zeros_like(dg_ref)

        dh = None
        for k in range(nk):
            part = _dot_nt(dproj_ref[:, k * n4:(k + 1) * n4], w_ref[k])
            dh = part if dh is None else dh + part
        r, xh = _rms_stats(x_ref[...])
        dg_ref[...] += jnp.sum(dh * xh, axis=0, keepdims=True)
        dx_ref[...] = dx1_ref[...] + _rms_bwd(dh, g_ref[...], r, xh)

    row = lambda w: pl.BlockSpec((tm, w), lambda i: (i, 0))
    return _call(
        body, name="bwd_in", grid=(t // tm,),
        in_specs=[row(nk * n4), row(d), row(d), _layer((1, d), layer), _layer((nk, d, n4), 0)],
        out_specs=[row(d), _const((1, d))],
        out_shape=[SDS((t, d), F32), SDS((1, d), F32)],
        operands=(dproj, dx1, x, norm_g, w_in_full), comm=comm)


def _wgrad(a, b, kblk, nblk, n_split, tm, name, a_layer=None, comm=None):
    t, n = b.shape
    k = a.shape[-1]
    kw, nw = k // kblk, n // nblk
    nws = nw // n_split
    n_steps = t // tm

    def body(a_ref, b_ref, o_ref):
        @pl.when(pl.program_id(2) == 0)
        def _():
            o_ref[...] = jnp.zeros_like(o_ref)

        res = _dot_tn(a_ref[...].astype(BF16), b_ref[...].astype(BF16))
        for s in range(n_split):
            o_ref[s] += res[:, s * nws:(s + 1) * nws]

    if a_layer is None:
        a_spec = pl.BlockSpec((tm, kw), lambda kb, nb, i: (i, kb))
    else:
        a_spec = pl.BlockSpec((None, tm, kw), lambda kb, nb, i: (a_layer, i, kb))
    return _call(
        body, name=name, grid=(kblk, nblk, n_steps),
        in_specs=[a_spec, pl.BlockSpec((tm, nw), lambda kb, nb, i: (i, nb))],
        out_specs=[pl.BlockSpec((None, n_split, kw, nws), lambda kb, nb, i: (kb, nb, 0, 0))],
        out_shape=[SDS((kblk, nblk * n_split, kw, nws), F32)],
        operands=(a, b), comm=comm)


def _row_tile(rows, cols, budget_bytes=2 << 20):
    best = None
    for cand in range(SUBLANES, rows + 1, SUBLANES):
        if rows % cand == 0 and cand * cols * 4 <= budget_bytes:
            best = cand
    return best if best is not None else rows


ROW_CHUNKS = 2


def _pair_sums(grads, recv, my_c, wire_dtype):
    n = len(grads)
    shapes = [(g.shape[2] // ROW_CHUNKS, g.shape[3]) for g in grads]

    def body(c_ref, *refs):
        for j in range(n):
            refs[2 * n + j][...] = (refs[2 * j][...] + refs[2 * j + 1][...]).astype(wire_dtype)

    in_specs, out_specs = [], []
    for th, c in shapes:
        in_specs.append(pl.BlockSpec((None, None, th, c), lambda k, i, c_ref: (k, c_ref[0], i, 0)))
        in_specs.append(pl.BlockSpec((None, None, th, c), lambda k, i, c_ref: (k, 0, i, 0)))
        out_specs.append(pl.BlockSpec((None, th, c), lambda k, i, c_ref: (k, i, 0)))
    grid_spec = pltpu.PrefetchScalarGridSpec(num_scalar_prefetch=1, grid=(N_CHIPS, ROW_CHUNKS),
                                             in_specs=in_specs, out_specs=out_specs)
    operands = [a for pair in zip(grads, recv) for a in pair]
    return pl.pallas_call(body, name="pair_sums", grid_spec=grid_spec,
                          out_shape=[SDS((N_CHIPS,) + g.shape[2:], wire_dtype) for g in grads],
                          compiler_params=_params(2))(my_c, *operands)


def _chip_sums(grads, recv, arrived, my_ck, stacked, layer):
    n = len(grads)
    shapes = [(g.shape[2] // ROW_CHUNKS, g.shape[3]) for g in grads]

    def body(ck_ref, *refs):
        for j in range(n):
            g_ref, r_ref, a1_ref, a2_ref, a3_ref = refs[5 * j:5 * j + 5]
            acc = g_ref[...] + r_ref[...]
            for ref in (a1_ref, a2_ref, a3_ref):
                acc = acc + ref[...].astype(F32)
            refs[6 * n + j][...] = acc

    in_specs, out_specs = [], []
    for th, c in shapes:
        block = (None, None, th, c)
        in_specs.append(pl.BlockSpec(block, lambda i, ck: (ck[1], ck[0], i, 0)))
        in_specs.append(pl.BlockSpec(block, lambda i, ck: (ck[1], 0, i, 0)))
        for flip in (1, 2, 3):
            in_specs.append(pl.BlockSpec(block, lambda i, ck, flip=flip: (0, jnp.bitwise_xor(ck[1], flip), i, 0)))
        out_specs.append(pl.BlockSpec(block, lambda i, ck: (layer, ck[0], i, 0)))
    grid_spec = pltpu.PrefetchScalarGridSpec(num_scalar_prefetch=1, grid=(ROW_CHUNKS,),
                                             in_specs=in_specs + [ANY] * n, out_specs=out_specs)
    operands = [a for g, r, ar in zip(grads, recv, arrived) for a in (g, r, ar, ar, ar)]
    return pl.pallas_call(body, name="chip_sums", grid_spec=grid_spec, out_shape=[SDS(s.shape, F32) for s in stacked],
                          input_output_aliases={1 + 5 * n + j: j for j in range(n)},
                          compiler_params=_params(1))(my_ck, *operands, *stacked)


def _adam_math(w, gv, m, v):
    c1 = 1.0 - ADAM_B1 ** ADAM_STEP
    c2 = 1.0 - ADAM_B2 ** ADAM_STEP
    mn = ADAM_B1 * m + (1.0 - ADAM_B1) * gv
    vn = ADAM_B2 * v + (1.0 - ADAM_B2) * (gv * gv)
    m_hat = mn / c1
    v_hat = vn / c2
    return -ADAM_LR * (m_hat / (jnp.sqrt(v_hat) + ADAM_EPS) + ADAM_WD * w), mn, vn


def _adamw(w, g, m, v):
    rows, cols = w.shape
    tr = _row_tile(rows, cols)

    def body(w_ref, g_ref, m_ref, v_ref, go_ref, d_ref, mo_ref, vo_ref):
        gv = g_ref[...]
        go_ref[...] = gv
        d_ref[...], mo_ref[...], vo_ref[...] = _adam_math(w_ref[...], gv, m_ref[...], v_ref[...])

    spec = pl.BlockSpec((tr, cols), lambda i: (i, 0))
    return pl.pallas_call(
        body, name="adamw", grid=(rows // tr,), in_specs=[spec] * 4, out_specs=[spec] * 4,
        out_shape=[SDS((rows, cols), F32)] * 4, compiler_params=_params(1))(w, g, m, v)


def _place():
    x, y, c = lax.axis_index("x"), lax.axis_index("y"), lax.axis_index("c")
    chips = [(1 - x, y), (x, 1 - y), (1 - x, 1 - y)]
    return x, y, c, 2 * x + y, chips


def _remote(src, dst, send_sem, recv_sem, device):
    return pltpu.make_async_remote_copy(src_ref=src, dst_ref=dst, send_sem=send_sem, recv_sem=recv_sem,
                                        device_id=device, device_id_type=MESH_IDS)


def _gather_comm(items, small=None):
    shards = [arr for arr, _ in items]
    layers = [layer for _, layer in items]
    n = len(shards)
    extra = 0 if small is None else 1

    def copies(ins, outs, sems):
        ici_send, ici_recv, d2d_send, d2d_recv, own_send, own_recv = sems
        x, y, c, k, chips = _place()
        sibling = (x, y, 1 - c)
        own, ici_out, ici_in, fwd_out, fwd_in = [], [], [], [], []
        for j in range(n):
            h = ins[j].shape[2] // 2
            mine, theirs = pl.ds(c * h, h), pl.ds((1 - c) * h, h)
            own.append(_remote(ins[j].at[pl.ds(layers[j], 1)], outs[j].at[:, pl.ds(k, 1)], own_send.at[j], own_recv.at[j], sibling))
            for ti, (cx, cy) in enumerate(chips):
                s = 3 * j + ti
                ici_out.append(_remote(ins[j].at[pl.ds(layers[j], 1), :, mine], outs[j].at[:, pl.ds(k, 1), mine],
                                       ici_send.at[s], ici_recv.at[s], (cx, cy, c)))
                landed = outs[j].at[:, pl.ds(2 * cx + cy, 1), mine]
                ici_in.append(_remote(landed, landed, ici_send.at[s], ici_recv.at[s], (cx, cy, c)))
                fwd_out.append(_remote(landed, landed, d2d_send.at[s], d2d_recv.at[s], sibling))
                passed = outs[j].at[:, pl.ds(2 * cx + cy, 1), theirs]
                fwd_in.append(_remote(passed, passed, d2d_send.at[s], d2d_recv.at[s], sibling))
        if extra:
            own.append(_remote(ins[n], outs[n].at[pl.ds(k, 1)], own_send.at[n], own_recv.at[n], sibling))
            for ti, (cx, cy) in enumerate(chips):
                s = 3 * n + ti
                ici_out.append(_remote(ins[n], outs[n].at[pl.ds(k, 1)], ici_send.at[s], ici_recv.at[s], (cx, cy, c)))
                slot = outs[n].at[pl.ds(2 * cx + cy, 1)]
                ici_in.append(_remote(slot, slot, ici_send.at[s], ici_recv.at[s], (cx, cy, c)))
        return own, ici_out, ici_in, fwd_out, fwd_in

    def start(ins, outs, sems):
        own, ici_out, _, _, _ = copies(ins, outs, sems)
        for cp in own + ici_out:
            cp.start()

    def relay(ins, outs, sems):
        _, _, ici_in, fwd_out, _ = copies(ins, outs, sems)
        for idx, cp in enumerate(ici_in):
            cp.wait_recv()
            if idx < len(fwd_out):
                fwd_out[idx].start()

    def finish(ins, outs, sems):
        own, ici_out, _, fwd_out, fwd_in = copies(ins, outs, sems)
        for cp in fwd_in:
            cp.wait_recv()
        for cp in ici_out + fwd_out:
            cp.wait_send()
        for cp in own:
            cp.wait()

    operands = list(shards) + ([small] if extra else [])
    out_shape = [SDS((1, N_CHIPS) + s.shape[2:], s.dtype) for s in shards]
    if extra:
        out_shape.append(SDS((N_CHIPS,) + small.shape[1:], small.dtype))
    sems = [pltpu.SemaphoreType.DMA((3 * (n + extra),)), pltpu.SemaphoreType.DMA((3 * (n + extra),)),
            pltpu.SemaphoreType.DMA((3 * n,)), pltpu.SemaphoreType.DMA((3 * n,)),
            pltpu.SemaphoreType.DMA((n + extra,)), pltpu.SemaphoreType.DMA((n + extra,))]
    return _Comm(operands, out_shape, sems, start, finish, relay)


def _swap_comm(grads):
    n = len(grads)

    def copies(ins, outs, sems):
        send_sem, recv_sem = sems
        x, y, c, _, _ = _place()
        return [_remote(ins[j].at[:, pl.ds(1 - c, 1)], outs[j], send_sem.at[j], recv_sem.at[j], (x, y, 1 - c))
                for j in range(n)]

    def start(ins, outs, sems):
        for cp in copies(ins, outs, sems):
            cp.start()

    def finish(ins, outs, sems):
        for cp in copies(ins, outs, sems):
            cp.wait()

    out_shape = [SDS((g.shape[0], 1) + g.shape[2:], g.dtype) for g in grads]
    return _Comm(list(grads), out_shape, [pltpu.SemaphoreType.DMA((n,)), pltpu.SemaphoreType.DMA((n,))], start, finish)


def _scatter_comm(sums):
    n = len(sums)

    def copies(ins, outs, sems):
        send_sem, recv_sem = sems
        x, y, c, k, chips = _place()
        out, landing = [], []
        for j in range(n):
            for ti, (cx, cy) in enumerate(chips):
                s = 3 * j + ti
                out.append(_remote(ins[j].at[:, pl.ds(2 * cx + cy, 1)], outs[j].at[:, pl.ds(k, 1)],
                                   send_sem.at[s], recv_sem.at[s], (cx, cy, c)))
                slot = outs[j].at[:, pl.ds(2 * cx + cy, 1)]
                landing.append(_remote(slot, slot, send_sem.at[s], recv_sem.at[s], (cx, cy, c)))
        return out, landing

    def start(ins, outs, sems):
        for cp in copies(ins, outs, sems)[0]:
            cp.start()

    def finish(ins, outs, sems):
        out, landing = copies(ins, outs, sems)
        for cp in landing:
            cp.wait_recv()
        for cp in out:
            cp.wait_send()

    out_shape = [SDS(s.shape, s.dtype) for s in sums]
    return _Comm(list(sums), out_shape, [pltpu.SemaphoreType.DMA((3 * n,)), pltpu.SemaphoreType.DMA((3 * n,))], start, finish)


def _swap_pieces(pieces):
    n = len(pieces)

    def body(*refs):
        bufs = refs[n:2 * n]
        send_sem, recv_sem = refs[2 * n:]
        x, y, c, _, _ = _place()
        for j in range(n):
            mine = bufs[j].at[:, pl.ds(c, 1)]
            _remote(mine, mine, send_sem.at[j], recv_sem.at[j], (x, y, 1 - c)).start()
        for j in range(n):
            mine, theirs = bufs[j].at[:, pl.ds(c, 1)], bufs[j].at[:, pl.ds(1 - c, 1)]
            _remote(mine, theirs, send_sem.at[j], recv_sem.at[j], (x, y, 1 - c)).wait()

    return pl.pallas_call(
        body, name="swap_pieces", in_specs=[ANY] * n, out_specs=[ANY] * n,
        out_shape=[SDS(p.shape, p.dtype) for p in pieces], input_output_aliases={j: j for j in range(n)},
        scratch_shapes=[pltpu.SemaphoreType.DMA((n,)), pltpu.SemaphoreType.DMA((n,))],
    )(*pieces)


def _gather_pieces(piece):
    def body(in_ref, out_ref, send_sem, recv_sem):
        x, y, c, k, chips = _place()
        peers = [(x, y, 1 - c)] + [(cx, cy, pc) for (cx, cy) in chips for pc in (c, 1 - c)]
        copies = []
        for ti, peer in enumerate(peers):
            cp = _remote(in_ref, out_ref.at[pl.ds(k, 1), pl.ds(c, 1)], send_sem.at[ti], recv_sem.at[ti], peer)
            cp.start()
            copies.append(cp)
        for ti, (px, py, pc) in enumerate(peers):
            _remote(in_ref, out_ref.at[pl.ds(2 * px + py, 1), pl.ds(pc, 1)], send_sem.at[ti], recv_sem.at[ti],
                    (px, py, pc)).wait_recv()
        for cp in copies:
            cp.wait_send()

    n_peers = 2 * N_CHIPS - 1
    return pl.pallas_call(
        body, name="gather_pieces", in_specs=[ANY], out_specs=ANY,
        out_shape=SDS((N_CHIPS, 2) + piece.shape[2:], piece.dtype),
        scratch_shapes=[pltpu.SemaphoreType.DMA((n_peers,)), pltpu.SemaphoreType.DMA((n_peers,))],
    )(piece)


def _pack_rows(parts, width, total_rows=None):
    rows = []
    for a in parts:
        a2 = a.reshape(-1, width)
        pad = (-a2.shape[0]) % SUBLANES
        rows.append(jnp.pad(a2, ((0, pad), (0, 0))) if pad else a2)
    out = jnp.concatenate(rows, axis=0)
    if total_rows is not None and out.shape[0] < total_rows:
        out = jnp.pad(out, ((0, total_rows - out.shape[0]), (0, 0)))
    return out


def _unpack_rows(packed, shapes, width):
    out, r = [], 0
    for shp in shapes:
        size = 1
        for s in shp:
            size *= s
        nr = size // width
        out.append(packed[r:r + nr].reshape(shp))
        r += nr + ((-nr) % SUBLANES)
    return out


def kernel(x, p, norm_g, w_in, w_out, conv_w, conv_b, conv_ln_g, conv_ln_b, sgu_ln_g, sgu_ln_b, sgu_w, sgu_b, pl_norm_g, pl_gate_w, pl_proj_w, final_g, loss_target, m_norm_g, m_w_in, m_w_out, m_conv_w, m_conv_b, m_conv_ln_g, m_conv_ln_b, m_sgu_ln_g, m_sgu_ln_b, m_sgu_w, m_sgu_b, m_pl_norm_g, m_pl_gate_w, m_pl_proj_w, m_final_g, v_norm_g, v_w_in, v_w_out, v_conv_w, v_conv_b, v_conv_ln_g, v_conv_ln_b, v_sgu_ln_g, v_sgu_ln_b, v_sgu_w, v_sgu_b, v_pl_norm_g, v_pl_gate_w, v_pl_proj_w, v_final_g):
    bsz, seq, d = x.shape
    depth = w_in.shape[0]
    e = w_out.shape[1] * N_CHIPS
    e3 = 3 * e
    n4 = w_in.shape[2]
    ple = p.shape[-1]
    dq = pl_proj_w.shape[2]
    k_taps = conv_w.shape[1]
    kp = k_taps + 1
    n_conv, n_sgu = conv_w.shape[0], sgu_ln_g.shape[0]
    t = bsz * seq
    tm_mm = min(512, seq)
    tm_wg, tm_wg_out = min(2048, t), min(1024, t)
    tm_mix = min(256, seq)
    my_c = lax.axis_index("c")
    my_k = 2 * lax.axis_index("x") + lax.axis_index("y")

    ec = e // N_CHIPS
    small_w = _pack_rows([conv_w.reshape(n_conv * k_taps, ec), sgu_ln_g, sgu_ln_b], ec)[None]
    shards = {"in": w_in.astype(BF16)[:, None], "out": w_out.astype(BF16)[:, None],
              "gate": pl_gate_w.astype(BF16)[:, None], "proj": pl_proj_w.astype(BF16)[:, None]}
    rest = ("out", "gate", "proj")
    gathered = {}
    gathered["in", 0], small_f = _run_comm(_gather_comm([(shards["in"], 0)], small_w), "gather_first")
    todo = [item for l in range(depth) for item in [(nm, l) for nm in rest] + ([("in", l + 1)] if l + 1 < depth else [])]

    def take_until(key):
        if key not in todo:
            return []
        n_items = todo.index(key) + 1
        items, todo[:] = todo[:n_items], todo[n_items:]
        return items

    def mixer_limit(l):
        ahead = l + 3 if l % 2 == 0 else l + 1
        return ("in", ahead) if ahead < depth else (rest[-1], depth - 1)

    def carried(keys):
        return _gather_comm([(shards[nm], ly) for nm, ly in keys]) if keys else None

    def unpack(res, keys):
        if not keys:
            return res
        gathered.update(zip(keys, res[1]))
        return res[0]

    conv_w_rows, sgu_g_rows, sgu_b_rows = _unpack_rows(
        jnp.transpose(small_f, (1, 0, 2)).reshape(small_f.shape[1], e),
        [(n_conv * k_taps, e), (n_sgu, e), (n_sgu, e)], e)
    conv_w_full = conv_w_rows.reshape(n_conv, k_taps, e)
    conv_w_pad = jnp.pad(conv_w_full, ((0, 0), (0, 1), (0, 0)))
    conv_w_fwd, conv_w_bwd = [
        jnp.transpose(conv_w_pad.reshape(n_conv, kp, e // min(cb, e), min(cb, e)), (0, 2, 1, 3)) for cb in (2 * LANES, LANES)]
    sgu_ln_g_full = sgu_g_rows.reshape(n_sgu, 1, e)
    sgu_ln_b_full = sgu_b_rows.reshape(n_sgu, 1, e)
    sgu_bt = jnp.transpose(sgu_b, (0, 2, 1))

    norm_g3 = norm_g[:, None]
    pl_norm_g3 = pl_norm_g[:, None]
    conv_b3, conv_ln_g3, conv_ln_b3 = conv_b[:, None], conv_ln_g[:, None], conv_ln_b[:, None]
    p3 = p.reshape(depth, t, ple)

    xs, hs, projs, us, x1s, y1s, weights = [], [], [], [], [], {}, []
    xc = x.reshape(t, d)
    for l in range(depth):
        j = l // 2
        xs.append(xc)
        keys = take_until(("in", 1) if depth > 1 else (rest[-1], 0)) if l == 0 else []
        h, proj = unpack(_fwd_in(xc, norm_g3, gathered["in", l], l, tm_mm, carried(keys)), keys)
        keys = take_until(mixer_limit(l))
        if l % 2 == 0:
            u, y1s[l] = unpack(_fwd_conv(proj, conv_w_fwd, conv_b3, conv_ln_g3, conv_ln_b3, j, bsz, seq, tm_mix,
                                         carried(keys)), keys)
        else:
            (u,) = unpack(_fwd_sgu(proj, sgu_ln_g_full, sgu_ln_b_full, sgu_w, sgu_bt, j, tm_mix, carried(keys)), keys)
        w_out_l, gate_l, proj_l = gathered["out", l].reshape(1, e, d), gathered["gate", l].reshape(1, d, d), gathered["proj", l]
        weights.append((gathered["in", l], w_out_l, gate_l, proj_l))
        x1, xc = _fwd_out(xc, u, w_out_l, pl_norm_g3, gate_l, p3, proj_l, l, tm_mm)
        hs.append(h)
        projs.append(proj)
        us.append(u)
        x1s.append(x1)

    loss_local, dx, d_final_g = _loss_head(xc, final_g[None], loss_target.reshape(t, d), tm_mm)
    loss = lax.psum(loss_local[0, 0], ("x", "y", "c"))

    c_arr = my_c.astype(jnp.int32).reshape(1)
    ck_arr = jnp.stack([my_c, my_k]).astype(jnp.int32)
    by_chip = lambda a: a.reshape((1, N_CHIPS) + a.shape[1:])
    d_norm_g, d_pl_norm_g = [None] * depth, [None] * depth
    d_conv = [None] * n_conv
    d_sgu = [None] * n_sgu
    locals_, siblings, arrived = [None] * depth, [None] * depth, [None] * depth
    pending = []

    def take():
        if not pending:
            return None
        kind, _, payload = pending[0]
        return _swap_comm(payload) if kind == "swap" else _scatter_comm(payload)

    def settle(res, comm):
        if comm is None:
            return res
        outs, brought = res
        kind, layer, payload = pending.pop()
        if kind == "swap":
            locals_[layer], siblings[layer] = payload, brought
            pending.append(("scatter", layer, [by_chip(wire) for wire in _pair_sums(payload, brought, c_arr, BF16)]))
        else:
            arrived[layer] = brought
        return outs

    def run_alone(name):
        comm = take()
        settle(([], _run_comm(comm, name)), comm)

    for l in reversed(range(depth)):
        j = l // 2
        w_in_l, w_out_l, gate_l, proj_l = weights[l]
        dx1, du, rn, ds, dqv, d_pl_norm_g[l] = _bwd_out(dx, x1s[l], p3, pl_norm_g3, gate_l, proj_l, w_out_l, l, tm_mm)
        (g_proj,) = _wgrad(p3, dqv, 1, 1, N_CHIPS, tm_wg, "wgrad_proj", a_layer=l)
        (g_gate,) = _wgrad(rn, ds, 1, 1, 1, tm_wg, "wgrad_gate")
        (g_out,) = _wgrad(us[l], dx1, 1, 1, 1, tm_wg_out, "wgrad_out")
        comm = take()
        if l % 2 == 0:
            dproj, dcw, dcb, dlg, dlb = settle(
                _bwd_conv(du, projs[l], y1s[l], conv_w_bwd, conv_ln_g3, conv_ln_b3, j, bsz, seq, tm_mix, comm), comm)
            d_conv[j] = (dcw, dcb, dlg, dlb)
        else:
            dproj, dsw, dsbt, dlg, dlb = settle(
                _bwd_sgu(du, projs[l], sgu_ln_g_full, sgu_ln_b_full, sgu_w, sgu_bt, j, tm_mix, comm), comm)
            d_sgu[j] = (dsw, dsbt, dlg, dlb)
        (g_in,) = _wgrad(hs[l], dproj, 1, N_CHIPS, 1, tm_wg, "wgrad_in")
        local = [g_in.reshape(N_CHIPS, 2, d // 2, n4), g_out.reshape(N_CHIPS, 2, e // (2 * N_CHIPS), d),
                 g_gate.reshape(N_CHIPS, 2, d // (2 * N_CHIPS), d), g_proj.reshape(N_CHIPS, 2, ple // 2, dq)]
        while pending:
            run_alone("reduce_step")
        pending.append(("swap", l, local))
        if l == 0:
            run_alone("swap_last")
        comm = take()
        dx, d_norm_g[l] = settle(_bwd_in(dproj, dx1, xs[l], norm_g3, w_in_l, l, tm_mm, comm), comm)
    while pending:
        run_alone("reduce_tail")
    grad_x = dx.reshape(bsz, seq, d)

    d_conv_w = jnp.stack([jnp.transpose(dc[0], (1, 0, 2)).reshape(kp, e)[:k_taps] for dc in d_conv])
    d_conv_b = jnp.stack([dc[1][0] for dc in d_conv])
    d_conv_ln_g = jnp.stack([dc[2][0] for dc in d_conv])
    d_conv_ln_b = jnp.stack([dc[3][0] for dc in d_conv])
    d_sgu_w = jnp.stack([dsg[0] for dsg in d_sgu])
    d_sgu_b = jnp.stack([jnp.transpose(dsg[1]) for dsg in d_sgu])
    d_sgu_ln_g = jnp.stack([dsg[2][0] for dsg in d_sgu])
    d_sgu_ln_b = jnp.stack([dsg[3][0] for dsg in d_sgu])
    small_grads = [jnp.concatenate(d_norm_g), d_conv_w, d_conv_b, d_conv_ln_g, d_conv_ln_b, d_sgu_ln_g, d_sgu_ln_b,
                   d_sgu_w, d_sgu_b, jnp.concatenate(d_pl_norm_g), d_final_g]
    small_shapes = [a.shape for a in small_grads]
    packed = _pack_rows(small_grads, d)
    pack_rows = packed.shape[0] + ((-packed.shape[0]) % (8 * SUBLANES))
    packed = _pack_rows(small_grads, d, pack_rows)
    gl_small = packed.reshape(N_CHIPS, 2, pack_rows // 8, d)
    (small_sibling,) = _run_comm(_swap_comm([gl_small]), "swap_small")
    small_pair = by_chip(_pair_sums([gl_small], [small_sibling], c_arr, F32)[0])

    (small_arrived,) = _run_comm(_scatter_comm([small_pair]), "scatter_small")
    reduced = [lax.empty((depth, 2) + gl.shape[2:], F32) for gl in locals_[0]]
    for l in range(depth):
        reduced = _chip_sums(locals_[l], siblings[l], arrived[l], ck_arr, reduced, l)
    (small_both,) = _chip_sums([gl_small], [small_sibling], [small_arrived], ck_arr,
                               [lax.empty((1, 2) + gl_small.shape[2:], F32)], 0)
    small_mine = lax.dynamic_slice_in_dim(small_both, my_c, 1, axis=1)
    reduced = _swap_pieces(reduced)
    small_all = _gather_pieces(small_mine)
    small_all = lax.dynamic_update_slice(small_all, small_mine, (my_k, my_c, 0, 0)).reshape(pack_rows, d)
    small_red = _unpack_rows(small_all, small_shapes, d)
    (gr_norm_g, gr_conv_w, gr_conv_b, gr_conv_ln_g, gr_conv_ln_b, gr_sgu_ln_g, gr_sgu_ln_b, gr_sgu_w, gr_sgu_b,
     gr_pl_norm_g, gr_final_g) = small_red
    gr_final_g = gr_final_g.reshape(d)
    gr_conv_w = lax.dynamic_slice_in_dim(gr_conv_w, my_k * ec, ec, axis=2)
    gr_sgu_ln_g = lax.dynamic_slice_in_dim(gr_sgu_ln_g, my_k * ec, ec, axis=1)
    gr_sgu_ln_b = lax.dynamic_slice_in_dim(gr_sgu_ln_b, my_k * ec, ec, axis=1)

    def shard_update(w, g, m, v):
        flat = lambda a: a.reshape(-1, w.shape[-1])
        return [o.reshape(w.shape) for o in _adamw(flat(w), flat(g), flat(m), flat(v))]

    up_in = shard_update(w_in, reduced[0], m_w_in, v_w_in)
    up_out = shard_update(w_out, reduced[1], m_w_out, v_w_out)
    up_gate = shard_update(pl_gate_w, reduced[2], m_pl_gate_w, v_pl_gate_w)
    up_proj = shard_update(pl_proj_w, reduced[3], m_pl_proj_w, v_pl_proj_w)

    small_names = ["norm_g", "conv_w", "conv_b", "conv_ln_g", "conv_ln_b", "sgu_ln_g", "sgu_ln_b", "sgu_w", "sgu_b",
                   "pl_norm_g", "final_g"]
    small_w_list = [norm_g, conv_w, conv_b, conv_ln_g, conv_ln_b, sgu_ln_g, sgu_ln_b, sgu_w, sgu_b, pl_norm_g, final_g]
    small_m_list = [m_norm_g, m_conv_w, m_conv_b, m_conv_ln_g, m_conv_ln_b, m_sgu_ln_g, m_sgu_ln_b, m_sgu_w, m_sgu_b,
                    m_pl_norm_g, m_final_g]
    small_v_list = [v_norm_g, v_conv_w, v_conv_b, v_conv_ln_g, v_conv_ln_b, v_sgu_ln_g, v_sgu_ln_b, v_sgu_w, v_sgu_b,
                    v_pl_norm_g, v_final_g]
    small_g_list = [gr_norm_g, gr_conv_w, gr_conv_b, gr_conv_ln_g, gr_conv_ln_b, gr_sgu_ln_g, gr_sgu_ln_b, gr_sgu_w,
                    gr_sgu_b, gr_pl_norm_g, gr_final_g]
    width = ec
    shapes_local = [a.shape for a in small_w_list]
    outs_small = _adamw(_pack_rows(small_w_list, width), _pack_rows(small_g_list, width),
                        _pack_rows(small_m_list, width), _pack_rows(small_v_list, width))
    unpacked = [_unpack_rows(o, shapes_local, width) for o in outs_small]
    ups = {name: [unpacked[kind][i] for kind in range(4)] for i, name in enumerate(small_names)}
    ups["w_in"], ups["w_out"], ups["pl_gate_w"], ups["pl_proj_w"] = up_in, up_out, up_gate, up_proj

    order = ["norm_g", "w_in", "w_out", "conv_w", "conv_b", "conv_ln_g", "conv_ln_b", "sgu_ln_g", "sgu_ln_b", "sgu_w",
             "sgu_b", "pl_norm_g", "pl_gate_w", "pl_proj_w", "final_g"]
    result = [loss, grad_x]
    for kind in range(4):
        result.extend(ups[name][kind] for name in order)
    return tuple(result)
```

```python
import functools

import jax
import jax.numpy as jnp
from jax import lax
from jax.experimental import pallas as pl
from jax.experimental.pallas import tpu as pltpu

F32 = jnp.float32
BF16 = jnp.bfloat16
SDS = jax.ShapeDtypeStruct

EPS = 1e-6
CHUNK = 128
GROUPS = 8
HALO = 32
N_CHIPS = 4
LANES = 128
SUBLANES = 8
V7X_VMEM_LIMIT = 56 << 20

ADAM_LR = 0.001
ADAM_B1 = 0.9
ADAM_B2 = 0.999
ADAM_EPS = 1e-08
ADAM_WD = 0.01
ADAM_STEP = 10

MESH_IDS = pl.DeviceIdType.MESH
ANY = pl.BlockSpec(memory_space=pl.ANY)


def _params(n_axes):
    return pltpu.CompilerParams(dimension_semantics=("arbitrary",) * n_axes, vmem_limit_bytes=V7X_VMEM_LIMIT)


def _const(shape):
    zeros = (0,) * len(shape)
    return pl.BlockSpec(shape, lambda *_: zeros)


def _layer(shape, layer):
    zeros = (0,) * len(shape)
    return pl.BlockSpec((None,) + tuple(shape), lambda *_: (layer,) + zeros, pipeline_mode=pl.Buffered(1))


class _Comm:
    def __init__(self, operands, out_shape, sems, start, finish, relay=None, relay_lead=0):
        self.operands, self.out_shape, self.sems, self.start, self.finish = operands, out_shape, sems, start, finish
        self.relay, self.relay_lead = relay, relay_lead


def _call(body, *, name, grid, in_specs, out_specs, out_shape, operands, scratch_shapes=(), comm=None):
    in_specs, out_specs, out_shape, scratch_shapes = list(in_specs), list(out_specs), list(out_shape), list(scratch_shapes)
    if comm is None:
        return pl.pallas_call(body, name=name, grid=grid, in_specs=in_specs, out_specs=out_specs, out_shape=out_shape,
                              scratch_shapes=scratch_shapes, compiler_params=_params(len(grid)))(*operands)
    n_in, n_out, n_sc = len(in_specs), len(out_specs), len(scratch_shapes)
    ci, co = len(comm.operands), len(comm.out_shape)

    def hosted(*refs):
        ins, cins = refs[:n_in], refs[n_in:n_in + ci]
        outs, couts = refs[n_in + ci:n_in + ci + n_out], refs[n_in + ci + n_out:n_in + ci + n_out + co]
        scratch = refs[n_in + ci + n_out + co:n_in + ci + n_out + co + n_sc]
        sems = refs[n_in + ci + n_out + co + n_sc:]
        n_steps = functools.reduce(lambda a, b: a * b, grid)
        step = functools.reduce(lambda acc, a: acc * grid[a] + pl.program_id(a), range(len(grid)), 0)
        first, last = step == 0, step == n_steps - 1

        @pl.when(first)
        def _():
            comm.start(cins, couts, sems)

        if comm.relay is not None:
            @pl.when(step == max(n_steps - 1 - comm.relay_lead, 0))
            def _():
                comm.relay(cins, couts, sems)

        body(*ins, *outs, *scratch)

        @pl.when(last)
        def _():
            comm.finish(cins, couts, sems)

    res = pl.pallas_call(
        hosted, name=name, grid=grid, in_specs=in_specs + [ANY] * ci, out_specs=out_specs + [ANY] * co,
        out_shape=out_shape + list(comm.out_shape), scratch_shapes=scratch_shapes + list(comm.sems),
        compiler_params=_params(len(grid)))(*operands, *comm.operands)
    return res[:n_out], res[n_out:]


def _run_comm(comm, name):
    ci, co = len(comm.operands), len(comm.out_shape)

    def body(*refs):
        comm.start(refs[:ci], refs[ci:ci + co], refs[ci + co:])
        if comm.relay is not None:
            comm.relay(refs[:ci], refs[ci:ci + co], refs[ci + co:])
        comm.finish(refs[:ci], refs[ci:ci + co], refs[ci + co:])

    return pl.pallas_call(body, name=name, in_specs=[ANY] * ci, out_specs=[ANY] * co, out_shape=list(comm.out_shape),
                          scratch_shapes=list(comm.sems))(*comm.operands)


def _sigmoid(v):
    return jax.nn.sigmoid(v)


def _dsilu(v, s):
    return s * (1.0 + v * (1.0 - s))


def _gelu_parts(v):
    cdf = 0.5 * (1.0 + lax.erf(v * 0.7071067811865476))
    pdf = jnp.exp2(v * v * -0.7213475204444817) * 0.3989422804014327
    return v * cdf, cdf + v * pdf


def _gelu(v):
    return 0.5 * v * (1.0 + lax.erf(v * 0.7071067811865476))


def _rms_stats(x):
    r = lax.rsqrt(jnp.mean(x * x, axis=-1, keepdims=True) + EPS)
    return r, x * r


def _rms_bwd(dy, g, r, xh):
    gdy = dy * g
    return r * (gdy - xh * jnp.mean(xh * gdy, axis=-1, keepdims=True))


def _ln_stats(x):
    mu = jnp.mean(x, axis=-1, keepdims=True)
    xc = x - mu
    rs = lax.rsqrt(jnp.mean(xc * xc, axis=-1, keepdims=True) + EPS)
    return rs, xc * rs


def _ln_bwd(dy, g, rs, xh):
    dxh = dy * g
    return rs * (dxh - jnp.mean(dxh, axis=-1, keepdims=True) - xh * jnp.mean(dxh * xh, axis=-1, keepdims=True))


def _dot(a, b):
    return jnp.dot(a, b, preferred_element_type=F32)


def _dot_nt(a, b):
    return lax.dot_general(a, b, (((1,), (1,)), ((), ())), preferred_element_type=F32)


def _dot_tn(a, b):
    return lax.dot_general(a, b, (((0,), (0,)), ((), ())), preferred_element_type=F32)


def _fwd_in(x, norm_g, w_in_full, layer, tm, comm=None):
    t, d = x.shape
    _, nk, _, n4 = w_in_full.shape

    def body(x_ref, g_ref, w_ref, h_ref, proj_ref):
        r, xh = _rms_stats(x_ref[...])
        h = (xh * g_ref[...]).astype(BF16)
        h_ref[...] = h
        for k in range(nk):
            proj_ref[:, k * n4:(k + 1) * n4] = _dot(h, w_ref[k]).astype(BF16)

    return _call(
        body, name="fwd_in", grid=(t // tm,),
        in_specs=[pl.BlockSpec((tm, d), lambda i: (i, 0)), _layer((1, d), layer), _layer((nk, d, n4), 0)],
        out_specs=[pl.BlockSpec((tm, d), lambda i: (i, 0)), pl.BlockSpec((tm, nk * n4), lambda i: (i, 0))],
        out_shape=[SDS((t, d), BF16), SDS((t, nk * n4), BF16)],
        operands=(x, norm_g, w_in_full), comm=comm)


def _halo_maps(nt, hb, n_halo_blocks):
    def prev(b, i):
        return (jnp.maximum((b * nt + i) * hb - 1, 0), 0)

    def nxt(b, i):
        return (jnp.minimum((b * nt + i + 1) * hb, n_halo_blocks - 1), 0)

    return prev, nxt


TAP_TILE_VREGS = 16


def _tap_rows(cb, tm):
    return min(TAP_TILE_VREGS * SUBLANES * LANES // cb, tm)


def _conv_taps(src_ref, w_ref, dst_ref, cb_idx, n_rows, rb, first, reverse):
    k_taps = w_ref.shape[1] - 1
    for r0 in range(0, n_rows, rb):
        acc = None
        for res in range(SUBLANES):
            rows = rb + (SUBLANES if res else 0)
            group = None
            for k in range(k_taps):
                off = first + k
                if off % SUBLANES != res:
                    continue
                wk = w_ref[cb_idx, pl.ds((k_taps - 1 - k) if reverse else k, 1), :]
                term = wk * src_ref[cb_idx, pl.ds(r0 + off - res, rows), :]
                group = term if group is None else group + term
            if group is None:
                continue
            part = group[res:res + rb] if res else group
            acc = part if acc is None else acc + part
        dst_ref[cb_idx, pl.ds(r0, rb), :] = acc


def _fwd_conv(proj, conv_w_blk, conv_b, ln_g, ln_b, layer, bsz, seq, tm, comm=None):
    t, e3 = proj.shape
    e = e3 // 3
    nt = seq // tm
    hb = tm // HALO
    _, ncb, kp, cb = conv_w_blk.shape
    rb = _tap_rows(cb, tm)
    prev, _ = _halo_maps(nt, hb, t // HALO)

    def body(proj_ref, halo_ref, w_ref, b_ref, g_ref, lb_ref, u_ref, y1_ref, y0s, y1s):
        i = pl.program_id(1)
        a = proj_ref[:, 0:e].astype(F32)
        b = proj_ref[:, e:2 * e].astype(F32)
        y0 = a * _sigmoid(b)
        ah = halo_ref[:, 0:e].astype(F32)
        bh = halo_ref[:, e:2 * e].astype(F32)
        y0h = jnp.where(i > 0, ah * _sigmoid(bh), 0.0)
        for c in range(ncb):
            y0s[c, 0:HALO, :] = y0h[:, c * cb:(c + 1) * cb]
            y0s[c, HALO:HALO + tm, :] = y0[:, c * cb:(c + 1) * cb]

        def per_block(c, carry):
            _conv_taps(y0s, w_ref, y1s, c, tm, rb, HALO - (kp - 2), False)
            return carry

        lax.fori_loop(0, ncb, per_block, 0)
        y1 = jnp.concatenate([y1s[c] for c in range(ncb)], axis=1) + b_ref[...]
        y1_ref[...] = y1
        rs, xh = _ln_stats(y1)
        y2 = xh * g_ref[...] + lb_ref[...]
        y = y2 * _sigmoid(y2)
        z = proj_ref[:, 2 * e:3 * e].astype(F32)
        u_ref[...] = (y * (z * _sigmoid(z))).astype(BF16)

    return _call(
        body, name="fwd_conv", grid=(bsz, nt),
        in_specs=[pl.BlockSpec((tm, e3), lambda b, i: (b * nt + i, 0)),
                  pl.BlockSpec((HALO, 2 * e), prev),
                  _layer((ncb, kp, cb), layer), _layer((1, e), layer), _layer((1, e), layer), _layer((1, e), layer)],
        out_specs=[pl.BlockSpec((tm, e), lambda b, i: (b * nt + i, 0)), pl.BlockSpec((tm, e), lambda b, i: (b * nt + i, 0))],
        out_shape=[SDS((t, e), BF16), SDS((t, e), F32)],
        scratch_shapes=[pltpu.VMEM((ncb, HALO + tm, cb), F32), pltpu.VMEM((ncb, tm, cb), F32)],
        operands=(proj, proj, conv_w_blk, conv_b, ln_g, ln_b), comm=comm)


def _tril_mask():
    rows = lax.broadcasted_iota(jnp.int32, (CHUNK, CHUNK), 0)
    cols = lax.broadcasted_iota(jnp.int32, (CHUNK, CHUNK), 1)
    return rows >= cols


def _fwd_sgu(proj, ln_g, ln_b, sgu_w, sgu_bt, layer, tm, comm=None):
    t, e3 = proj.shape
    e = e3 // 3
    gw = e // GROUPS
    nch = tm // CHUNK

    def body(proj_ref, g_ref, lb_ref, w_ref, bt_ref, u_ref, mixed):
        a = proj_ref[:, 0:e].astype(F32)
        b = proj_ref[:, e:2 * e].astype(F32)
        z = proj_ref[:, 2 * e:3 * e].astype(F32)
        rs, xh = _ln_stats(_gelu(b))
        v = (xh * g_ref[...] + lb_ref[...]).astype(BF16)
        mask = _tril_mask()
        for g in range(GROUPS):
            wm = jnp.where(mask, w_ref[g], 0.0).astype(BF16)
            bias = bt_ref[:, g:g + 1]
            for n in range(nch):
                blk = v[n * CHUNK:(n + 1) * CHUNK, g * gw:(g + 1) * gw]
                mixed[n * CHUNK:(n + 1) * CHUNK, g * gw:(g + 1) * gw] = _dot(wm, blk) + bias
        y = _gelu(a) * mixed[...]
        u_ref[...] = (y * (z * _sigmoid(z))).astype(BF16)

    return _call(
        body, name="fwd_sgu", grid=(t // tm,),
        in_specs=[pl.BlockSpec((tm, e3), lambda i: (i, 0)), _layer((1, e), layer), _layer((1, e), layer),
                  _layer((GROUPS, CHUNK, CHUNK), layer), _layer((CHUNK, GROUPS), layer)],
        out_specs=[pl.BlockSpec((tm, e), lambda i: (i, 0))],
        out_shape=[SDS((t, e), BF16)],
        scratch_shapes=[pltpu.VMEM((tm, e), F32)],
        operands=(proj, ln_g, ln_b, sgu_w, sgu_bt), comm=comm)


def _ple_forward(x1, p_ref, plg_ref, gw_ref, pw_ref):
    nk, _, dq = pw_ref.shape
    r, xh = _rms_stats(x1)
    rn = (xh * plg_ref[...]).astype(BF16)
    gate = _sigmoid(_dot(rn, gw_ref[...]))
    pb = p_ref[...].astype(BF16)
    q = jnp.concatenate([_dot(pb, pw_ref[k]) for k in range(nk)], axis=1)
    return r, xh, rn, gate, q


def _fwd_out(x, u, w_out_full, pl_norm_g, gate_w_full, p, proj_w_full, layer, tm):
    t, d = x.shape
    e = u.shape[1]
    ple = p.shape[-1]
    nk, dq = proj_w_full.shape[1], proj_w_full.shape[3]

    def body(x_ref, u_ref, wo_ref, plg_ref, gw_ref, p_ref, pw_ref, x1_ref, x2_ref):
        x1 = x_ref[...] + _dot(u_ref[...], wo_ref[...])
        x1_ref[...] = x1
        _, _, _, gate, q = _ple_forward(x1, p_ref, plg_ref, gw_ref, pw_ref)
        x2_ref[...] = x1 + gate * q

    return _call(
        body, name="fwd_out", grid=(t // tm,),
        in_specs=[pl.BlockSpec((tm, d), lambda i: (i, 0)), pl.BlockSpec((tm, e), lambda i: (i, 0)),
                  _layer((e, d), 0), _layer((1, d), layer), _layer((d, d), 0),
                  pl.BlockSpec((None, tm, ple), lambda i: (layer, i, 0)), _layer((nk, ple, dq), 0)],
        out_specs=[pl.BlockSpec((tm, d), lambda i: (i, 0)), pl.BlockSpec((tm, d), lambda i: (i, 0))],
        out_shape=[SDS((t, d), F32), SDS((t, d), F32)],
        operands=(x, u, w_out_full, pl_norm_g, gate_w_full, p, proj_w_full))


def _loss_head(x, final_g, target, tm):
    t, d = x.shape
    n_steps = t // tm

    def body(x_ref, g_ref, tgt_ref, loss_ref, dx_ref, dg_ref, sq_acc):
        i = pl.program_id(0)

        @pl.when(i == 0)
        def _():
            sq_acc[...] = jnp.zeros_like(sq_acc)
            dg_ref[...] = jnp.zeros_like(dg_ref)

        g = g_ref[...]
        r, xh = _rms_stats(x_ref[...])
        diff = xh * g - tgt_ref[...]
        sq_acc[...] += jnp.sum(diff * diff, axis=0, keepdims=True)
        dout = diff * (1.0 / d)
        dg_ref[...] += jnp.sum(dout * xh, axis=0, keepdims=True)
        dx_ref[...] = _rms_bwd(dout, g, r, xh)

        @pl.when(i == n_steps - 1)
        def _():
            loss_ref[...] = jnp.sum(sq_acc[...], axis=1, keepdims=True) * (0.5 / d)

    return pl.pallas_call(
        body, name="loss_head", grid=(n_steps,),
        in_specs=[pl.BlockSpec((tm, d), lambda i: (i, 0)), _const((1, d)), pl.BlockSpec((tm, d), lambda i: (i, 0))],
        out_specs=[_const((1, 1)), pl.BlockSpec((tm, d), lambda i: (i, 0)), _const((1, d))],
        out_shape=[SDS((1, 1), F32), SDS((t, d), F32), SDS((1, d), F32)],
        scratch_shapes=[pltpu.VMEM((1, d), F32)],
        compiler_params=_params(1),
    )(x, final_g, target)


def _bwd_out(dx2, x1, p, pl_norm_g, gate_w_full, proj_w_full, w_out_full, layer, tm):
    t, d = dx2.shape
    e = w_out_full.shape[1]
    ple = p.shape[-1]
    nk, dq_w = proj_w_full.shape[1], proj_w_full.shape[3]

    def body(dx2_ref, x1_ref, p_ref, plg_ref, gw_ref, pw_ref, wo_ref, dx1_ref, du_ref, rn_ref, ds_ref, dq_ref, dplg_ref):
        @pl.when(pl.program_id(0) == 0)
        def _():
            dplg_ref[...] = jnp.zeros_like(dplg_ref)

        dx2v = dx2_ref[...]
        r, xh, rn, gate, q = _ple_forward(x1_ref[...], p_ref, plg_ref, gw_ref, pw_ref)
        rn_ref[...] = rn
        dq_ref[...] = (dx2v * gate).astype(BF16)
        ds = (dx2v * q * gate * (1.0 - gate)).astype(BF16)
        ds_ref[...] = ds
        dr = _dot_nt(ds, gw_ref[...])
        dplg_ref[...] += jnp.sum(dr * xh, axis=0, keepdims=True)
        dx1 = dx2v + _rms_bwd(dr, plg_ref[...], r, xh)
        dx1_ref[...] = dx1
        du_ref[...] = _dot_nt(dx1.astype(BF16), wo_ref[...]).astype(BF16)

    row = lambda w: pl.BlockSpec((tm, w), lambda i: (i, 0))
    return _call(
        body, name="bwd_out", grid=(t // tm,),
        in_specs=[row(d), row(d), pl.BlockSpec((None, tm, ple), lambda i: (layer, i, 0)),
                  _layer((1, d), layer), _layer((d, d), 0), _layer((nk, ple, dq_w), 0), _layer((e, d), 0)],
        out_specs=[row(d), row(e), row(d), row(d), row(d), _const((1, d))],
        out_shape=[SDS((t, d), F32), SDS((t, e), BF16), SDS((t, d), BF16), SDS((t, d), BF16), SDS((t, d), BF16),
                   SDS((1, d), F32)],
        operands=(dx2, x1, p, pl_norm_g, gate_w_full, proj_w_full, w_out_full))


def _bwd_conv(du, proj, y1, conv_w_blk, ln_g, ln_b, layer, bsz, seq, tm, comm=None):
    t, e3 = proj.shape
    e = e3 // 3
    nt = seq // tm
    hb = tm // HALO
    _, ncb, kp, cb = conv_w_blk.shape
    rb = _tap_rows(cb, tm)
    k_taps = kp - 1
    prev, nxt = _halo_maps(nt, hb, t // HALO)
    z_halo = lambda b, i: (nxt(b, i)[0], 2)

    def ln_silu_bwd(du_v, z_v, y1_v, g, lb):
        rs, xh = _ln_stats(y1_v)
        y2 = xh * g + lb
        sg = _sigmoid(y2)
        sz = _sigmoid(z_v)
        dy = du_v * (z_v * sz)
        dy2 = dy * _dsilu(y2, sg)
        return _ln_bwd(dy2, g, rs, xh), dy2, xh, du_v * (y2 * sg) * _dsilu(z_v, sz)

    def body(du_ref, proj_ref, y1_ref, duh_ref, zh_ref, y1h_ref, abh_ref, w_ref, g_ref, lb_ref,
             dproj_ref, dw_ref, dcb_ref, dg_ref, dlb_ref, y0s, dy1s, dy0s, ysh):
        b_id, i = pl.program_id(0), pl.program_id(1)

        @pl.when((b_id == 0) & (i == 0))
        def _():
            dw_ref[...] = jnp.zeros_like(dw_ref)
            dcb_ref[...] = jnp.zeros_like(dcb_ref)
            dg_ref[...] = jnp.zeros_like(dg_ref)
            dlb_ref[...] = jnp.zeros_like(dlb_ref)

        g, lb = g_ref[...], lb_ref[...]
        a = proj_ref[:, 0:e].astype(F32)
        b = proj_ref[:, e:2 * e].astype(F32)
        z = proj_ref[:, 2 * e:3 * e].astype(F32)
        sb = _sigmoid(b)
        y0 = a * sb
        dy1, dy2, xh, dz = ln_silu_bwd(du_ref[...].astype(F32), z, y1_ref[...], g, lb)
        dproj_ref[:, 2 * e:3 * e] = dz.astype(BF16)
        dg_ref[...] += jnp.sum(dy2 * xh, axis=0, keepdims=True)
        dlb_ref[...] += jnp.sum(dy2, axis=0, keepdims=True)
        dcb_ref[...] += jnp.sum(dy1, axis=0, keepdims=True)
        dy1h, _, _, _ = ln_silu_bwd(duh_ref[...].astype(F32), zh_ref[...].astype(F32), y1h_ref[...], g, lb)
        dy1h = jnp.where(i < nt - 1, dy1h, 0.0)
        ah = abh_ref[:, 0:e].astype(F32)
        bh = abh_ref[:, e:2 * e].astype(F32)
        y0h = jnp.where(i > 0, ah * _sigmoid(bh), 0.0)
        for c in range(ncb):
            cols = slice(c * cb, (c + 1) * cb)
            y0s[c, 0:HALO, :] = y0h[:, cols]
            y0s[c, HALO:HALO + tm, :] = y0[:, cols]
            dy1s[c, 0:tm, :] = dy1[:, cols]
            dy1s[c, tm:tm + HALO, :] = dy1h[:, cols]

        def per_block(c, carry):
            _conv_taps(dy1s, w_ref, dy0s, c, tm, rb, 0, True)
            for res in range(1, SUBLANES):
                ysh[res - 1] = y0s[c, pl.ds(res, tm + HALO - SUBLANES), :]
            for k in range(k_taps):
                off = HALO - (k_taps - 1) + k
                res = off % SUBLANES
                acc = None
                for r0 in range(0, tm, rb):
                    rows = pl.ds(r0 + off - res, rb)
                    shifted = ysh[res - 1, rows, :] if res else y0s[c, rows, :]
                    term = dy1s[c, pl.ds(r0, rb), :] * shifted
                    acc = term if acc is None else acc + term
                dw_ref[c, pl.ds(k, 1), :] += jnp.sum(acc, axis=0, keepdims=True)
            return carry

        lax.fori_loop(0, ncb, per_block, 0)
        dy0 = jnp.concatenate([dy0s[c] for c in range(ncb)], axis=1)
        dproj_ref[:, 0:e] = (dy0 * sb).astype(BF16)
        dproj_ref[:, e:2 * e] = (dy0 * a * sb * (1.0 - sb)).astype(BF16)

    tile = lambda w: pl.BlockSpec((tm, w), lambda b, i: (b * nt + i, 0))
    return _call(
        body, name="bwd_conv", grid=(bsz, nt),
        in_specs=[tile(e), tile(e3), tile(e),
                  pl.BlockSpec((HALO, e), nxt), pl.BlockSpec((HALO, e), z_halo), pl.BlockSpec((HALO, e), nxt),
                  pl.BlockSpec((HALO, 2 * e), prev),
                  _layer((ncb, kp, cb), layer), _layer((1, e), layer), _layer((1, e), layer)],
        out_specs=[tile(e3), _const((ncb, kp, cb)), _const((1, e)), _const((1, e)), _const((1, e))],
        out_shape=[SDS((t, e3), BF16), SDS((ncb, kp, cb), F32), SDS((1, e), F32), SDS((1, e), F32), SDS((1, e), F32)],
        scratch_shapes=[pltpu.VMEM((ncb, HALO + tm, cb), F32), pltpu.VMEM((ncb, tm + HALO, cb), F32),
                        pltpu.VMEM((ncb, tm, cb), F32), pltpu.VMEM((SUBLANES - 1, tm + HALO - SUBLANES, cb), F32)],
        operands=(du, proj, y1, du, proj, y1, proj, conv_w_blk, ln_g, ln_b), comm=comm)


def _bwd_sgu(du, proj, ln_g, ln_b, sgu_w, sgu_bt, layer, tm, comm=None):
    t, e3 = proj.shape
    e = e3 // 3
    gw = e // GROUPS
    nch = tm // CHUNK

    def body(du_ref, proj_ref, g_ref, lb_ref, w_ref, bt_ref, dproj_ref, dw_ref, dbt_ref, dg_ref, dlb_ref, mixed, dmix, dv):
        @pl.when(pl.program_id(0) == 0)
        def _():
            dw_ref[...] = jnp.zeros_like(dw_ref)
            dbt_ref[...] = jnp.zeros_like(dbt_ref)
            dg_ref[...] = jnp.zeros_like(dg_ref)
            dlb_ref[...] = jnp.zeros_like(dlb_ref)

        g, lb = g_ref[...], lb_ref[...]
        a = proj_ref[:, 0:e].astype(F32)
        b = proj_ref[:, e:2 * e].astype(F32)
        z = proj_ref[:, 2 * e:3 * e].astype(F32)
        ug, dug = _gelu_parts(a)
        vb, dvb_db = _gelu_parts(b)
        rs, xh = _ln_stats(vb)
        v = (xh * g + lb).astype(BF16)
        mask = _tril_mask()
        for gi in range(GROUPS):
            wm = jnp.where(mask, w_ref[gi], 0.0).astype(BF16)
            bias = bt_ref[:, gi:gi + 1]
            for c in range(nch):
                blk = v[c * CHUNK:(c + 1) * CHUNK, gi * gw:(gi + 1) * gw]
                mixed[c * CHUNK:(c + 1) * CHUNK, gi * gw:(gi + 1) * gw] = _dot(wm, blk) + bias
        mx = mixed[...]
        sz = _sigmoid(z)
        duv = du_ref[...].astype(F32)
        dy = duv * (z * sz)
        dproj_ref[:, 2 * e:3 * e] = (duv * (ug * mx) * _dsilu(z, sz)).astype(BF16)
        dproj_ref[:, 0:e] = (dy * mx * dug).astype(BF16)
        dmix[...] = dy * ug
        for gi in range(GROUPS):
            wm = jnp.where(mask, w_ref[gi], 0.0).astype(BF16)
            dw_acc = None
            db_acc = None
            for c in range(nch):
                rows, cols = slice(c * CHUNK, (c + 1) * CHUNK), slice(gi * gw, (gi + 1) * gw)
                dm = dmix[rows, cols]
                dmb = dm.astype(BF16)
                dw_n = _dot_nt(dmb, v[rows, cols])
                db_n = jnp.sum(dm, axis=1, keepdims=True)
                dw_acc = dw_n if dw_acc is None else dw_acc + dw_n
                db_acc = db_n if db_acc is None else db_acc + db_n
                dv[rows, cols] = _dot_tn(wm, dmb)
            dw_ref[gi] += jnp.where(mask, dw_acc, 0.0)
            dbt_ref[:, gi:gi + 1] += db_acc
        dvv = dv[...]
        dg_ref[...] += jnp.sum(dvv * xh, axis=0, keepdims=True)
        dlb_ref[...] += jnp.sum(dvv, axis=0, keepdims=True)
        dproj_ref[:, e:2 * e] = (_ln_bwd(dvv, g, rs, xh) * dvb_db).astype(BF16)

    return _call(
        body, name="bwd_sgu", grid=(t // tm,),
        in_specs=[pl.BlockSpec((tm, e), lambda i: (i, 0)), pl.BlockSpec((tm, e3), lambda i: (i, 0)),
                  _layer((1, e), layer), _layer((1, e), layer),
                  _layer((GROUPS, CHUNK, CHUNK), layer), _layer((CHUNK, GROUPS), layer)],
        out_specs=[pl.BlockSpec((tm, e3), lambda i: (i, 0)), _const((GROUPS, CHUNK, CHUNK)), _const((CHUNK, GROUPS)),
                   _const((1, e)), _const((1, e))],
        out_shape=[SDS((t, e3), BF16), SDS((GROUPS, CHUNK, CHUNK), F32), SDS((CHUNK, GROUPS), F32),
                   SDS((1, e), F32), SDS((1, e), F32)],
        scratch_shapes=[pltpu.VMEM((tm, e), F32), pltpu.VMEM((tm, e), F32), pltpu.VMEM((tm, e), F32)],
        operands=(du, proj, ln_g, ln_b, sgu_w, sgu_bt), comm=comm)


def _bwd_in(dproj, dx1, x, norm_g, w_in_full, layer, tm, comm=None):
    t, d = x.shape
    _, nk, _, n4 = w_in_full.shape

    def body(dproj_ref, dx1_ref, x_ref, g_ref, w_ref, dx_ref, dg_ref):
        @pl.when(pl.program_id(0) == 0)
        def _():
            dg_ref[...] = jnp.zeros_like(dg_ref)

        dh = None
        for k in range(nk):
            part = _dot_nt(dproj_ref[:, k * n4:(k + 1) * n4], w_ref[k])
            dh = part if dh is None else dh + part
        r, xh = _rms_stats(x_ref[...])
        dg_ref[...] += jnp.sum(dh * xh, axis=0, keepdims=True)
        dx_ref[...] = dx1_ref[...] + _rms_bwd(dh, g_ref[...], r, xh)

    row = lambda w: pl.BlockSpec((tm, w), lambda i: (i, 0))
    return _call(
        body, name="bwd_in", grid=(t // tm,),
        in_specs=[row(nk * n4), row(d), row(d), _layer((1, d), layer), _layer((nk, d, n4), 0)],
        out_specs=[row(d), _const((1, d))],
        out_shape=[SDS((t, d), F32), SDS((1, d), F32)],
        operands=(dproj, dx1, x, norm_g, w_in_full), comm=comm)


def _wgrad(a, b, kblk, nblk, n_split, tm, name, a_layer=None, comm=None):
    t, n = b.shape
    k = a.shape[-1]
    kw, nw = k // kblk, n // nblk
    nws = nw // n_split
    n_steps = t // tm

    def body(a_ref, b_ref, o_ref):
        @pl.when(pl.program_id(2) == 0)
        def _():
            o_ref[...] = jnp.zeros_like(o_ref)

        res = _dot_tn(a_ref[...].astype(BF16), b_ref[...].astype(BF16))
        for s in range(n_split):
            o_ref[s] += res[:, s * nws:(s + 1) * nws]

    if a_layer is None:
        a_spec = pl.BlockSpec((tm, kw), lambda kb, nb, i: (i, kb))
    else:
        a_spec = pl.BlockSpec((None, tm, kw), lambda kb, nb, i: (a_layer, i, kb))
    return _call(
        body, name=name, grid=(kblk, nblk, n_steps),
        in_specs=[a_spec, pl.BlockSpec((tm, nw), lambda kb, nb, i: (i, nb))],
        out_specs=[pl.BlockSpec((None, n_split, kw, nws), lambda kb, nb, i: (kb, nb, 0, 0))],
        out_shape=[SDS((kblk, nblk * n_split, kw, nws), F32)],
        operands=(a, b), comm=comm)


def _row_tile(rows, cols, budget_bytes=2 << 20):
    best = None
    for cand in range(SUBLANES, rows + 1, SUBLANES):
        if rows % cand == 0 and cand * cols * 4 <= budget_bytes:
            best = cand
    return best if best is not None else rows


ROW_CHUNKS = 2


def _pair_sums(grads, recv, my_c, wire_dtype):
    n = len(grads)
    shapes = [(g.shape[2] // ROW_CHUNKS, g.shape[3]) for g in grads]

    def body(c_ref, *refs):
        for j in range(n):
            refs[2 * n + j][...] = (refs[2 * j][...] + refs[2 * j + 1][...]).astype(wire_dtype)

    in_specs, out_specs = [], []
    for th, c in shapes:
        in_specs.append(pl.BlockSpec((None, None, th, c), lambda k, i, c_ref: (k, c_ref[0], i, 0)))
        in_specs.append(pl.BlockSpec((None, None, th, c), lambda k, i, c_ref: (k, 0, i, 0)))
        out_specs.append(pl.BlockSpec((None, th, c), lambda k, i, c_ref: (k, i, 0)))
    grid_spec = pltpu.PrefetchScalarGridSpec(num_scalar_prefetch=1, grid=(N_CHIPS, ROW_CHUNKS),
                                             in_specs=in_specs, out_specs=out_specs)
    operands = [a for pair in zip(grads, recv) for a in pair]
    return pl.pallas_call(body, name="pair_sums", grid_spec=grid_spec,
                          out_shape=[SDS((N_CHIPS,) + g.shape[2:], wire_dtype) for g in grads],
                          compiler_params=_params(2))(my_c, *operands)


def _chip_sums(grads, recv, arrived, my_ck, stacked, layer):
    n = len(grads)
    shapes = [(g.shape[2] // ROW_CHUNKS, g.shape[3]) for g in grads]

    def body(ck_ref, *refs):
        for j in range(n):
            g_ref, r_ref, a1_ref, a2_ref, a3_ref = refs[5 * j:5 * j + 5]
            acc = g_ref[...] + r_ref[...]
            for ref in (a1_ref, a2_ref, a3_ref):
                acc = acc + ref[...].astype(F32)
            refs[6 * n + j][...] = acc

    in_specs, out_specs = [], []
    for th, c in shapes:
        block = (None, None, th, c)
        in_specs.append(pl.BlockSpec(block, lambda i, ck: (ck[1], ck[0], i, 0)))
        in_specs.append(pl.BlockSpec(block, lambda i, ck: (ck[1], 0, i, 0)))
        for flip in (1, 2, 3):
            in_specs.append(pl.BlockSpec(block, lambda i, ck, flip=flip: (0, jnp.bitwise_xor(ck[1], flip), i, 0)))
        out_specs.append(pl.BlockSpec(block, lambda i, ck: (layer, ck[0], i, 0)))
    grid_spec = pltpu.PrefetchScalarGridSpec(num_scalar_prefetch=1, grid=(ROW_CHUNKS,),
                                             in_specs=in_specs + [ANY] * n, out_specs=out_specs)
    operands = [a for g, r, ar in zip(grads, recv, arrived) for a in (g, r, ar, ar, ar)]
    return pl.pallas_call(body, name="chip_sums", grid_spec=grid_spec, out_shape=[SDS(s.shape, F32) for s in stacked],
                          input_output_aliases={1 + 5 * n + j: j for j in range(n)},
                          compiler_params=_params(1))(my_ck, *operands, *stacked)


def _adam_math(w, gv, m, v):
    c1 = 1.0 - ADAM_B1 ** ADAM_STEP
    c2 = 1.0 - ADAM_B2 ** ADAM_STEP
    mn = ADAM_B1 * m + (1.0 - ADAM_B1) * gv
    vn = ADAM_B2 * v + (1.0 - ADAM_B2) * (gv * gv)
    m_hat = mn / c1
    v_hat = vn / c2
    return -ADAM_LR * (m_hat / (jnp.sqrt(v_hat) + ADAM_EPS) + ADAM_WD * w), mn, vn


def _adamw(w, g, m, v):
    rows, cols = w.shape
    tr = _row_tile(rows, cols)

    def body(w_ref, g_ref, m_ref, v_ref, go_ref, d_ref, mo_ref, vo_ref):
        gv = g_ref[...]
        go_ref[...] = gv
        d_ref[...], mo_ref[...], vo_ref[...] = _adam_math(w_ref[...], gv, m_ref[...], v_ref[...])

    spec = pl.BlockSpec((tr, cols), lambda i: (i, 0))
    return pl.pallas_call(
        body, name="adamw", grid=(rows // tr,), in_specs=[spec] * 4, out_specs=[spec] * 4,
        out_shape=[SDS((rows, cols), F32)] * 4, compiler_params=_params(1))(w, g, m, v)


def _place():
    x, y, c = lax.axis_index("x"), lax.axis_index("y"), lax.axis_index("c")
    chips = [(1 - x, y), (x, 1 - y), (1 - x, 1 - y)]
    return x, y, c, 2 * x + y, chips


def _remote(src, dst, send_sem, recv_sem, device):
    return pltpu.make_async_remote_copy(src_ref=src, dst_ref=dst, send_sem=send_sem, recv_sem=recv_sem,
                                        device_id=device, device_id_type=MESH_IDS)


def _gather_comm(items, small=None, relay_lead=0):
    shards = [arr for arr, _ in items]
    layers = [layer for _, layer in items]
    n = len(shards)
    extra = 0 if small is None else 1

    def copies(ins, outs, sems):
        ici_send, ici_recv, d2d_send, d2d_recv, own_send, own_recv = sems
        x, y, c, k, chips = _place()
        sibling = (x, y, 1 - c)
        own, ici_out, ici_in, fwd_out, fwd_in = [], [], [], [], []
        for j in range(n):
            h = ins[j].shape[2] // 2
            mine, theirs = pl.ds(c * h, h), pl.ds((1 - c) * h, h)
            own.append(_remote(ins[j].at[pl.ds(layers[j], 1)], outs[j].at[:, pl.ds(k, 1)], own_send.at[j], own_recv.at[j], sibling))
            for ti, (cx, cy) in enumerate(chips):
                s = 3 * j + ti
                ici_out.append(_remote(ins[j].at[pl.ds(layers[j], 1), :, mine], outs[j].at[:, pl.ds(k, 1), mine],
                                       ici_send.at[s], ici_recv.at[s], (cx, cy, c)))
                landed = outs[j].at[:, pl.ds(2 * cx + cy, 1), mine]
                ici_in.append(_remote(landed, landed, ici_send.at[s], ici_recv.at[s], (cx, cy, c)))
                fwd_out.append(_remote(landed, landed, d2d_send.at[s], d2d_recv.at[s], sibling))
                passed = outs[j].at[:, pl.ds(2 * cx + cy, 1), theirs]
                fwd_in.append(_remote(passed, passed, d2d_send.at[s], d2d_recv.at[s], sibling))
        if extra:
            own.append(_remote(ins[n], outs[n].at[pl.ds(k, 1)], own_send.at[n], own_recv.at[n], sibling))
            for ti, (cx, cy) in enumerate(chips):
                s = 3 * n + ti
                ici_out.append(_remote(ins[n], outs[n].at[pl.ds(k, 1)], ici_send.at[s], ici_recv.at[s], (cx, cy, c)))
                slot = outs[n].at[pl.ds(2 * cx + cy, 1)]
                ici_in.append(_remote(slot, slot, ici_send.at[s], ici_recv.at[s], (cx, cy, c)))
        return own, ici_out, ici_in, fwd_out, fwd_in

    def start(ins, outs, sems):
        own, ici_out, _, _, _ = copies(ins, outs, sems)
        for cp in own + ici_out:
            cp.start()

    def relay(ins, outs, sems):
        _, _, ici_in, fwd_out, _ = copies(ins, outs, sems)
        for idx, cp in enumerate(ici_in):
            cp.wait_recv()
            if idx < len(fwd_out):
                fwd_out[idx].start()

    def finish(ins, outs, sems):
        own, ici_out, _, fwd_out, fwd_in = copies(ins, outs, sems)
        for cp in fwd_in:
            cp.wait_recv()
        for cp in ici_out + fwd_out:
            cp.wait_send()
        for cp in own:
            cp.wait()

    operands = list(shards) + ([small] if extra else [])
    out_shape = [SDS((1, N_CHIPS) + s.shape[2:], s.dtype) for s in shards]
    if extra:
        out_shape.append(SDS((N_CHIPS,) + small.shape[1:], small.dtype))
    sems = [pltpu.SemaphoreType.DMA((3 * (n + extra),)), pltpu.SemaphoreType.DMA((3 * (n + extra),)),
            pltpu.SemaphoreType.DMA((3 * n,)), pltpu.SemaphoreType.DMA((3 * n,)),
            pltpu.SemaphoreType.DMA((n + extra,)), pltpu.SemaphoreType.DMA((n + extra,))]
    return _Comm(operands, out_shape, sems, start, finish, relay, relay_lead)


def _swap_comm(grads):
    n = len(grads)

    def copies(ins, outs, sems):
        send_sem, recv_sem = sems
        x, y, c, _, _ = _place()
        return [_remote(ins[j].at[:, pl.ds(1 - c, 1)], outs[j], send_sem.at[j], recv_sem.at[j], (x, y, 1 - c))
                for j in range(n)]

    def start(ins, outs, sems):
        for cp in copies(ins, outs, sems):
            cp.start()

    def finish(ins, outs, sems):
        for cp in copies(ins, outs, sems):
            cp.wait()

    out_shape = [SDS((g.shape[0], 1) + g.shape[2:], g.dtype) for g in grads]
    return _Comm(list(grads), out_shape, [pltpu.SemaphoreType.DMA((n,)), pltpu.SemaphoreType.DMA((n,))], start, finish)


def _scatter_comm(sums):
    n = len(sums)

    def copies(ins, outs, sems):
        send_sem, recv_sem = sems
        x, y, c, k, chips = _place()
        out, landing = [], []
        for j in range(n):
            for ti, (cx, cy) in enumerate(chips):
                s = 3 * j + ti
                out.append(_remote(ins[j].at[:, pl.ds(2 * cx + cy, 1)], outs[j].at[:, pl.ds(k, 1)],
                                   send_sem.at[s], recv_sem.at[s], (cx, cy, c)))
                slot = outs[j].at[:, pl.ds(2 * cx + cy, 1)]
                landing.append(_remote(slot, slot, send_sem.at[s], recv_sem.at[s], (cx, cy, c)))
        return out, landing

    def start(ins, outs, sems):
        for cp in copies(ins, outs, sems)[0]:
            cp.start()

    def finish(ins, outs, sems):
        out, landing = copies(ins, outs, sems)
        for cp in landing:
            cp.wait_recv()
        for cp in out:
            cp.wait_send()

    out_shape = [SDS(s.shape, s.dtype) for s in sums]
    return _Comm(list(sums), out_shape, [pltpu.SemaphoreType.DMA((3 * n,)), pltpu.SemaphoreType.DMA((3 * n,))], start, finish)


def _swap_pieces(pieces):
    n = len(pieces)

    def body(*refs):
        bufs = refs[n:2 * n]
        send_sem, recv_sem = refs[2 * n:]
        x, y, c, _, _ = _place()
        for j in range(n):
            mine = bufs[j].at[:, pl.ds(c, 1)]
            _remote(mine, mine, send_sem.at[j], recv_sem.at[j], (x, y, 1 - c)).start()
        for j in range(n):
            mine, theirs = bufs[j].at[:, pl.ds(c, 1)], bufs[j].at[:, pl.ds(1 - c, 1)]
            _remote(mine, theirs, send_sem.at[j], recv_sem.at[j], (x, y, 1 - c)).wait()

    return pl.pallas_call(
        body, name="swap_pieces", in_specs=[ANY] * n, out_specs=[ANY] * n,
        out_shape=[SDS(p.shape, p.dtype) for p in pieces], input_output_aliases={j: j for j in range(n)},
        scratch_shapes=[pltpu.SemaphoreType.DMA((n,)), pltpu.SemaphoreType.DMA((n,))],
    )(*pieces)


def _gather_pieces(piece):
    def body(in_ref, out_ref, send_sem, recv_sem):
        x, y, c, k, chips = _place()
        peers = [(x, y, 1 - c)] + [(cx, cy, pc) for (cx, cy) in chips for pc in (c, 1 - c)]
        copies = []
        for ti, peer in enumerate(peers):
            cp = _remote(in_ref, out_ref.at[pl.ds(k, 1), pl.ds(c, 1)], send_sem.at[ti], recv_sem.at[ti], peer)
            cp.start()
            copies.append(cp)
        for ti, (px, py, pc) in enumerate(peers):
            _remote(in_ref, out_ref.at[pl.ds(2 * px + py, 1), pl.ds(pc, 1)], send_sem.at[ti], recv_sem.at[ti],
                    (px, py, pc)).wait_recv()
        for cp in copies:
            cp.wait_send()

    n_peers = 2 * N_CHIPS - 1
    return pl.pallas_call(
        body, name="gather_pieces", in_specs=[ANY], out_specs=ANY,
        out_shape=SDS((N_CHIPS, 2) + piece.shape[2:], piece.dtype),
        scratch_shapes=[pltpu.SemaphoreType.DMA((n_peers,)), pltpu.SemaphoreType.DMA((n_peers,))],
    )(piece)


def _pack_rows(parts, width, total_rows=None):
    rows = []
    for a in parts:
        a2 = a.reshape(-1, width)
        pad = (-a2.shape[0]) % SUBLANES
        rows.append(jnp.pad(a2, ((0, pad), (0, 0))) if pad else a2)
    out = jnp.concatenate(rows, axis=0)
    if total_rows is not None and out.shape[0] < total_rows:
        out = jnp.pad(out, ((0, total_rows - out.shape[0]), (0, 0)))
    return out


def _unpack_rows(packed, shapes, width):
    out, r = [], 0
    for shp in shapes:
        size = 1
        for s in shp:
            size *= s
        nr = size // width
        out.append(packed[r:r + nr].reshape(shp))
        r += nr + ((-nr) % SUBLANES)
    return out


def kernel(x, p, norm_g, w_in, w_out, conv_w, conv_b, conv_ln_g, conv_ln_b, sgu_ln_g, sgu_ln_b, sgu_w, sgu_b, pl_norm_g, pl_gate_w, pl_proj_w, final_g, loss_target, m_norm_g, m_w_in, m_w_out, m_conv_w, m_conv_b, m_conv_ln_g, m_conv_ln_b, m_sgu_ln_g, m_sgu_ln_b, m_sgu_w, m_sgu_b, m_pl_norm_g, m_pl_gate_w, m_pl_proj_w, m_final_g, v_norm_g, v_w_in, v_w_out, v_conv_w, v_conv_b, v_conv_ln_g, v_conv_ln_b, v_sgu_ln_g, v_sgu_ln_b, v_sgu_w, v_sgu_b, v_pl_norm_g, v_pl_gate_w, v_pl_proj_w, v_final_g):
    bsz, seq, d = x.shape
    depth = w_in.shape[0]
    e = w_out.shape[1] * N_CHIPS
    e3 = 3 * e
    n4 = w_in.shape[2]
    ple = p.shape[-1]
    dq = pl_proj_w.shape[2]
    k_taps = conv_w.shape[1]
    kp = k_taps + 1
    n_conv, n_sgu = conv_w.shape[0], sgu_ln_g.shape[0]
    t = bsz * seq
    tm_mm = min(512, seq)
    tm_wg, tm_wg_out = min(2048, t), min(1024, t)
    tm_mix = min(256, seq)
    my_c = lax.axis_index("c")
    my_k = 2 * lax.axis_index("x") + lax.axis_index("y")

    ec = e // N_CHIPS
    small_w = _pack_rows([conv_w.reshape(n_conv * k_taps, ec), sgu_ln_g, sgu_ln_b], ec)[None]
    shards = {"in": w_in.astype(BF16)[:, None], "out": w_out.astype(BF16)[:, None],
              "gate": pl_gate_w.astype(BF16)[:, None], "proj": pl_proj_w.astype(BF16)[:, None]}
    rest = ("out", "gate", "proj")
    gathered = {}
    gathered["in", 0], small_f = _run_comm(_gather_comm([(shards["in"], 0)], small_w), "gather_first")
    todo = [item for l in range(depth) for item in [(nm, l) for nm in rest] + ([("in", l + 1)] if l + 1 < depth else [])]

    def take_until(key):
        if key not in todo:
            return []
        n_items = todo.index(key) + 1
        items, todo[:] = todo[:n_items], todo[n_items:]
        return items

    def mixer_limit(l):
        ahead = l + 3 if l % 2 == 0 else l + 1
        return ("in", ahead) if ahead < depth else (rest[-1], depth - 1)

    def carried(keys, relay_lead=0):
        return _gather_comm([(shards[nm], ly) for nm, ly in keys], relay_lead=relay_lead) if keys else None

    mixer_relay_lead = 4

    def unpack(res, keys):
        if not keys:
            return res
        gathered.update(zip(keys, res[1]))
        return res[0]

    conv_w_rows, sgu_g_rows, sgu_b_rows = _unpack_rows(
        jnp.transpose(small_f, (1, 0, 2)).reshape(small_f.shape[1], e),
        [(n_conv * k_taps, e), (n_sgu, e), (n_sgu, e)], e)
    conv_w_full = conv_w_rows.reshape(n_conv, k_taps, e)
    conv_w_pad = jnp.pad(conv_w_full, ((0, 0), (0, 1), (0, 0)))
    conv_w_fwd, conv_w_bwd = [
        jnp.transpose(conv_w_pad.reshape(n_conv, kp, e // min(cb, e), min(cb, e)), (0, 2, 1, 3)) for cb in (2 * LANES, LANES)]
    sgu_ln_g_full = sgu_g_rows.reshape(n_sgu, 1, e)
    sgu_ln_b_full = sgu_b_rows.reshape(n_sgu, 1, e)
    sgu_bt = jnp.transpose(sgu_b, (0, 2, 1))

    norm_g3 = norm_g[:, None]
    pl_norm_g3 = pl_norm_g[:, None]
    conv_b3, conv_ln_g3, conv_ln_b3 = conv_b[:, None], conv_ln_g[:, None], conv_ln_b[:, None]
    p3 = p.reshape(depth, t, ple)

    xs, hs, projs, us, x1s, y1s, weights = [], [], [], [], [], {}, []
    xc = x.reshape(t, d)
    for l in range(depth):
        j = l // 2
        xs.append(xc)
        keys = take_until(("in", 1) if depth > 1 else (rest[-1], 0)) if l == 0 else []
        h, proj = unpack(_fwd_in(xc, norm_g3, gathered["in", l], l, tm_mm, carried(keys)), keys)
        keys = take_until(mixer_limit(l))
        if l % 2 == 0:
            u, y1s[l] = unpack(_fwd_conv(proj, conv_w_fwd, conv_b3, conv_ln_g3, conv_ln_b3, j, bsz, seq, tm_mix,
                                         carried(keys, mixer_relay_lead)), keys)
        else:
            (u,) = unpack(_fwd_sgu(proj, sgu_ln_g_full, sgu_ln_b_full, sgu_w, sgu_bt, j, tm_mix, carried(keys, mixer_relay_lead)), keys)
        w_out_l, gate_l, proj_l = gathered["out", l].reshape(1, e, d), gathered["gate", l].reshape(1, d, d), gathered["proj", l]
        weights.append((gathered["in", l], w_out_l, gate_l, proj_l))
        x1, xc = _fwd_out(xc, u, w_out_l, pl_norm_g3, gate_l, p3, proj_l, l, tm_mm)
        hs.append(h)
        projs.append(proj)
        us.append(u)
        x1s.append(x1)

    loss_local, dx, d_final_g = _loss_head(xc, final_g[None], loss_target.reshape(t, d), tm_mm)
    loss = lax.psum(loss_local[0, 0], ("x", "y", "c"))

    c_arr = my_c.astype(jnp.int32).reshape(1)
    ck_arr = jnp.stack([my_c, my_k]).astype(jnp.int32)
    by_chip = lambda a: a.reshape((1, N_CHIPS) + a.shape[1:])
    d_norm_g, d_pl_norm_g = [None] * depth, [None] * depth
    d_conv = [None] * n_conv
    d_sgu = [None] * n_sgu
    locals_, siblings, arrived = [None] * depth, [None] * depth, [None] * depth
    pending = []

    def take():
        if not pending:
            return None
        kind, _, payload = pending[0]
        return _swap_comm(payload) if kind == "swap" else _scatter_comm(payload)

    def settle(res, comm):
        if comm is None:
            return res
        outs, brought = res
        kind, layer, payload = pending.pop()
        if kind == "swap":
            locals_[layer], siblings[layer] = payload, brought
            pending.append(("scatter", layer, [by_chip(wire) for wire in _pair_sums(payload, brought, c_arr, BF16)]))
        else:
            arrived[layer] = brought
        return outs

    def run_alone(name):
        comm = take()
        settle(([], _run_comm(comm, name)), comm)

    for l in reversed(range(depth)):
        j = l // 2
        w_in_l, w_out_l, gate_l, proj_l = weights[l]
        dx1, du, rn, ds, dqv, d_pl_norm_g[l] = _bwd_out(dx, x1s[l], p3, pl_norm_g3, gate_l, proj_l, w_out_l, l, tm_mm)
        (g_proj,) = _wgrad(p3, dqv, 1, 1, N_CHIPS, tm_wg, "wgrad_proj", a_layer=l)
        (g_gate,) = _wgrad(rn, ds, 1, 1, 1, tm_wg, "wgrad_gate")
        (g_out,) = _wgrad(us[l], dx1, 1, 1, 1, tm_wg_out, "wgrad_out")
        comm = take()
        if l % 2 == 0:
            dproj, dcw, dcb, dlg, dlb = settle(
                _bwd_conv(du, projs[l], y1s[l], conv_w_bwd, conv_ln_g3, conv_ln_b3, j, bsz, seq, tm_mix, comm), comm)
            d_conv[j] = (dcw, dcb, dlg, dlb)
        else:
            dproj, dsw, dsbt, dlg, dlb = settle(
                _bwd_sgu(du, projs[l], sgu_ln_g_full, sgu_ln_b_full, sgu_w, sgu_bt, j, tm_mix, comm), comm)
            d_sgu[j] = (dsw, dsbt, dlg, dlb)
        (g_in,) = _wgrad(hs[l], dproj, 1, N_CHIPS, 1, tm_wg, "wgrad_in")
        local = [g_in.reshape(N_CHIPS, 2, d // 2, n4), g_out.reshape(N_CHIPS, 2, e // (2 * N_CHIPS), d),
                 g_gate.reshape(N_CHIPS, 2, d // (2 * N_CHIPS), d), g_proj.reshape(N_CHIPS, 2, ple // 2, dq)]
        while pending:
            run_alone("reduce_step")
        pending.append(("swap", l, local))
        if l == 0:
            run_alone("swap_last")
        comm = take()
        dx, d_norm_g[l] = settle(_bwd_in(dproj, dx1, xs[l], norm_g3, w_in_l, l, tm_mm, comm), comm)
    while pending:
        run_alone("reduce_tail")
    grad_x = dx.reshape(bsz, seq, d)

    d_conv_w = jnp.stack([jnp.transpose(dc[0], (1, 0, 2)).reshape(kp, e)[:k_taps] for dc in d_conv])
    d_conv_b = jnp.stack([dc[1][0] for dc in d_conv])
    d_conv_ln_g = jnp.stack([dc[2][0] for dc in d_conv])
    d_conv_ln_b = jnp.stack([dc[3][0] for dc in d_conv])
    d_sgu_w = jnp.stack([dsg[0] for dsg in d_sgu])
    d_sgu_b = jnp.stack([jnp.transpose(dsg[1]) for dsg in d_sgu])
    d_sgu_ln_g = jnp.stack([dsg[2][0] for dsg in d_sgu])
    d_sgu_ln_b = jnp.stack([dsg[3][0] for dsg in d_sgu])
    small_grads = [jnp.concatenate(d_norm_g), d_conv_w, d_conv_b, d_conv_ln_g, d_conv_ln_b, d_sgu_ln_g, d_sgu_ln_b,
                   d_sgu_w, d_sgu_b, jnp.concatenate(d_pl_norm_g), d_final_g]
    small_shapes = [a.shape for a in small_grads]
    packed = _pack_rows(small_grads, d)
    pack_rows = packed.shape[0] + ((-packed.shape[0]) % (8 * SUBLANES))
    packed = _pack_rows(small_grads, d, pack_rows)
    gl_small = packed.reshape(N_CHIPS, 2, pack_rows // 8, d)
    (small_sibling,) = _run_comm(_swap_comm([gl_small]), "swap_small")
    small_pair = by_chip(_pair_sums([gl_small], [small_sibling], c_arr, F32)[0])

    (small_arrived,) = _run_comm(_scatter_comm([small_pair]), "scatter_small")
    reduced = [lax.empty((depth, 2) + gl.shape[2:], F32) for gl in locals_[0]]
    for l in range(depth):
        reduced = _chip_sums(locals_[l], siblings[l], arrived[l], ck_arr, reduced, l)
    (small_both,) = _chip_sums([gl_small], [small_sibling], [small_arrived], ck_arr,
                               [lax.empty((1, 2) + gl_small.shape[2:], F32)], 0)
    small_mine = lax.dynamic_slice_in_dim(small_both, my_c, 1, axis=1)
    reduced = _swap_pieces(reduced)
    small_all = _gather_pieces(small_mine)
    small_all = lax.dynamic_update_slice(small_all, small_mine, (my_k, my_c, 0, 0)).reshape(pack_rows, d)
    small_red = _unpack_rows(small_all, small_shapes, d)
    (gr_norm_g, gr_conv_w, gr_conv_b, gr_conv_ln_g, gr_conv_ln_b, gr_sgu_ln_g, gr_sgu_ln_b, gr_sgu_w, gr_sgu_b,
     gr_pl_norm_g, gr_final_g) = small_red
    gr_final_g = gr_final_g.reshape(d)
    gr_conv_w = lax.dynamic_slice_in_dim(gr_conv_w, my_k * ec, ec, axis=2)
    gr_sgu_ln_g = lax.dynamic_slice_in_dim(gr_sgu_ln_g, my_k * ec, ec, axis=1)
    gr_sgu_ln_b = lax.dynamic_slice_in_dim(gr_sgu_ln_b, my_k * ec, ec, axis=1)

    def shard_update(w, g, m, v):
        flat = lambda a: a.reshape(-1, w.shape[-1])
        return [o.reshape(w.shape) for o in _adamw(flat(w), flat(g), flat(m), flat(v))]

    up_in = shard_update(w_in, reduced[0], m_w_in, v_w_in)
    up_out = shard_update(w_out, reduced[1], m_w_out, v_w_out)
    up_gate = shard_update(pl_gate_w, reduced[2], m_pl_gate_w, v_pl_gate_w)
    up_proj = shard_update(pl_proj_w, reduced[3], m_pl_proj_w, v_pl_proj_w)

    small_names = ["norm_g", "conv_w", "conv_b", "conv_ln_g", "conv_ln_b", "sgu_ln_g", "sgu_ln_b", "sgu_w", "sgu_b",
                   "pl_norm_g", "final_g"]
    small_w_list = [norm_g, conv_w, conv_b, conv_ln_g, conv_ln_b, sgu_ln_g, sgu_ln_b, sgu_w, sgu_b, pl_norm_g, final_g]
    small_m_list = [m_norm_g, m_conv_w, m_conv_b, m_conv_ln_g, m_conv_ln_b, m_sgu_ln_g, m_sgu_ln_b, m_sgu_w, m_sgu_b,
                    m_pl_norm_g, m_final_g]
    small_v_list = [v_norm_g, v_conv_w, v_conv_b, v_conv_ln_g, v_conv_ln_b, v_sgu_ln_g, v_sgu_ln_b, v_sgu_w, v_sgu_b,
                    v_pl_norm_g, v_final_g]
    small_g_list = [gr_norm_g, gr_conv_w, gr_conv_b, gr_conv_ln_g, gr_conv_ln_b, gr_sgu_ln_g, gr_sgu_ln_b, gr_sgu_w,
                    gr_sgu_b, gr_pl_norm_g, gr_final_g]
    width = ec
    shapes_local = [a.shape for a in small_w_list]
    outs_small = _adamw(_pack_rows(small_w_list, width), _pack_rows(small_g_list, width),
                        _pack_rows(small_m_list, width), _pack_rows(small_v_list, width))
    unpacked = [_unpack_rows(o, shapes_local, width) for o in outs_small]
    ups = {name: [unpacked[kind][i] for kind in range(4)] for i, name in enumerate(small_names)}
    ups["w_in"], ups["w_out"], ups["pl_gate_w"], ups["pl_proj_w"] = up_in, up_out, up_gate, up_proj

    order = ["norm_g", "w_in", "w_out", "conv_w", "conv_b", "conv_ln_g", "conv_ln_b", "sgu_ln_g", "sgu_ln_b", "sgu_w",
             "sgu_b", "pl_norm_g", "pl_gate_w", "pl_proj_w", "final_g"]
    result = [loss, grad_x]
    for kind in range(4):
        result.extend(ups[name][kind] for name in order)
    return tuple(result)
```

```python
import functools

import jax
import jax.numpy as jnp
from jax import lax
from jax.experimental import pallas as pl
from jax.experimental.pallas import tpu as pltpu

F32 = jnp.float32
BF16 = jnp.bfloat16
SDS = jax.ShapeDtypeStruct

EPS = 1e-6
CHUNK = 128
GROUPS = 8
HALO = 32
N_CHIPS = 4
LANES = 128
SUBLANES = 8
V7X_VMEM_LIMIT = 56 << 20

ADAM_LR = 0.001
ADAM_B1 = 0.9
ADAM_B2 = 0.999
ADAM_EPS = 1e-08
ADAM_WD = 0.01
ADAM_STEP = 10

MESH_IDS = pl.DeviceIdType.MESH
ANY = pl.BlockSpec(memory_space=pl.ANY)


def _params(n_axes):
    return pltpu.CompilerParams(dimension_semantics=("arbitrary",) * n_axes, vmem_limit_bytes=V7X_VMEM_LIMIT)


def _const(shape):
    zeros = (0,) * len(shape)
    return pl.BlockSpec(shape, lambda *_: zeros)


def _layer(shape, layer):
    zeros = (0,) * len(shape)
    return pl.BlockSpec((None,) + tuple(shape), lambda *_: (layer,) + zeros, pipeline_mode=pl.Buffered(1))


class _Comm:
    def __init__(self, operands, out_shape, sems, start, finish, relay=None, relay_lead=0):
        self.operands, self.out_shape, self.sems, self.start, self.finish = operands, out_shape, sems, start, finish
        self.relay, self.relay_lead = relay, relay_lead


def _call(body, *, name, grid, in_specs, out_specs, out_shape, operands, scratch_shapes=(), comm=None):
    in_specs, out_specs, out_shape, scratch_shapes = list(in_specs), list(out_specs), list(out_shape), list(scratch_shapes)
    if comm is None:
        return pl.pallas_call(body, name=name, grid=grid, in_specs=in_specs, out_specs=out_specs, out_shape=out_shape,
                              scratch_shapes=scratch_shapes, compiler_params=_params(len(grid)))(*operands)
    n_in, n_out, n_sc = len(in_specs), len(out_specs), len(scratch_shapes)
    ci, co = len(comm.operands), len(comm.out_shape)

    def hosted(*refs):
        ins, cins = refs[:n_in], refs[n_in:n_in + ci]
        outs, couts = refs[n_in + ci:n_in + ci + n_out], refs[n_in + ci + n_out:n_in + ci + n_out + co]
        scratch = refs[n_in + ci + n_out + co:n_in + ci + n_out + co + n_sc]
        sems = refs[n_in + ci + n_out + co + n_sc:]
        n_steps = functools.reduce(lambda a, b: a * b, grid)
        step = functools.reduce(lambda acc, a: acc * grid[a] + pl.program_id(a), range(len(grid)), 0)
        first, last = step == 0, step == n_steps - 1

        @pl.when(first)
        def _():
            comm.start(cins, couts, sems)

        if comm.relay is not None:
            @pl.when(step == max(n_steps - 1 - comm.relay_lead, 0))
            def _():
                comm.relay(cins, couts, sems)

        body(*ins, *outs, *scratch)

        @pl.when(last)
        def _():
            comm.finish(cins, couts, sems)

    res = pl.pallas_call(
        hosted, name=name, grid=grid, in_specs=in_specs + [ANY] * ci, out_specs=out_specs + [ANY] * co,
        out_shape=out_shape + list(comm.out_shape), scratch_shapes=scratch_shapes + list(comm.sems),
        compiler_params=_params(len(grid)))(*operands, *comm.operands)
    return res[:n_out], res[n_out:]


def _run_comm(comm, name):
    ci, co = len(comm.operands), len(comm.out_shape)

    def body(*refs):
        comm.start(refs[:ci], refs[ci:ci + co], refs[ci + co:])
        if comm.relay is not None:
            comm.relay(refs[:ci], refs[ci:ci + co], refs[ci + co:])
        comm.finish(refs[:ci], refs[ci:ci + co], refs[ci + co:])

    return pl.pallas_call(body, name=name, in_specs=[ANY] * ci, out_specs=[ANY] * co, out_shape=list(comm.out_shape),
                          scratch_shapes=list(comm.sems))(*comm.operands)


def _sigmoid(v):
    return jax.nn.sigmoid(v)


def _dsilu(v, s):
    return s * (1.0 + v * (1.0 - s))


def _gelu_parts(v):
    cdf = 0.5 * (1.0 + lax.erf(v * 0.7071067811865476))
    pdf = jnp.exp2(v * v * -0.7213475204444817) * 0.3989422804014327
    return v * cdf, cdf + v * pdf


def _gelu(v):
    return 0.5 * v * (1.0 + lax.erf(v * 0.7071067811865476))


def _rms_stats(x):
    r = lax.rsqrt(jnp.mean(x * x, axis=-1, keepdims=True) + EPS)
    return r, x * r


def _rms_bwd(dy, g, r, xh):
    gdy = dy * g
    return r * (gdy - xh * jnp.mean(xh * gdy, axis=-1, keepdims=True))


def _ln_stats(x):
    mu = jnp.mean(x, axis=-1, keepdims=True)
    xc = x - mu
    rs = lax.rsqrt(jnp.mean(xc * xc, axis=-1, keepdims=True) + EPS)
    return rs, xc * rs


def _ln_bwd(dy, g, rs, xh):
    dxh = dy * g
    return rs * (dxh - jnp.mean(dxh, axis=-1, keepdims=True) - xh * jnp.mean(dxh * xh, axis=-1, keepdims=True))


def _dot(a, b):
    return jnp.dot(a, b, preferred_element_type=F32)


def _dot_nt(a, b):
    return lax.dot_general(a, b, (((1,), (1,)), ((), ())), preferred_element_type=F32)


def _dot_tn(a, b):
    return lax.dot_general(a, b, (((0,), (0,)), ((), ())), preferred_element_type=F32)


def _fwd_in(x, norm_g, w_in_full, layer, tm, comm=None):
    t, d = x.shape
    _, nk, _, n4 = w_in_full.shape

    def body(x_ref, g_ref, w_ref, h_ref, proj_ref):
        r, xh = _rms_stats(x_ref[...])
        h = (xh * g_ref[...]).astype(BF16)
        h_ref[...] = h
        for k in range(nk):
            proj_ref[:, k * n4:(k + 1) * n4] = _dot(h, w_ref[k]).astype(BF16)

    return _call(
        body, name="fwd_in", grid=(t // tm,),
        in_specs=[pl.BlockSpec((tm, d), lambda i: (i, 0)), _layer((1, d), layer), _layer((nk, d, n4), 0)],
        out_specs=[pl.BlockSpec((tm, d), lambda i: (i, 0)), pl.BlockSpec((tm, nk * n4), lambda i: (i, 0))],
        out_shape=[SDS((t, d), BF16), SDS((t, nk * n4), BF16)],
        operands=(x, norm_g, w_in_full), comm=comm)


def _halo_maps(nt, hb, n_halo_blocks):
    def prev(b, i):
        return (jnp.maximum((b * nt + i) * hb - 1, 0), 0)

    def nxt(b, i):
        return (jnp.minimum((b * nt + i + 1) * hb, n_halo_blocks - 1), 0)

    return prev, nxt


TAP_TILE_VREGS = 16


def _tap_rows(cb, tm):
    return min(TAP_TILE_VREGS * SUBLANES * LANES // cb, tm)


def _conv_taps(src_ref, w_ref, dst_ref, cb_idx, n_rows, rb, first, reverse):
    k_taps = w_ref.shape[1] - 1
    for r0 in range(0, n_rows, rb):
        acc = None
        for res in range(SUBLANES):
            rows = rb + (SUBLANES if res else 0)
            group = None
            for k in range(k_taps):
                off = first + k
                if off % SUBLANES != res:
                    continue
                wk = w_ref[cb_idx, pl.ds((k_taps - 1 - k) if reverse else k, 1), :]
                term = wk * src_ref[cb_idx, pl.ds(r0 + off - res, rows), :]
                group = term if group is None else group + term
            if group is None:
                continue
            part = group[res:res + rb] if res else group
            acc = part if acc is None else acc + part
        dst_ref[cb_idx, pl.ds(r0, rb), :] = acc


def _fwd_conv(proj, conv_w_blk, conv_b, ln_g, ln_b, layer, bsz, seq, tm, comm=None):
    t, e3 = proj.shape
    e = e3 // 3
    nt = seq // tm
    hb = tm // HALO
    _, ncb, kp, cb = conv_w_blk.shape
    rb = _tap_rows(cb, tm)
    prev, _ = _halo_maps(nt, hb, t // HALO)

    def body(proj_ref, halo_ref, w_ref, b_ref, g_ref, lb_ref, u_ref, y1_ref, y0s, y1s):
        i = pl.program_id(1)
        a = proj_ref[:, 0:e].astype(F32)
        b = proj_ref[:, e:2 * e].astype(F32)
        y0 = a * _sigmoid(b)
        ah = halo_ref[:, 0:e].astype(F32)
        bh = halo_ref[:, e:2 * e].astype(F32)
        y0h = jnp.where(i > 0, ah * _sigmoid(bh), 0.0)
        for c in range(ncb):
            y0s[c, 0:HALO, :] = y0h[:, c * cb:(c + 1) * cb]
            y0s[c, HALO:HALO + tm, :] = y0[:, c * cb:(c + 1) * cb]

        def per_block(c, carry):
            _conv_taps(y0s, w_ref, y1s, c, tm, rb, HALO - (kp - 2), False)
            return carry

        lax.fori_loop(0, ncb, per_block, 0)
        y1 = jnp.concatenate([y1s[c] for c in range(ncb)], axis=1) + b_ref[...]
        y1_ref[...] = y1
        rs, xh = _ln_stats(y1)
        y2 = xh * g_ref[...] + lb_ref[...]
        y = y2 * _sigmoid(y2)
        z = proj_ref[:, 2 * e:3 * e].astype(F32)
        u_ref[...] = (y * (z * _sigmoid(z))).astype(BF16)

    return _call(
        body, name="fwd_conv", grid=(bsz, nt),
        in_specs=[pl.BlockSpec((tm, e3), lambda b, i: (b * nt + i, 0)),
                  pl.BlockSpec((HALO, 2 * e), prev),
                  _layer((ncb, kp, cb), layer), _layer((1, e), layer), _layer((1, e), layer), _layer((1, e), layer)],
        out_specs=[pl.BlockSpec((tm, e), lambda b, i: (b * nt + i, 0)), pl.BlockSpec((tm, e), lambda b, i: (b * nt + i, 0))],
        out_shape=[SDS((t, e), BF16), SDS((t, e), F32)],
        scratch_shapes=[pltpu.VMEM((ncb, HALO + tm, cb), F32), pltpu.VMEM((ncb, tm, cb), F32)],
        operands=(proj, proj, conv_w_blk, conv_b, ln_g, ln_b), comm=comm)


def _tril_mask():
    rows = lax.broadcasted_iota(jnp.int32, (CHUNK, CHUNK), 0)
    cols = lax.broadcasted_iota(jnp.int32, (CHUNK, CHUNK), 1)
    return rows >= cols


def _fwd_sgu(proj, ln_g, ln_b, sgu_w, sgu_bt, layer, tm, comm=None):
    t, e3 = proj.shape
    e = e3 // 3
    gw = e // GROUPS
    nch = tm // CHUNK

    def body(proj_ref, g_ref, lb_ref, w_ref, bt_ref, u_ref, mixed):
        a = proj_ref[:, 0:e].astype(F32)
        b = proj_ref[:, e:2 * e].astype(F32)
        z = proj_ref[:, 2 * e:3 * e].astype(F32)
        rs, xh = _ln_stats(_gelu(b))
        v = (xh * g_ref[...] + lb_ref[...]).astype(BF16)
        mask = _tril_mask()
        for g in range(GROUPS):
            wm = jnp.where(mask, w_ref[g], 0.0).astype(BF16)
            bias = bt_ref[:, g:g + 1]
            for n in range(nch):
                blk = v[n * CHUNK:(n + 1) * CHUNK, g * gw:(g + 1) * gw]
                mixed[n * CHUNK:(n + 1) * CHUNK, g * gw:(g + 1) * gw] = _dot(wm, blk) + bias
        y = _gelu(a) * mixed[...]
        u_ref[...] = (y * (z * _sigmoid(z))).astype(BF16)

    return _call(
        body, name="fwd_sgu", grid=(t // tm,),
        in_specs=[pl.BlockSpec((tm, e3), lambda i: (i, 0)), _layer((1, e), layer), _layer((1, e), layer),
                  _layer((GROUPS, CHUNK, CHUNK), layer), _layer((CHUNK, GROUPS), layer)],
        out_specs=[pl.BlockSpec((tm, e), lambda i: (i, 0))],
        out_shape=[SDS((t, e), BF16)],
        scratch_shapes=[pltpu.VMEM((tm, e), F32)],
        operands=(proj, ln_g, ln_b, sgu_w, sgu_bt), comm=comm)


def _ple_forward(x1, p_ref, plg_ref, gw_ref, pw_ref):
    nk, _, dq = pw_ref.shape
    r, xh = _rms_stats(x1)
    rn = (xh * plg_ref[...]).astype(BF16)
    gate = _sigmoid(_dot(rn, gw_ref[...]))
    pb = p_ref[...].astype(BF16)
    q = jnp.concatenate([_dot(pb, pw_ref[k]) for k in range(nk)], axis=1)
    return r, xh, rn, gate, q


def _fwd_out(x, u, w_out_full, pl_norm_g, gate_w_full, p, proj_w_full, layer, tm):
    t, d = x.shape
    e = u.shape[1]
    ple = p.shape[-1]
    nk, dq = proj_w_full.shape[1], proj_w_full.shape[3]

    def body(x_ref, u_ref, wo_ref, plg_ref, gw_ref, p_ref, pw_ref, x1_ref, x2_ref):
        x1 = x_ref[...] + _dot(u_ref[...], wo_ref[...])
        x1_ref[...] = x1
        _, _, _, gate, q = _ple_forward(x1, p_ref, plg_ref, gw_ref, pw_ref)
        x2_ref[...] = x1 + gate * q

    return _call(
        body, name="fwd_out", grid=(t // tm,),
        in_specs=[pl.BlockSpec((tm, d), lambda i: (i, 0)), pl.BlockSpec((tm, e), lambda i: (i, 0)),
                  _layer((e, d), 0), _layer((1, d), layer), _layer((d, d), 0),
                  pl.BlockSpec((None, tm, ple), lambda i: (layer, i, 0)), _layer((nk, ple, dq), 0)],
        out_specs=[pl.BlockSpec((tm, d), lambda i: (i, 0)), pl.BlockSpec((tm, d), lambda i: (i, 0))],
        out_shape=[SDS((t, d), F32), SDS((t, d), F32)],
        operands=(x, u, w_out_full, pl_norm_g, gate_w_full, p, proj_w_full))


def _loss_head(x, final_g, target, tm):
    t, d = x.shape
    n_steps = t // tm

    def body(x_ref, g_ref, tgt_ref, loss_ref, dx_ref, dg_ref, sq_acc):
        i = pl.program_id(0)

        @pl.when(i == 0)
        def _():
            sq_acc[...] = jnp.zeros_like(sq_acc)
            dg_ref[...] = jnp.zeros_like(dg_ref)

        g = g_ref[...]
        r, xh = _rms_stats(x_ref[...])
        diff = xh * g - tgt_ref[...]
        sq_acc[...] += jnp.sum(diff * diff, axis=0, keepdims=True)
        dout = diff * (1.0 / d)
        dg_ref[...] += jnp.sum(dout * xh, axis=0, keepdims=True)
        dx_ref[...] = _rms_bwd(dout, g, r, xh)

        @pl.when(i == n_steps - 1)
        def _():
            loss_ref[...] = jnp.sum(sq_acc[...], axis=1, keepdims=True) * (0.5 / d)

    return pl.pallas_call(
        body, name="loss_head", grid=(n_steps,),
        in_specs=[pl.BlockSpec((tm, d), lambda i: (i, 0)), _const((1, d)), pl.BlockSpec((tm, d), lambda i: (i, 0))],
        out_specs=[_const((1, 1)), pl.BlockSpec((tm, d), lambda i: (i, 0)), _const((1, d))],
        out_shape=[SDS((1, 1), F32), SDS((t, d), F32), SDS((1, d), F32)],
        scratch_shapes=[pltpu.VMEM((1, d), F32)],
        compiler_params=_params(1),
    )(x, final_g, target)


def _bwd_out(dx2, x1, p, pl_norm_g, gate_w_full, proj_w_full, w_out_full, layer, tm):
    t, d = dx2.shape
    e = w_out_full.shape[1]
    ple = p.shape[-1]
    nk, dq_w = proj_w_full.shape[1], proj_w_full.shape[3]

    def body(dx2_ref, x1_ref, p_ref, plg_ref, gw_ref, pw_ref, wo_ref, dx1_ref, du_ref, rn_ref, ds_ref, dq_ref, dplg_ref):
        @pl.when(pl.program_id(0) == 0)
        def _():
            dplg_ref[...] = jnp.zeros_like(dplg_ref)

        dx2v = dx2_ref[...]
        r, xh, rn, gate, q = _ple_forward(x1_ref[...], p_ref, plg_ref, gw_ref, pw_ref)
        rn_ref[...] = rn
        dq_ref[...] = (dx2v * gate).astype(BF16)
        ds = (dx2v * q * gate * (1.0 - gate)).astype(BF16)
        ds_ref[...] = ds
        dr = _dot_nt(ds, gw_ref[...])
        dplg_ref[...] += jnp.sum(dr * xh, axis=0, keepdims=True)
        dx1 = dx2v + _rms_bwd(dr, plg_ref[...], r, xh)
        dx1_ref[...] = dx1
        du_ref[...] = _dot_nt(dx1.astype(BF16), wo_ref[...]).astype(BF16)

    row = lambda w: pl.BlockSpec((tm, w), lambda i: (i, 0))
    return _call(
        body, name="bwd_out", grid=(t // tm,),
        in_specs=[row(d), row(d), pl.BlockSpec((None, tm, ple), lambda i: (layer, i, 0)),
                  _layer((1, d), layer), _layer((d, d), 0), _layer((nk, ple, dq_w), 0), _layer((e, d), 0)],
        out_specs=[row(d), row(e), row(d), row(d), row(d), _const((1, d))],
        out_shape=[SDS((t, d), F32), SDS((t, e), BF16), SDS((t, d), BF16), SDS((t, d), BF16), SDS((t, d), BF16),
                   SDS((1, d), F32)],
        operands=(dx2, x1, p, pl_norm_g, gate_w_full, proj_w_full, w_out_full))


def _bwd_conv(du, proj, y1, conv_w_blk, ln_g, ln_b, layer, bsz, seq, tm, comm=None):
    t, e3 = proj.shape
    e = e3 // 3
    nt = seq // tm
    hb = tm // HALO
    _, ncb, kp, cb = conv_w_blk.shape
    rb = _tap_rows(cb, tm)
    k_taps = kp - 1
    prev, nxt = _halo_maps(nt, hb, t // HALO)
    z_halo = lambda b, i: (nxt(b, i)[0], 2)

    def ln_silu_bwd(du_v, z_v, y1_v, g, lb):
        rs, xh = _ln_stats(y1_v)
        y2 = xh * g + lb
        sg = _sigmoid(y2)
        sz = _sigmoid(z_v)
        dy = du_v * (z_v * sz)
        dy2 = dy * _dsilu(y2, sg)
        return _ln_bwd(dy2, g, rs, xh), dy2, xh, du_v * (y2 * sg) * _dsilu(z_v, sz)

    def body(du_ref, proj_ref, y1_ref, duh_ref, zh_ref, y1h_ref, abh_ref, w_ref, g_ref, lb_ref,
             dproj_ref, dw_ref, dcb_ref, dg_ref, dlb_ref, y0s, dy1s, dy0s, ysh):
        b_id, i = pl.program_id(0), pl.program_id(1)

        @pl.when((b_id == 0) & (i == 0))
        def _():
            dw_ref[...] = jnp.zeros_like(dw_ref)
            dcb_ref[...] = jnp.zeros_like(dcb_ref)
            dg_ref[...] = jnp.zeros_like(dg_ref)
            dlb_ref[...] = jnp.zeros_like(dlb_ref)

        g, lb = g_ref[...], lb_ref[...]
        a = proj_ref[:, 0:e].astype(F32)
        b = proj_ref[:, e:2 * e].astype(F32)
        z = proj_ref[:, 2 * e:3 * e].astype(F32)
        sb = _sigmoid(b)
        y0 = a * sb
        dy1, dy2, xh, dz = ln_silu_bwd(du_ref[...].astype(F32), z, y1_ref[...], g, lb)
        dproj_ref[:, 2 * e:3 * e] = dz.astype(BF16)
        dg_ref[...] += jnp.sum(dy2 * xh, axis=0, keepdims=True)
        dlb_ref[...] += jnp.sum(dy2, axis=0, keepdims=True)
        dcb_ref[...] += jnp.sum(dy1, axis=0, keepdims=True)
        dy1h, _, _, _ = ln_silu_bwd(duh_ref[...].astype(F32), zh_ref[...].astype(F32), y1h_ref[...], g, lb)
        dy1h = jnp.where(i < nt - 1, dy1h, 0.0)
        ah = abh_ref[:, 0:e].astype(F32)
        bh = abh_ref[:, e:2 * e].astype(F32)
        y0h = jnp.where(i > 0, ah * _sigmoid(bh), 0.0)
        for c in range(ncb):
            cols = slice(c * cb, (c + 1) * cb)
            y0s[c, 0:HALO, :] = y0h[:, cols]
            y0s[c, HALO:HALO + tm, :] = y0[:, cols]
            dy1s[c, 0:tm, :] = dy1[:, cols]
            dy1s[c, tm:tm + HALO, :] = dy1h[:, cols]

        def per_block(c, carry):
            _conv_taps(dy1s, w_ref, dy0s, c, tm, rb, 0, True)
            for res in range(1, SUBLANES):
                ysh[res - 1] = y0s[c, pl.ds(res, tm + HALO - SUBLANES), :]
            for k in range(k_taps):
                off = HALO - (k_taps - 1) + k
                res = off % SUBLANES
                acc = None
                for r0 in range(0, tm, rb):
                    rows = pl.ds(r0 + off - res, rb)
                    shifted = ysh[res - 1, rows, :] if res else y0s[c, rows, :]
                    term = dy1s[c, pl.ds(r0, rb), :] * shifted
                    acc = term if acc is None else acc + term
                dw_ref[c, pl.ds(k, 1), :] += jnp.sum(acc, axis=0, keepdims=True)
            return carry

        lax.fori_loop(0, ncb, per_block, 0)
        dy0 = jnp.concatenate([dy0s[c] for c in range(ncb)], axis=1)
        dproj_ref[:, 0:e] = (dy0 * sb).astype(BF16)
        dproj_ref[:, e:2 * e] = (dy0 * a * sb * (1.0 - sb)).astype(BF16)

    tile = lambda w: pl.BlockSpec((tm, w), lambda b, i: (b * nt + i, 0))
    return _call(
        body, name="bwd_conv", grid=(bsz, nt),
        in_specs=[tile(e), tile(e3), tile(e),
                  pl.BlockSpec((HALO, e), nxt), pl.BlockSpec((HALO, e), z_halo), pl.BlockSpec((HALO, e), nxt),
                  pl.BlockSpec((HALO, 2 * e), prev),
                  _layer((ncb, kp, cb), layer), _layer((1, e), layer), _layer((1, e), layer)],
        out_specs=[tile(e3), _const((ncb, kp, cb)), _const((1, e)), _const((1, e)), _const((1, e))],
        out_shape=[SDS((t, e3), BF16), SDS((ncb, kp, cb), F32), SDS((1, e), F32), SDS((1, e), F32), SDS((1, e), F32)],
        scratch_shapes=[pltpu.VMEM((ncb, HALO + tm, cb), F32), pltpu.VMEM((ncb, tm + HALO, cb), F32),
                        pltpu.VMEM((ncb, tm, cb), F32), pltpu.VMEM((SUBLANES - 1, tm + HALO - SUBLANES, cb), F32)],
        operands=(du, proj, y1, du, proj, y1, proj, conv_w_blk, ln_g, ln_b), comm=comm)


def _bwd_sgu(du, proj, ln_g, ln_b, sgu_w, sgu_bt, layer, tm, comm=None):
    t, e3 = proj.shape
    e = e3 // 3
    gw = e // GROUPS
    nch = tm // CHUNK

    def body(du_ref, proj_ref, g_ref, lb_ref, w_ref, bt_ref, dproj_ref, dw_ref, dbt_ref, dg_ref, dlb_ref, mixed, dmix, dv):
        @pl.when(pl.program_id(0) == 0)
        def _():
            dw_ref[...] = jnp.zeros_like(dw_ref)
            dbt_ref[...] = jnp.zeros_like(dbt_ref)
            dg_ref[...] = jnp.zeros_like(dg_ref)
            dlb_ref[...] = jnp.zeros_like(dlb_ref)

        g, lb = g_ref[...], lb_ref[...]
        a = proj_ref[:, 0:e].astype(F32)
        b = proj_ref[:, e:2 * e].astype(F32)
        z = proj_ref[:, 2 * e:3 * e].astype(F32)
        ug, dug = _gelu_parts(a)
        vb, dvb_db = _gelu_parts(b)
        rs, xh = _ln_stats(vb)
        v = (xh * g + lb).astype(BF16)
        mask = _tril_mask()
        for gi in range(GROUPS):
            wm = jnp.where(mask, w_ref[gi], 0.0).astype(BF16)
            bias = bt_ref[:, gi:gi + 1]
            for c in range(nch):
                blk = v[c * CHUNK:(c + 1) * CHUNK, gi * gw:(gi + 1) * gw]
                mixed[c * CHUNK:(c + 1) * CHUNK, gi * gw:(gi + 1) * gw] = _dot(wm, blk) + bias
        mx = mixed[...]
        sz = _sigmoid(z)
        duv = du_ref[...].astype(F32)
        dy = duv * (z * sz)
        dproj_ref[:, 2 * e:3 * e] = (duv * (ug * mx) * _dsilu(z, sz)).astype(BF16)
        dproj_ref[:, 0:e] = (dy * mx * dug).astype(BF16)
        dmix[...] = dy * ug
        for gi in range(GROUPS):
            wm = jnp.where(mask, w_ref[gi], 0.0).astype(BF16)
            dw_acc = None
            db_acc = None
            for c in range(nch):
                rows, cols = slice(c * CHUNK, (c + 1) * CHUNK), slice(gi * gw, (gi + 1) * gw)
                dm = dmix[rows, cols]
                dmb = dm.astype(BF16)
                dw_n = _dot_nt(dmb, v[rows, cols])
                db_n = jnp.sum(dm, axis=1, keepdims=True)
                dw_acc = dw_n if dw_acc is None else dw_acc + dw_n
                db_acc = db_n if db_acc is None else db_acc + db_n
                dv[rows, cols] = _dot_tn(wm, dmb)
            dw_ref[gi] += jnp.where(mask, dw_acc, 0.0)
            dbt_ref[:, gi:gi + 1] += db_acc
        dvv = dv[...]
        dg_ref[...] += jnp.sum(dvv * xh, axis=0, keepdims=True)
        dlb_ref[...] += jnp.sum(dvv, axis=0, keepdims=True)
        dproj_ref[:, e:2 * e] = (_ln_bwd(dvv, g, rs, xh) * dvb_db).astype(BF16)

    return _call(
        body, name="bwd_sgu", grid=(t // tm,),
        in_specs=[pl.BlockSpec((tm, e), lambda i: (i, 0)), pl.BlockSpec((tm, e3), lambda i: (i, 0)),
                  _layer((1, e), layer), _layer((1, e), layer),
                  _layer((GROUPS, CHUNK, CHUNK), layer), _layer((CHUNK, GROUPS), layer)],
        out_specs=[pl.BlockSpec((tm, e3), lambda i: (i, 0)), _const((GROUPS, CHUNK, CHUNK)), _const((CHUNK, GROUPS)),
                   _const((1, e)), _const((1, e))],
        out_shape=[SDS((t, e3), BF16), SDS((GROUPS, CHUNK, CHUNK), F32), SDS((CHUNK, GROUPS), F32),
                   SDS((1, e), F32), SDS((1, e), F32)],
        scratch_shapes=[pltpu.VMEM((tm, e), F32), pltpu.VMEM((tm, e), F32), pltpu.VMEM((tm, e), F32)],
        operands=(du, proj, ln_g, ln_b, sgu_w, sgu_bt), comm=comm)


def _bwd_in(dproj, dx1, x, norm_g, w_in_full, layer, tm, comm=None):
    t, d = x.shape
    _, nk, _, n4 = w_in_full.shape

    def body(dproj_ref, dx1_ref, x_ref, g_ref, w_ref, dx_ref, dg_ref):
        @pl.when(pl.program_id(0) == 0)
        def _():
            dg_ref[...] = jnp.zeros_like(dg_ref)

        dh = None
        for k in range(nk):
            part = _dot_nt(dproj_ref[:, k * n4:(k + 1) * n4], w_ref[k])
            dh = part if dh is None else dh + part
        r, xh = _rms_stats(x_ref[...])
        dg_ref[...] += jnp.sum(dh * xh, axis=0, keepdims=True)
        dx_ref[...] = dx1_ref[...] + _rms_bwd(dh, g_ref[...], r, xh)

    row = lambda w: pl.BlockSpec((tm, w), lambda i: (i, 0))
    return _call(
        body, name="bwd_in", grid=(t // tm,),
        in_specs=[row(nk * n4), row(d), row(d), _layer((1, d), layer), _layer((nk, d, n4), 0)],
        out_specs=[row(d), _const((1, d))],
        out_shape=[SDS((t, d), F32), SDS((1, d), F32)],
        operands=(dproj, dx1, x, norm_g, w_in_full), comm=comm)


def _wgrad(a, b, kblk, nblk, n_split, tm, name, a_layer=None, comm=None):
    t, n = b.shape
    k = a.shape[-1]
    kw, nw = k // kblk, n // nblk
    nws = nw // n_split
    n_steps = t // tm

    def body(a_ref, b_ref, o_ref):
        @pl.when(pl.program_id(2) == 0)
        def _():
            o_ref[...] = jnp.zeros_like(o_ref)

        res = _dot_tn(a_ref[...].astype(BF16), b_ref[...].astype(BF16))
        for s in range(n_split):
            o_ref[s] += res[:, s * nws:(s + 1) * nws]

    if a_layer is None:
        a_spec = pl.BlockSpec((tm, kw), lambda kb, nb, i: (i, kb))
    else:
        a_spec = pl.BlockSpec((None, tm, kw), lambda kb, nb, i: (a_layer, i, kb))
    return _call(
        body, name=name, grid=(kblk, nblk, n_steps),
        in_specs=[a_spec, pl.BlockSpec((tm, nw), lambda kb, nb, i: (i, nb))],
        out_specs=[pl.BlockSpec((None, n_split, kw, nws), lambda kb, nb, i: (kb, nb, 0, 0))],
        out_shape=[SDS((kblk, nblk * n_split, kw, nws), F32)],
        operands=(a, b), comm=comm)


def _row_tile(rows, cols, budget_bytes=2 << 20):
    best = None
    for cand in range(SUBLANES, rows + 1, SUBLANES):
        if rows % cand == 0 and cand * cols * 4 <= budget_bytes:
            best = cand
    return best if best is not None else rows


ROW_CHUNKS = 1


def _pair_sums(grads, recv, my_c, wire_dtype):
    n = len(grads)
    shapes = [(g.shape[2] // ROW_CHUNKS, g.shape[3]) for g in grads]

    def body(c_ref, *refs):
        for j in range(n):
            refs[2 * n + j][...] = (refs[2 * j][...] + refs[2 * j + 1][...]).astype(wire_dtype)

    in_specs, out_specs = [], []
    for th, c in shapes:
        in_specs.append(pl.BlockSpec((None, None, th, c), lambda k, i, c_ref: (k, c_ref[0], i, 0)))
        in_specs.append(pl.BlockSpec((None, None, th, c), lambda k, i, c_ref: (k, 0, i, 0)))
        out_specs.append(pl.BlockSpec((None, th, c), lambda k, i, c_ref: (k, i, 0)))
    grid_spec = pltpu.PrefetchScalarGridSpec(num_scalar_prefetch=1, grid=(N_CHIPS, ROW_CHUNKS),
                                             in_specs=in_specs, out_specs=out_specs)
    operands = [a for pair in zip(grads, recv) for a in pair]
    return pl.pallas_call(body, name="pair_sums", grid_spec=grid_spec,
                          out_shape=[SDS((N_CHIPS,) + g.shape[2:], wire_dtype) for g in grads],
                          compiler_params=_params(2))(my_c, *operands)


def _chip_sums(grads, recv, arrived, my_ck, stacked, layer):
    n = len(grads)
    shapes = [(g.shape[2] // ROW_CHUNKS, g.shape[3]) for g in grads]

    def body(ck_ref, *refs):
        for j in range(n):
            g_ref, r_ref, a1_ref, a2_ref, a3_ref = refs[5 * j:5 * j + 5]
            acc = g_ref[...] + r_ref[...]
            for ref in (a1_ref, a2_ref, a3_ref):
                acc = acc + ref[...].astype(F32)
            refs[6 * n + j][...] = acc

    in_specs, out_specs = [], []
    for th, c in shapes:
        block = (None, None, th, c)
        in_specs.append(pl.BlockSpec(block, lambda i, ck: (ck[1], ck[0], i, 0)))
        in_specs.append(pl.BlockSpec(block, lambda i, ck: (ck[1], 0, i, 0)))
        for flip in (1, 2, 3):
            in_specs.append(pl.BlockSpec(block, lambda i, ck, flip=flip: (0, jnp.bitwise_xor(ck[1], flip), i, 0)))
        out_specs.append(pl.BlockSpec(block, lambda i, ck: (layer, ck[0], i, 0)))
    grid_spec = pltpu.PrefetchScalarGridSpec(num_scalar_prefetch=1, grid=(ROW_CHUNKS,),
                                             in_specs=in_specs + [ANY] * n, out_specs=out_specs)
    operands = [a for g, r, ar in zip(grads, recv, arrived) for a in (g, r, ar, ar, ar)]
    return pl.pallas_call(body, name="chip_sums", grid_spec=grid_spec, out_shape=[SDS(s.shape, F32) for s in stacked],
                          input_output_aliases={1 + 5 * n + j: j for j in range(n)},
                          compiler_params=_params(1))(my_ck, *operands, *stacked)


def _adam_math(w, gv, m, v):
    c1 = 1.0 - ADAM_B1 ** ADAM_STEP
    c2 = 1.0 - ADAM_B2 ** ADAM_STEP
    mn = ADAM_B1 * m + (1.0 - ADAM_B1) * gv
    vn = ADAM_B2 * v + (1.0 - ADAM_B2) * (gv * gv)
    m_hat = mn / c1
    v_hat = vn / c2
    return -ADAM_LR * (m_hat / (jnp.sqrt(v_hat) + ADAM_EPS) + ADAM_WD * w), mn, vn


def _adamw(w, g, m, v):
    rows, cols = w.shape
    tr = _row_tile(rows, cols)

    def body(w_ref, g_ref, m_ref, v_ref, go_ref, d_ref, mo_ref, vo_ref):
        gv = g_ref[...]
        go_ref[...] = gv
        d_ref[...], mo_ref[...], vo_ref[...] = _adam_math(w_ref[...], gv, m_ref[...], v_ref[...])

    spec = pl.BlockSpec((tr, cols), lambda i: (i, 0))
    return pl.pallas_call(
        body, name="adamw", grid=(rows // tr,), in_specs=[spec] * 4, out_specs=[spec] * 4,
        out_shape=[SDS((rows, cols), F32)] * 4, compiler_params=_params(1))(w, g, m, v)


def _place():
    x, y, c = lax.axis_index("x"), lax.axis_index("y"), lax.axis_index("c")
    chips = [(1 - x, y), (x, 1 - y), (1 - x, 1 - y)]
    return x, y, c, 2 * x + y, chips


def _remote(src, dst, send_sem, recv_sem, device):
    return pltpu.make_async_remote_copy(src_ref=src, dst_ref=dst, send_sem=send_sem, recv_sem=recv_sem,
                                        device_id=device, device_id_type=MESH_IDS)


def _gather_comm(items, small=None, relay_lead=0):
    shards = [arr for arr, _ in items]
    layers = [layer for _, layer in items]
    n = len(shards)
    extra = 0 if small is None else 1

    def copies(ins, outs, sems):
        ici_send, ici_recv, d2d_send, d2d_recv, own_send, own_recv = sems
        x, y, c, k, chips = _place()
        sibling = (x, y, 1 - c)
        own, ici_out, ici_in, fwd_out, fwd_in = [], [], [], [], []
        for j in range(n):
            h = ins[j].shape[2] // 2
            mine, theirs = pl.ds(c * h, h), pl.ds((1 - c) * h, h)
            own.append(_remote(ins[j].at[pl.ds(layers[j], 1)], outs[j].at[:, pl.ds(k, 1)], own_send.at[j], own_recv.at[j], sibling))
            for ti, (cx, cy) in enumerate(chips):
                s = 3 * j + ti
                ici_out.append(_remote(ins[j].at[pl.ds(layers[j], 1), :, mine], outs[j].at[:, pl.ds(k, 1), mine],
                                       ici_send.at[s], ici_recv.at[s], (cx, cy, c)))
                landed = outs[j].at[:, pl.ds(2 * cx + cy, 1), mine]
                ici_in.append(_remote(landed, landed, ici_send.at[s], ici_recv.at[s], (cx, cy, c)))
                fwd_out.append(_remote(landed, landed, d2d_send.at[s], d2d_recv.at[s], sibling))
                passed = outs[j].at[:, pl.ds(2 * cx + cy, 1), theirs]
                fwd_in.append(_remote(passed, passed, d2d_send.at[s], d2d_recv.at[s], sibling))
        if extra:
            own.append(_remote(ins[n], outs[n].at[pl.ds(k, 1)], own_send.at[n], own_recv.at[n], sibling))
            for ti, (cx, cy) in enumerate(chips):
                s = 3 * n + ti
                ici_out.append(_remote(ins[n], outs[n].at[pl.ds(k, 1)], ici_send.at[s], ici_recv.at[s], (cx, cy, c)))
                slot = outs[n].at[pl.ds(2 * cx + cy, 1)]
                ici_in.append(_remote(slot, slot, ici_send.at[s], ici_recv.at[s], (cx, cy, c)))
        return own, ici_out, ici_in, fwd_out, fwd_in

    def start(ins, outs, sems):
        own, ici_out, _, _, _ = copies(ins, outs, sems)
        for cp in own + ici_out:
            cp.start()

    def relay(ins, outs, sems):
        _, _, ici_in, fwd_out, _ = copies(ins, outs, sems)
        for idx, cp in enumerate(ici_in):
            cp.wait_recv()
            if idx < len(fwd_out):
                fwd_out[idx].start()

    def finish(ins, outs, sems):
        own, ici_out, _, fwd_out, fwd_in = copies(ins, outs, sems)
        for cp in fwd_in:
            cp.wait_recv()
        for cp in ici_out + fwd_out:
            cp.wait_send()
        for cp in own:
            cp.wait()

    operands = list(shards) + ([small] if extra else [])
    out_shape = [SDS((1, N_CHIPS) + s.shape[2:], s.dtype) for s in shards]
    if extra:
        out_shape.append(SDS((N_CHIPS,) + small.shape[1:], small.dtype))
    sems = [pltpu.SemaphoreType.DMA((3 * (n + extra),)), pltpu.SemaphoreType.DMA((3 * (n + extra),)),
            pltpu.SemaphoreType.DMA((3 * n,)), pltpu.SemaphoreType.DMA((3 * n,)),
            pltpu.SemaphoreType.DMA((n + extra,)), pltpu.SemaphoreType.DMA((n + extra,))]
    return _Comm(operands, out_shape, sems, start, finish, relay, relay_lead)


def _swap_comm(grads):
    n = len(grads)

    def copies(ins, outs, sems):
        send_sem, recv_sem = sems
        x, y, c, _, _ = _place()
        return [_remote(ins[j].at[:, pl.ds(1 - c, 1)], outs[j], send_sem.at[j], recv_sem.at[j], (x, y, 1 - c))
                for j in range(n)]

    def start(ins, outs, sems):
        for cp in copies(ins, outs, sems):
            cp.start()

    def finish(ins, outs, sems):
        for cp in copies(ins, outs, sems):
            cp.wait()

    out_shape = [SDS((g.shape[0], 1) + g.shape[2:], g.dtype) for g in grads]
    return _Comm(list(grads), out_shape, [pltpu.SemaphoreType.DMA((n,)), pltpu.SemaphoreType.DMA((n,))], start, finish)


def _scatter_comm(sums):
    n = len(sums)

    def copies(ins, outs, sems):
        send_sem, recv_sem = sems
        x, y, c, k, chips = _place()
        out, landing = [], []
        for j in range(n):
            for ti, (cx, cy) in enumerate(chips):
                s = 3 * j + ti
                out.append(_remote(ins[j].at[:, pl.ds(2 * cx + cy, 1)], outs[j].at[:, pl.ds(k, 1)],
                                   send_sem.at[s], recv_sem.at[s], (cx, cy, c)))
                slot = outs[j].at[:, pl.ds(2 * cx + cy, 1)]
                landing.append(_remote(slot, slot, send_sem.at[s], recv_sem.at[s], (cx, cy, c)))
        return out, landing

    def start(ins, outs, sems):
        for cp in copies(ins, outs, sems)[0]:
            cp.start()

    def finish(ins, outs, sems):
        out, landing = copies(ins, outs, sems)
        for cp in landing:
            cp.wait_recv()
        for cp in out:
            cp.wait_send()

    out_shape = [SDS(s.shape, s.dtype) for s in sums]
    return _Comm(list(sums), out_shape, [pltpu.SemaphoreType.DMA((3 * n,)), pltpu.SemaphoreType.DMA((3 * n,))], start, finish)


def _swap_pieces(pieces):
    n = len(pieces)

    def body(*refs):
        bufs = refs[n:2 * n]
        send_sem, recv_sem = refs[2 * n:]
        x, y, c, _, _ = _place()
        for j in range(n):
            mine = bufs[j].at[:, pl.ds(c, 1)]
            _remote(mine, mine, send_sem.at[j], recv_sem.at[j], (x, y, 1 - c)).start()
        for j in range(n):
            mine, theirs = bufs[j].at[:, pl.ds(c, 1)], bufs[j].at[:, pl.ds(1 - c, 1)]
            _remote(mine, theirs, send_sem.at[j], recv_sem.at[j], (x, y, 1 - c)).wait()

    return pl.pallas_call(
        body, name="swap_pieces", in_specs=[ANY] * n, out_specs=[ANY] * n,
        out_shape=[SDS(p.shape, p.dtype) for p in pieces], input_output_aliases={j: j for j in range(n)},
        scratch_shapes=[pltpu.SemaphoreType.DMA((n,)), pltpu.SemaphoreType.DMA((n,))],
    )(*pieces)


def _gather_pieces(piece):
    def body(in_ref, out_ref, send_sem, recv_sem):
        x, y, c, k, chips = _place()
        peers = [(x, y, 1 - c)] + [(cx, cy, pc) for (cx, cy) in chips for pc in (c, 1 - c)]
        copies = []
        for ti, peer in enumerate(peers):
            cp = _remote(in_ref, out_ref.at[pl.ds(k, 1), pl.ds(c, 1)], send_sem.at[ti], recv_sem.at[ti], peer)
            cp.start()
            copies.append(cp)
        for ti, (px, py, pc) in enumerate(peers):
            _remote(in_ref, out_ref.at[pl.ds(2 * px + py, 1), pl.ds(pc, 1)], send_sem.at[ti], recv_sem.at[ti],
                    (px, py, pc)).wait_recv()
        for cp in copies:
            cp.wait_send()

    n_peers = 2 * N_CHIPS - 1
    return pl.pallas_call(
        body, name="gather_pieces", in_specs=[ANY], out_specs=ANY,
        out_shape=SDS((N_CHIPS, 2) + piece.shape[2:], piece.dtype),
        scratch_shapes=[pltpu.SemaphoreType.DMA((n_peers,)), pltpu.SemaphoreType.DMA((n_peers,))],
    )(piece)


def _pack_rows(parts, width, total_rows=None):
    rows = []
    for a in parts:
        a2 = a.reshape(-1, width)
        pad = (-a2.shape[0]) % SUBLANES
        rows.append(jnp.pad(a2, ((0, pad), (0, 0))) if pad else a2)
    out = jnp.concatenate(rows, axis=0)
    if total_rows is not None and out.shape[0] < total_rows:
        out = jnp.pad(out, ((0, total_rows - out.shape[0]), (0, 0)))
    return out


def _unpack_rows(packed, shapes, width):
    out, r = [], 0
    for shp in shapes:
        size = 1
        for s in shp:
            size *= s
        nr = size // width
        out.append(packed[r:r + nr].reshape(shp))
        r += nr + ((-nr) % SUBLANES)
    return out


def kernel(x, p, norm_g, w_in, w_out, conv_w, conv_b, conv_ln_g, conv_ln_b, sgu_ln_g, sgu_ln_b, sgu_w, sgu_b, pl_norm_g, pl_gate_w, pl_proj_w, final_g, loss_target, m_norm_g, m_w_in, m_w_out, m_conv_w, m_conv_b, m_conv_ln_g, m_conv_ln_b, m_sgu_ln_g, m_sgu_ln_b, m_sgu_w, m_sgu_b, m_pl_norm_g, m_pl_gate_w, m_pl_proj_w, m_final_g, v_norm_g, v_w_in, v_w_out, v_conv_w, v_conv_b, v_conv_ln_g, v_conv_ln_b, v_sgu_ln_g, v_sgu_ln_b, v_sgu_w, v_sgu_b, v_pl_norm_g, v_pl_gate_w, v_pl_proj_w, v_final_g):
    bsz, seq, d = x.shape
    depth = w_in.shape[0]
    e = w_out.shape[1] * N_CHIPS
    e3 = 3 * e
    n4 = w_in.shape[2]
    ple = p.shape[-1]
    dq = pl_proj_w.shape[2]
    k_taps = conv_w.shape[1]
    kp = k_taps + 1
    n_conv, n_sgu = conv_w.shape[0], sgu_ln_g.shape[0]
    t = bsz * seq
    tm_mm = min(512, seq)
    tm_wg, tm_wg_out = min(2048, t), min(1024, t)
    tm_mix = min(256, seq)
    my_c = lax.axis_index("c")
    my_k = 2 * lax.axis_index("x") + lax.axis_index("y")

    ec = e // N_CHIPS
    small_w = _pack_rows([conv_w.reshape(n_conv * k_taps, ec), sgu_ln_g, sgu_ln_b], ec)[None]
    shards = {"in": w_in.astype(BF16)[:, None], "out": w_out.astype(BF16)[:, None],
              "gate": pl_gate_w.astype(BF16)[:, None], "proj": pl_proj_w.astype(BF16)[:, None]}
    rest = ("out", "gate", "proj")
    gathered = {}
    gathered["in", 0], small_f = _run_comm(_gather_comm([(shards["in"], 0)], small_w), "gather_first")
    todo = [item for l in range(depth) for item in [(nm, l) for nm in rest] + ([("in", l + 1)] if l + 1 < depth else [])]

    def take_until(key):
        if key not in todo:
            return []
        n_items = todo.index(key) + 1
        items, todo[:] = todo[:n_items], todo[n_items:]
        return items

    def mixer_limit(l):
        ahead = l + 3 if l % 2 == 0 else l + 1
        return ("in", ahead) if ahead < depth else (rest[-1], depth - 1)

    def carried(keys, relay_lead=0):
        return _gather_comm([(shards[nm], ly) for nm, ly in keys], relay_lead=relay_lead) if keys else None

    mixer_relay_lead = 4

    def unpack(res, keys):
        if not keys:
            return res
        gathered.update(zip(keys, res[1]))
        return res[0]

    conv_w_rows, sgu_g_rows, sgu_b_rows = _unpack_rows(
        jnp.transpose(small_f, (1, 0, 2)).reshape(small_f.shape[1], e),
        [(n_conv * k_taps, e), (n_sgu, e), (n_sgu, e)], e)
    conv_w_full = conv_w_rows.reshape(n_conv, k_taps, e)
    conv_w_pad = jnp.pad(conv_w_full, ((0, 0), (0, 1), (0, 0)))
    conv_w_fwd, conv_w_bwd = [
        jnp.transpose(conv_w_pad.reshape(n_conv, kp, e // min(cb, e), min(cb, e)), (0, 2, 1, 3)) for cb in (2 * LANES, LANES)]
    sgu_ln_g_full = sgu_g_rows.reshape(n_sgu, 1, e)
    sgu_ln_b_full = sgu_b_rows.reshape(n_sgu, 1, e)
    sgu_bt = jnp.transpose(sgu_b, (0, 2, 1))

    norm_g3 = norm_g[:, None]
    pl_norm_g3 = pl_norm_g[:, None]
    conv_b3, conv_ln_g3, conv_ln_b3 = conv_b[:, None], conv_ln_g[:, None], conv_ln_b[:, None]
    p3 = p.reshape(depth, t, ple)

    xs, hs, projs, us, x1s, y1s, weights = [], [], [], [], [], {}, []
    xc = x.reshape(t, d)
    for l in range(depth):
        j = l // 2
        xs.append(xc)
        keys = take_until(("in", 1) if depth > 1 else (rest[-1], 0)) if l == 0 else []
        h, proj = unpack(_fwd_in(xc, norm_g3, gathered["in", l], l, tm_mm, carried(keys, 1)), keys)
        keys = take_until(mixer_limit(l))
        if l % 2 == 0:
            u, y1s[l] = unpack(_fwd_conv(proj, conv_w_fwd, conv_b3, conv_ln_g3, conv_ln_b3, j, bsz, seq, tm_mix,
                                         carried(keys, mixer_relay_lead)), keys)
        else:
            (u,) = unpack(_fwd_sgu(proj, sgu_ln_g_full, sgu_ln_b_full, sgu_w, sgu_bt, j, tm_mix, carried(keys, mixer_relay_lead)), keys)
        w_out_l, gate_l, proj_l = gathered["out", l].reshape(1, e, d), gathered["gate", l].reshape(1, d, d), gathered["proj", l]
        weights.append((gathered["in", l], w_out_l, gate_l, proj_l))
        x1, xc = _fwd_out(xc, u, w_out_l, pl_norm_g3, gate_l, p3, proj_l, l, tm_mm)
        hs.append(h)
        projs.append(proj)
        us.append(u)
        x1s.append(x1)

    loss_local, dx, d_final_g = _loss_head(xc, final_g[None], loss_target.reshape(t, d), tm_mm)
    loss = lax.psum(loss_local[0, 0], ("x", "y", "c"))

    c_arr = my_c.astype(jnp.int32).reshape(1)
    ck_arr = jnp.stack([my_c, my_k]).astype(jnp.int32)
    by_chip = lambda a: a.reshape((1, N_CHIPS) + a.shape[1:])
    d_norm_g, d_pl_norm_g = [None] * depth, [None] * depth
    d_conv = [None] * n_conv
    d_sgu = [None] * n_sgu
    locals_, siblings, arrived = [None] * depth, [None] * depth, [None] * depth
    pending = []

    def take():
        if not pending:
            return None
        kind, _, payload = pending[0]
        return _swap_comm(payload) if kind == "swap" else _scatter_comm(payload)

    def settle(res, comm):
        if comm is None:
            return res
        outs, brought = res
        kind, layer, payload = pending.pop()
        if kind == "swap":
            locals_[layer], siblings[layer] = payload, brought
            pending.append(("scatter", layer, [by_chip(wire) for wire in _pair_sums(payload, brought, c_arr, BF16)]))
        else:
            arrived[layer] = brought
        return outs

    def run_alone(name):
        comm = take()
        settle(([], _run_comm(comm, name)), comm)

    for l in reversed(range(depth)):
        j = l // 2
        w_in_l, w_out_l, gate_l, proj_l = weights[l]
        dx1, du, rn, ds, dqv, d_pl_norm_g[l] = _bwd_out(dx, x1s[l], p3, pl_norm_g3, gate_l, proj_l, w_out_l, l, tm_mm)
        (g_proj,) = _wgrad(p3, dqv, 1, 1, N_CHIPS, tm_wg, "wgrad_proj", a_layer=l)
        (g_gate,) = _wgrad(rn, ds, 1, 1, 1, tm_wg, "wgrad_gate")
        (g_out,) = _wgrad(us[l], dx1, 1, 1, 1, tm_wg_out, "wgrad_out")
        comm = take()
        if l % 2 == 0:
            dproj, dcw, dcb, dlg, dlb = settle(
                _bwd_conv(du, projs[l], y1s[l], conv_w_bwd, conv_ln_g3, conv_ln_b3, j, bsz, seq, tm_mix, comm), comm)
            d_conv[j] = (dcw, dcb, dlg, dlb)
        else:
            dproj, dsw, dsbt, dlg, dlb = settle(
                _bwd_sgu(du, projs[l], sgu_ln_g_full, sgu_ln_b_full, sgu_w, sgu_bt, j, tm_mix, comm), comm)
            d_sgu[j] = (dsw, dsbt, dlg, dlb)
        (g_in,) = _wgrad(hs[l], dproj, 1, N_CHIPS, 1, tm_wg, "wgrad_in")
        local = [g_in.reshape(N_CHIPS, 2, d // 2, n4), g_out.reshape(N_CHIPS, 2, e // (2 * N_CHIPS), d),
                 g_gate.reshape(N_CHIPS, 2, d // (2 * N_CHIPS), d), g_proj.reshape(N_CHIPS, 2, ple // 2, dq)]
        while pending:
            run_alone("reduce_step")
        pending.append(("swap", l, local))
        if l == 0:
            run_alone("swap_last")
        comm = take()
        dx, d_norm_g[l] = settle(_bwd_in(dproj, dx1, xs[l], norm_g3, w_in_l, l, tm_mm, comm), comm)
    while pending:
        run_alone("reduce_tail")
    grad_x = dx.reshape(bsz, seq, d)

    d_conv_w = jnp.stack([jnp.transpose(dc[0], (1, 0, 2)).reshape(kp, e)[:k_taps] for dc in d_conv])
    d_conv_b = jnp.stack([dc[1][0] for dc in d_conv])
    d_conv_ln_g = jnp.stack([dc[2][0] for dc in d_conv])
    d_conv_ln_b = jnp.stack([dc[3][0] for dc in d_conv])
    d_sgu_w = jnp.stack([dsg[0] for dsg in d_sgu])
    d_sgu_b = jnp.stack([jnp.transpose(dsg[1]) for dsg in d_sgu])
    d_sgu_ln_g = jnp.stack([dsg[2][0] for dsg in d_sgu])
    d_sgu_ln_b = jnp.stack([dsg[3][0] for dsg in d_sgu])
    small_grads = [jnp.concatenate(d_norm_g), d_conv_w, d_conv_b, d_conv_ln_g, d_conv_ln_b, d_sgu_ln_g, d_sgu_ln_b,
                   d_sgu_w, d_sgu_b, jnp.concatenate(d_pl_norm_g), d_final_g]
    small_shapes = [a.shape for a in small_grads]
    packed = _pack_rows(small_grads, d)
    pack_rows = packed.shape[0] + ((-packed.shape[0]) % (8 * SUBLANES))
    packed = _pack_rows(small_grads, d, pack_rows)
    gl_small = packed.reshape(N_CHIPS, 2, pack_rows // 8, d)
    (small_sibling,) = _run_comm(_swap_comm([gl_small]), "swap_small")
    small_pair = by_chip(_pair_sums([gl_small], [small_sibling], c_arr, F32)[0])

    (small_arrived,) = _run_comm(_scatter_comm([small_pair]), "scatter_small")
    reduced = [lax.empty((depth, 2) + gl.shape[2:], F32) for gl in locals_[0]]
    for l in range(depth):
        reduced = _chip_sums(locals_[l], siblings[l], arrived[l], ck_arr, reduced, l)
    (small_both,) = _chip_sums([gl_small], [small_sibling], [small_arrived], ck_arr,
                               [lax.empty((1, 2) + gl_small.shape[2:], F32)], 0)
    small_mine = lax.dynamic_slice_in_dim(small_both, my_c, 1, axis=1)
    reduced = _swap_pieces(reduced)
    small_all = _gather_pieces(small_mine)
    small_all = lax.dynamic_update_slice(small_all, small_mine, (my_k, my_c, 0, 0)).reshape(pack_rows, d)
    small_red = _unpack_rows(small_all, small_shapes, d)
    (gr_norm_g, gr_conv_w, gr_conv_b, gr_conv_ln_g, gr_conv_ln_b, gr_sgu_ln_g, gr_sgu_ln_b, gr_sgu_w, gr_sgu_b,
     gr_pl_norm_g, gr_final_g) = small_red
    gr_final_g = gr_final_g.reshape(d)
    gr_conv_w = lax.dynamic_slice_in_dim(gr_conv_w, my_k * ec, ec, axis=2)
    gr_sgu_ln_g = lax.dynamic_slice_in_dim(gr_sgu_ln_g, my_k * ec, ec, axis=1)
    gr_sgu_ln_b = lax.dynamic_slice_in_dim(gr_sgu_ln_b, my_k * ec, ec, axis=1)

    def shard_update(w, g, m, v):
        flat = lambda a: a.reshape(-1, w.shape[-1])
        return [o.reshape(w.shape) for o in _adamw(flat(w), flat(g), flat(m), flat(v))]

    up_in = shard_update(w_in, reduced[0], m_w_in, v_w_in)
    up_out = shard_update(w_out, reduced[1], m_w_out, v_w_out)
    up_gate = shard_update(pl_gate_w, reduced[2], m_pl_gate_w, v_pl_gate_w)
    up_proj = shard_update(pl_proj_w, reduced[3], m_pl_proj_w, v_pl_proj_w)

    small_names = ["norm_g", "conv_w", "conv_b", "conv_ln_g", "conv_ln_b", "sgu_ln_g", "sgu_ln_b", "sgu_w", "sgu_b",
                   "pl_norm_g", "final_g"]
    small_w_list = [norm_g, conv_w, conv_b, conv_ln_g, conv_ln_b, sgu_ln_g, sgu_ln_b, sgu_w, sgu_b, pl_norm_g, final_g]
    small_m_list = [m_norm_g, m_conv_w, m_conv_b, m_conv_ln_g, m_conv_ln_b, m_sgu_ln_g, m_sgu_ln_b, m_sgu_w, m_sgu_b,
                    m_pl_norm_g, m_final_g]
    small_v_list = [v_norm_g, v_conv_w, v_conv_b, v_conv_ln_g, v_conv_ln_b, v_sgu_ln_g, v_sgu_ln_b, v_sgu_w, v_sgu_b,
                    v_pl_norm_g, v_final_g]
    small_g_list = [gr_norm_g, gr_conv_w, gr_conv_b, gr_conv_ln_g, gr_conv_ln_b, gr_sgu_ln_g, gr_sgu_ln_b, gr_sgu_w,
                    gr_sgu_b, gr_pl_norm_g, gr_final_g]
    width = ec
    shapes_local = [a.shape for a in small_w_list]
    outs_small = _adamw(_pack_rows(small_w_list, width), _pack_rows(small_g_list, width),
                        _pack_rows(small_m_list, width), _pack_rows(small_v_list, width))
    unpacked = [_unpack_rows(o, shapes_local, width) for o in outs_small]
    ups = {name: [unpacked[kind][i] for kind in range(4)] for i, name in enumerate(small_names)}
    ups["w_in"], ups["w_out"], ups["pl_gate_w"], ups["pl_proj_w"] = up_in, up_out, up_gate, up_proj

    order = ["norm_g", "w_in", "w_out", "conv_w", "conv_b", "conv_ln_g", "conv_ln_b", "sgu_ln_g", "sgu_ln_b", "sgu_w",
             "sgu_b", "pl_norm_g", "pl_gate_w", "pl_proj_w", "final_g"]
    result = [loss, grad_x]
    for kind in range(4):
        result.extend(ups[name][kind] for name in order)
    return tuple(result)
```

```python
import functools

import jax
import jax.numpy as jnp
from jax import lax
from jax.experimental import pallas as pl
from jax.experimental.pallas import tpu as pltpu

F32 = jnp.float32
BF16 = jnp.bfloat16
SDS = jax.ShapeDtypeStruct

EPS = 1e-6
CHUNK = 128
GROUPS = 8
HALO = 32
N_CHIPS = 4
LANES = 128
SUBLANES = 8
V7X_VMEM_LIMIT = 56 << 20

ADAM_LR = 0.001
ADAM_B1 = 0.9
ADAM_B2 = 0.999
ADAM_EPS = 1e-08
ADAM_WD = 0.01
ADAM_STEP = 10

MESH_IDS = pl.DeviceIdType.MESH
ANY = pl.BlockSpec(memory_space=pl.ANY)


def _params(n_axes):
    return pltpu.CompilerParams(dimension_semantics=("arbitrary",) * n_axes, vmem_limit_bytes=V7X_VMEM_LIMIT)


def _const(shape):
    zeros = (0,) * len(shape)
    return pl.BlockSpec(shape, lambda *_: zeros)


def _layer(shape, layer):
    zeros = (0,) * len(shape)
    return pl.BlockSpec((None,) + tuple(shape), lambda *_: (layer,) + zeros, pipeline_mode=pl.Buffered(1))


class _Comm:
    def __init__(self, operands, out_shape, sems, start, finish, relay=None, relay_lead=0):
        self.operands, self.out_shape, self.sems, self.start, self.finish = operands, out_shape, sems, start, finish
        self.relay, self.relay_lead = relay, relay_lead


def _call(body, *, name, grid, in_specs, out_specs, out_shape, operands, scratch_shapes=(), comm=None):
    in_specs, out_specs, out_shape, scratch_shapes = list(in_specs), list(out_specs), list(out_shape), list(scratch_shapes)
    if comm is None:
        return pl.pallas_call(body, name=name, grid=grid, in_specs=in_specs, out_specs=out_specs, out_shape=out_shape,
                              scratch_shapes=scratch_shapes, compiler_params=_params(len(grid)))(*operands)
    n_in, n_out, n_sc = len(in_specs), len(out_specs), len(scratch_shapes)
    ci, co = len(comm.operands), len(comm.out_shape)

    def hosted(*refs):
        ins, cins = refs[:n_in], refs[n_in:n_in + ci]
        outs, couts = refs[n_in + ci:n_in + ci + n_out], refs[n_in + ci + n_out:n_in + ci + n_out + co]
        scratch = refs[n_in + ci + n_out + co:n_in + ci + n_out + co + n_sc]
        sems = refs[n_in + ci + n_out + co + n_sc:]
        n_steps = functools.reduce(lambda a, b: a * b, grid)
        step = functools.reduce(lambda acc, a: acc * grid[a] + pl.program_id(a), range(len(grid)), 0)
        first, last = step == 0, step == n_steps - 1

        @pl.when(first)
        def _():
            comm.start(cins, couts, sems)

        if comm.relay is not None:
            @pl.when(step == max(n_steps - 1 - comm.relay_lead, 0))
            def _():
                comm.relay(cins, couts, sems)

        body(*ins, *outs, *scratch)

        @pl.when(last)
        def _():
            comm.finish(cins, couts, sems)

    res = pl.pallas_call(
        hosted, name=name, grid=grid, in_specs=in_specs + [ANY] * ci, out_specs=out_specs + [ANY] * co,
        out_shape=out_shape + list(comm.out_shape), scratch_shapes=scratch_shapes + list(comm.sems),
        compiler_params=_params(len(grid)))(*operands, *comm.operands)
    return res[:n_out], res[n_out:]


def _run_comm(comm, name):
    ci, co = len(comm.operands), len(comm.out_shape)

    def body(*refs):
        comm.start(refs[:ci], refs[ci:ci + co], refs[ci + co:])
        if comm.relay is not None:
            comm.relay(refs[:ci], refs[ci:ci + co], refs[ci + co:])
        comm.finish(refs[:ci], refs[ci:ci + co], refs[ci + co:])

    return pl.pallas_call(body, name=name, in_specs=[ANY] * ci, out_specs=[ANY] * co, out_shape=list(comm.out_shape),
                          scratch_shapes=list(comm.sems))(*comm.operands)


def _sigmoid(v):
    return jax.nn.sigmoid(v)


def _dsilu(v, s):
    return s * (1.0 + v * (1.0 - s))


def _gelu_parts(v):
    cdf = 0.5 * (1.0 + lax.erf(v * 0.7071067811865476))
    pdf = jnp.exp2(v * v * -0.7213475204444817) * 0.3989422804014327
    return v * cdf, cdf + v * pdf


def _gelu(v):
    return 0.5 * v * (1.0 + lax.erf(v * 0.7071067811865476))


def _rms_stats(x):
    r = lax.rsqrt(jnp.mean(x * x, axis=-1, keepdims=True) + EPS)
    return r, x * r


def _rms_bwd(dy, g, r, xh):
    gdy = dy * g
    return r * (gdy - xh * jnp.mean(xh * gdy, axis=-1, keepdims=True))


def _ln_stats(x):
    mu = jnp.mean(x, axis=-1, keepdims=True)
    xc = x - mu
    rs = lax.rsqrt(jnp.mean(xc * xc, axis=-1, keepdims=True) + EPS)
    return rs, xc * rs


def _ln_bwd(dy, g, rs, xh):
    dxh = dy * g
    return rs * (dxh - jnp.mean(dxh, axis=-1, keepdims=True) - xh * jnp.mean(dxh * xh, axis=-1, keepdims=True))


def _dot(a, b):
    return jnp.dot(a, b, preferred_element_type=F32)


def _dot_nt(a, b):
    return lax.dot_general(a, b, (((1,), (1,)), ((), ())), preferred_element_type=F32)


def _dot_tn(a, b):
    return lax.dot_general(a, b, (((0,), (0,)), ((), ())), preferred_element_type=F32)


def _fwd_in(x, norm_g, w_in_full, layer, tm, comm=None):
    t, d = x.shape
    _, nk, _, n4 = w_in_full.shape

    def body(x_ref, g_ref, w_ref, h_ref, proj_ref):
        r, xh = _rms_stats(x_ref[...])
        h = (xh * g_ref[...]).astype(BF16)
        h_ref[...] = h
        for k in range(nk):
            proj_ref[:, k * n4:(k + 1) * n4] = _dot(h, w_ref[k]).astype(BF16)

    return _call(
        body, name="fwd_in", grid=(t // tm,),
        in_specs=[pl.BlockSpec((tm, d), lambda i: (i, 0)), _layer((1, d), layer), _layer((nk, d, n4), 0)],
        out_specs=[pl.BlockSpec((tm, d), lambda i: (i, 0)), pl.BlockSpec((tm, nk * n4), lambda i: (i, 0))],
        out_shape=[SDS((t, d), BF16), SDS((t, nk * n4), BF16)],
        operands=(x, norm_g, w_in_full), comm=comm)


def _halo_maps(nt, hb, n_halo_blocks):
    def prev(b, i):
        return (jnp.maximum((b * nt + i) * hb - 1, 0), 0)

    def nxt(b, i):
        return (jnp.minimum((b * nt + i + 1) * hb, n_halo_blocks - 1), 0)

    return prev, nxt


TAP_TILE_VREGS = 16


def _tap_rows(cb, tm):
    return min(TAP_TILE_VREGS * SUBLANES * LANES // cb, tm)


def _conv_taps(src_ref, w_ref, dst_ref, cb_idx, n_rows, first, reverse):
    k_taps = w_ref.shape[1] - 1
    rb = min(TAP_TILE_VREGS * SUBLANES, n_rows)
    for r0, l0 in [(r, l) for l in range(0, w_ref.shape[2], LANES) for r in range(0, n_rows, rb)]:
        cols = slice(l0, l0 + LANES)
        acc = None
        for res in range(SUBLANES):
            rows = rb + (SUBLANES if res else 0)
            group = None
            for k in range(k_taps):
                off = first + k
                if off % SUBLANES != res:
                    continue
                wk = w_ref[cb_idx, pl.ds((k_taps - 1 - k) if reverse else k, 1), cols]
                term = wk * src_ref[cb_idx, pl.ds(r0 + off - res, rows), cols]
                group = term if group is None else group + term
            if group is None:
                continue
            part = group[res:res + rb] if res else group
            acc = part if acc is None else acc + part
        dst_ref[cb_idx, pl.ds(r0, rb), cols] = acc


def _fwd_conv(proj, conv_w_blk, conv_b, ln_g, ln_b, layer, bsz, seq, tm, comm=None):
    t, e3 = proj.shape
    e = e3 // 3
    nt = seq // tm
    hb = tm // HALO
    _, ncb, kp, cb = conv_w_blk.shape
    rb = _tap_rows(cb, tm)
    prev, _ = _halo_maps(nt, hb, t // HALO)

    def body(proj_ref, halo_ref, w_ref, b_ref, g_ref, lb_ref, u_ref, y1_ref, y0s, y1s):
        i = pl.program_id(1)
        a = proj_ref[:, 0:e].astype(F32)
        b = proj_ref[:, e:2 * e].astype(F32)
        y0 = a * _sigmoid(b)
        ah = halo_ref[:, 0:e].astype(F32)
        bh = halo_ref[:, e:2 * e].astype(F32)
        y0h = jnp.where(i > 0, ah * _sigmoid(bh), 0.0)
        for c in range(ncb):
            y0s[c, 0:HALO, :] = y0h[:, c * cb:(c + 1) * cb]
            y0s[c, HALO:HALO + tm, :] = y0[:, c * cb:(c + 1) * cb]

        def per_block(c, carry):
            _conv_taps(y0s, w_ref, y1s, c, tm, HALO - (kp - 2), False)
            return carry

        lax.fori_loop(0, ncb, per_block, 0)
        y1 = jnp.concatenate([y1s[c] for c in range(ncb)], axis=1) + b_ref[...]
        y1_ref[...] = y1
        rs, xh = _ln_stats(y1)
        y2 = xh * g_ref[...] + lb_ref[...]
        y = y2 * _sigmoid(y2)
        z = proj_ref[:, 2 * e:3 * e].astype(F32)
        u_ref[...] = (y * (z * _sigmoid(z))).astype(BF16)

    return _call(
        body, name="fwd_conv", grid=(bsz, nt),
        in_specs=[pl.BlockSpec((tm, e3), lambda b, i: (b * nt + i, 0)),
                  pl.BlockSpec((HALO, 2 * e), prev),
                  _layer((ncb, kp, cb), layer), _layer((1, e), layer), _layer((1, e), layer), _layer((1, e), layer)],
        out_specs=[pl.BlockSpec((tm, e), lambda b, i: (b * nt + i, 0)), pl.BlockSpec((tm, e), lambda b, i: (b * nt + i, 0))],
        out_shape=[SDS((t, e), BF16), SDS((t, e), F32)],
        scratch_shapes=[pltpu.VMEM((ncb, HALO + tm, cb), F32), pltpu.VMEM((ncb, tm, cb), F32)],
        operands=(proj, proj, conv_w_blk, conv_b, ln_g, ln_b), comm=comm)


def _tril_mask():
    rows = lax.broadcasted_iota(jnp.int32, (CHUNK, CHUNK), 0)
    cols = lax.broadcasted_iota(jnp.int32, (CHUNK, CHUNK), 1)
    return rows >= cols


def _fwd_sgu(proj, ln_g, ln_b, sgu_w, sgu_bt, layer, tm, comm=None):
    t, e3 = proj.shape
    e = e3 // 3
    gw = e // GROUPS
    nch = tm // CHUNK

    def body(proj_ref, g_ref, lb_ref, w_ref, bt_ref, u_ref, mixed):
        a = proj_ref[:, 0:e].astype(F32)
        b = proj_ref[:, e:2 * e].astype(F32)
        z = proj_ref[:, 2 * e:3 * e].astype(F32)
        rs, xh = _ln_stats(_gelu(b))
        v = (xh * g_ref[...] + lb_ref[...]).astype(BF16)
        mask = _tril_mask()
        for g in range(GROUPS):
            wm = jnp.where(mask, w_ref[g], 0.0).astype(BF16)
            bias = bt_ref[:, g:g + 1]
            for n in range(nch):
                blk = v[n * CHUNK:(n + 1) * CHUNK, g * gw:(g + 1) * gw]
                mixed[n * CHUNK:(n + 1) * CHUNK, g * gw:(g + 1) * gw] = _dot(wm, blk) + bias
        y = _gelu(a) * mixed[...]
        u_ref[...] = (y * (z * _sigmoid(z))).astype(BF16)

    return _call(
        body, name="fwd_sgu", grid=(t // tm,),
        in_specs=[pl.BlockSpec((tm, e3), lambda i: (i, 0)), _layer((1, e), layer), _layer((1, e), layer),
                  _layer((GROUPS, CHUNK, CHUNK), layer), _layer((CHUNK, GROUPS), layer)],
        out_specs=[pl.BlockSpec((tm, e), lambda i: (i, 0))],
        out_shape=[SDS((t, e), BF16)],
        scratch_shapes=[pltpu.VMEM((tm, e), F32)],
        operands=(proj, ln_g, ln_b, sgu_w, sgu_bt), comm=comm)


def _ple_forward(x1, p_ref, plg_ref, gw_ref, pw_ref):
    nk, _, dq = pw_ref.shape
    r, xh = _rms_stats(x1)
    rn = (xh * plg_ref[...]).astype(BF16)
    gate = _sigmoid(_dot(rn, gw_ref[...]))
    pb = p_ref[...].astype(BF16)
    q = jnp.concatenate([_dot(pb, pw_ref[k]) for k in range(nk)], axis=1)
    return r, xh, rn, gate, q


def _fwd_out(x, u, w_out_full, pl_norm_g, gate_w_full, p, proj_w_full, layer, tm):
    t, d = x.shape
    e = u.shape[1]
    ple = p.shape[-1]
    nk, dq = proj_w_full.shape[1], proj_w_full.shape[3]

    def body(x_ref, u_ref, wo_ref, plg_ref, gw_ref, p_ref, pw_ref, x1_ref, x2_ref):
        x1 = x_ref[...] + _dot(u_ref[...], wo_ref[...])
        x1_ref[...] = x1
        _, _, _, gate, q = _ple_forward(x1, p_ref, plg_ref, gw_ref, pw_ref)
        x2_ref[...] = x1 + gate * q

    return _call(
        body, name="fwd_out", grid=(t // tm,),
        in_specs=[pl.BlockSpec((tm, d), lambda i: (i, 0)), pl.BlockSpec((tm, e), lambda i: (i, 0)),
                  _layer((e, d), 0), _layer((1, d), layer), _layer((d, d), 0),
                  pl.BlockSpec((None, tm, ple), lambda i: (layer, i, 0)), _layer((nk, ple, dq), 0)],
        out_specs=[pl.BlockSpec((tm, d), lambda i: (i, 0)), pl.BlockSpec((tm, d), lambda i: (i, 0))],
        out_shape=[SDS((t, d), F32), SDS((t, d), F32)],
        operands=(x, u, w_out_full, pl_norm_g, gate_w_full, p, proj_w_full))


def _loss_head(x, final_g, target, tm):
    t, d = x.shape
    n_steps = t // tm

    def body(x_ref, g_ref, tgt_ref, loss_ref, dx_ref, dg_ref, sq_acc):
        i = pl.program_id(0)

        @pl.when(i == 0)
        def _():
            sq_acc[...] = jnp.zeros_like(sq_acc)
            dg_ref[...] = jnp.zeros_like(dg_ref)

        g = g_ref[...]
        r, xh = _rms_stats(x_ref[...])
        diff = xh * g - tgt_ref[...]
        sq_acc[...] += jnp.sum(diff * diff, axis=0, keepdims=True)
        dout = diff * (1.0 / d)
        dg_ref[...] += jnp.sum(dout * xh, axis=0, keepdims=True)
        dx_ref[...] = _rms_bwd(dout, g, r, xh)

        @pl.when(i == n_steps - 1)
        def _():
            loss_ref[...] = jnp.sum(sq_acc[...], axis=1, keepdims=True) * (0.5 / d)

    return pl.pallas_call(
        body, name="loss_head", grid=(n_steps,),
        in_specs=[pl.BlockSpec((tm, d), lambda i: (i, 0)), _const((1, d)), pl.BlockSpec((tm, d), lambda i: (i, 0))],
        out_specs=[_const((1, 1)), pl.BlockSpec((tm, d), lambda i: (i, 0)), _const((1, d))],
        out_shape=[SDS((1, 1), F32), SDS((t, d), F32), SDS((1, d), F32)],
        scratch_shapes=[pltpu.VMEM((1, d), F32)],
        compiler_params=_params(1),
    )(x, final_g, target)


def _bwd_out(dx2, x1, p, pl_norm_g, gate_w_full, proj_w_full, w_out_full, layer, tm):
    t, d = dx2.shape
    e = w_out_full.shape[1]
    ple = p.shape[-1]
    nk, dq_w = proj_w_full.shape[1], proj_w_full.shape[3]

    def body(dx2_ref, x1_ref, p_ref, plg_ref, gw_ref, pw_ref, wo_ref, dx1_ref, du_ref, rn_ref, ds_ref, dq_ref, dplg_ref):
        @pl.when(pl.program_id(0) == 0)
        def _():
            dplg_ref[...] = jnp.zeros_like(dplg_ref)

        dx2v = dx2_ref[...]
        r, xh, rn, gate, q = _ple_forward(x1_ref[...], p_ref, plg_ref, gw_ref, pw_ref)
        rn_ref[...] = rn
        dq_ref[...] = (dx2v * gate).astype(BF16)
        ds = (dx2v * q * gate * (1.0 - gate)).astype(BF16)
        ds_ref[...] = ds
        dr = _dot_nt(ds, gw_ref[...])
        dplg_ref[...] += jnp.sum(dr * xh, axis=0, keepdims=True)
        dx1 = dx2v + _rms_bwd(dr, plg_ref[...], r, xh)
        dx1_ref[...] = dx1
        du_ref[...] = _dot_nt(dx1.astype(BF16), wo_ref[...]).astype(BF16)

    row = lambda w: pl.BlockSpec((tm, w), lambda i: (i, 0))
    return _call(
        body, name="bwd_out", grid=(t // tm,),
        in_specs=[row(d), row(d), pl.BlockSpec((None, tm, ple), lambda i: (layer, i, 0)),
                  _layer((1, d), layer), _layer((d, d), 0), _layer((nk, ple, dq_w), 0), _layer((e, d), 0)],
        out_specs=[row(d), row(e), row(d), row(d), row(d), _const((1, d))],
        out_shape=[SDS((t, d), F32), SDS((t, e), BF16), SDS((t, d), BF16), SDS((t, d), BF16), SDS((t, d), BF16),
                   SDS((1, d), F32)],
        operands=(dx2, x1, p, pl_norm_g, gate_w_full, proj_w_full, w_out_full))


def _bwd_conv(du, proj, y1, conv_w_blk, ln_g, ln_b, layer, bsz, seq, tm, comm=None):
    t, e3 = proj.shape
    e = e3 // 3
    nt = seq // tm
    hb = tm // HALO
    _, ncb, kp, cb = conv_w_blk.shape
    rb = _tap_rows(cb, tm)
    k_taps = kp - 1
    prev, nxt = _halo_maps(nt, hb, t // HALO)
    z_halo = lambda b, i: (nxt(b, i)[0], 2)

    def ln_silu_bwd(du_v, z_v, y1_v, g, lb):
        rs, xh = _ln_stats(y1_v)
        y2 = xh * g + lb
        sg = _sigmoid(y2)
        sz = _sigmoid(z_v)
        dy = du_v * (z_v * sz)
        dy2 = dy * _dsilu(y2, sg)
        return _ln_bwd(dy2, g, rs, xh), dy2, xh, du_v * (y2 * sg) * _dsilu(z_v, sz)

    def body(du_ref, proj_ref, y1_ref, duh_ref, zh_ref, y1h_ref, abh_ref, w_ref, g_ref, lb_ref,
             dproj_ref, dw_ref, dcb_ref, dg_ref, dlb_ref, y0s, dy1s, dy0s, ysh):
        b_id, i = pl.program_id(0), pl.program_id(1)

        @pl.when((b_id == 0) & (i == 0))
        def _():
            dw_ref[...] = jnp.zeros_like(dw_ref)
            dcb_ref[...] = jnp.zeros_like(dcb_ref)
            dg_ref[...] = jnp.zeros_like(dg_ref)
            dlb_ref[...] = jnp.zeros_like(dlb_ref)

        g, lb = g_ref[...], lb_ref[...]
        a = proj_ref[:, 0:e].astype(F32)
        b = proj_ref[:, e:2 * e].astype(F32)
        z = proj_ref[:, 2 * e:3 * e].astype(F32)
        sb = _sigmoid(b)
        y0 = a * sb
        dy1, dy2, xh, dz = ln_silu_bwd(du_ref[...].astype(F32), z, y1_ref[...], g, lb)
        dproj_ref[:, 2 * e:3 * e] = dz.astype(BF16)
        dg_ref[...] += jnp.sum(dy2 * xh, axis=0, keepdims=True)
        dlb_ref[...] += jnp.sum(dy2, axis=0, keepdims=True)
        dcb_ref[...] += jnp.sum(dy1, axis=0, keepdims=True)
        dy1h, _, _, _ = ln_silu_bwd(duh_ref[...].astype(F32), zh_ref[...].astype(F32), y1h_ref[...], g, lb)
        dy1h = jnp.where(i < nt - 1, dy1h, 0.0)
        ah = abh_ref[:, 0:e].astype(F32)
        bh = abh_ref[:, e:2 * e].astype(F32)
        y0h = jnp.where(i > 0, ah * _sigmoid(bh), 0.0)
        for c in range(ncb):
            cols = slice(c * cb, (c + 1) * cb)
            y0s[c, 0:HALO, :] = y0h[:, cols]
            y0s[c, HALO:HALO + tm, :] = y0[:, cols]
            dy1s[c, 0:tm, :] = dy1[:, cols]
            dy1s[c, tm:tm + HALO, :] = dy1h[:, cols]

        def per_block(c, carry):
            _conv_taps(dy1s, w_ref, dy0s, c, tm, 0, True)
            for res in range(1, SUBLANES):
                ysh[res - 1] = y0s[c, pl.ds(res, tm + HALO - SUBLANES), :]
            for k in range(k_taps):
                off = HALO - (k_taps - 1) + k
                res = off % SUBLANES
                acc = None
                for r0 in range(0, tm, rb):
                    rows = pl.ds(r0 + off - res, rb)
                    shifted = ysh[res - 1, rows, :] if res else y0s[c, rows, :]
                    term = dy1s[c, pl.ds(r0, rb), :] * shifted
                    acc = term if acc is None else acc + term
                dw_ref[c, pl.ds(k, 1), :] += jnp.sum(acc, axis=0, keepdims=True)
            return carry

        lax.fori_loop(0, ncb, per_block, 0)
        dy0 = jnp.concatenate([dy0s[c] for c in range(ncb)], axis=1)
        dproj_ref[:, 0:e] = (dy0 * sb).astype(BF16)
        dproj_ref[:, e:2 * e] = (dy0 * a * sb * (1.0 - sb)).astype(BF16)

    tile = lambda w: pl.BlockSpec((tm, w), lambda b, i: (b * nt + i, 0))
    return _call(
        body, name="bwd_conv", grid=(bsz, nt),
        in_specs=[tile(e), tile(e3), tile(e),
                  pl.BlockSpec((HALO, e), nxt), pl.BlockSpec((HALO, e), z_halo), pl.BlockSpec((HALO, e), nxt),
                  pl.BlockSpec((HALO, 2 * e), prev),
                  _layer((ncb, kp, cb), layer), _layer((1, e), layer), _layer((1, e), layer)],
        out_specs=[tile(e3), _const((ncb, kp, cb)), _const((1, e)), _const((1, e)), _const((1, e))],
        out_shape=[SDS((t, e3), BF16), SDS((ncb, kp, cb), F32), SDS((1, e), F32), SDS((1, e), F32), SDS((1, e), F32)],
        scratch_shapes=[pltpu.VMEM((ncb, HALO + tm, cb), F32), pltpu.VMEM((ncb, tm + HALO, cb), F32),
                        pltpu.VMEM((ncb, tm, cb), F32), pltpu.VMEM((SUBLANES - 1, tm + HALO - SUBLANES, cb), F32)],
        operands=(du, proj, y1, du, proj, y1, proj, conv_w_blk, ln_g, ln_b), comm=comm)


def _bwd_sgu(du, proj, ln_g, ln_b, sgu_w, sgu_bt, layer, tm, comm=None):
    t, e3 = proj.shape
    e = e3 // 3
    gw = e // GROUPS
    nch = tm // CHUNK

    def body(du_ref, proj_ref, g_ref, lb_ref, w_ref, bt_ref, dproj_ref, dw_ref, dbt_ref, dg_ref, dlb_ref, mixed, dmix, dv):
        @pl.when(pl.program_id(0) == 0)
        def _():
            dw_ref[...] = jnp.zeros_like(dw_ref)
            dbt_ref[...] = jnp.zeros_like(dbt_ref)
            dg_ref[...] = jnp.zeros_like(dg_ref)
            dlb_ref[...] = jnp.zeros_like(dlb_ref)

        g, lb = g_ref[...], lb_ref[...]
        a = proj_ref[:, 0:e].astype(F32)
        b = proj_ref[:, e:2 * e].astype(F32)
        z = proj_ref[:, 2 * e:3 * e].astype(F32)
        ug, dug = _gelu_parts(a)
        vb, dvb_db = _gelu_parts(b)
        rs, xh = _ln_stats(vb)
        v = (xh * g + lb).astype(BF16)
        mask = _tril_mask()
        for gi in range(GROUPS):
            wm = jnp.where(mask, w_ref[gi], 0.0).astype(BF16)
            bias = bt_ref[:, gi:gi + 1]
            for c in range(nch):
                blk = v[c * CHUNK:(c + 1) * CHUNK, gi * gw:(gi + 1) * gw]
                mixed[c * CHUNK:(c + 1) * CHUNK, gi * gw:(gi + 1) * gw] = _dot(wm, blk) + bias
        mx = mixed[...]
        sz = _sigmoid(z)
        duv = du_ref[...].astype(F32)
        dy = duv * (z * sz)
        dproj_ref[:, 2 * e:3 * e] = (duv * (ug * mx) * _dsilu(z, sz)).astype(BF16)
        dproj_ref[:, 0:e] = (dy * mx * dug).astype(BF16)
        dmix[...] = dy * ug
        for gi in range(GROUPS):
            wm = jnp.where(mask, w_ref[gi], 0.0).astype(BF16)
            dw_acc = None
            db_acc = None
            for c in range(nch):
                rows, cols = slice(c * CHUNK, (c + 1) * CHUNK), slice(gi * gw, (gi + 1) * gw)
                dm = dmix[rows, cols]
                dmb = dm.astype(BF16)
                dw_n = _dot_nt(dmb, v[rows, cols])
                db_n = jnp.sum(dm, axis=1, keepdims=True)
                dw_acc = dw_n if dw_acc is None else dw_acc + dw_n
                db_acc = db_n if db_acc is None else db_acc + db_n
                dv[rows, cols] = _dot_tn(wm, dmb)
            dw_ref[gi] += jnp.where(mask, dw_acc, 0.0)
            dbt_ref[:, gi:gi + 1] += db_acc
        dvv = dv[...]
        dg_ref[...] += jnp.sum(dvv * xh, axis=0, keepdims=True)
        dlb_ref[...] += jnp.sum(dvv, axis=0, keepdims=True)
        dproj_ref[:, e:2 * e] = (_ln_bwd(dvv, g, rs, xh) * dvb_db).astype(BF16)

    return _call(
        body, name="bwd_sgu", grid=(t // tm,),
        in_specs=[pl.BlockSpec((tm, e), lambda i: (i, 0)), pl.BlockSpec((tm, e3), lambda i: (i, 0)),
                  _layer((1, e), layer), _layer((1, e), layer),
                  _layer((GROUPS, CHUNK, CHUNK), layer), _layer((CHUNK, GROUPS), layer)],
        out_specs=[pl.BlockSpec((tm, e3), lambda i: (i, 0)), _const((GROUPS, CHUNK, CHUNK)), _const((CHUNK, GROUPS)),
                   _const((1, e)), _const((1, e))],
        out_shape=[SDS((t, e3), BF16), SDS((GROUPS, CHUNK, CHUNK), F32), SDS((CHUNK, GROUPS), F32),
                   SDS((1, e), F32), SDS((1, e), F32)],
        scratch_shapes=[pltpu.VMEM((tm, e), F32), pltpu.VMEM((tm, e), F32), pltpu.VMEM((tm, e), F32)],
        operands=(du, proj, ln_g, ln_b, sgu_w, sgu_bt), comm=comm)


def _bwd_in(dproj, dx1, x, norm_g, w_in_full, layer, tm, comm=None):
    t, d = x.shape
    _, nk, _, n4 = w_in_full.shape

    def body(dproj_ref, dx1_ref, x_ref, g_ref, w_ref, dx_ref, dg_ref):
        @pl.when(pl.program_id(0) == 0)
        def _():
            dg_ref[...] = jnp.zeros_like(dg_ref)

        dh = None
        for k in range(nk):
            part = _dot_nt(dproj_ref[:, k * n4:(k + 1) * n4], w_ref[k])
            dh = part if dh is None else dh + part
        r, xh = _rms_stats(x_ref[...])
        dg_ref[...] += jnp.sum(dh * xh, axis=0, keepdims=True)
        dx_ref[...] = dx1_ref[...] + _rms_bwd(dh, g_ref[...], r, xh)

    row = lambda w: pl.BlockSpec((tm, w), lambda i: (i, 0))
    return _call(
        body, name="bwd_in", grid=(t // tm,),
        in_specs=[row(nk * n4), row(d), row(d), _layer((1, d), layer), _layer((nk, d, n4), 0)],
        out_specs=[row(d), _const((1, d))],
        out_shape=[SDS((t, d), F32), SDS((1, d), F32)],
        operands=(dproj, dx1, x, norm_g, w_in_full), comm=comm)


def _wgrad(a, b, kblk, nblk, n_split, tm, name, a_layer=None, comm=None):
    t, n = b.shape
    k = a.shape[-1]
    kw, nw = k // kblk, n // nblk
    nws = nw // n_split
    n_steps = t // tm

    def body(a_ref, b_ref, o_ref):
        @pl.when(pl.program_id(2) == 0)
        def _():
            o_ref[...] = jnp.zeros_like(o_ref)

        res = _dot_tn(a_ref[...].astype(BF16), b_ref[...].astype(BF16))
        for s in range(n_split):
            o_ref[s] += res[:, s * nws:(s + 1) * nws]

    if a_layer is None:
        a_spec = pl.BlockSpec((tm, kw), lambda kb, nb, i: (i, kb))
    else:
        a_spec = pl.BlockSpec((None, tm, kw), lambda kb, nb, i: (a_layer, i, kb))
    return _call(
        body, name=name, grid=(kblk, nblk, n_steps),
        in_specs=[a_spec, pl.BlockSpec((tm, nw), lambda kb, nb, i: (i, nb))],
        out_specs=[pl.BlockSpec((None, n_split, kw, nws), lambda kb, nb, i: (kb, nb, 0, 0))],
        out_shape=[SDS((kblk, nblk * n_split, kw, nws), F32)],
        operands=(a, b), comm=comm)


def _row_tile(rows, cols, budget_bytes=2 << 20):
    best = None
    for cand in range(SUBLANES, rows + 1, SUBLANES):
        if rows % cand == 0 and cand * cols * 4 <= budget_bytes:
            best = cand
    return best if best is not None else rows


ROW_CHUNKS = 2


def _pair_sums(grads, recv, my_c, wire_dtype):
    n = len(grads)
    shapes = [(g.shape[2] // ROW_CHUNKS, g.shape[3]) for g in grads]

    def body(c_ref, *refs):
        for j in range(n):
            refs[2 * n + j][...] = (refs[2 * j][...] + refs[2 * j + 1][...]).astype(wire_dtype)

    in_specs, out_specs = [], []
    for th, c in shapes:
        in_specs.append(pl.BlockSpec((None, None, th, c), lambda k, i, c_ref: (k, c_ref[0], i, 0)))
        in_specs.append(pl.BlockSpec((None, None, th, c), lambda k, i, c_ref: (k, 0, i, 0)))
        out_specs.append(pl.BlockSpec((None, th, c), lambda k, i, c_ref: (k, i, 0)))
    grid_spec = pltpu.PrefetchScalarGridSpec(num_scalar_prefetch=1, grid=(N_CHIPS, ROW_CHUNKS),
                                             in_specs=in_specs, out_specs=out_specs)
    operands = [a for pair in zip(grads, recv) for a in pair]
    return pl.pallas_call(body, name="pair_sums", grid_spec=grid_spec,
                          out_shape=[SDS((N_CHIPS,) + g.shape[2:], wire_dtype) for g in grads],
                          compiler_params=_params(2))(my_c, *operands)


def _chip_sums(grads, recv, arrived, my_ck, stacked, layer):
    n = len(grads)
    shapes = [(g.shape[2] // ROW_CHUNKS, g.shape[3]) for g in grads]

    def body(ck_ref, *refs):
        for j in range(n):
            g_ref, r_ref, a1_ref, a2_ref, a3_ref = refs[5 * j:5 * j + 5]
            acc = g_ref[...] + r_ref[...]
            for ref in (a1_ref, a2_ref, a3_ref):
                acc = acc + ref[...].astype(F32)
            refs[6 * n + j][...] = acc

    in_specs, out_specs = [], []
    for th, c in shapes:
        block = (None, None, th, c)
        in_specs.append(pl.BlockSpec(block, lambda i, ck: (ck[1], ck[0], i, 0)))
        in_specs.append(pl.BlockSpec(block, lambda i, ck: (ck[1], 0, i, 0)))
        for flip in (1, 2, 3):
            in_specs.append(pl.BlockSpec(block, lambda i, ck, flip=flip: (0, jnp.bitwise_xor(ck[1], flip), i, 0)))
        out_specs.append(pl.BlockSpec(block, lambda i, ck: (layer, ck[0], i, 0)))
    grid_spec = pltpu.PrefetchScalarGridSpec(num_scalar_prefetch=1, grid=(ROW_CHUNKS,),
                                             in_specs=in_specs + [ANY] * n, out_specs=out_specs)
    operands = [a for g, r, ar in zip(grads, recv, arrived) for a in (g, r, ar, ar, ar)]
    return pl.pallas_call(body, name="chip_sums", grid_spec=grid_spec, out_shape=[SDS(s.shape, F32) for s in stacked],
                          input_output_aliases={1 + 5 * n + j: j for j in range(n)},
                          compiler_params=_params(1))(my_ck, *operands, *stacked)


def _adam_math(w, gv, m, v):
    c1 = 1.0 - ADAM_B1 ** ADAM_STEP
    c2 = 1.0 - ADAM_B2 ** ADAM_STEP
    mn = ADAM_B1 * m + (1.0 - ADAM_B1) * gv
    vn = ADAM_B2 * v + (1.0 - ADAM_B2) * (gv * gv)
    m_hat = mn / c1
    v_hat = vn / c2
    return -ADAM_LR * (m_hat / (jnp.sqrt(v_hat) + ADAM_EPS) + ADAM_WD * w), mn, vn


def _adamw(w, g, m, v):
    rows, cols = w.shape
    tr = _row_tile(rows, cols)

    def body(w_ref, g_ref, m_ref, v_ref, go_ref, d_ref, mo_ref, vo_ref):
        gv = g_ref[...]
        go_ref[...] = gv
        d_ref[...], mo_ref[...], vo_ref[...] = _adam_math(w_ref[...], gv, m_ref[...], v_ref[...])

    spec = pl.BlockSpec((tr, cols), lambda i: (i, 0))
    return pl.pallas_call(
        body, name="adamw", grid=(rows // tr,), in_specs=[spec] * 4, out_specs=[spec] * 4,
        out_shape=[SDS((rows, cols), F32)] * 4, compiler_params=_params(1))(w, g, m, v)


def _place():
    x, y, c = lax.axis_index("x"), lax.axis_index("y"), lax.axis_index("c")
    chips = [(1 - x, y), (x, 1 - y), (1 - x, 1 - y)]
    return x, y, c, 2 * x + y, chips


def _remote(src, dst, send_sem, recv_sem, device):
    return pltpu.make_async_remote_copy(src_ref=src, dst_ref=dst, send_sem=send_sem, recv_sem=recv_sem,
                                        device_id=device, device_id_type=MESH_IDS)


def _gather_comm(items, small=None, relay_lead=0):
    shards = [arr for arr, _ in items]
    layers = [layer for _, layer in items]
    n = len(shards)
    extra = 0 if small is None else 1

    def copies(ins, outs, sems):
        ici_send, ici_recv, d2d_send, d2d_recv, own_send, own_recv = sems
        x, y, c, k, chips = _place()
        sibling = (x, y, 1 - c)
        own, ici_out, ici_in, fwd_out, fwd_in = [], [], [], [], []
        for j in range(n):
            h = ins[j].shape[2] // 2
            mine, theirs = pl.ds(c * h, h), pl.ds((1 - c) * h, h)
            own.append(_remote(ins[j].at[pl.ds(layers[j], 1)], outs[j].at[:, pl.ds(k, 1)], own_send.at[j], own_recv.at[j], sibling))
            for ti, (cx, cy) in enumerate(chips):
                s = 3 * j + ti
                ici_out.append(_remote(ins[j].at[pl.ds(layers[j], 1), :, mine], outs[j].at[:, pl.ds(k, 1), mine],
                                       ici_send.at[s], ici_recv.at[s], (cx, cy, c)))
                landed = outs[j].at[:, pl.ds(2 * cx + cy, 1), mine]
                ici_in.append(_remote(landed, landed, ici_send.at[s], ici_recv.at[s], (cx, cy, c)))
                fwd_out.append(_remote(landed, landed, d2d_send.at[s], d2d_recv.at[s], sibling))
                passed = outs[j].at[:, pl.ds(2 * cx + cy, 1), theirs]
                fwd_in.append(_remote(passed, passed, d2d_send.at[s], d2d_recv.at[s], sibling))
        if extra:
            own.append(_remote(ins[n], outs[n].at[pl.ds(k, 1)], own_send.at[n], own_recv.at[n], sibling))
            for ti, (cx, cy) in enumerate(chips):
                s = 3 * n + ti
                ici_out.append(_remote(ins[n], outs[n].at[pl.ds(k, 1)], ici_send.at[s], ici_recv.at[s], (cx, cy, c)))
                slot = outs[n].at[pl.ds(2 * cx + cy, 1)]
                ici_in.append(_remote(slot, slot, ici_send.at[s], ici_recv.at[s], (cx, cy, c)))
        return own, ici_out, ici_in, fwd_out, fwd_in

    def start(ins, outs, sems):
        own, ici_out, _, _, _ = copies(ins, outs, sems)
        for cp in own + ici_out:
            cp.start()

    def relay(ins, outs, sems):
        _, _, ici_in, fwd_out, _ = copies(ins, outs, sems)
        for idx, cp in enumerate(ici_in):
            cp.wait_recv()
            if idx < len(fwd_out):
                fwd_out[idx].start()

    def finish(ins, outs, sems):
        own, ici_out, _, fwd_out, fwd_in = copies(ins, outs, sems)
        for cp in fwd_in:
            cp.wait_recv()
        for cp in ici_out + fwd_out:
            cp.wait_send()
        for cp in own:
            cp.wait()

    operands = list(shards) + ([small] if extra else [])
    out_shape = [SDS((1, N_CHIPS) + s.shape[2:], s.dtype) for s in shards]
    if extra:
        out_shape.append(SDS((N_CHIPS,) + small.shape[1:], small.dtype))
    sems = [pltpu.SemaphoreType.DMA((3 * (n + extra),)), pltpu.SemaphoreType.DMA((3 * (n + extra),)),
            pltpu.SemaphoreType.DMA((3 * n,)), pltpu.SemaphoreType.DMA((3 * n,)),
            pltpu.SemaphoreType.DMA((n + extra,)), pltpu.SemaphoreType.DMA((n + extra,))]
    return _Comm(operands, out_shape, sems, start, finish, relay, relay_lead)


def _swap_comm(grads):
    n = len(grads)

    def copies(ins, outs, sems):
        send_sem, recv_sem = sems
        x, y, c, _, _ = _place()
        return [_remote(ins[j].at[:, pl.ds(1 - c, 1)], outs[j], send_sem.at[j], recv_sem.at[j], (x, y, 1 - c))
                for j in range(n)]

    def start(ins, outs, sems):
        for cp in copies(ins, outs, sems):
            cp.start()

    def finish(ins, outs, sems):
        for cp in copies(ins, outs, sems):
            cp.wait()

    out_shape = [SDS((g.shape[0], 1) + g.shape[2:], g.dtype) for g in grads]
    return _Comm(list(grads), out_shape, [pltpu.SemaphoreType.DMA((n,)), pltpu.SemaphoreType.DMA((n,))], start, finish)


def _scatter_comm(sums):
    n = len(sums)

    def copies(ins, outs, sems):
        send_sem, recv_sem = sems
        x, y, c, k, chips = _place()
        out, landing = [], []
        for j in range(n):
            for ti, (cx, cy) in enumerate(chips):
                s = 3 * j + ti
                out.append(_remote(ins[j].at[:, pl.ds(2 * cx + cy, 1)], outs[j].at[:, pl.ds(k, 1)],
                                   send_sem.at[s], recv_sem.at[s], (cx, cy, c)))
                slot = outs[j].at[:, pl.ds(2 * cx + cy, 1)]
                landing.append(_remote(slot, slot, send_sem.at[s], recv_sem.at[s], (cx, cy, c)))
        return out, landing

    def start(ins, outs, sems):
        for cp in copies(ins, outs, sems)[0]:
            cp.start()

    def finish(ins, outs, sems):
        out, landing = copies(ins, outs, sems)
        for cp in landing:
            cp.wait_recv()
        for cp in out:
            cp.wait_send()

    out_shape = [SDS(s.shape, s.dtype) for s in sums]
    return _Comm(list(sums), out_shape, [pltpu.SemaphoreType.DMA((3 * n,)), pltpu.SemaphoreType.DMA((3 * n,))], start, finish)


def _swap_pieces(pieces):
    n = len(pieces)

    def body(*refs):
        bufs = refs[n:2 * n]
        send_sem, recv_sem = refs[2 * n:]
        x, y, c, _, _ = _place()
        for j in range(n):
            mine = bufs[j].at[:, pl.ds(c, 1)]
            _remote(mine, mine, send_sem.at[j], recv_sem.at[j], (x, y, 1 - c)).start()
        for j in range(n):
            mine, theirs = bufs[j].at[:, pl.ds(c, 1)], bufs[j].at[:, pl.ds(1 - c, 1)]
            _remote(mine, theirs, send_sem.at[j], recv_sem.at[j], (x, y, 1 - c)).wait()

    return pl.pallas_call(
        body, name="swap_pieces", in_specs=[ANY] * n, out_specs=[ANY] * n,
        out_shape=[SDS(p.shape, p.dtype) for p in pieces], input_output_aliases={j: j for j in range(n)},
        scratch_shapes=[pltpu.SemaphoreType.DMA((n,)), pltpu.SemaphoreType.DMA((n,))],
    )(*pieces)


def _gather_pieces(piece):
    def body(in_ref, out_ref, send_sem, recv_sem):
        x, y, c, k, chips = _place()
        peers = [(x, y, 1 - c)] + [(cx, cy, pc) for (cx, cy) in chips for pc in (c, 1 - c)]
        copies = []
        for ti, peer in enumerate(peers):
            cp = _remote(in_ref, out_ref.at[pl.ds(k, 1), pl.ds(c, 1)], send_sem.at[ti], recv_sem.at[ti], peer)
            cp.start()
            copies.append(cp)
        for ti, (px, py, pc) in enumerate(peers):
            _remote(in_ref, out_ref.at[pl.ds(2 * px + py, 1), pl.ds(pc, 1)], send_sem.at[ti], recv_sem.at[ti],
                    (px, py, pc)).wait_recv()
        for cp in copies:
            cp.wait_send()

    n_peers = 2 * N_CHIPS - 1
    return pl.pallas_call(
        body, name="gather_pieces", in_specs=[ANY], out_specs=ANY,
        out_shape=SDS((N_CHIPS, 2) + piece.shape[2:], piece.dtype),
        scratch_shapes=[pltpu.SemaphoreType.DMA((n_peers,)), pltpu.SemaphoreType.DMA((n_peers,))],
    )(piece)


def _pack_rows(parts, width, total_rows=None):
    rows = []
    for a in parts:
        a2 = a.reshape(-1, width)
        pad = (-a2.shape[0]) % SUBLANES
        rows.append(jnp.pad(a2, ((0, pad), (0, 0))) if pad else a2)
    out = jnp.concatenate(rows, axis=0)
    if total_rows is not None and out.shape[0] < total_rows:
        out = jnp.pad(out, ((0, total_rows - out.shape[0]), (0, 0)))
    return out


def _unpack_rows(packed, shapes, width):
    out, r = [], 0
    for shp in shapes:
        size = 1
        for s in shp:
            size *= s
        nr = size // width
        out.append(packed[r:r + nr].reshape(shp))
        r += nr + ((-nr) % SUBLANES)
    return out


def kernel(x, p, norm_g, w_in, w_out, conv_w, conv_b, conv_ln_g, conv_ln_b, sgu_ln_g, sgu_ln_b, sgu_w, sgu_b, pl_norm_g, pl_gate_w, pl_proj_w, final_g, loss_target, m_norm_g, m_w_in, m_w_out, m_conv_w, m_conv_b, m_conv_ln_g, m_conv_ln_b, m_sgu_ln_g, m_sgu_ln_b, m_sgu_w, m_sgu_b, m_pl_norm_g, m_pl_gate_w, m_pl_proj_w, m_final_g, v_norm_g, v_w_in, v_w_out, v_conv_w, v_conv_b, v_conv_ln_g, v_conv_ln_b, v_sgu_ln_g, v_sgu_ln_b, v_sgu_w, v_sgu_b, v_pl_norm_g, v_pl_gate_w, v_pl_proj_w, v_final_g):
    bsz, seq, d = x.shape
    depth = w_in.shape[0]
    e = w_out.shape[1] * N_CHIPS
    e3 = 3 * e
    n4 = w_in.shape[2]
    ple = p.shape[-1]
    dq = pl_proj_w.shape[2]
    k_taps = conv_w.shape[1]
    kp = k_taps + 1
    n_conv, n_sgu = conv_w.shape[0], sgu_ln_g.shape[0]
    t = bsz * seq
    tm_mm = min(512, seq)
    tm_wg, tm_wg_out = min(2048, t), min(1024, t)
    tm_mix = min(256, seq)
    my_c = lax.axis_index("c")
    my_k = 2 * lax.axis_index("x") + lax.axis_index("y")

    ec = e // N_CHIPS
    small_w = _pack_rows([conv_w.reshape(n_conv * k_taps, ec), sgu_ln_g, sgu_ln_b], ec)[None]
    shards = {"in": w_in.astype(BF16)[:, None], "out": w_out.astype(BF16)[:, None],
              "gate": pl_gate_w.astype(BF16)[:, None], "proj": pl_proj_w.astype(BF16)[:, None]}
    rest = ("out", "gate", "proj")
    gathered = {}
    gathered["in", 0], small_f = _run_comm(_gather_comm([(shards["in"], 0)], small_w), "gather_first")
    todo = [item for l in range(depth) for item in [(nm, l) for nm in rest] + ([("in", l + 1)] if l + 1 < depth else [])]

    def take_until(key):
        if key not in todo:
            return []
        n_items = todo.index(key) + 1
        items, todo[:] = todo[:n_items], todo[n_items:]
        return items

    def mixer_limit(l):
        ahead = l + 3 if l % 2 == 0 else l + 1
        return ("in", ahead) if ahead < depth else (rest[-1], depth - 1)

    def carried(keys, relay_lead=0):
        return _gather_comm([(shards[nm], ly) for nm, ly in keys], relay_lead=relay_lead) if keys else None

    mixer_relay_lead = 4

    def unpack(res, keys):
        if not keys:
            return res
        gathered.update(zip(keys, res[1]))
        return res[0]

    conv_w_rows, sgu_g_rows, sgu_b_rows = _unpack_rows(
        jnp.transpose(small_f, (1, 0, 2)).reshape(small_f.shape[1], e),
        [(n_conv * k_taps, e), (n_sgu, e), (n_sgu, e)], e)
    conv_w_full = conv_w_rows.reshape(n_conv, k_taps, e)
    conv_w_pad = jnp.pad(conv_w_full, ((0, 0), (0, 1), (0, 0)))
    conv_w_fwd, conv_w_bwd = [
        jnp.transpose(conv_w_pad.reshape(n_conv, kp, e // min(cb, e), min(cb, e)), (0, 2, 1, 3)) for cb in (2 * LANES, LANES)]
    sgu_ln_g_full = sgu_g_rows.reshape(n_sgu, 1, e)
    sgu_ln_b_full = sgu_b_rows.reshape(n_sgu, 1, e)
    sgu_bt = jnp.transpose(sgu_b, (0, 2, 1))

    norm_g3 = norm_g[:, None]
    pl_norm_g3 = pl_norm_g[:, None]
    conv_b3, conv_ln_g3, conv_ln_b3 = conv_b[:, None], conv_ln_g[:, None], conv_ln_b[:, None]
    p3 = p.reshape(depth, t, ple)

    xs, hs, projs, us, x1s, y1s, weights = [], [], [], [], [], {}, []
    xc = x.reshape(t, d)
    for l in range(depth):
        j = l // 2
        xs.append(xc)
        keys = take_until(("in", 1) if depth > 1 else (rest[-1], 0)) if l == 0 else []
        h, proj = unpack(_fwd_in(xc, norm_g3, gathered["in", l], l, tm_mm, carried(keys)), keys)
        keys = take_until(mixer_limit(l))
        if l % 2 == 0:
            u, y1s[l] = unpack(_fwd_conv(proj, conv_w_fwd, conv_b3, conv_ln_g3, conv_ln_b3, j, bsz, seq, tm_mix,
                                         carried(keys, mixer_relay_lead)), keys)
        else:
            (u,) = unpack(_fwd_sgu(proj, sgu_ln_g_full, sgu_ln_b_full, sgu_w, sgu_bt, j, tm_mix, carried(keys, mixer_relay_lead)), keys)
        w_out_l, gate_l, proj_l = gathered["out", l].reshape(1, e, d), gathered["gate", l].reshape(1, d, d), gathered["proj", l]
        weights.append((gathered["in", l], w_out_l, gate_l, proj_l))
        x1, xc = _fwd_out(xc, u, w_out_l, pl_norm_g3, gate_l, p3, proj_l, l, tm_mm)
        hs.append(h)
        projs.append(proj)
        us.append(u)
        x1s.append(x1)

    loss_local, dx, d_final_g = _loss_head(xc, final_g[None], loss_target.reshape(t, d), tm_mm)
    loss = lax.psum(loss_local[0, 0], ("x", "y", "c"))

    c_arr = my_c.astype(jnp.int32).reshape(1)
    ck_arr = jnp.stack([my_c, my_k]).astype(jnp.int32)
    by_chip = lambda a: a.reshape((1, N_CHIPS) + a.shape[1:])
    d_norm_g, d_pl_norm_g = [None] * depth, [None] * depth
    d_conv = [None] * n_conv
    d_sgu = [None] * n_sgu
    locals_, siblings, arrived = [None] * depth, [None] * depth, [None] * depth
    pending = []

    def take():
        if not pending:
            return None
        kind, _, payload = pending[0]
        return _swap_comm(payload) if kind == "swap" else _scatter_comm(payload)

    def settle(res, comm):
        if comm is None:
            return res
        outs, brought = res
        kind, layer, payload = pending.pop()
        if kind == "swap":
            locals_[layer], siblings[layer] = payload, brought
            pending.append(("scatter", layer, [by_chip(wire) for wire in _pair_sums(payload, brought, c_arr, BF16)]))
        else:
            arrived[layer] = brought
        return outs

    def run_alone(name):
        comm = take()
        settle(([], _run_comm(comm, name)), comm)

    for l in reversed(range(depth)):
        j = l // 2
        w_in_l, w_out_l, gate_l, proj_l = weights[l]
        dx1, du, rn, ds, dqv, d_pl_norm_g[l] = _bwd_out(dx, x1s[l], p3, pl_norm_g3, gate_l, proj_l, w_out_l, l, tm_mm)
        (g_proj,) = _wgrad(p3, dqv, 1, 1, N_CHIPS, tm_wg, "wgrad_proj", a_layer=l)
        (g_gate,) = _wgrad(rn, ds, 1, 1, 1, tm_wg, "wgrad_gate")
        (g_out,) = _wgrad(us[l], dx1, 1, 1, 1, tm_wg_out, "wgrad_out")
        comm = take()
        if l % 2 == 0:
            dproj, dcw, dcb, dlg, dlb = settle(
                _bwd_conv(du, projs[l], y1s[l], conv_w_bwd, conv_ln_g3, conv_ln_b3, j, bsz, seq, tm_mix, comm), comm)
            d_conv[j] = (dcw, dcb, dlg, dlb)
        else:
            dproj, dsw, dsbt, dlg, dlb = settle(
                _bwd_sgu(du, projs[l], sgu_ln_g_full, sgu_ln_b_full, sgu_w, sgu_bt, j, tm_mix, comm), comm)
            d_sgu[j] = (dsw, dsbt, dlg, dlb)
        (g_in,) = _wgrad(hs[l], dproj, 1, N_CHIPS, 1, tm_wg, "wgrad_in")
        local = [g_in.reshape(N_CHIPS, 2, d // 2, n4), g_out.reshape(N_CHIPS, 2, e // (2 * N_CHIPS), d),
                 g_gate.reshape(N_CHIPS, 2, d // (2 * N_CHIPS), d), g_proj.reshape(N_CHIPS, 2, ple // 2, dq)]
        while pending:
            run_alone("reduce_step")
        pending.append(("swap", l, local))
        if l == 0:
            run_alone("swap_last")
        comm = take()
        dx, d_norm_g[l] = settle(_bwd_in(dproj, dx1, xs[l], norm_g3, w_in_l, l, tm_mm, comm), comm)
    while pending:
        run_alone("reduce_tail")
    grad_x = dx.reshape(bsz, seq, d)

    d_conv_w = jnp.stack([jnp.transpose(dc[0], (1, 0, 2)).reshape(kp, e)[:k_taps] for dc in d_conv])
    d_conv_b = jnp.stack([dc[1][0] for dc in d_conv])
    d_conv_ln_g = jnp.stack([dc[2][0] for dc in d_conv])
    d_conv_ln_b = jnp.stack([dc[3][0] for dc in d_conv])
    d_sgu_w = jnp.stack([dsg[0] for dsg in d_sgu])
    d_sgu_b = jnp.stack([jnp.transpose(dsg[1]) for dsg in d_sgu])
    d_sgu_ln_g = jnp.stack([dsg[2][0] for dsg in d_sgu])
    d_sgu_ln_b = jnp.stack([dsg[3][0] for dsg in d_sgu])
    small_grads = [jnp.concatenate(d_norm_g), d_conv_w, d_conv_b, d_conv_ln_g, d_conv_ln_b, d_sgu_ln_g, d_sgu_ln_b,
                   d_sgu_w, d_sgu_b, jnp.concatenate(d_pl_norm_g), d_final_g]
    small_shapes = [a.shape for a in small_grads]
    packed = _pack_rows(small_grads, d)
    pack_rows = packed.shape[0] + ((-packed.shape[0]) % (8 * SUBLANES))
    packed = _pack_rows(small_grads, d, pack_rows)
    gl_small = packed.reshape(N_CHIPS, 2, pack_rows // 8, d)
    (small_sibling,) = _run_comm(_swap_comm([gl_small]), "swap_small")
    small_pair = by_chip(_pair_sums([gl_small], [small_sibling], c_arr, F32)[0])

    (small_arrived,) = _run_comm(_scatter_comm([small_pair]), "scatter_small")
    reduced = [lax.empty((depth, 2) + gl.shape[2:], F32) for gl in locals_[0]]
    for l in range(depth):
        reduced = _chip_sums(locals_[l], siblings[l], arrived[l], ck_arr, reduced, l)
    (small_both,) = _chip_sums([gl_small], [small_sibling], [small_arrived], ck_arr,
                               [lax.empty((1, 2) + gl_small.shape[2:], F32)], 0)
    small_mine = lax.dynamic_slice_in_dim(small_both, my_c, 1, axis=1)
    reduced = _swap_pieces(reduced)
    small_all = _gather_pieces(small_mine)
    small_all = lax.dynamic_update_slice(small_all, small_mine, (my_k, my_c, 0, 0)).reshape(pack_rows, d)
    small_red = _unpack_rows(small_all, small_shapes, d)
    (gr_norm_g, gr_conv_w, gr_conv_b, gr_conv_ln_g, gr_conv_ln_b, gr_sgu_ln_g, gr_sgu_ln_b, gr_sgu_w, gr_sgu_b,
     gr_pl_norm_g, gr_final_g) = small_red
    gr_final_g = gr_final_g.reshape(d)
    gr_conv_w = lax.dynamic_slice_in_dim(gr_conv_w, my_k * ec, ec, axis=2)
    gr_sgu_ln_g = lax.dynamic_slice_in_dim(gr_sgu_ln_g, my_k * ec, ec, axis=1)
    gr_sgu_ln_b = lax.dynamic_slice_in_dim(gr_sgu_ln_b, my_k * ec, ec, axis=1)

    def shard_update(w, g, m, v):
        flat = lambda a: a.reshape(-1, w.shape[-1])
        return [o.reshape(w.shape) for o in _adamw(flat(w), flat(g), flat(m), flat(v))]

    up_in = shard_update(w_in, reduced[0], m_w_in, v_w_in)
    up_out = shard_update(w_out, reduced[1], m_w_out, v_w_out)
    up_gate = shard_update(pl_gate_w, reduced[2], m_pl_gate_w, v_pl_gate_w)
    up_proj = shard_update(pl_proj_w, reduced[3], m_pl_proj_w, v_pl_proj_w)

    small_names = ["norm_g", "conv_w", "conv_b", "conv_ln_g", "conv_ln_b", "sgu_ln_g", "sgu_ln_b", "sgu_w", "sgu_b",
                   "pl_norm_g", "final_g"]
    small_w_list = [norm_g, conv_w, conv_b, conv_ln_g, conv_ln_b, sgu_ln_g, sgu_ln_b, sgu_w, sgu_b, pl_norm_g, final_g]
    small_m_list = [m_norm_g, m_conv_w, m_conv_b, m_conv_ln_g, m_conv_ln_b, m_sgu_ln_g, m_sgu_ln_b, m_sgu_w, m_sgu_b,
                    m_pl_norm_g, m_final_g]
    small_v_list = [v_norm_g, v_conv_w, v_conv_b, v_conv_ln_g, v_conv_ln_b, v_sgu_ln_g, v_sgu_ln_b, v_sgu_w, v_sgu_b,
                    v_pl_norm_g, v_final_g]
    small_g_list = [gr_norm_g, gr_conv_w, gr_conv_b, gr_conv_ln_g, gr_conv_ln_b, gr_sgu_ln_g, gr_sgu_ln_b, gr_sgu_w,
                    gr_sgu_b, gr_pl_norm_g, gr_final_g]
    width = ec
    shapes_local = [a.shape for a in small_w_list]
    outs_small = _adamw(_pack_rows(small_w_list, width), _pack_rows(small_g_list, width),
                        _pack_rows(small_m_list, width), _pack_rows(small_v_list, width))
    unpacked = [_unpack_rows(o, shapes_local, width) for o in outs_small]
    ups = {name: [unpacked[kind][i] for kind in range(4)] for i, name in enumerate(small_names)}
    ups["w_in"], ups["w_out"], ups["pl_gate_w"], ups["pl_proj_w"] = up_in, up_out, up_gate, up_proj

    order = ["norm_g", "w_in", "w_out", "conv_w", "conv_b", "conv_ln_g", "conv_ln_b", "sgu_ln_g", "sgu_ln_b", "sgu_w",
             "sgu_b", "pl_norm_g", "pl_gate_w", "pl_proj_w", "final_g"]
    result = [loss, grad_x]
    for kind in range(4):
        result.extend(ups[name][kind] for name in order)
    return tuple(result)
```
